```python
import jax, jax.numpy as jnp
from jax import lax
import numpy as np

D_MODEL = 1024
BATCH = 16
SEQ = 256
DEPTH = 2
DEC_BATCH = 4
DEC_SEQ = 4096
PAST_LEN = 256

GRID_W = 64
D_MIX = D_MODEL
D_RNN = D_MIX // 2
RNN_HEADS = 8
RNN_HEAD_DIM = D_RNN // RNN_HEADS
CONV_W = 4
CONV_PAD = (2, 1)
LRU_C = 8.0
D_TMLP = D_MIX // 4
TMLP_HEADS = 4
TMLP_HEAD_DIM = D_TMLP // TMLP_HEADS
CHUNK = 128
D_FNET = D_MIX // 4
FNET_GROUPS = 4
FNET_GROUP_DIM = D_FNET // FNET_GROUPS
D_IN = 2 * D_RNN + 2 * D_TMLP + D_FNET
IN_SPLITS = (D_RNN, 2 * D_RNN, 2 * D_RNN + D_TMLP, 2 * D_RNN + 2 * D_TMLP)
N_EGROUPS = 4
N_EPG = 8
N_EXPERTS = N_EGROUPS * N_EPG
TOP_K = 2
D_EXPERT = 512
N_MOD = 6
EPS = 1e-6

kernel_name = "hybrid_lru_chunkmlp_fnet_hmoe_diffusion_step"


def rmsnorm(x, g):
    xf = x.astype(jnp.float32)
    y = xf * lax.rsqrt(jnp.mean(xf * xf, axis=-1, keepdims=True) + EPS)
    return (y * g.astype(jnp.float32)).astype(x.dtype)


def grid_pos_embed(rows, d, dtype):
    t = jnp.arange(rows * GRID_W)
    row = (t // GRID_W).astype(jnp.float32)
    col = (t % GRID_W).astype(jnp.float32)
    nf = d // 4
    freq = 1.0 / (10000.0 ** (jnp.arange(nf, dtype=jnp.float32) / nf))
    er = row[:, None] * freq
    ec = col[:, None] * freq
    return jnp.concatenate([jnp.sin(er), jnp.cos(er), jnp.sin(ec), jnp.cos(ec)], axis=-1).astype(dtype)


def modulation(cond, w, b):
    m = jax.nn.silu(cond) @ w + b
    return m[:, None, :]


def dwconv(x, w, b):
    y = lax.conv_general_dilated(x, w[:, None, :].astype(x.dtype), window_strides=(1,), padding=[CONV_PAD],
                                 dimension_numbers=('NWC', 'WIO', 'NWC'), feature_group_count=x.shape[-1])
    return y + b


def _lin_combine(e1, e2):
    a1, b1 = e1
    a2, b2 = e2
    return a1 * a2, a2 * b1 + b2


def rg_lru_dir(x, wa, ba, wx, bx, lam, h0, reverse):
    bsz, L, _ = x.shape
    xh = x.reshape(bsz, L, RNN_HEADS, RNN_HEAD_DIM)
    r = jax.nn.sigmoid(jnp.einsum('blhi,hij->blhj', xh, wa).reshape(bsz, L, D_RNN) + ba)
    i = jax.nn.sigmoid(jnp.einsum('blhi,hij->blhj', xh, wx).reshape(bsz, L, D_RNN) + bx)
    log_a = -LRU_C * r * jax.nn.softplus(-lam)
    a = jnp.exp(log_a)
    b = jnp.sqrt(-jnp.expm1(2.0 * log_a)) * (i * x)
    a_cum, h_zero = lax.associative_scan(_lin_combine, (a, b), axis=1, reverse=reverse)
    h = a_cum * h0[:, None, :] + h_zero
    h_final = h[:, 0] if reverse else h[:, -1]
    return h, h_final


def recurrent_mixer(xr, gr, p, h0f, h0b):
    xc = dwconv(xr, p['conv_w'], p['conv_b']).astype(jnp.float32)
    hf, sf = rg_lru_dir(xc, p['lru_wa'][0], p['lru_ba'][0], p['lru_wx'][0], p['lru_bx'][0], p['lru_lambda'][0], h0f, False)
    hb, sb = rg_lru_dir(xc, p['lru_wa'][1], p['lru_ba'][1], p['lru_wx'][1], p['lru_bx'][1], p['lru_lambda'][1], h0b, True)
    y = (hf + hb) * jax.nn.gelu(gr.astype(jnp.float32))
    return y.astype(xr.dtype), sf, sb


def chunk_mlp(u, v, ws, bs):
    bsz, L, _ = v.shape
    n = L // CHUNK
    vh = v.reshape(bsz, n, CHUNK, TMLP_HEADS, TMLP_HEAD_DIM)
    s = jnp.einsum('hpq,bnqhc->bnphc', ws, vh) + bs.T[None, None, :, :, None]
    return u * s.reshape(bsz, L, D_TMLP)


def fourier_mixer(z, w):
    bsz, L, _ = z.shape
    zg = z.reshape(bsz, L, FNET_GROUPS, FNET_GROUP_DIM).astype(jnp.float32)
    f = jnp.fft.fft2(zg, axes=(1, 3), norm='ortho').real
    y = jnp.einsum('blgc,gcd->blgd', f, w.astype(jnp.float32))
    return y.reshape(bsz, L, D_FNET).astype(z.dtype)


def hier_moe(h, p):
    bsz, L, d = h.shape
    t = h.reshape(-1, d)
    gl = (t @ p['router_g'] + p['router_g_b']).astype(jnp.float32)
    gp = jax.nn.softmax(gl, axis=-1)
    _, gsel = lax.top_k(gl, 1)
    oh_g = jax.nn.one_hot(gsel[:, 0], N_EGROUPS, dtype=jnp.float32)
    pg = jnp.sum(gp * oh_g, axis=-1)
    el = (t @ p['router_e'] + p['router_e_b']).astype(jnp.float32).reshape(-1, N_EGROUPS, N_EPG)
    el_sel = jnp.einsum('tge,tg->te', el, oh_g)
    top_v, top_i = lax.top_k(el_sel, TOP_K)
    fw = jax.nn.softmax(top_v, axis=-1)
    fine = jnp.sum(jax.nn.one_hot(top_i, N_EPG, dtype=jnp.float32) * fw[..., None], axis=1)
    gate = (oh_g[:, :, None] * fine[:, None, :] * pg[:, None, None]).astype(t.dtype)
    out = jnp.zeros_like(t)
    for g in range(N_EGROUPS):
        a = jnp.einsum('td,edf->tef', t, p['e_w1'][g])
        b = jnp.einsum('td,edf->tef', t, p['e_w3'][g])
        hid = jax.nn.silu(a) * b * gate[:, g, :, None]
        out = out + jnp.einsum('tef,efd->td', hid, p['e_w2'][g])
    return out.reshape(bsz, L, d)


def trunk_layer(x, mod, h0f, h0b, p):
    sh1, sc1, g1, sh2, sc2, g2 = jnp.split(mod, N_MOD, axis=-1)
    h = rmsnorm(x, p['g_mix']) * (1 + sc1) + sh1
    z = h @ p['w_in']
    xr, gr, u, v, zf = jnp.split(z, IN_SPLITS, axis=-1)
    yr, sf, sb = recurrent_mixer(xr, gr, p, h0f, h0b)
    yt = chunk_mlp(u, v, p['tmlp_ws'], p['tmlp_b'])
    yf = fourier_mixer(zf, p['fnet_w'])
    y = jnp.concatenate([yr, yt, yf], axis=-1) @ p['w_out']
    x = x + g1 * y
    h = rmsnorm(x, p['g_ffn']) * (1 + sc2) + sh2
    x = x + g2 * hier_moe(h, p)
    return x, sf, sb


def setup_inputs(seed: int = 0) -> dict:
    key = jax.random.key(seed)
    ks = jax.random.split(key, 32)
    f32 = jnp.float32
    nrm = lambda k, s, sc: jax.random.normal(k, s, f32) * sc
    u = jax.random.uniform(ks[20], (DEPTH, 2, D_RNN), f32, 0.9, 0.999)
    a = u ** (1.0 / LRU_C)
    lam = jnp.log(a / (1.0 - a))
    return {
        "x_prompt": nrm(ks[0], (BATCH, SEQ, D_MODEL), 1.0),
        "x_sample": nrm(ks[1], (DEC_BATCH, DEC_SEQ, D_MODEL), 1.0),
        "state_lru": nrm(ks[2], (DEC_BATCH, DEPTH, 2, D_RNN), 0.5),
        "c": nrm(ks[3], (DEC_BATCH, D_MODEL), 1.0),
        "c_ctx": nrm(ks[4], (D_MODEL,), 1.0),
        "w_ada": nrm(ks[5], (DEPTH, D_MODEL, N_MOD * D_MODEL), 0.5 * D_MODEL ** -0.5),
        "b_ada": nrm(ks[6], (DEPTH, N_MOD * D_MODEL), 0.01),
        "g_mix": 1.0 + nrm(ks[7], (DEPTH, D_MODEL), 0.01),
        "g_ffn": 1.0 + nrm(ks[8], (DEPTH, D_MODEL), 0.01),
        "g_final": 1.0 + nrm(ks[9], (D_MODEL,), 0.01),
        "w_in": nrm(ks[10], (DEPTH, D_MODEL, D_IN), D_MODEL ** -0.5),
        "w_out": nrm(ks[11], (DEPTH, D_MIX, D_MODEL), D_MIX ** -0.5),
        "conv_w": nrm(ks[12], (DEPTH, CONV_W, D_RNN), CONV_W ** -0.5),
        "conv_b": nrm(ks[13], (DEPTH, D_RNN), 0.01),
        "lru_wa": nrm(ks[14], (DEPTH, 2, RNN_HEADS, RNN_HEAD_DIM, RNN_HEAD_DIM), RNN_HEAD_DIM ** -0.5),
        "lru_ba": nrm(ks[15], (DEPTH, 2, D_RNN), 0.01),
        "lru_wx": nrm(ks[16], (DEPTH, 2, RNN_HEADS, RNN_HEAD_DIM, RNN_HEAD_DIM), RNN_HEAD_DIM ** -0.5),
        "lru_bx": nrm(ks[17], (DEPTH, 2, D_RNN), 0.01),
        "lru_lambda": lam,
        "tmlp_ws": nrm(ks[18], (DEPTH, TMLP_HEADS, CHUNK, CHUNK), CHUNK ** -0.5),
        "tmlp_b": 1.0 + nrm(ks[19], (DEPTH, TMLP_HEADS, CHUNK), 0.01),
        "fnet_w": nrm(ks[21], (DEPTH, FNET_GROUPS, FNET_GROUP_DIM, FNET_GROUP_DIM), FNET_GROUP_DIM ** -0.5),
        "router_g": nrm(ks[22], (DEPTH, D_MODEL, N_EGROUPS), D_MODEL ** -0.5),
        "router_g_b": nrm(ks[23], (DEPTH, N_EGROUPS), 0.01),
        "router_e": nrm(ks[24], (DEPTH, D_MODEL, N_EXPERTS), D_MODEL ** -0.5),
        "router_e_b": nrm(ks[25], (DEPTH, N_EXPERTS), 0.01),
        "e_w1": nrm(ks[26], (DEPTH, N_EGROUPS, N_EPG, D_MODEL, D_EXPERT), D_MODEL ** -0.5),
        "e_w3": nrm(ks[27], (DEPTH, N_EGROUPS, N_EPG, D_MODEL, D_EXPERT), D_MODEL ** -0.5),
        "e_w2": nrm(ks[28], (DEPTH, N_EGROUPS, N_EPG, D_EXPERT, D_MODEL), D_EXPERT ** -0.5),
    }


def reference(x_prompt, x_sample, state_lru, c, c_ctx, w_ada, b_ada, g_mix, g_ffn, g_final, w_in, w_out,
              conv_w, conv_b, lru_wa, lru_ba, lru_wx, lru_bx, lru_lambda, tmlp_ws, tmlp_b, fnet_w,
              router_g, router_g_b, router_e, router_e_b, e_w1, e_w3, e_w2):
    n_lat = x_sample.shape[1]
    rows = n_lat // GRID_W
    xs = x_sample + grid_pos_embed(rows, D_MODEL, x_sample.dtype)[None]
    xp = x_prompt
    bp = x_prompt.shape[0]
    zeros = jnp.zeros((bp, D_RNN), jnp.float32)
    layer_states = []
    for l in range(DEPTH):
        p = {
            'g_mix': g_mix[l], 'g_ffn': g_ffn[l], 'w_in': w_in[l], 'w_out': w_out[l],
            'conv_w': conv_w[l], 'conv_b': conv_b[l],
            'lru_wa': lru_wa[l].astype(jnp.float32), 'lru_ba': lru_ba[l].astype(jnp.float32),
            'lru_wx': lru_wx[l].astype(jnp.float32), 'lru_bx': lru_bx[l].astype(jnp.float32),
            'lru_lambda': lru_lambda[l].astype(jnp.float32),
            'tmlp_ws': tmlp_ws[l], 'tmlp_b': tmlp_b[l], 'fnet_w': fnet_w[l],
            'router_g': router_g[l], 'router_g_b': router_g_b[l],
            'router_e': router_e[l], 'router_e_b': router_e_b[l],
            'e_w1': e_w1[l], 'e_w3': e_w3[l], 'e_w2': e_w2[l],
        }
        mod_ctx = modulation(c_ctx[None, :], w_ada[l], b_ada[l])
        mod_lat = modulation(c, w_ada[l], b_ada[l])
        xp, sf, sb = trunk_layer(xp, mod_ctx, zeros, zeros, p)
        layer_states.append(jnp.stack([sf, sb], axis=1))
        h0f = state_lru[:, l, 0].astype(jnp.float32)
        h0b = state_lru[:, l, 1].astype(jnp.float32)
        xs, _, _ = trunk_layer(xs, mod_lat, h0f, h0b, p)
    new_state_lru = jnp.stack(layer_states, axis=1).astype(x_prompt.dtype)
    y_prompt = rmsnorm(xp, g_final)
    y_sample = rmsnorm(xs, g_final)
    return (y_prompt, y_sample, new_state_lru)
```

```python
import functools
import math

import jax
import jax.numpy as jnp
from jax import lax
from jax.experimental import pallas as pl
from jax.experimental.pallas import tpu as pltpu

F32 = jnp.float32
BF16 = jnp.bfloat16

D_MODEL = 1024
DEPTH = 2
GRID_W = 64
D_RNN = 512
RNN_HEAD_DIM = 64
CONV_W = 4
LRU_C = 8.0
D_TMLP = 256
TMLP_HEADS = 4
CHUNK = 128
D_FNET = 256
FNET_GROUP_DIM = 64
D_IN = 2 * D_RNN + 2 * D_TMLP + D_FNET
N_EGROUPS = 4
N_EPG = 8
N_EXPERTS = N_EGROUPS * N_EPG
D_EXPERT = 512
N_MOD = 6
EPS = 1e-6

LANES = 128
SUBLANES = 8
LRU_CB = LANES
ROW_TILE = 256
EXPERT_TILE = 256
ROUTER_LANES = LANES
NEG_BIG = -1e30
VMEM_LIMIT = 56 * 1024 * 1024


def _cparams(sem):
    return pltpu.CompilerParams(dimension_semantics=sem, vmem_limit_bytes=VMEM_LIMIT)


def _mod_kernel(c_ref, w_ref, b_ref, o_ref):
    c = c_ref[...]
    s = c * jax.nn.sigmoid(c)
    o_ref[0] = jnp.dot(s.astype(BF16), w_ref[0].astype(BF16), preferred_element_type=F32) + b_ref[0]


def _modulation(cond8, w_ada, b_ada):
    tn = 1536
    return pl.pallas_call(
        _mod_kernel,
        grid=(DEPTH, N_MOD * D_MODEL // tn),
        in_specs=[
            pl.BlockSpec((SUBLANES, D_MODEL), lambda l, j: (0, 0)),
            pl.BlockSpec((1, D_MODEL, tn), lambda l, j: (l, 0, j)),
            pl.BlockSpec((1, 1, tn), lambda l, j: (l, 0, j)),
        ],
        out_specs=pl.BlockSpec((1, SUBLANES, tn), lambda l, j: (l, 0, j)),
        out_shape=jax.ShapeDtypeStruct((DEPTH, SUBLANES, N_MOD * D_MODEL), F32),
        compiler_params=_cparams(("arbitrary", "arbitrary")),
        name="modulation",
    )(cond8, w_ada, b_ada.reshape(DEPTH, 1, N_MOD * D_MODEL))


def _fprep_kernel(w_ref, a_ref, b_ref):
    r = lax.broadcasted_iota(jnp.int32, (D_FNET, D_FNET), 0)
    c = lax.broadcasted_iota(jnp.int32, (D_FNET, D_FNET), 1)
    same = (r >> 6) == (c >> 6)
    ph = ((r & 63) * (c & 63)) & 63
    ang = ph.astype(F32) * (2.0 * math.pi / FNET_GROUP_DIM)
    scale = 1.0 / math.sqrt(FNET_GROUP_DIM)
    cm = jnp.where(same, jnp.cos(ang) * scale, 0.0)
    sm = jnp.where(same, jnp.sin(ang) * scale, 0.0)
    w = w_ref[0]
    a_ref[0] = jnp.dot(cm, w, precision=lax.Precision.HIGHEST, preferred_element_type=F32).astype(BF16)
    b_ref[0] = jnp.dot(sm, w, precision=lax.Precision.HIGHEST, preferred_element_type=F32).astype(BF16)


def _fnet_prep(wbd):
    spec = pl.BlockSpec((1, D_FNET, D_FNET), lambda l: (l, 0, 0))
    return pl.pallas_call(
        _fprep_kernel,
        grid=(DEPTH,),
        in_specs=[spec],
        out_specs=[spec, spec],
        out_shape=[jax.ShapeDtypeStruct((DEPTH, D_FNET, D_FNET), BF16)] * 2,
        compiler_params=_cparams(("arbitrary",)),
        name="fnet_prep",
    )(wbd)


def _table_kernel(c_ref, s_ref, *, L, tr):
    k = lax.broadcasted_iota(jnp.int32, (tr, L), 0) + pl.program_id(0) * tr
    n = lax.broadcasted_iota(jnp.int32, (tr, L), 1)
    ph = (k * n) & (L - 1)
    ang = ph.astype(F32) * (2.0 * math.pi / L)
    scale = 1.0 / math.sqrt(L)
    c_ref[...] = (jnp.cos(ang) * scale).astype(BF16)
    s_ref[...] = (jnp.sin(ang) * (-scale)).astype(BF16)


def _dft_tables(L):
    tr = 256
    spec = pl.BlockSpec((tr, L), lambda i: (i, 0))
    return pl.pallas_call(
        functools.partial(_table_kernel, L=L, tr=tr),
        grid=(L // tr,),
        out_specs=[spec, spec],
        out_shape=[jax.ShapeDtypeStruct((L, L), BF16)] * 2,
        compiler_params=_cparams(("arbitrary",)),
        name=f"dft_tables_{L}",
    )()


def _dft_kernel(c_ref, s_ref, va_ref, vb_ref, o_ref):
    o_ref[...] = (jnp.dot(c_ref[...], va_ref[...], preferred_element_type=F32)
                  + jnp.dot(s_ref[...], vb_ref[...], preferred_element_type=F32))


def _dft_apply(ct, st, va, vb):
    L, ncols = va.shape
    tk, tc = 256, 1024
    return pl.pallas_call(
        _dft_kernel,
        grid=(L // tk, ncols // tc),
        in_specs=[
            pl.BlockSpec((tk, L), lambda i, j: (i, 0)),
            pl.BlockSpec((tk, L), lambda i, j: (i, 0)),
            pl.BlockSpec((L, tc), lambda i, j: (0, j)),
            pl.BlockSpec((L, tc), lambda i, j: (0, j)),
        ],
        out_specs=pl.BlockSpec((tk, tc), lambda i, j: (i, j)),
        out_shape=jax.ShapeDtypeStruct((L, ncols), F32),
        compiler_params=_cparams(("arbitrary", "arbitrary")),
        name=f"dft_apply_{L}",
    )(ct, st, va, vb)


def _rms_mod(x, g, scale, shift):
    ms = jnp.mean(x * x, axis=-1, keepdims=True)
    return (x * lax.rsqrt(ms + EPS)) * g * (1.0 + scale) + shift


def _in_kernel(*refs, add_pos, row_base, row_step):
    if add_pos:
        x_ref, pos_ref, mod_ref, g_ref, w_ref, a_ref, b_ref, zr_ref, zuv_ref, va_ref, vb_ref = refs
    else:
        x_ref, mod_ref, g_ref, w_ref, a_ref, b_ref, zr_ref, zuv_ref, va_ref, vb_ref = refs
    x = x_ref[0]
    if add_pos:
        x = x + pos_ref[...]
    row = row_base + pl.program_id(0) * row_step
    m = mod_ref[pl.ds(row, 1), :]
    h = _rms_mod(x, g_ref[...], m[:, D_MODEL:2 * D_MODEL], m[:, 0:D_MODEL])
    z = jnp.dot(h.astype(BF16), w_ref[...], preferred_element_type=F32)
    zr_ref[0] = z[:, 0:2 * D_RNN]
    zuv_ref[0] = z[:, 2 * D_RNN:2 * D_RNN + 2 * D_TMLP]
    zf = z[:, 2 * D_RNN + 2 * D_TMLP:D_IN].astype(BF16)
    va_ref[...] = jnp.dot(zf, a_ref[...], preferred_element_type=F32).astype(BF16)
    vb_ref[...] = jnp.dot(zf, b_ref[...], preferred_element_type=F32).astype(BF16)


def _in_proj(x, pos, mod, g, w_in_b, fa, fb, *, row_base, row_step):
    nseq, L, _ = x.shape
    tm = ROW_TILE
    add_pos = pos is not None
    in_specs = [pl.BlockSpec((1, tm, D_MODEL), lambda b, t: (b, t, 0))]
    args = [x]
    if add_pos:
        in_specs.append(pl.BlockSpec((tm, D_MODEL), lambda b, t: (t, 0)))
        args.append(pos)
    in_specs += [
        pl.BlockSpec((SUBLANES, N_MOD * D_MODEL), lambda b, t: (0, 0)),
        pl.BlockSpec((1, D_MODEL), lambda b, t: (0, 0)),
        pl.BlockSpec((D_MODEL, D_IN), lambda b, t: (0, 0)),
        pl.BlockSpec((D_FNET, D_FNET), lambda b, t: (0, 0)),
        pl.BlockSpec((D_FNET, D_FNET), lambda b, t: (0, 0)),
    ]
    args += [mod, g, w_in_b, fa, fb]
    return pl.pallas_call(
        functools.partial(_in_kernel, add_pos=add_pos, row_base=row_base, row_step=row_step),
        grid=(nseq, L // tm),
        in_specs=in_specs,
        out_specs=[
            pl.BlockSpec((1, tm, 2 * D_RNN), lambda b, t: (b, t, 0)),
            pl.BlockSpec((1, tm, 2 * D_TMLP), lambda b, t: (b, t, 0)),
            pl.BlockSpec((tm, D_FNET), lambda b, t: (t, b)),
            pl.BlockSpec((tm, D_FNET), lambda b, t: (t, b)),
        ],
        out_shape=[
            jax.ShapeDtypeStruct((nseq, L, 2 * D_RNN), F32),
            jax.ShapeDtypeStruct((nseq, L, 2 * D_TMLP), F32),
            jax.ShapeDtypeStruct((L, nseq * D_FNET), BF16),
            jax.ShapeDtypeStruct((L, nseq * D_FNET), BF16),
        ],
        compiler_params=_cparams(("arbitrary", "arbitrary")),
        name=f"in_proj_{L}",
    )(*args)


def _gelu_tanh(x):
    return 0.5 * x * (1.0 + jnp.tanh(math.sqrt(2.0 / math.pi) * (x + 0.044715 * (x * x * x))))


def _rows_to_tile(rows):
    sub = lax.broadcasted_iota(jnp.int32, (SUBLANES, LANES), 0)
    out = jnp.zeros((SUBLANES, LANES), F32)
    for s, r in enumerate(rows):
        out = jnp.where(sub == s, jnp.broadcast_to(r, (SUBLANES, LANES)), out)
    return out


def _lru_kernel(xr_ref, gr_ref, cw_ref, cb_ref, wa_ref, wx_ref, ba_ref, bx_ref, lam_ref, h0_ref,
                y_ref, st_ref, xnat, pext, af, bf, ab, bb, hnat, *, L):
    S = L // SUBLANES
    pitch = S + SUBLANES
    n = S * SUBLANES
    chunk = 256

    for s in range(SUBLANES):
        xnat[s * pitch:s * pitch + S, :] = xr_ref[0, s * S:(s + 1) * S, :]

    def perm_in(j, c):
        dst = pl.multiple_of((j + 2) * SUBLANES, SUBLANES)
        pext[pl.ds(dst, SUBLANES), :] = xnat[pl.ds(j, SUBLANES, stride=pitch), :]
        return c

    lax.fori_loop(0, S, perm_in, 0)

    sub = lax.broadcasted_iota(jnp.int32, (SUBLANES, LANES), 0)

    def from_prev_segment(v):
        return jnp.where(sub == 0, 0.0, pltpu.roll(v, 1, axis=0))

    def from_next_segment(v):
        return jnp.where(sub == SUBLANES - 1, 0.0, pltpu.roll(v, SUBLANES - 1, axis=0))

    pext[0:8, :] = from_prev_segment(pext[S * 8:(S + 1) * 8, :])
    pext[8:16, :] = from_prev_segment(pext[(S + 1) * 8:(S + 2) * 8, :])
    pext[(S + 2) * 8:(S + 3) * 8, :] = from_next_segment(pext[16:24, :])

    lam = lam_ref[...]
    nl = -lam
    sp = jnp.maximum(nl, 0.0) + jnp.log1p(jnp.exp(-jnp.abs(nl)))
    a_refs = (af, ab)
    b_refs = (bf, bb)

    def gates(i, c):
        base = pl.multiple_of(i * chunk, chunk)
        xc = (cw_ref[0:1, :] * pext[pl.ds(base, chunk), :]
              + cw_ref[1:2, :] * pext[pl.ds(base + 8, chunk), :]
              + cw_ref[2:3, :] * pext[pl.ds(base + 16, chunk), :]
              + cw_ref[3:4, :] * pext[pl.ds(base + 24, chunk), :]
              + cb_ref[...])
        xcb = xc.astype(BF16)
        for d in range(2):
            r = jax.nn.sigmoid(jnp.dot(xcb, wa_ref[d, 0], preferred_element_type=F32) + ba_ref[d:d + 1, :])
            ig = jax.nn.sigmoid(jnp.dot(xcb, wx_ref[d, 0], preferred_element_type=F32) + bx_ref[d:d + 1, :])
            log_a = (-LRU_C) * r * sp[d:d + 1, :]
            a = jnp.exp(log_a)
            coef = jnp.sqrt(jnp.tanh(-log_a) * (a * a + 1.0))
            a_refs[d][pl.ds(base, chunk), :] = a
            b_refs[d][pl.ds(base, chunk), :] = coef * (ig * xc)
        return c

    lax.fori_loop(0, n // chunk, gates, 0)

    def scan(j, carry):
        hf, pf, hb, pb = carry
        jf = pl.multiple_of(j * SUBLANES, SUBLANES)
        jb = pl.multiple_of((S - 1 - j) * SUBLANES, SUBLANES)
        a1 = af[pl.ds(jf, SUBLANES), :]
        hf = a1 * hf + bf[pl.ds(jf, SUBLANES), :]
        pf = a1 * pf
        bf[pl.ds(jf, SUBLANES), :] = hf
        af[pl.ds(jf, SUBLANES), :] = pf
        a2 = ab[pl.ds(jb, SUBLANES), :]
        hb = a2 * hb + bb[pl.ds(jb, SUBLANES), :]
        pb = a2 * pb
        bb[pl.ds(jb, SUBLANES), :] = hb
        ab[pl.ds(jb, SUBLANES), :] = pb
        return hf, pf, hb, pb

    zero = jnp.zeros((SUBLANES, LANES), F32)
    one = jnp.ones((SUBLANES, LANES), F32)
    hf, pf, hb, pb = lax.fori_loop(0, S, scan, (zero, one, zero, one))

    rows_f = [h0_ref[0, 0:1, :]]
    for s in range(1, SUBLANES):
        rows_f.append(pf[s - 1:s, :] * rows_f[-1] + hf[s - 1:s, :])
    st_ref[0, 0:1, :] = pf[7:8, :] * rows_f[7] + hf[7:8, :]
    rows_b = [None] * SUBLANES
    rows_b[7] = h0_ref[0, 1:2, :]
    for s in range(SUBLANES - 2, -1, -1):
        rows_b[s] = pb[s + 1:s + 2, :] * rows_b[s + 1] + hb[s + 1:s + 2, :]
    st_ref[0, 1:2, :] = pb[0:1, :] * rows_b[0] + hb[0:1, :]
    init_f = _rows_to_tile(rows_f)
    init_b = _rows_to_tile(rows_b)

    def perm_out(j, c):
        src = pl.multiple_of(j * SUBLANES, SUBLANES)
        v = (bf[pl.ds(src, SUBLANES), :] + af[pl.ds(src, SUBLANES), :] * init_f
             + bb[pl.ds(src, SUBLANES), :] + ab[pl.ds(src, SUBLANES), :] * init_b)
        hnat[pl.ds(j, SUBLANES, stride=pitch), :] = v
        return c

    lax.fori_loop(0, S, perm_out, 0)

    for s in range(SUBLANES):
        g = gr_ref[0, s * S:(s + 1) * S, :]
        y_ref[0, s * S:(s + 1) * S, :] = hnat[s * pitch:s * pitch + S, :] * _gelu_tanh(g)


def _lru_mixer(zr, conv_w, conv_b, wa_bd, wx_bd, ba, bx, lam, h0):
    nseq, L, _ = zr.shape
    ncb = D_RNN // LRU_CB
    S = L // SUBLANES
    pitch = S + SUBLANES
    vec2 = pl.BlockSpec((2, LRU_CB), lambda b, c: (0, c))
    wspec = pl.BlockSpec((2, 1, LRU_CB, LRU_CB), lambda b, c: (0, c, 0, 0))
    return pl.pallas_call(
        functools.partial(_lru_kernel, L=L),
        grid=(nseq, ncb),
        in_specs=[
            pl.BlockSpec((1, L, LRU_CB), lambda b, c: (b, 0, c)),
            pl.BlockSpec((1, L, LRU_CB), lambda b, c: (b, 0, c + ncb)),
            pl.BlockSpec((CONV_W, LRU_CB), lambda b, c: (0, c)),
            pl.BlockSpec((1, LRU_CB), lambda b, c: (0, c)),
            wspec, wspec, vec2, vec2, vec2,
            pl.BlockSpec((1, 2, LRU_CB), lambda b, c: (b, 0, c)),
        ],
        out_specs=[
            pl.BlockSpec((1, L, LRU_CB), lambda b, c: (b, 0, c)),
            pl.BlockSpec((1, 2, LRU_CB), lambda b, c: (b, 0, c)),
        ],
        out_shape=[
            jax.ShapeDtypeStruct((nseq, L, D_RNN), F32),
            jax.ShapeDtypeStruct((nseq, 2, D_RNN), F32),
        ],
        scratch_shapes=[
            pltpu.VMEM((SUBLANES * pitch, LANES), F32),
            pltpu.VMEM(((S + 3) * SUBLANES, LANES), F32),
            pltpu.VMEM((L, LANES), F32),
            pltpu.VMEM((L, LANES), F32),
            pltpu.VMEM((L, LANES), F32),
            pltpu.VMEM((L, LANES), F32),
            pltpu.VMEM((SUBLANES * pitch, LANES), F32),
        ],
        compiler_params=_cparams(("arbitrary", "arbitrary")),
        name=f"lru_mixer_{L}",
    )(zr, zr, conv_w, conv_b, wa_bd, wx_bd, ba, bx, lam, h0)


def _out_kernel(*refs, add_pos, row_base, row_step):
    if add_pos:
        (x_ref, pos_ref, yr_ref, zuv_ref, yf_ref, mod_ref, ws_ref, bt_ref, wo_ref, g_ref, wr_ref, br_ref,
         x1_ref, h2_ref, ri_ref) = refs
    else:
        (x_ref, yr_ref, zuv_ref, yf_ref, mod_ref, ws_ref, bt_ref, wo_ref, g_ref, wr_ref, br_ref,
         x1_ref, h2_ref, ri_ref) = refs
    tm = x_ref.shape[1]
    x = x_ref[0]
    if add_pos:
        x = x + pos_ref[...]
    row = row_base + pl.program_id(0) * row_step
    m = mod_ref[pl.ds(row, 1), :]
    g1 = m[:, 2 * D_MODEL:3 * D_MODEL]
    sh2 = m[:, 3 * D_MODEL:4 * D_MODEL]
    sc2 = m[:, 4 * D_MODEL:5 * D_MODEL]

    head = lax.broadcasted_iota(jnp.int32, (CHUNK, D_TMLP), 1) >> 6
    yt_parts = []
    for ci in range(tm // CHUNK):
        u = zuv_ref[0, ci * CHUNK:(ci + 1) * CHUNK, 0:D_TMLP]
        v = zuv_ref[0, ci * CHUNK:(ci + 1) * CHUNK, D_TMLP:2 * D_TMLP].astype(BF16)
        s = jnp.zeros((CHUNK, D_TMLP), F32)
        for h in range(TMLP_HEADS):
            sh = jnp.dot(ws_ref[h], v, preferred_element_type=F32) + bt_ref[:, h:h + 1]
            s = jnp.where(head == h, sh, s)
        yt_parts.append(u * s)
    yt = jnp.concatenate(yt_parts, axis=0) if len(yt_parts) > 1 else yt_parts[0]

    y = (jnp.dot(yr_ref[0].astype(BF16), wo_ref[0:D_RNN, :], preferred_element_type=F32)
         + jnp.dot(yt.astype(BF16), wo_ref[D_RNN:D_RNN + D_TMLP, :], preferred_element_type=F32)
         + jnp.dot(yf_ref[...].astype(BF16), wo_ref[D_RNN + D_TMLP:D_MODEL, :], preferred_element_type=F32))
    x1 = x + g1 * y
    x1_ref[0] = x1
    h2 = _rms_mod(x1, g_ref[...], sc2, sh2)
    h2_ref[0] = h2

    logits = jnp.dot(h2, wr_ref[...], precision=lax.Precision.HIGHEST, preferred_element_type=F32) + br_ref[...]
    lane = lax.broadcasted_iota(jnp.int32, (tm, ROUTER_LANES), 1)
    lane_f = lane.astype(F32)
    is_g = lane < N_EGROUPS
    gl = jnp.where(is_g, logits, NEG_BIG)
    gmax = jnp.max(gl, axis=-1, keepdims=True)
    gsel = jnp.min(jnp.where(gl == gmax, lane_f, 1e4), axis=-1, keepdims=True)
    pg = 1.0 / jnp.sum(jnp.where(is_g, jnp.exp(logits - gmax), 0.0), axis=-1, keepdims=True)
    grp_f = ((lane - N_EGROUPS) >> 3).astype(F32)
    emask = (lane >= N_EGROUPS) & (lane < N_EGROUPS + N_EXPERTS) & (grp_f == gsel)
    el = jnp.where(emask, logits, NEG_BIG)
    v1 = jnp.max(el, axis=-1, keepdims=True)
    i1 = jnp.min(jnp.where(el == v1, lane_f, 1e4), axis=-1, keepdims=True)
    el2 = jnp.where(lane_f == i1, NEG_BIG, el)
    v2 = jnp.max(el2, axis=-1, keepdims=True)
    i2 = jnp.min(jnp.where(el2 == v2, lane_f, 1e4), axis=-1, keepdims=True)
    e2x = jnp.exp(v2 - v1)
    fw1 = 1.0 / (1.0 + e2x)
    fw2 = e2x * fw1
    ri = jnp.where(lane == 0, i1 - N_EGROUPS,
                   jnp.where(lane == 1, i2 - N_EGROUPS,
                             jnp.where(lane == 2, pg * fw1,
                                       jnp.where(lane == 3, pg * fw2, 0.0))))
    ri_ref[0] = ri


def _out_proj(x, pos, yr, zuv, yf, mod, ws_b, bt, wo_b, g, wr, br, *, row_base, row_step):
    nseq, L, _ = x.shape
    tm = ROW_TILE
    add_pos = pos is not None
    in_specs = [pl.BlockSpec((1, tm, D_MODEL), lambda b, t: (b, t, 0))]
    args = [x]
    if add_pos:
        in_specs.append(pl.BlockSpec((tm, D_MODEL), lambda b, t: (t, 0)))
        args.append(pos)
    in_specs += [
        pl.BlockSpec((1, tm, D_RNN), lambda b, t: (b, t, 0)),
        pl.BlockSpec((1, tm, 2 * D_TMLP), lambda b, t: (b, t, 0)),
        pl.BlockSpec((tm, D_FNET), lambda b, t: (t, b)),
        pl.BlockSpec((SUBLANES, N_MOD * D_MODEL), lambda b, t: (0, 0)),
        pl.BlockSpec((TMLP_HEADS, CHUNK, CHUNK), lambda b, t: (0, 0, 0)),
        pl.BlockSpec((CHUNK, TMLP_HEADS), lambda b, t: (0, 0)),
        pl.BlockSpec((D_MODEL, D_MODEL), lambda b, t: (0, 0)),
        pl.BlockSpec((1, D_MODEL), lambda b, t: (0, 0)),
        pl.BlockSpec((D_MODEL, ROUTER_LANES), lambda b, t: (0, 0)),
        pl.BlockSpec((1, ROUTER_LANES), lambda b, t: (0, 0)),
    ]
    args += [yr, zuv, yf, mod, ws_b, bt, wo_b, g, wr, br]
    tok = pl.BlockSpec((1, tm, D_MODEL), lambda b, t: (b, t, 0))
    return pl.pallas_call(
        functools.partial(_out_kernel, add_pos=add_pos, row_base=row_base, row_step=row_step),
        grid=(nseq, L // tm),
        in_specs=in_specs,
        out_specs=[tok, tok, pl.BlockSpec((1, tm, ROUTER_LANES), lambda b, t: (b, t, 0))],
        out_shape=[
            jax.ShapeDtypeStruct((nseq, L, D_MODEL), F32),
            jax.ShapeDtypeStruct((nseq, L, D_MODEL), F32),
            jax.ShapeDtypeStruct((nseq, L, ROUTER_LANES), F32),
        ],
        compiler_params=_cparams(("arbitrary", "arbitrary")),
        name=f"out_proj_{L}",
    )(*args)


def _row_copy(src_ref, src_row, dst_ref, dst_row, sem):
    return pltpu.make_async_copy(src_ref.at[pl.ds(src_row, 1)], dst_ref.at[pl.ds(dst_row, 1)], sem)


def _dispatch_kernel(pos_ref, h_ref, xs_in_ref, xs_ref, sem):
    del xs_in_ref
    tm = h_ref.shape[0]

    def start(r, c):
        _row_copy(h_ref, r, xs_ref, pos_ref[0, 0, r], sem).start()
        _row_copy(h_ref, r, xs_ref, pos_ref[0, 1, r], sem).start()
        return c

    def wait(r, c):
        _row_copy(h_ref, r, xs_ref, pos_ref[0, 0, r], sem).wait()
        _row_copy(h_ref, r, xs_ref, pos_ref[0, 1, r], sem).wait()
        return c

    lax.fori_loop(0, tm, start, 0)
    lax.fori_loop(0, tm, wait, 0)


def _dispatch(pos, h2, xs):
    ntok = h2.shape[0]
    tm = ROW_TILE
    return pl.pallas_call(
        _dispatch_kernel,
        grid=(ntok // tm,),
        in_specs=[
            pl.BlockSpec((1, 2, tm), lambda i: (i, 0, 0), memory_space=pltpu.SMEM),
            pl.BlockSpec((tm, D_MODEL), lambda i: (i, 0)),
            pl.BlockSpec(memory_space=pl.ANY),
        ],
        out_specs=pl.BlockSpec(memory_space=pl.ANY),
        out_shape=jax.ShapeDtypeStruct(xs.shape, xs.dtype),
        scratch_shapes=[pltpu.SemaphoreType.DMA(())],
        input_output_aliases={2: 0},
        compiler_params=_cparams(("arbitrary",)),
        name=f"moe_dispatch_{ntok}",
    )(pos, h2, xs)


def _expert_kernel(te_ref, nv_ref, xs_ref, w1_ref, w3_ref, w2_ref, ys_ref, w1b, w3b, w2b, prev):
    i = pl.program_id(0)

    @pl.when(i == 0)
    def _():
        prev[0] = -1

    @pl.when(i >= nv_ref[0])
    def _():
        ys_ref[...] = jnp.zeros_like(ys_ref)

    @pl.when(i < nv_ref[0])
    def _():
        e = te_ref[i]

        @pl.when(e != prev[0])
        def _():
            w1b[...] = w1_ref[0].astype(BF16)
            w3b[...] = w3_ref[0].astype(BF16)
            w2b[...] = w2_ref[0].astype(BF16)
            prev[0] = e

        x = xs_ref[...].astype(BF16)
        a = jnp.dot(x, w1b[...], preferred_element_type=F32)
        b = jnp.dot(x, w3b[...], preferred_element_type=F32)
        hid = (a * jax.nn.sigmoid(a)) * b
        ys_ref[...] = jnp.dot(hid.astype(BF16), w2b[...], preferred_element_type=F32)


def _experts(tile_expert, nvalid, xs, w1, w3, w2):
    nt = xs.shape[0] // EXPERT_TILE
    row = lambda i, te, nv: (jnp.minimum(i, nv[0] - 1), 0)
    wsel = lambda i, te, nv: (te[i], 0, 0)
    return pl.pallas_call(
        _expert_kernel,
        grid_spec=pltpu.PrefetchScalarGridSpec(
            num_scalar_prefetch=2,
            grid=(nt,),
            in_specs=[
                pl.BlockSpec((EXPERT_TILE, D_MODEL), row),
                pl.BlockSpec((1, D_MODEL, D_EXPERT), wsel),
                pl.BlockSpec((1, D_MODEL, D_EXPERT), wsel),
                pl.BlockSpec((1, D_EXPERT, D_MODEL), wsel),
            ],
            out_specs=pl.BlockSpec((EXPERT_TILE, D_MODEL), lambda i, te, nv: (i, 0)),
            scratch_shapes=[
                pltpu.VMEM((D_MODEL, D_EXPERT), BF16),
                pltpu.VMEM((D_MODEL, D_EXPERT), BF16),
                pltpu.VMEM((D_EXPERT, D_MODEL), BF16),
                pltpu.SMEM((1,), jnp.int32),
            ],
        ),
        out_shape=jax.ShapeDtypeStruct(xs.shape, F32),
        compiler_params=_cparams(("arbitrary",)),
        name="moe_experts",
    )(tile_expert, nvalid, xs, w1, w3, w2)


def _combine_kernel(pos_ref, x1_ref, ri_ref, mod_ref, gf_ref, ys_ref, o_ref, y1buf, y2buf, sem,
                    *, row_base, row_step, tiles_per_seq, final):
    tm = x1_ref.shape[0]

    def start(r, c):
        _row_copy(ys_ref, pos_ref[0, 0, r], y1buf, r, sem).start()
        _row_copy(ys_ref, pos_ref[0, 1, r], y2buf, r, sem).start()
        return c

    def wait(r, c):
        _row_copy(ys_ref, pos_ref[0, 0, r], y1buf, r, sem).wait()
        _row_copy(ys_ref, pos_ref[0, 1, r], y2buf, r, sem).wait()
        return c

    lax.fori_loop(0, tm, start, 0)
    lax.fori_loop(0, tm, wait, 0)
    row = row_base + (pl.program_id(0) // tiles_per_seq) * row_step
    g2 = mod_ref[pl.ds(row, 1), 5 * D_MODEL:6 * D_MODEL]
    ri = ri_ref[...]
    y = ri[:, 2:3] * y1buf[...] + ri[:, 3:4] * y2buf[...]
    x2 = x1_ref[...] + g2 * y
    if final:
        ms = jnp.mean(x2 * x2, axis=-1, keepdims=True)
        o_ref[...] = (x2 * lax.rsqrt(ms + EPS)) * gf_ref[...]
    else:
        o_ref[...] = x2


def _combine(pos, x1, ri, mod, g_final, ys, *, row_base, row_step, tiles_per_seq, final):
    ntok = x1.shape[0]
    tm = ROW_TILE
    tok = pl.BlockSpec((tm, D_MODEL), lambda i: (i, 0))
    return pl.pallas_call(
        functools.partial(_combine_kernel, row_base=row_base, row_step=row_step, tiles_per_seq=tiles_per_seq,
                          final=final),
        grid=(ntok // tm,),
        in_specs=[
            pl.BlockSpec((1, 2, tm), lambda i: (i, 0, 0), memory_space=pltpu.SMEM),
            tok,
            pl.BlockSpec((tm, ROUTER_LANES), lambda i: (i, 0)),
            pl.BlockSpec((SUBLANES, N_MOD * D_MODEL), lambda i: (0, 0)),
            pl.BlockSpec((1, D_MODEL), lambda i: (0, 0)),
            pl.BlockSpec(memory_space=pl.ANY),
        ],
        out_specs=tok,
        out_shape=jax.ShapeDtypeStruct((ntok, D_MODEL), F32),
        scratch_shapes=[
            pltpu.VMEM((tm, D_MODEL), F32),
            pltpu.VMEM((tm, D_MODEL), F32),
            pltpu.SemaphoreType.DMA(()),
        ],
        compiler_params=_cparams(("arbitrary",)),
        name=f"moe_combine_{ntok}",
    )(pos, x1, ri, mod, g_final, ys)


def _route(ri_all):
    e = ri_all[:, 0:2].astype(jnp.int32)
    oh = (e[:, :, None] == jnp.arange(N_EXPERTS, dtype=jnp.int32)).astype(jnp.int32)
    per_tok = oh[:, 0] + oh[:, 1]
    before = jnp.cumsum(per_tok, axis=0) - per_tok
    counts = jnp.sum(per_tok, axis=0)
    tiles = (counts + EXPERT_TILE - 1) // EXPERT_TILE
    tile_end = jnp.cumsum(tiles)
    offs = (tile_end - tiles) * EXPERT_TILE
    rank = jnp.sum(oh * before[:, None, :], axis=-1)
    pos = jnp.sum(oh * offs[None, None, :], axis=-1) + rank
    return pos.astype(jnp.int32), tile_end.astype(jnp.int32)


def _tile_map(tile_end, nt, layer):
    nvalid = tile_end[-1]
    i = jnp.minimum(jnp.arange(nt, dtype=jnp.int32), nvalid - 1)
    te = jnp.sum((i[:, None] >= tile_end[None, :]).astype(jnp.int32), axis=-1)
    return (te + layer * N_EXPERTS).astype(jnp.int32), nvalid.reshape(1).astype(jnp.int32)


def _grid_pos_embed(rows, d):
    t = jnp.arange(rows * GRID_W)
    row = (t // GRID_W).astype(F32)
    col = (t % GRID_W).astype(F32)
    nf = d // 4
    freq = 1.0 / (10000.0 ** (jnp.arange(nf, dtype=F32) / nf))
    er = row[:, None] * freq
    ec = col[:, None] * freq
    return jnp.concatenate([jnp.sin(er), jnp.cos(er), jnp.sin(ec), jnp.cos(ec)], axis=-1)


def _block_diag(w, nblk):
    *lead, H, d, _ = w.shape
    w = w.reshape(*lead, H // nblk, nblk, d, d)
    eye = jnp.eye(nblk, dtype=w.dtype)
    out = jnp.einsum('...gij,gh->...gihj', w, eye)
    return out.reshape(*lead, H // nblk, nblk * d, nblk * d)


def kernel(x_prompt, x_sample, state_lru, c, c_ctx, w_ada, b_ada, g_mix, g_ffn, g_final, w_in, w_out, conv_w, conv_b, lru_wa, lru_ba, lru_wx, lru_bx, lru_lambda, tmlp_ws, tmlp_b, fnet_w, router_g, router_g_b, router_e, router_e_b, e_w1, e_w3, e_w2):
    bp, lp, _ = x_prompt.shape
    bs, ls, _ = x_sample.shape
    n_ctx = bp * lp
    n_lat = bs * ls
    n_tok = n_ctx + n_lat

    cond8 = jnp.concatenate([c_ctx[None, :], c, jnp.zeros((SUBLANES - 1 - bs, D_MODEL), F32)], axis=0)
    pos = _grid_pos_embed(ls // GRID_W, D_MODEL)
    w_in_b = w_in.astype(BF16)
    w_out_b = w_out.astype(BF16)
    heads_per_cb = LRU_CB // RNN_HEAD_DIM
    wa_bd = _block_diag(lru_wa, heads_per_cb).astype(BF16)
    wx_bd = _block_diag(lru_wx, heads_per_cb).astype(BF16)
    ws_b = tmlp_ws.astype(BF16)
    bt = jnp.swapaxes(tmlp_b, 1, 2)
    fnet_bd = _block_diag(fnet_w, D_FNET // FNET_GROUP_DIM)[:, 0]
    wr = jnp.concatenate([router_g, router_e,
                          jnp.zeros((DEPTH, D_MODEL, ROUTER_LANES - N_EGROUPS - N_EXPERTS), F32)], axis=-1)
    br = jnp.concatenate([router_g_b, router_e_b,
                          jnp.zeros((DEPTH, ROUTER_LANES - N_EGROUPS - N_EXPERTS), F32)], axis=-1)
    ew1 = e_w1.reshape(DEPTH * N_EXPERTS, D_MODEL, D_EXPERT)
    ew3 = e_w3.reshape(DEPTH * N_EXPERTS, D_MODEL, D_EXPERT)
    ew2 = e_w2.reshape(DEPTH * N_EXPERTS, D_EXPERT, D_MODEL)

    mods = _modulation(cond8, w_ada, b_ada)
    fa, fb = _fnet_prep(fnet_bd)
    ct_p, st_p = _dft_tables(lp)
    ct_s, st_s = _dft_tables(ls)

    nt = 2 * n_tok // EXPERT_TILE + N_EXPERTS
    zeros_p = jnp.zeros((bp, D_RNN), F32)

    xp, xs = x_prompt, x_sample
    states = []
    for l in range(DEPTH):
        mod = mods[l]
        g_mix_l = g_mix[l][None, :]
        g_ffn_l = g_ffn[l][None, :]
        paths = []
        for (x, ct, st, h0, row_base, row_step, is_lat) in (
                (xp, ct_p, st_p, jnp.stack([zeros_p, zeros_p], axis=1), 0, 0, False),
                (xs, ct_s, st_s, state_lru[:, l], 1, 1, True)):
            pe = pos if (is_lat and l == 0) else None
            zr, zuv, va, vb = _in_proj(x, pe, mod, g_mix_l, w_in_b[l], fa[l], fb[l],
                                       row_base=row_base, row_step=row_step)
            yr, st_new = _lru_mixer(zr, conv_w[l], conv_b[l][None, :], wa_bd[l], wx_bd[l],
                                    lru_ba[l], lru_bx[l], lru_lambda[l], h0)
            yf = _dft_apply(ct, st, va, vb)
            x1, h2, ri = _out_proj(x, pe, yr, zuv, yf, mod, ws_b[l], bt[l], w_out_b[l], g_ffn_l,
                                   wr[l], br[l][None, :], row_base=row_base, row_step=row_step)
            paths.append((x1, h2, ri, st_new))
        states.append(paths[0][3])

        ri_all = jnp.concatenate([paths[0][2].reshape(n_ctx, ROUTER_LANES),
                                  paths[1][2].reshape(n_lat, ROUTER_LANES)], axis=0)
        pos_all, tile_end = _route(ri_all)
        tile_expert, nvalid = _tile_map(tile_end, nt, l)
        pos_tiles = jnp.swapaxes(pos_all.reshape(n_tok // ROW_TILE, ROW_TILE, 2), 1, 2)
        nct = n_ctx // ROW_TILE
        xsorted = jnp.zeros((nt * EXPERT_TILE, D_MODEL), F32)
        xsorted = _dispatch(pos_tiles[:nct], paths[0][1].reshape(n_ctx, D_MODEL), xsorted)
        xsorted = _dispatch(pos_tiles[nct:], paths[1][1].reshape(n_lat, D_MODEL), xsorted)
        ysorted = _experts(tile_expert, nvalid, xsorted, ew1, ew3, ew2)
        gfin = g_final[None, :]
        final = l == DEPTH - 1
        xp = _combine(pos_tiles[:nct], paths[0][0].reshape(n_ctx, D_MODEL),
                      paths[0][2].reshape(n_ctx, ROUTER_LANES), mod, gfin, ysorted,
                      row_base=0, row_step=0, tiles_per_seq=lp // ROW_TILE, final=final).reshape(bp, lp, D_MODEL)
        xs = _combine(pos_tiles[nct:], paths[1][0].reshape(n_lat, D_MODEL),
                      paths[1][2].reshape(n_lat, ROUTER_LANES), mod, gfin, ysorted,
                      row_base=1, row_step=1, tiles_per_seq=ls // ROW_TILE, final=final).reshape(bs, ls, D_MODEL)

    new_state = jnp.stack(states, axis=1).astype(x_prompt.dtype)
    return (xp, xs, new_state)
```

```python
import functools
import math

import jax
import jax.numpy as jnp
from jax import lax
from jax.experimental import pallas as pl
from jax.experimental.pallas import tpu as pltpu

F32 = jnp.float32
BF16 = jnp.bfloat16

D_MODEL = 1024
DEPTH = 2
GRID_W = 64
D_RNN = 512
RNN_HEAD_DIM = 64
CONV_W = 4
LRU_C = 8.0
D_TMLP = 256
TMLP_HEADS = 4
CHUNK = 128
D_FNET = 256
FNET_GROUP_DIM = 64
D_IN = 2 * D_RNN + 2 * D_TMLP + D_FNET
N_EGROUPS = 4
N_EPG = 8
N_EXPERTS = N_EGROUPS * N_EPG
D_EXPERT = 512
N_MOD = 6
EPS = 1e-6

LANES = 128
SUBLANES = 8
LRU_CB = LANES
ROW_TILE = 256
EXPERT_TILE = 256
ROUTER_LANES = LANES
NEG_BIG = -1e30
VMEM_LIMIT = 56 * 1024 * 1024


def _cparams(sem):
    return pltpu.CompilerParams(dimension_semantics=sem, vmem_limit_bytes=VMEM_LIMIT)


def _mod_kernel(c_ref, w_ref, b_ref, o_ref):
    c = c_ref[...]
    s = c * jax.nn.sigmoid(c)
    o_ref[0] = jnp.dot(s.astype(BF16), w_ref[0].astype(BF16), preferred_element_type=F32) + b_ref[0]


def _modulation(cond8, w_ada, b_ada):
    tn = 1536
    return pl.pallas_call(
        _mod_kernel,
        grid=(DEPTH, N_MOD * D_MODEL // tn),
        in_specs=[
            pl.BlockSpec((SUBLANES, D_MODEL), lambda l, j: (0, 0)),
            pl.BlockSpec((1, D_MODEL, tn), lambda l, j: (l, 0, j)),
            pl.BlockSpec((1, 1, tn), lambda l, j: (l, 0, j)),
        ],
        out_specs=pl.BlockSpec((1, SUBLANES, tn), lambda l, j: (l, 0, j)),
        out_shape=jax.ShapeDtypeStruct((DEPTH, SUBLANES, N_MOD * D_MODEL), F32),
        compiler_params=_cparams(("arbitrary", "arbitrary")),
        name="modulation",
    )(cond8, w_ada, b_ada.reshape(DEPTH, 1, N_MOD * D_MODEL))


def _fprep_kernel(w_ref, a_ref, b_ref):
    r = lax.broadcasted_iota(jnp.int32, (D_FNET, D_FNET), 0)
    c = lax.broadcasted_iota(jnp.int32, (D_FNET, D_FNET), 1)
    same = (r >> 6) == (c >> 6)
    ph = ((r & 63) * (c & 63)) & 63
    ang = ph.astype(F32) * (2.0 * math.pi / FNET_GROUP_DIM)
    scale = 1.0 / math.sqrt(FNET_GROUP_DIM)
    cm = jnp.where(same, jnp.cos(ang) * scale, 0.0)
    sm = jnp.where(same, jnp.sin(ang) * scale, 0.0)
    w = w_ref[0]
    a_ref[0] = jnp.dot(cm, w, precision=lax.Precision.HIGHEST, preferred_element_type=F32).astype(BF16)
    b_ref[0] = jnp.dot(sm, w, precision=lax.Precision.HIGHEST, preferred_element_type=F32).astype(BF16)


def _fnet_prep(wbd):
    spec = pl.BlockSpec((1, D_FNET, D_FNET), lambda l: (l, 0, 0))
    return pl.pallas_call(
        _fprep_kernel,
        grid=(DEPTH,),
        in_specs=[spec],
        out_specs=[spec, spec],
        out_shape=[jax.ShapeDtypeStruct((DEPTH, D_FNET, D_FNET), BF16)] * 2,
        compiler_params=_cparams(("arbitrary",)),
        name="fnet_prep",
    )(wbd)


TABLE_ROWS = 64


def _base_table_kernel(cj_ref, sj_ref, cm_ref, sm_ref, *, L):
    j = lax.broadcasted_iota(jnp.int32, (TABLE_ROWS, L), 0)
    n = lax.broadcasted_iota(jnp.int32, (TABLE_ROWS, L), 1)
    w = 2.0 * math.pi / L
    fine = ((j * n) & (L - 1)).astype(F32) * w
    coarse = ((j * TABLE_ROWS * n) & (L - 1)).astype(F32) * w
    scale = 1.0 / math.sqrt(L)
    cj_ref[...] = jnp.cos(fine)
    sj_ref[...] = jnp.sin(fine)
    cm_ref[...] = jnp.cos(coarse) * scale
    sm_ref[...] = jnp.sin(coarse) * scale


def _table_kernel(cj_ref, sj_ref, cm_ref, sm_ref, c_ref, s_ref):
    m = pl.program_id(0)
    c0 = cm_ref[pl.ds(m, 1), :]
    s0 = sm_ref[pl.ds(m, 1), :]
    cj = cj_ref[...]
    sj = sj_ref[...]
    c_ref[...] = (cj * c0 - sj * s0).astype(BF16)
    s_ref[...] = (-(sj * c0 + cj * s0)).astype(BF16)


def _dft_tables(L):
    small = pl.BlockSpec((TABLE_ROWS, L), lambda *_: (0, 0))
    base = pl.pallas_call(
        functools.partial(_base_table_kernel, L=L),
        out_specs=[small] * 4,
        out_shape=[jax.ShapeDtypeStruct((TABLE_ROWS, L), F32)] * 4,
        compiler_params=pltpu.CompilerParams(vmem_limit_bytes=VMEM_LIMIT),
        name=f"dft_base_tables_{L}",
    )()
    spec = pl.BlockSpec((TABLE_ROWS, L), lambda i: (i, 0))
    return pl.pallas_call(
        _table_kernel,
        grid=(L // TABLE_ROWS,),
        in_specs=[small] * 4,
        out_specs=[spec, spec],
        out_shape=[jax.ShapeDtypeStruct((L, L), BF16)] * 2,
        compiler_params=_cparams(("arbitrary",)),
        name=f"dft_tables_{L}",
    )(*base)


def _dft_kernel(c_ref, s_ref, va_ref, vb_ref, o_ref):
    o_ref[...] = (jnp.dot(c_ref[...], va_ref[...], preferred_element_type=F32)
                  + jnp.dot(s_ref[...], vb_ref[...], preferred_element_type=F32))


def _dft_apply(ct, st, va, vb):
    L, ncols = va.shape
    tk, tc = 256, 1024
    return pl.pallas_call(
        _dft_kernel,
        grid=(L // tk, ncols // tc),
        in_specs=[
            pl.BlockSpec((tk, L), lambda i, j: (i, 0)),
            pl.BlockSpec((tk, L), lambda i, j: (i, 0)),
            pl.BlockSpec((L, tc), lambda i, j: (0, j)),
            pl.BlockSpec((L, tc), lambda i, j: (0, j)),
        ],
        out_specs=pl.BlockSpec((tk, tc), lambda i, j: (i, j)),
        out_shape=jax.ShapeDtypeStruct((L, ncols), F32),
        compiler_params=_cparams(("arbitrary", "arbitrary")),
        name=f"dft_apply_{L}",
    )(ct, st, va, vb)


def _rms_mod(x, g, scale, shift):
    ms = jnp.mean(x * x, axis=-1, keepdims=True)
    return (x * lax.rsqrt(ms + EPS)) * g * (1.0 + scale) + shift


def _in_kernel(*refs, add_pos, row_base, row_step):
    if add_pos:
        x_ref, pos_ref, mod_ref, g_ref, w_ref, a_ref, b_ref, zr_ref, zuv_ref, va_ref, vb_ref = refs
    else:
        x_ref, mod_ref, g_ref, w_ref, a_ref, b_ref, zr_ref, zuv_ref, va_ref, vb_ref = refs
    x = x_ref[0]
    if add_pos:
        x = x + pos_ref[...]
    row = row_base + pl.program_id(0) * row_step
    m = mod_ref[pl.ds(row, 1), :]
    h = _rms_mod(x, g_ref[...], m[:, D_MODEL:2 * D_MODEL], m[:, 0:D_MODEL])
    z = jnp.dot(h.astype(BF16), w_ref[...], preferred_element_type=F32)
    zr_ref[0] = z[:, 0:2 * D_RNN]
    zuv_ref[0] = z[:, 2 * D_RNN:2 * D_RNN + 2 * D_TMLP]
    zf = z[:, 2 * D_RNN + 2 * D_TMLP:D_IN].astype(BF16)
    va_ref[...] = jnp.dot(zf, a_ref[...], preferred_element_type=F32).astype(BF16)
    vb_ref[...] = jnp.dot(zf, b_ref[...], preferred_element_type=F32).astype(BF16)


def _in_proj(x, pos, mod, g, w_in_b, fa, fb, *, row_base, row_step):
    nseq, L, _ = x.shape
    tm = ROW_TILE
    add_pos = pos is not None
    in_specs = [pl.BlockSpec((1, tm, D_MODEL), lambda b, t: (b, t, 0))]
    args = [x]
    if add_pos:
        in_specs.append(pl.BlockSpec((tm, D_MODEL), lambda b, t: (t, 0)))
        args.append(pos)
    in_specs += [
        pl.BlockSpec((SUBLANES, N_MOD * D_MODEL), lambda b, t: (0, 0)),
        pl.BlockSpec((1, D_MODEL), lambda b, t: (0, 0)),
        pl.BlockSpec((D_MODEL, D_IN), lambda b, t: (0, 0)),
        pl.BlockSpec((D_FNET, D_FNET), lambda b, t: (0, 0)),
        pl.BlockSpec((D_FNET, D_FNET), lambda b, t: (0, 0)),
    ]
    args += [mod, g, w_in_b, fa, fb]
    return pl.pallas_call(
        functools.partial(_in_kernel, add_pos=add_pos, row_base=row_base, row_step=row_step),
        grid=(nseq, L // tm),
        in_specs=in_specs,
        out_specs=[
            pl.BlockSpec((1, tm, 2 * D_RNN), lambda b, t: (b, t, 0)),
            pl.BlockSpec((1, tm, 2 * D_TMLP), lambda b, t: (b, t, 0)),
            pl.BlockSpec((tm, D_FNET), lambda b, t: (t, b)),
            pl.BlockSpec((tm, D_FNET), lambda b, t: (t, b)),
        ],
        out_shape=[
            jax.ShapeDtypeStruct((nseq, L, 2 * D_RNN), F32),
            jax.ShapeDtypeStruct((nseq, L, 2 * D_TMLP), F32),
            jax.ShapeDtypeStruct((L, nseq * D_FNET), BF16),
            jax.ShapeDtypeStruct((L, nseq * D_FNET), BF16),
        ],
        compiler_params=_cparams(("arbitrary", "arbitrary")),
        name=f"in_proj_{L}",
    )(*args)


def _gelu_tanh(x):
    return 0.5 * x * (1.0 + jnp.tanh(math.sqrt(2.0 / math.pi) * (x + 0.044715 * (x * x * x))))


def _rows_to_tile(rows):
    sub = lax.broadcasted_iota(jnp.int32, (SUBLANES, LANES), 0)
    out = jnp.zeros((SUBLANES, LANES), F32)
    for s, r in enumerate(rows):
        out = jnp.where(sub == s, jnp.broadcast_to(r, (SUBLANES, LANES)), out)
    return out


def _lru_kernel(xr_ref, gr_ref, cw_ref, cb_ref, wa_ref, wx_ref, ba_ref, bx_ref, lam_ref, h0_ref,
                y_ref, st_ref, xnat, pext, af, bf, ab, bb, hnat, *, L):
    S = L // SUBLANES
    pitch = S + SUBLANES
    n = S * SUBLANES
    chunk = 256

    for s in range(SUBLANES):
        xnat[s * pitch:s * pitch + S, :] = xr_ref[0, s * S:(s + 1) * S, :]

    def perm_in(j, c):
        dst = pl.multiple_of((j + 2) * SUBLANES, SUBLANES)
        pext[pl.ds(dst, SUBLANES), :] = xnat[pl.ds(j, SUBLANES, stride=pitch), :]
        return c

    lax.fori_loop(0, S, perm_in, 0, unroll=8)

    sub = lax.broadcasted_iota(jnp.int32, (SUBLANES, LANES), 0)

    def from_prev_segment(v):
        return jnp.where(sub == 0, 0.0, pltpu.roll(v, 1, axis=0))

    def from_next_segment(v):
        return jnp.where(sub == SUBLANES - 1, 0.0, pltpu.roll(v, SUBLANES - 1, axis=0))

    pext[0:8, :] = from_prev_segment(pext[S * 8:(S + 1) * 8, :])
    pext[8:16, :] = from_prev_segment(pext[(S + 1) * 8:(S + 2) * 8, :])
    pext[(S + 2) * 8:(S + 3) * 8, :] = from_next_segment(pext[16:24, :])

    lam = lam_ref[...]
    nl = -lam
    sp = jnp.maximum(nl, 0.0) + jnp.log1p(jnp.exp(-jnp.abs(nl)))
    c_la = (-0.5 * LRU_C) * sp
    ba_h = 0.5 * ba_ref[...]
    bx_h = 0.5 * bx_ref[...]
    a_refs = (af, ab)
    b_refs = (bf, bb)

    def gates(i, c):
        base = pl.multiple_of(i * chunk, chunk)
        xc = (cw_ref[0:1, :] * pext[pl.ds(base, chunk), :]
              + cw_ref[1:2, :] * pext[pl.ds(base + 8, chunk), :]
              + cw_ref[2:3, :] * pext[pl.ds(base + 16, chunk), :]
              + cw_ref[3:4, :] * pext[pl.ds(base + 24, chunk), :]
              + cb_ref[...])
        xcb = xc.astype(BF16)
        xh = 0.5 * xc
        for d in range(2):
            tr = jnp.tanh(jnp.dot(xcb, wa_ref[d, 0], preferred_element_type=F32) + ba_h[d:d + 1, :])
            ti = jnp.tanh(jnp.dot(xcb, wx_ref[d, 0], preferred_element_type=F32) + bx_h[d:d + 1, :])
            log_a = c_la[d:d + 1, :] * (1.0 + tr)
            a = jnp.exp(log_a)
            coef = jnp.sqrt(jnp.tanh(log_a) * (-1.0 - a * a))
            a_refs[d][pl.ds(base, chunk), :] = a
            b_refs[d][pl.ds(base, chunk), :] = coef * ((1.0 + ti) * xh)
        return c

    lax.fori_loop(0, n // chunk, gates, 0)

    def scan(j, carry):
        hf, pf, hb, pb = carry
        jf = pl.multiple_of(j * SUBLANES, SUBLANES)
        jb = pl.multiple_of((S - 1 - j) * SUBLANES, SUBLANES)
        a1 = af[pl.ds(jf, SUBLANES), :]
        hf = a1 * hf + bf[pl.ds(jf, SUBLANES), :]
        pf = a1 * pf
        bf[pl.ds(jf, SUBLANES), :] = hf
        af[pl.ds(jf, SUBLANES), :] = pf
        a2 = ab[pl.ds(jb, SUBLANES), :]
        hb = a2 * hb + bb[pl.ds(jb, SUBLANES), :]
        pb = a2 * pb
        bb[pl.ds(jb, SUBLANES), :] = hb
        ab[pl.ds(jb, SUBLANES), :] = pb
        return hf, pf, hb, pb

    zero = jnp.zeros((SUBLANES, LANES), F32)
    one = jnp.ones((SUBLANES, LANES), F32)
    hf, pf, hb, pb = lax.fori_loop(0, S, scan, (zero, one, zero, one), unroll=8)

    rows_f = [h0_ref[0, 0:1, :]]
    for s in range(1, SUBLANES):
        rows_f.append(pf[s - 1:s, :] * rows_f[-1] + hf[s - 1:s, :])
    st_ref[0, 0:1, :] = pf[7:8, :] * rows_f[7] + hf[7:8, :]
    rows_b = [None] * SUBLANES
    rows_b[7] = h0_ref[0, 1:2, :]
    for s in range(SUBLANES - 2, -1, -1):
        rows_b[s] = pb[s + 1:s + 2, :] * rows_b[s + 1] + hb[s + 1:s + 2, :]
    st_ref[0, 1:2, :] = pb[0:1, :] * rows_b[0] + hb[0:1, :]
    init_f = _rows_to_tile(rows_f)
    init_b = _rows_to_tile(rows_b)

    def perm_out(j, c):
        src = pl.multiple_of(j * SUBLANES, SUBLANES)
        v = (bf[pl.ds(src, SUBLANES), :] + af[pl.ds(src, SUBLANES), :] * init_f
             + bb[pl.ds(src, SUBLANES), :] + ab[pl.ds(src, SUBLANES), :] * init_b)
        hnat[pl.ds(j, SUBLANES, stride=pitch), :] = v
        return c

    lax.fori_loop(0, S, perm_out, 0, unroll=8)

    for s in range(SUBLANES):
        g = gr_ref[0, s * S:(s + 1) * S, :]
        y_ref[0, s * S:(s + 1) * S, :] = hnat[s * pitch:s * pitch + S, :] * _gelu_tanh(g)


def _lru_mixer(zr, conv_w, conv_b, wa_bd, wx_bd, ba, bx, lam, h0):
    nseq, L, _ = zr.shape
    ncb = D_RNN // LRU_CB
    S = L // SUBLANES
    pitch = S + SUBLANES
    vec2 = pl.BlockSpec((2, LRU_CB), lambda b, c: (0, c))
    wspec = pl.BlockSpec((2, 1, LRU_CB, LRU_CB), lambda b, c: (0, c, 0, 0))
    return pl.pallas_call(
        functools.partial(_lru_kernel, L=L),
        grid=(nseq, ncb),
        in_specs=[
            pl.BlockSpec((1, L, LRU_CB), lambda b, c: (b, 0, c)),
            pl.BlockSpec((1, L, LRU_CB), lambda b, c: (b, 0, c + ncb)),
            pl.BlockSpec((CONV_W, LRU_CB), lambda b, c: (0, c)),
            pl.BlockSpec((1, LRU_CB), lambda b, c: (0, c)),
            wspec, wspec, vec2, vec2, vec2,
            pl.BlockSpec((1, 2, LRU_CB), lambda b, c: (b, 0, c)),
        ],
        out_specs=[
            pl.BlockSpec((1, L, LRU_CB), lambda b, c: (b, 0, c)),
            pl.BlockSpec((1, 2, LRU_CB), lambda b, c: (b, 0, c)),
        ],
        out_shape=[
            jax.ShapeDtypeStruct((nseq, L, D_RNN), F32),
            jax.ShapeDtypeStruct((nseq, 2, D_RNN), F32),
        ],
        scratch_shapes=[
            pltpu.VMEM((SUBLANES * pitch, LANES), F32),
            pltpu.VMEM(((S + 3) * SUBLANES, LANES), F32),
            pltpu.VMEM((L, LANES), F32),
            pltpu.VMEM((L, LANES), F32),
            pltpu.VMEM((L, LANES), F32),
            pltpu.VMEM((L, LANES), F32),
            pltpu.VMEM((SUBLANES * pitch, LANES), F32),
        ],
        compiler_params=_cparams(("arbitrary", "arbitrary")),
        name=f"lru_mixer_{L}",
    )(zr, zr, conv_w, conv_b, wa_bd, wx_bd, ba, bx, lam, h0)


def _out_kernel(*refs, add_pos, row_base, row_step):
    if add_pos:
        (x_ref, pos_ref, yr_ref, zuv_ref, yf_ref, mod_ref, ws_ref, bt_ref, wo_ref, g_ref, wr_ref, br_ref,
         x1_ref, h2_ref, ri_ref) = refs
    else:
        (x_ref, yr_ref, zuv_ref, yf_ref, mod_ref, ws_ref, bt_ref, wo_ref, g_ref, wr_ref, br_ref,
         x1_ref, h2_ref, ri_ref) = refs
    tm = x_ref.shape[1]
    x = x_ref[0]
    if add_pos:
        x = x + pos_ref[...]
    row = row_base + pl.program_id(0) * row_step
    m = mod_ref[pl.ds(row, 1), :]
    g1 = m[:, 2 * D_MODEL:3 * D_MODEL]
    sh2 = m[:, 3 * D_MODEL:4 * D_MODEL]
    sc2 = m[:, 4 * D_MODEL:5 * D_MODEL]

    head = lax.broadcasted_iota(jnp.int32, (CHUNK, D_TMLP), 1) >> 6
    yt_parts = []
    for ci in range(tm // CHUNK):
        u = zuv_ref[0, ci * CHUNK:(ci + 1) * CHUNK, 0:D_TMLP]
        v = zuv_ref[0, ci * CHUNK:(ci + 1) * CHUNK, D_TMLP:2 * D_TMLP].astype(BF16)
        s = jnp.zeros((CHUNK, D_TMLP), F32)
        for h in range(TMLP_HEADS):
            sh = jnp.dot(ws_ref[h], v, preferred_element_type=F32) + bt_ref[:, h:h + 1]
            s = jnp.where(head == h, sh, s)
        yt_parts.append(u * s)
    yt = jnp.concatenate(yt_parts, axis=0) if len(yt_parts) > 1 else yt_parts[0]

    y = (jnp.dot(yr_ref[0].astype(BF16), wo_ref[0:D_RNN, :], preferred_element_type=F32)
         + jnp.dot(yt.astype(BF16), wo_ref[D_RNN:D_RNN + D_TMLP, :], preferred_element_type=F32)
         + jnp.dot(yf_ref[...].astype(BF16), wo_ref[D_RNN + D_TMLP:D_MODEL, :], preferred_element_type=F32))
    x1 = x + g1 * y
    x1_ref[0] = x1
    h2 = _rms_mod(x1, g_ref[...], sc2, sh2)
    h2_ref[0] = h2

    wr = wr_ref[...]
    w_hi = wr.astype(BF16)
    w_lo = (wr - w_hi.astype(F32)).astype(BF16)
    h_hi = h2.astype(BF16)
    h_lo = (h2 - h_hi.astype(F32)).astype(BF16)
    p_hi = jnp.dot(h_hi, jnp.concatenate([w_hi, w_lo], axis=-1), preferred_element_type=F32)
    p_lo = jnp.dot(h_lo, w_hi, preferred_element_type=F32)
    logits = p_hi[:, 0:ROUTER_LANES] + p_hi[:, ROUTER_LANES:2 * ROUTER_LANES] + p_lo + br_ref[...]
    lane = lax.broadcasted_iota(jnp.int32, (tm, ROUTER_LANES), 1)
    lane_f = lane.astype(F32)
    is_g = lane < N_EGROUPS
    gl = jnp.where(is_g, logits, NEG_BIG)
    gmax = jnp.max(gl, axis=-1, keepdims=True)
    gsel = jnp.min(jnp.where(gl == gmax, lane_f, 1e4), axis=-1, keepdims=True)
    pg = 1.0 / jnp.sum(jnp.where(is_g, jnp.exp(logits - gmax), 0.0), axis=-1, keepdims=True)
    grp_f = ((lane - N_EGROUPS) >> 3).astype(F32)
    emask = (lane >= N_EGROUPS) & (lane < N_EGROUPS + N_EXPERTS) & (grp_f == gsel)
    el = jnp.where(emask, logits, NEG_BIG)
    v1 = jnp.max(el, axis=-1, keepdims=True)
    i1 = jnp.min(jnp.where(el == v1, lane_f, 1e4), axis=-1, keepdims=True)
    el2 = jnp.where(lane_f == i1, NEG_BIG, el)
    v2 = jnp.max(el2, axis=-1, keepdims=True)
    i2 = jnp.min(jnp.where(el2 == v2, lane_f, 1e4), axis=-1, keepdims=True)
    e2x = jnp.exp(v2 - v1)
    fw1 = 1.0 / (1.0 + e2x)
    fw2 = e2x * fw1
    ri = jnp.where(lane == 0, i1 - N_EGROUPS,
                   jnp.where(lane == 1, i2 - N_EGROUPS,
                             jnp.where(lane == 2, pg * fw1,
                                       jnp.where(lane == 3, pg * fw2, 0.0))))
    ri_ref[0] = ri


def _out_proj(x, pos, yr, zuv, yf, mod, ws_b, bt, wo_b, g, wr, br, *, row_base, row_step):
    nseq, L, _ = x.shape
    tm = ROW_TILE
    add_pos = pos is not None
    in_specs = [pl.BlockSpec((1, tm, D_MODEL), lambda b, t: (b, t, 0))]
    args = [x]
    if add_pos:
        in_specs.append(pl.BlockSpec((tm, D_MODEL), lambda b, t: (t, 0)))
        args.append(pos)
    in_specs += [
        pl.BlockSpec((1, tm, D_RNN), lambda b, t: (b, t, 0)),
        pl.BlockSpec((1, tm, 2 * D_TMLP), lambda b, t: (b, t, 0)),
        pl.BlockSpec((tm, D_FNET), lambda b, t: (t, b)),
        pl.BlockSpec((SUBLANES, N_MOD * D_MODEL), lambda b, t: (0, 0)),
        pl.BlockSpec((TMLP_HEADS, CHUNK, CHUNK), lambda b, t: (0, 0, 0)),
        pl.BlockSpec((CHUNK, TMLP_HEADS), lambda b, t: (0, 0)),
        pl.BlockSpec((D_MODEL, D_MODEL), lambda b, t: (0, 0)),
        pl.BlockSpec((1, D_MODEL), lambda b, t: (0, 0)),
        pl.BlockSpec((D_MODEL, ROUTER_LANES), lambda b, t: (0, 0)),
        pl.BlockSpec((1, ROUTER_LANES), lambda b, t: (0, 0)),
    ]
    args += [yr, zuv, yf, mod, ws_b, bt, wo_b, g, wr, br]
    tok = pl.BlockSpec((1, tm, D_MODEL), lambda b, t: (b, t, 0))
    return pl.pallas_call(
        functools.partial(_out_kernel, add_pos=add_pos, row_base=row_base, row_step=row_step),
        grid=(nseq, L // tm),
        in_specs=in_specs,
        out_specs=[tok, tok, pl.BlockSpec((1, tm, ROUTER_LANES), lambda b, t: (b, t, 0))],
        out_shape=[
            jax.ShapeDtypeStruct((nseq, L, D_MODEL), F32),
            jax.ShapeDtypeStruct((nseq, L, D_MODEL), F32),
            jax.ShapeDtypeStruct((nseq, L, ROUTER_LANES), F32),
        ],
        compiler_params=_cparams(("arbitrary", "arbitrary")),
        name=f"out_proj_{L}",
    )(*args)


def _row_copy(src_ref, src_row, dst_ref, dst_row, sem):
    return pltpu.make_async_copy(src_ref.at[pl.ds(src_row, 1)], dst_ref.at[pl.ds(dst_row, 1)], sem)


def _scatter_rows(pos_ref, h_ref, xs_ref, sem):
    tm = h_ref.shape[0]

    def start(r, c):
        _row_copy(h_ref, r, xs_ref, pos_ref[0, 0, r], sem).start()
        _row_copy(h_ref, r, xs_ref, pos_ref[0, 1, r], sem).start()
        return c

    lax.fori_loop(0, tm, start, 0, unroll=8)
    for _ in range(2):
        pltpu.make_async_copy(h_ref, xs_ref.at[pl.ds(0, tm)], sem).wait()


def _dispatch_kernel(pos_ref, hp_ref, hs_ref, xs_ref, sem, *, n_first):
    i = pl.program_id(0)

    @pl.when(i < n_first)
    def _():
        _scatter_rows(pos_ref, hp_ref, xs_ref, sem)

    @pl.when(i >= n_first)
    def _():
        _scatter_rows(pos_ref, hs_ref, xs_ref, sem)


def _dispatch(pos, h_first, h_second):
    tm = ROW_TILE
    n_first = h_first.shape[0] // tm
    n_second = h_second.shape[0] // tm
    nrows = 2 * (h_first.shape[0] + h_second.shape[0])
    return pl.pallas_call(
        functools.partial(_dispatch_kernel, n_first=n_first),
        grid=(n_first + n_second,),
        in_specs=[
            pl.BlockSpec((1, 2, tm), lambda i: (i, 0, 0), memory_space=pltpu.SMEM),
            pl.BlockSpec((tm, D_MODEL), lambda i: (jnp.minimum(i, n_first - 1), 0)),
            pl.BlockSpec((tm, D_MODEL), lambda i: (jnp.maximum(i - n_first, 0), 0)),
        ],
        out_specs=pl.BlockSpec(memory_space=pl.ANY),
        out_shape=jax.ShapeDtypeStruct((nrows, D_MODEL), F32),
        scratch_shapes=[pltpu.SemaphoreType.DMA(())],
        compiler_params=_cparams(("arbitrary",)),
        name="moe_dispatch",
    )(pos, h_first, h_second)


def _expert_kernel(wt_ref, we_ref, lo_ref, hi_ref, first_ref, xs_ref, w1_ref, w3_ref, w2_ref, ys_ref,
                   w1b, w3b, w2b, prev):
    w = pl.program_id(0)

    @pl.when(w == 0)
    def _():
        prev[0] = -1

    e = we_ref[w]

    @pl.when(e != prev[0])
    def _():
        w1b[...] = w1_ref[0].astype(BF16)
        w3b[...] = w3_ref[0].astype(BF16)
        w2b[...] = w2_ref[0].astype(BF16)
        prev[0] = e

    @pl.when(first_ref[w] == 1)
    def _():
        ys_ref[...] = jnp.zeros_like(ys_ref)

    lo = lo_ref[w]
    hi = hi_ref[w]

    @pl.when(hi > lo)
    def _():
        x = xs_ref[...].astype(BF16)
        a = jnp.dot(x, w1b[...], preferred_element_type=F32)
        b = jnp.dot(x, w3b[...], preferred_element_type=F32)
        hid = (a * jax.nn.sigmoid(a)) * b
        res = jnp.dot(hid.astype(BF16), w2b[...], preferred_element_type=F32)
        rows = lax.broadcasted_iota(jnp.int32, res.shape, 0)
        ys_ref[...] = jnp.where((rows >= lo) & (rows < hi), res, ys_ref[...])


def _experts(items, xs, w1, w3, w2):
    nw = items[0].shape[0]
    row = lambda w, wt, we, lo, hi, fi: (wt[w], 0)
    wsel = lambda w, wt, we, lo, hi, fi: (we[w], 0, 0)
    return pl.pallas_call(
        _expert_kernel,
        grid_spec=pltpu.PrefetchScalarGridSpec(
            num_scalar_prefetch=5,
            grid=(nw,),
            in_specs=[
                pl.BlockSpec((EXPERT_TILE, D_MODEL), row),
                pl.BlockSpec((1, D_MODEL, D_EXPERT), wsel),
                pl.BlockSpec((1, D_MODEL, D_EXPERT), wsel),
                pl.BlockSpec((1, D_EXPERT, D_MODEL), wsel),
            ],
            out_specs=pl.BlockSpec((EXPERT_TILE, D_MODEL), row),
            scratch_shapes=[
                pltpu.VMEM((D_MODEL, D_EXPERT), BF16),
                pltpu.VMEM((D_MODEL, D_EXPERT), BF16),
                pltpu.VMEM((D_EXPERT, D_MODEL), BF16),
                pltpu.SMEM((1,), jnp.int32),
            ],
        ),
        out_shape=jax.ShapeDtypeStruct(xs.shape, F32),
        compiler_params=_cparams(("arbitrary",)),
        name="moe_experts",
    )(*items, xs, w1, w3, w2)


def _combine_kernel(pos_ref, x1_ref, ri_ref, mod_ref, gf_ref, ys_ref, o_ref, y1buf, y2buf, sem,
                    *, row_base, row_step, tiles_per_seq, final):
    tm = x1_ref.shape[0]

    def start(r, c):
        _row_copy(ys_ref, pos_ref[0, 0, r], y1buf, r, sem).start()
        _row_copy(ys_ref, pos_ref[0, 1, r], y2buf, r, sem).start()
        return c

    lax.fori_loop(0, tm, start, 0, unroll=8)
    pltpu.make_async_copy(ys_ref.at[pl.ds(0, tm)], y1buf, sem).wait()
    pltpu.make_async_copy(ys_ref.at[pl.ds(0, tm)], y2buf, sem).wait()
    row = row_base + (pl.program_id(0) // tiles_per_seq) * row_step
    g2 = mod_ref[pl.ds(row, 1), 5 * D_MODEL:6 * D_MODEL]
    ri = ri_ref[...]
    y = ri[:, 2:3] * y1buf[...] + ri[:, 3:4] * y2buf[...]
    x2 = x1_ref[...] + g2 * y
    if final:
        ms = jnp.mean(x2 * x2, axis=-1, keepdims=True)
        o_ref[...] = (x2 * lax.rsqrt(ms + EPS)) * gf_ref[...]
    else:
        o_ref[...] = x2


def _combine(pos, x1, ri, mod, g_final, ys, *, row_base, row_step, tiles_per_seq, final):
    ntok = x1.shape[0]
    tm = ROW_TILE
    tok = pl.BlockSpec((tm, D_MODEL), lambda i: (i, 0))
    return pl.pallas_call(
        functools.partial(_combine_kernel, row_base=row_base, row_step=row_step, tiles_per_seq=tiles_per_seq,
                          final=final),
        grid=(ntok // tm,),
        in_specs=[
            pl.BlockSpec((1, 2, tm), lambda i: (i, 0, 0), memory_space=pltpu.SMEM),
            tok,
            pl.BlockSpec((tm, ROUTER_LANES), lambda i: (i, 0)),
            pl.BlockSpec((SUBLANES, N_MOD * D_MODEL), lambda i: (0, 0)),
            pl.BlockSpec((1, D_MODEL), lambda i: (0, 0)),
            pl.BlockSpec(memory_space=pl.ANY),
        ],
        out_specs=tok,
        out_shape=jax.ShapeDtypeStruct((ntok, D_MODEL), F32),
        scratch_shapes=[
            pltpu.VMEM((tm, D_MODEL), F32),
            pltpu.VMEM((tm, D_MODEL), F32),
            pltpu.SemaphoreType.DMA(()),
        ],
        compiler_params=_cparams(("arbitrary",)),
        name=f"moe_combine_{ntok}",
    )(pos, x1, ri, mod, g_final, ys)


def _route(ri_all):
    e = ri_all[:, 0:2].astype(jnp.int32)
    oh = (e[:, :, None] == jnp.arange(N_EXPERTS, dtype=jnp.int32)).astype(jnp.int32)
    per_tok = oh[:, 0] + oh[:, 1]
    before = jnp.cumsum(per_tok, axis=0) - per_tok
    counts = jnp.sum(per_tok, axis=0)
    offs = jnp.cumsum(counts) - counts
    rank = jnp.sum(oh * before[:, None, :], axis=-1)
    pos = jnp.sum(oh * offs[None, None, :], axis=-1) + rank
    return pos.astype(jnp.int32), counts, offs


def _work_items(counts, offs, nw, layer):
    tm = EXPERT_TILE
    first_tile = offs // tm
    last_tile = (offs + counts - 1) // tm
    n_e = jnp.where(counts > 0, last_tile - first_tile + 1, 0)
    w_end = jnp.cumsum(n_e)
    w_start = w_end - n_e
    total = w_end[-1]
    w = jnp.arange(nw, dtype=jnp.int32)
    wc = jnp.minimum(w, total - 1)
    e_w = jnp.sum((wc[:, None] >= w_end[None, :]).astype(jnp.int32), axis=-1)
    tile_w = first_tile[e_w] + (wc - w_start[e_w])
    lo = jnp.clip(offs[e_w] - tile_w * tm, 0, tm)
    hi = jnp.clip(offs[e_w] + counts[e_w] - tile_w * tm, 0, tm)
    valid = w < total
    lo = jnp.where(valid, lo, 0)
    hi = jnp.where(valid, hi, 0)
    first = (w == 0) | (tile_w != jnp.roll(tile_w, 1))
    i32 = lambda a: a.astype(jnp.int32)
    return (i32(tile_w), i32(e_w + layer * N_EXPERTS), i32(lo), i32(hi), i32(first))


def _grid_pos_embed(rows, d):
    t = jnp.arange(rows * GRID_W)
    row = (t // GRID_W).astype(F32)
    col = (t % GRID_W).astype(F32)
    nf = d // 4
    freq = 1.0 / (10000.0 ** (jnp.arange(nf, dtype=F32) / nf))
    er = row[:, None] * freq
    ec = col[:, None] * freq
    return jnp.concatenate([jnp.sin(er), jnp.cos(er), jnp.sin(ec), jnp.cos(ec)], axis=-1)


def _block_diag(w, nblk):
    *lead, H, d, _ = w.shape
    w = w.reshape(*lead, H // nblk, nblk, d, d)
    eye = jnp.eye(nblk, dtype=w.dtype)
    out = jnp.einsum('...gij,gh->...gihj', w, eye)
    return out.reshape(*lead, H // nblk, nblk * d, nblk * d)


def kernel(x_prompt, x_sample, state_lru, c, c_ctx, w_ada, b_ada, g_mix, g_ffn, g_final, w_in, w_out, conv_w, conv_b, lru_wa, lru_ba, lru_wx, lru_bx, lru_lambda, tmlp_ws, tmlp_b, fnet_w, router_g, router_g_b, router_e, router_e_b, e_w1, e_w3, e_w2):
    bp, lp, _ = x_prompt.shape
    bs, ls, _ = x_sample.shape
    n_ctx = bp * lp
    n_lat = bs * ls
    n_tok = n_ctx + n_lat

    cond8 = jnp.concatenate([c_ctx[None, :], c, jnp.zeros((SUBLANES - 1 - bs, D_MODEL), F32)], axis=0)
    pos = _grid_pos_embed(ls // GRID_W, D_MODEL)
    w_in_b = w_in.astype(BF16)
    w_out_b = w_out.astype(BF16)
    heads_per_cb = LRU_CB // RNN_HEAD_DIM
    wa_bd = (0.5 * _block_diag(lru_wa, heads_per_cb)).astype(BF16)
    wx_bd = (0.5 * _block_diag(lru_wx, heads_per_cb)).astype(BF16)
    ws_b = tmlp_ws.astype(BF16)
    bt = jnp.swapaxes(tmlp_b, 1, 2)
    fnet_bd = _block_diag(fnet_w, D_FNET // FNET_GROUP_DIM)[:, 0]
    wr = jnp.concatenate([router_g, router_e,
                          jnp.zeros((DEPTH, D_MODEL, ROUTER_LANES - N_EGROUPS - N_EXPERTS), F32)], axis=-1)
    br = jnp.concatenate([router_g_b, router_e_b,
                          jnp.zeros((DEPTH, ROUTER_LANES - N_EGROUPS - N_EXPERTS), F32)], axis=-1)
    ew1 = e_w1.reshape(DEPTH * N_EXPERTS, D_MODEL, D_EXPERT)
    ew3 = e_w3.reshape(DEPTH * N_EXPERTS, D_MODEL, D_EXPERT)
    ew2 = e_w2.reshape(DEPTH * N_EXPERTS, D_EXPERT, D_MODEL)

    mods = _modulation(cond8, w_ada, b_ada)
    fa, fb = _fnet_prep(fnet_bd)
    ct_p, st_p = _dft_tables(lp)
    ct_s, st_s = _dft_tables(ls)

    nw = 2 * n_tok // EXPERT_TILE + N_EXPERTS
    zeros_p = jnp.zeros((bp, D_RNN), F32)

    xp, xs = x_prompt, x_sample
    states = []
    for l in range(DEPTH):
        mod = mods[l]
        g_mix_l = g_mix[l][None, :]
        g_ffn_l = g_ffn[l][None, :]
        paths = []
        for (x, ct, st, h0, row_base, row_step, is_lat) in (
                (xp, ct_p, st_p, jnp.stack([zeros_p, zeros_p], axis=1), 0, 0, False),
                (xs, ct_s, st_s, state_lru[:, l], 1, 1, True)):
            pe = pos if (is_lat and l == 0) else None
            zr, zuv, va, vb = _in_proj(x, pe, mod, g_mix_l, w_in_b[l], fa[l], fb[l],
                                       row_base=row_base, row_step=row_step)
            yr, st_new = _lru_mixer(zr, conv_w[l], conv_b[l][None, :], wa_bd[l], wx_bd[l],
                                    lru_ba[l], lru_bx[l], lru_lambda[l], h0)
            yf = _dft_apply(ct, st, va, vb)
            x1, h2, ri = _out_proj(x, pe, yr, zuv, yf, mod, ws_b[l], bt[l], w_out_b[l], g_ffn_l,
                                   wr[l], br[l][None, :], row_base=row_base, row_step=row_step)
            paths.append((x1, h2, ri, st_new))
        states.append(paths[0][3])

        ri_all = jnp.concatenate([paths[0][2].reshape(n_ctx, ROUTER_LANES),
                                  paths[1][2].reshape(n_lat, ROUTER_LANES)], axis=0)
        pos_all, counts, offs = _route(ri_all)
        items = _work_items(counts, offs, nw, l)
        pos_tiles = jnp.swapaxes(pos_all.reshape(n_tok // ROW_TILE, ROW_TILE, 2), 1, 2)
        nct = n_ctx // ROW_TILE
        xsorted = _dispatch(pos_tiles, paths[0][1].reshape(n_ctx, D_MODEL), paths[1][1].reshape(n_lat, D_MODEL))
        ysorted = _experts(items, xsorted, ew1, ew3, ew2)
        gfin = g_final[None, :]
        final = l == DEPTH - 1
        xp = _combine(pos_tiles[:nct], paths[0][0].reshape(n_ctx, D_MODEL),
                      paths[0][2].reshape(n_ctx, ROUTER_LANES), mod, gfin, ysorted,
                      row_base=0, row_step=0, tiles_per_seq=lp // ROW_TILE, final=final).reshape(bp, lp, D_MODEL)
        xs = _combine(pos_tiles[nct:], paths[1][0].reshape(n_lat, D_MODEL),
                      paths[1][2].reshape(n_lat, ROUTER_LANES), mod, gfin, ysorted,
                      row_base=1, row_step=1, tiles_per_seq=ls // ROW_TILE, final=final).reshape(bs, ls, D_MODEL)

    new_state = jnp.stack(states, axis=1).astype(x_prompt.dtype)
    return (xp, xs, new_state)
```

```python
import functools
import math

import jax
import jax.numpy as jnp
from jax import lax
from jax.experimental import pallas as pl
from jax.experimental.pallas import tpu as pltpu

F32 = jnp.float32
BF16 = jnp.bfloat16

D_MODEL = 1024
DEPTH = 2
GRID_W = 64
D_RNN = 512
RNN_HEAD_DIM = 64
CONV_W = 4
LRU_C = 8.0
D_TMLP = 256
TMLP_HEADS = 4
CHUNK = 128
D_FNET = 256
FNET_GROUP_DIM = 64
D_IN = 2 * D_RNN + 2 * D_TMLP + D_FNET
N_EGROUPS = 4
N_EPG = 8
N_EXPERTS = N_EGROUPS * N_EPG
D_EXPERT = 512
N_MOD = 6
EPS = 1e-6

LANES = 128
SUBLANES = 8
LRU_CB = LANES
ROW_TILE = 256
EXPERT_TILE = 256
ROUTER_LANES = LANES
NEG_BIG = -1e30
VMEM_LIMIT = 56 * 1024 * 1024


def _cparams(sem):
    return pltpu.CompilerParams(dimension_semantics=sem, vmem_limit_bytes=VMEM_LIMIT)


def _mod_kernel(c_ref, w_ref, b_ref, o_ref):
    c = c_ref[...]
    s = c * jax.nn.sigmoid(c)
    o_ref[0] = jnp.dot(s.astype(BF16), w_ref[0].astype(BF16), preferred_element_type=F32) + b_ref[0]


def _modulation(cond8, w_ada, b_ada):
    tn = 1536
    return pl.pallas_call(
        _mod_kernel,
        grid=(DEPTH, N_MOD * D_MODEL // tn),
        in_specs=[
            pl.BlockSpec((SUBLANES, D_MODEL), lambda l, j: (0, 0)),
            pl.BlockSpec((1, D_MODEL, tn), lambda l, j: (l, 0, j)),
            pl.BlockSpec((1, 1, tn), lambda l, j: (l, 0, j)),
        ],
        out_specs=pl.BlockSpec((1, SUBLANES, tn), lambda l, j: (l, 0, j)),
        out_shape=jax.ShapeDtypeStruct((DEPTH, SUBLANES, N_MOD * D_MODEL), F32),
        compiler_params=_cparams(("arbitrary", "arbitrary")),
        name="modulation",
    )(cond8, w_ada, b_ada.reshape(DEPTH, 1, N_MOD * D_MODEL))


def _fprep_kernel(w_ref, a_ref, b_ref):
    r = lax.broadcasted_iota(jnp.int32, (D_FNET, D_FNET), 0)
    c = lax.broadcasted_iota(jnp.int32, (D_FNET, D_FNET), 1)
    same = (r >> 6) == (c >> 6)
    ph = ((r & 63) * (c & 63)) & 63
    ang = ph.astype(F32) * (2.0 * math.pi / FNET_GROUP_DIM)
    scale = 1.0 / math.sqrt(FNET_GROUP_DIM)
    cm = jnp.where(same, jnp.cos(ang) * scale, 0.0)
    sm = jnp.where(same, jnp.sin(ang) * scale, 0.0)
    w = w_ref[0]
    a_ref[0] = jnp.dot(cm, w, precision=lax.Precision.HIGHEST, preferred_element_type=F32).astype(BF16)
    b_ref[0] = jnp.dot(sm, w, precision=lax.Precision.HIGHEST, preferred_element_type=F32).astype(BF16)


def _fnet_prep(wbd):
    spec = pl.BlockSpec((1, D_FNET, D_FNET), lambda l: (l, 0, 0))
    return pl.pallas_call(
        _fprep_kernel,
        grid=(DEPTH,),
        in_specs=[spec],
        out_specs=[spec, spec],
        out_shape=[jax.ShapeDtypeStruct((DEPTH, D_FNET, D_FNET), BF16)] * 2,
        compiler_params=_cparams(("arbitrary",)),
        name="fnet_prep",
    )(wbd)


TABLE_ROWS = 64


def _base_table_kernel(cj_ref, sj_ref, cm_ref, sm_ref, *, L):
    j = lax.broadcasted_iota(jnp.int32, (TABLE_ROWS, L), 0)
    n = lax.broadcasted_iota(jnp.int32, (TABLE_ROWS, L), 1)
    w = 2.0 * math.pi / L
    fine = ((j * n) & (L - 1)).astype(F32) * w
    coarse = ((j * TABLE_ROWS * n) & (L - 1)).astype(F32) * w
    scale = 1.0 / math.sqrt(L)
    cj_ref[...] = jnp.cos(fine)
    sj_ref[...] = jnp.sin(fine)
    cm_ref[...] = jnp.cos(coarse) * scale
    sm_ref[...] = jnp.sin(coarse) * scale


def _table_kernel(cj_ref, sj_ref, cm_ref, sm_ref, c_ref, s_ref):
    m = pl.program_id(0)
    c0 = cm_ref[pl.ds(m, 1), :]
    s0 = sm_ref[pl.ds(m, 1), :]
    cj = cj_ref[...]
    sj = sj_ref[...]
    c_ref[...] = (cj * c0 - sj * s0).astype(BF16)
    s_ref[...] = (-(sj * c0 + cj * s0)).astype(BF16)


def _dft_tables(L):
    small = pl.BlockSpec((TABLE_ROWS, L), lambda *_: (0, 0))
    base = pl.pallas_call(
        functools.partial(_base_table_kernel, L=L),
        out_specs=[small] * 4,
        out_shape=[jax.ShapeDtypeStruct((TABLE_ROWS, L), F32)] * 4,
        compiler_params=pltpu.CompilerParams(vmem_limit_bytes=VMEM_LIMIT),
        name=f"dft_base_tables_{L}",
    )()
    spec = pl.BlockSpec((TABLE_ROWS, L), lambda i: (i, 0))
    return pl.pallas_call(
        _table_kernel,
        grid=(L // TABLE_ROWS,),
        in_specs=[small] * 4,
        out_specs=[spec, spec],
        out_shape=[jax.ShapeDtypeStruct((L, L), BF16)] * 2,
        compiler_params=_cparams(("arbitrary",)),
        name=f"dft_tables_{L}",
    )(*base)


def _dft_kernel(c_ref, s_ref, va_ref, vb_ref, o_ref):
    o_ref[...] = (jnp.dot(c_ref[...], va_ref[...], preferred_element_type=F32)
                  + jnp.dot(s_ref[...], vb_ref[...], preferred_element_type=F32))


def _dft_apply(ct, st, va, vb):
    L, ncols = va.shape
    tk, tc = 256, 1024
    return pl.pallas_call(
        _dft_kernel,
        grid=(L // tk, ncols // tc),
        in_specs=[
            pl.BlockSpec((tk, L), lambda i, j: (i, 0)),
            pl.BlockSpec((tk, L), lambda i, j: (i, 0)),
            pl.BlockSpec((L, tc), lambda i, j: (0, j)),
            pl.BlockSpec((L, tc), lambda i, j: (0, j)),
        ],
        out_specs=pl.BlockSpec((tk, tc), lambda i, j: (i, j)),
        out_shape=jax.ShapeDtypeStruct((L, ncols), F32),
        compiler_params=_cparams(("arbitrary", "arbitrary")),
        name=f"dft_apply_{L}",
    )(ct, st, va, vb)


def _rms_mod(x, g, scale, shift):
    ms = jnp.mean(x * x, axis=-1, keepdims=True)
    return (x * lax.rsqrt(ms + EPS)) * g * (1.0 + scale) + shift


def _in_kernel(*refs, add_pos, row_base, row_step):
    if add_pos:
        x_ref, pos_ref, mod_ref, g_ref, w_ref, a_ref, b_ref, zr_ref, zuv_ref, va_ref, vb_ref = refs
    else:
        x_ref, mod_ref, g_ref, w_ref, a_ref, b_ref, zr_ref, zuv_ref, va_ref, vb_ref = refs
    x = x_ref[0]
    if add_pos:
        x = x + pos_ref[...]
    row = row_base + pl.program_id(0) * row_step
    m = mod_ref[pl.ds(row, 1), :]
    h = _rms_mod(x, g_ref[...], m[:, D_MODEL:2 * D_MODEL], m[:, 0:D_MODEL])
    z = jnp.dot(h.astype(BF16), w_ref[...], preferred_element_type=F32)
    zr_ref[0] = z[:, 0:2 * D_RNN]
    zuv_ref[0] = z[:, 2 * D_RNN:2 * D_RNN + 2 * D_TMLP]
    zf = z[:, 2 * D_RNN + 2 * D_TMLP:D_IN].astype(BF16)
    va_ref[...] = jnp.dot(zf, a_ref[...], preferred_element_type=F32).astype(BF16)
    vb_ref[...] = jnp.dot(zf, b_ref[...], preferred_element_type=F32).astype(BF16)


def _in_proj(x, pos, mod, g, w_in_b, fa, fb, *, row_base, row_step):
    nseq, L, _ = x.shape
    tm = ROW_TILE
    add_pos = pos is not None
    in_specs = [pl.BlockSpec((1, tm, D_MODEL), lambda b, t: (b, t, 0))]
    args = [x]
    if add_pos:
        in_specs.append(pl.BlockSpec((tm, D_MODEL), lambda b, t: (t, 0)))
        args.append(pos)
    in_specs += [
        pl.BlockSpec((SUBLANES, N_MOD * D_MODEL), lambda b, t: (0, 0)),
        pl.BlockSpec((1, D_MODEL), lambda b, t: (0, 0)),
        pl.BlockSpec((D_MODEL, D_IN), lambda b, t: (0, 0)),
        pl.BlockSpec((D_FNET, D_FNET), lambda b, t: (0, 0)),
        pl.BlockSpec((D_FNET, D_FNET), lambda b, t: (0, 0)),
    ]
    args += [mod, g, w_in_b, fa, fb]
    return pl.pallas_call(
        functools.partial(_in_kernel, add_pos=add_pos, row_base=row_base, row_step=row_step),
        grid=(nseq, L // tm),
        in_specs=in_specs,
        out_specs=[
            pl.BlockSpec((1, tm, 2 * D_RNN), lambda b, t: (b, t, 0)),
            pl.BlockSpec((1, tm, 2 * D_TMLP), lambda b, t: (b, t, 0)),
            pl.BlockSpec((tm, D_FNET), lambda b, t: (t, b)),
            pl.BlockSpec((tm, D_FNET), lambda b, t: (t, b)),
        ],
        out_shape=[
            jax.ShapeDtypeStruct((nseq, L, 2 * D_RNN), F32),
            jax.ShapeDtypeStruct((nseq, L, 2 * D_TMLP), F32),
            jax.ShapeDtypeStruct((L, nseq * D_FNET), BF16),
            jax.ShapeDtypeStruct((L, nseq * D_FNET), BF16),
        ],
        compiler_params=_cparams(("arbitrary", "arbitrary")),
        name=f"in_proj_{L}",
    )(*args)


def _gelu_tanh(x):
    return 0.5 * x * (1.0 + jnp.tanh(math.sqrt(2.0 / math.pi) * (x + 0.044715 * (x * x * x))))


def _rows_to_tile(rows):
    sub = lax.broadcasted_iota(jnp.int32, (SUBLANES, LANES), 0)
    out = jnp.zeros((SUBLANES, LANES), F32)
    for s, r in enumerate(rows):
        out = jnp.where(sub == s, jnp.broadcast_to(r, (SUBLANES, LANES)), out)
    return out


def _lru_kernel(xr_ref, gr_ref, cw_ref, cb_ref, wa_ref, wx_ref, ba_ref, bx_ref, lam_ref, h0_ref,
                y_ref, st_ref, xnat, pext, af, bf, ab, bb, hnat, *, L):
    S = L // SUBLANES
    pitch = S + SUBLANES
    n = S * SUBLANES
    chunk = 256

    for s in range(SUBLANES):
        xnat[s * pitch:s * pitch + S, :] = xr_ref[0, s * S:(s + 1) * S, :]

    def perm_in(j, c):
        dst = pl.multiple_of((j + 2) * SUBLANES, SUBLANES)
        pext[pl.ds(dst, SUBLANES), :] = xnat[pl.ds(j, SUBLANES, stride=pitch), :]
        return c

    lax.fori_loop(0, S, perm_in, 0, unroll=8)

    sub = lax.broadcasted_iota(jnp.int32, (SUBLANES, LANES), 0)

    def from_prev_segment(v):
        return jnp.where(sub == 0, 0.0, pltpu.roll(v, 1, axis=0))

    def from_next_segment(v):
        return jnp.where(sub == SUBLANES - 1, 0.0, pltpu.roll(v, SUBLANES - 1, axis=0))

    pext[0:8, :] = from_prev_segment(pext[S * 8:(S + 1) * 8, :])
    pext[8:16, :] = from_prev_segment(pext[(S + 1) * 8:(S + 2) * 8, :])
    pext[(S + 2) * 8:(S + 3) * 8, :] = from_next_segment(pext[16:24, :])

    lam = lam_ref[...]
    nl = -lam
    sp = jnp.maximum(nl, 0.0) + jnp.log1p(jnp.exp(-jnp.abs(nl)))
    c_la = (-0.5 * LRU_C) * sp
    ba_h = 0.5 * ba_ref[...]
    bx_h = 0.5 * bx_ref[...]
    a_refs = (af, ab)
    b_refs = (bf, bb)

    def gates(i, c):
        base = pl.multiple_of(i * chunk, chunk)
        xc = (cw_ref[0:1, :] * pext[pl.ds(base, chunk), :]
              + cw_ref[1:2, :] * pext[pl.ds(base + 8, chunk), :]
              + cw_ref[2:3, :] * pext[pl.ds(base + 16, chunk), :]
              + cw_ref[3:4, :] * pext[pl.ds(base + 24, chunk), :]
              + cb_ref[...])
        xcb = xc.astype(BF16)
        xh = 0.5 * xc
        for d in range(2):
            tr = jnp.tanh(jnp.dot(xcb, wa_ref[d, 0], preferred_element_type=F32) + ba_h[d:d + 1, :])
            ti = jnp.tanh(jnp.dot(xcb, wx_ref[d, 0], preferred_element_type=F32) + bx_h[d:d + 1, :])
            log_a = c_la[d:d + 1, :] * (1.0 + tr)
            a = jnp.exp(log_a)
            v = jnp.tanh(log_a) * (-1.0 - a * a)
            coef = jnp.where(v > 0.0, v * lax.rsqrt(v), 0.0)
            a_refs[d][pl.ds(base, chunk), :] = a
            b_refs[d][pl.ds(base, chunk), :] = coef * ((1.0 + ti) * xh)
        return c

    lax.fori_loop(0, n // chunk, gates, 0)

    def scan(j, carry):
        hf, pf, hb, pb = carry
        jf = pl.multiple_of(j * SUBLANES, SUBLANES)
        jb = pl.multiple_of((S - 1 - j) * SUBLANES, SUBLANES)
        a1 = af[pl.ds(jf, SUBLANES), :]
        hf = a1 * hf + bf[pl.ds(jf, SUBLANES), :]
        pf = a1 * pf
        bf[pl.ds(jf, SUBLANES), :] = hf
        af[pl.ds(jf, SUBLANES), :] = pf
        a2 = ab[pl.ds(jb, SUBLANES), :]
        hb = a2 * hb + bb[pl.ds(jb, SUBLANES), :]
        pb = a2 * pb
        bb[pl.ds(jb, SUBLANES), :] = hb
        ab[pl.ds(jb, SUBLANES), :] = pb
        return hf, pf, hb, pb

    zero = jnp.zeros((SUBLANES, LANES), F32)
    one = jnp.ones((SUBLANES, LANES), F32)
    hf, pf, hb, pb = lax.fori_loop(0, S, scan, (zero, one, zero, one), unroll=8)

    rows_f = [h0_ref[0, 0:1, :]]
    for s in range(1, SUBLANES):
        rows_f.append(pf[s - 1:s, :] * rows_f[-1] + hf[s - 1:s, :])
    st_ref[0, 0:1, :] = pf[7:8, :] * rows_f[7] + hf[7:8, :]
    rows_b = [None] * SUBLANES
    rows_b[7] = h0_ref[0, 1:2, :]
    for s in range(SUBLANES - 2, -1, -1):
        rows_b[s] = pb[s + 1:s + 2, :] * rows_b[s + 1] + hb[s + 1:s + 2, :]
    st_ref[0, 1:2, :] = pb[0:1, :] * rows_b[0] + hb[0:1, :]
    init_f = _rows_to_tile(rows_f)
    init_b = _rows_to_tile(rows_b)

    def perm_out(j, c):
        src = pl.multiple_of(j * SUBLANES, SUBLANES)
        v = (bf[pl.ds(src, SUBLANES), :] + af[pl.ds(src, SUBLANES), :] * init_f
             + bb[pl.ds(src, SUBLANES), :] + ab[pl.ds(src, SUBLANES), :] * init_b)
        hnat[pl.ds(j, SUBLANES, stride=pitch), :] = v
        return c

    lax.fori_loop(0, S, perm_out, 0, unroll=8)

    for s in range(SUBLANES):
        g = gr_ref[0, s * S:(s + 1) * S, :]
        y_ref[0, s * S:(s + 1) * S, :] = hnat[s * pitch:s * pitch + S, :] * _gelu_tanh(g)


def _lru_mixer(zr, conv_w, conv_b, wa_bd, wx_bd, ba, bx, lam, h0):
    nseq, L, _ = zr.shape
    ncb = D_RNN // LRU_CB
    S = L // SUBLANES
    pitch = S + SUBLANES
    vec2 = pl.BlockSpec((2, LRU_CB), lambda b, c: (0, c))
    wspec = pl.BlockSpec((2, 1, LRU_CB, LRU_CB), lambda b, c: (0, c, 0, 0))
    return pl.pallas_call(
        functools.partial(_lru_kernel, L=L),
        grid=(nseq, ncb),
        in_specs=[
            pl.BlockSpec((1, L, LRU_CB), lambda b, c: (b, 0, c)),
            pl.BlockSpec((1, L, LRU_CB), lambda b, c: (b, 0, c + ncb)),
            pl.BlockSpec((CONV_W, LRU_CB), lambda b, c: (0, c)),
            pl.BlockSpec((1, LRU_CB), lambda b, c: (0, c)),
            wspec, wspec, vec2, vec2, vec2,
            pl.BlockSpec((1, 2, LRU_CB), lambda b, c: (b, 0, c)),
        ],
        out_specs=[
            pl.BlockSpec((1, L, LRU_CB), lambda b, c: (b, 0, c)),
            pl.BlockSpec((1, 2, LRU_CB), lambda b, c: (b, 0, c)),
        ],
        out_shape=[
            jax.ShapeDtypeStruct((nseq, L, D_RNN), F32),
            jax.ShapeDtypeStruct((nseq, 2, D_RNN), F32),
        ],
        scratch_shapes=[
            pltpu.VMEM((SUBLANES * pitch, LANES), F32),
            pltpu.VMEM(((S + 3) * SUBLANES, LANES), F32),
            pltpu.VMEM((L, LANES), F32),
            pltpu.VMEM((L, LANES), F32),
            pltpu.VMEM((L, LANES), F32),
            pltpu.VMEM((L, LANES), F32),
            pltpu.VMEM((SUBLANES * pitch, LANES), F32),
        ],
        compiler_params=_cparams(("arbitrary", "arbitrary")),
        name=f"lru_mixer_{L}",
    )(zr, zr, conv_w, conv_b, wa_bd, wx_bd, ba, bx, lam, h0)


def _out_kernel(*refs, add_pos, row_base, row_step):
    if add_pos:
        (x_ref, pos_ref, yr_ref, zuv_ref, yf_ref, mod_ref, ws_ref, bt_ref, wo_ref, g_ref, wr_ref, br_ref,
         x1_ref, h2_ref, ri_ref) = refs
    else:
        (x_ref, yr_ref, zuv_ref, yf_ref, mod_ref, ws_ref, bt_ref, wo_ref, g_ref, wr_ref, br_ref,
         x1_ref, h2_ref, ri_ref) = refs
    tm = x_ref.shape[1]
    x = x_ref[0]
    if add_pos:
        x = x + pos_ref[...]
    row = row_base + pl.program_id(0) * row_step
    m = mod_ref[pl.ds(row, 1), :]
    g1 = m[:, 2 * D_MODEL:3 * D_MODEL]
    sh2 = m[:, 3 * D_MODEL:4 * D_MODEL]
    sc2 = m[:, 4 * D_MODEL:5 * D_MODEL]

    head = lax.broadcasted_iota(jnp.int32, (CHUNK, D_TMLP), 1) >> 6
    yt_parts = []
    for ci in range(tm // CHUNK):
        u = zuv_ref[0, ci * CHUNK:(ci + 1) * CHUNK, 0:D_TMLP]
        v = zuv_ref[0, ci * CHUNK:(ci + 1) * CHUNK, D_TMLP:2 * D_TMLP].astype(BF16)
        s = jnp.zeros((CHUNK, D_TMLP), F32)
        for h in range(TMLP_HEADS):
            sh = jnp.dot(ws_ref[h], v, preferred_element_type=F32) + bt_ref[:, h:h + 1]
            s = jnp.where(head == h, sh, s)
        yt_parts.append(u * s)
    yt = jnp.concatenate(yt_parts, axis=0) if len(yt_parts) > 1 else yt_parts[0]

    y = (jnp.dot(yr_ref[0].astype(BF16), wo_ref[0:D_RNN, :], preferred_element_type=F32)
         + jnp.dot(yt.astype(BF16), wo_ref[D_RNN:D_RNN + D_TMLP, :], preferred_element_type=F32)
         + jnp.dot(yf_ref[...].astype(BF16), wo_ref[D_RNN + D_TMLP:D_MODEL, :], preferred_element_type=F32))
    x1 = x + g1 * y
    x1_ref[0] = x1
    h2 = _rms_mod(x1, g_ref[...], sc2, sh2)
    h2_ref[0] = h2

    wr = wr_ref[...]
    w_hi = wr.astype(BF16)
    w_lo = (wr - w_hi.astype(F32)).astype(BF16)
    h_hi = h2.astype(BF16)
    h_lo = (h2 - h_hi.astype(F32)).astype(BF16)
    p_hi = jnp.dot(h_hi, jnp.concatenate([w_hi, w_lo], axis=-1), preferred_element_type=F32)
    p_lo = jnp.dot(h_lo, w_hi, preferred_element_type=F32)
    logits = p_hi[:, 0:ROUTER_LANES] + p_hi[:, ROUTER_LANES:2 * ROUTER_LANES] + p_lo + br_ref[...]
    lane = lax.broadcasted_iota(jnp.int32, (tm, ROUTER_LANES), 1)
    lane_f = lane.astype(F32)
    is_g = lane < N_EGROUPS
    gl = jnp.where(is_g, logits, NEG_BIG)
    gmax = jnp.max(gl, axis=-1, keepdims=True)
    gsel = jnp.min(jnp.where(gl == gmax, lane_f, 1e4), axis=-1, keepdims=True)
    pg = 1.0 / jnp.sum(jnp.where(is_g, jnp.exp(logits - gmax), 0.0), axis=-1, keepdims=True)
    grp_f = ((lane - N_EGROUPS) >> 3).astype(F32)
    emask = (lane >= N_EGROUPS) & (lane < N_EGROUPS + N_EXPERTS) & (grp_f == gsel)
    el = jnp.where(emask, logits, NEG_BIG)
    v1 = jnp.max(el, axis=-1, keepdims=True)
    i1 = jnp.min(jnp.where(el == v1, lane_f, 1e4), axis=-1, keepdims=True)
    el2 = jnp.where(lane_f == i1, NEG_BIG, el)
    v2 = jnp.max(el2, axis=-1, keepdims=True)
    i2 = jnp.min(jnp.where(el2 == v2, lane_f, 1e4), axis=-1, keepdims=True)
    e2x = jnp.exp(v2 - v1)
    fw1 = 1.0 / (1.0 + e2x)
    fw2 = e2x * fw1
    ri = jnp.where(lane == 0, i1 - N_EGROUPS,
                   jnp.where(lane == 1, i2 - N_EGROUPS,
                             jnp.where(lane == 2, pg * fw1,
                                       jnp.where(lane == 3, pg * fw2, 0.0))))
    ri_ref[0] = ri


def _out_proj(x, pos, yr, zuv, yf, mod, ws_b, bt, wo_b, g, wr, br, *, row_base, row_step):
    nseq, L, _ = x.shape
    tm = ROW_TILE
    add_pos = pos is not None
    in_specs = [pl.BlockSpec((1, tm, D_MODEL), lambda b, t: (b, t, 0))]
    args = [x]
    if add_pos:
        in_specs.append(pl.BlockSpec((tm, D_MODEL), lambda b, t: (t, 0)))
        args.append(pos)
    in_specs += [
        pl.BlockSpec((1, tm, D_RNN), lambda b, t: (b, t, 0)),
        pl.BlockSpec((1, tm, 2 * D_TMLP), lambda b, t: (b, t, 0)),
        pl.BlockSpec((tm, D_FNET), lambda b, t: (t, b)),
        pl.BlockSpec((SUBLANES, N_MOD * D_MODEL), lambda b, t: (0, 0)),
        pl.BlockSpec((TMLP_HEADS, CHUNK, CHUNK), lambda b, t: (0, 0, 0)),
        pl.BlockSpec((CHUNK, TMLP_HEADS), lambda b, t: (0, 0)),
        pl.BlockSpec((D_MODEL, D_MODEL), lambda b, t: (0, 0)),
        pl.BlockSpec((1, D_MODEL), lambda b, t: (0, 0)),
        pl.BlockSpec((D_MODEL, ROUTER_LANES), lambda b, t: (0, 0)),
        pl.BlockSpec((1, ROUTER_LANES), lambda b, t: (0, 0)),
    ]
    args += [yr, zuv, yf, mod, ws_b, bt, wo_b, g, wr, br]
    tok = pl.BlockSpec((1, tm, D_MODEL), lambda b, t: (b, t, 0))
    return pl.pallas_call(
        functools.partial(_out_kernel, add_pos=add_pos, row_base=row_base, row_step=row_step),
        grid=(nseq, L // tm),
        in_specs=in_specs,
        out_specs=[tok, tok, pl.BlockSpec((1, tm, ROUTER_LANES), lambda b, t: (b, t, 0))],
        out_shape=[
            jax.ShapeDtypeStruct((nseq, L, D_MODEL), F32),
            jax.ShapeDtypeStruct((nseq, L, D_MODEL), F32),
            jax.ShapeDtypeStruct((nseq, L, ROUTER_LANES), F32),
        ],
        compiler_params=_cparams(("arbitrary", "arbitrary")),
        name=f"out_proj_{L}",
    )(*args)


def _row_copy(src_ref, src_row, dst_ref, dst_row, sem):
    return pltpu.make_async_copy(src_ref.at[pl.ds(src_row, 1)], dst_ref.at[pl.ds(dst_row, 1)], sem)


def _scatter_rows(pos_ref, h_ref, xs_ref, sem):
    tm = h_ref.shape[0]

    for r in range(tm):
        _row_copy(h_ref, r, xs_ref, pos_ref[0, 0, r], sem).start(priority=0)
        _row_copy(h_ref, r, xs_ref, pos_ref[0, 1, r], sem).start(priority=1)
    for _ in range(2):
        pltpu.make_async_copy(h_ref, xs_ref.at[pl.ds(0, tm)], sem).wait()


def _dispatch_kernel(pos_ref, hp_ref, hs_ref, xs_ref, sem, *, n_first):
    i = pl.program_id(0)

    @pl.when(i < n_first)
    def _():
        _scatter_rows(pos_ref, hp_ref, xs_ref, sem)

    @pl.when(i >= n_first)
    def _():
        _scatter_rows(pos_ref, hs_ref, xs_ref, sem)


def _dispatch(pos, h_first, h_second):
    tm = ROW_TILE
    n_first = h_first.shape[0] // tm
    n_second = h_second.shape[0] // tm
    nrows = 2 * (h_first.shape[0] + h_second.shape[0])
    return pl.pallas_call(
        functools.partial(_dispatch_kernel, n_first=n_first),
        grid=(n_first + n_second,),
        in_specs=[
            pl.BlockSpec((1, 2, tm), lambda i: (i, 0, 0), memory_space=pltpu.SMEM),
            pl.BlockSpec((tm, D_MODEL), lambda i: (jnp.minimum(i, n_first - 1), 0)),
            pl.BlockSpec((tm, D_MODEL), lambda i: (jnp.maximum(i - n_first, 0), 0)),
        ],
        out_specs=pl.BlockSpec(memory_space=pl.ANY),
        out_shape=jax.ShapeDtypeStruct((nrows, D_MODEL), F32),
        scratch_shapes=[pltpu.SemaphoreType.DMA(())],
        compiler_params=_cparams(("arbitrary",)),
        name="moe_dispatch",
    )(pos, h_first, h_second)


def _expert_kernel(wt_ref, we_ref, lo_ref, hi_ref, first_ref, xs_ref, w1_ref, w3_ref, w2_ref, ys_ref,
                   w1b, w3b, w2b, prev):
    w = pl.program_id(0)

    @pl.when(w == 0)
    def _():
        prev[0] = -1

    e = we_ref[w]

    @pl.when(e != prev[0])
    def _():
        w1b[...] = w1_ref[0].astype(BF16)
        w3b[...] = w3_ref[0].astype(BF16)
        w2b[...] = w2_ref[0].astype(BF16)
        prev[0] = e

    @pl.when(first_ref[w] == 1)
    def _():
        ys_ref[...] = jnp.zeros_like(ys_ref)

    lo = lo_ref[w]
    hi = hi_ref[w]

    @pl.when(hi > lo)
    def _():
        x = xs_ref[...].astype(BF16)
        a = jnp.dot(x, w1b[...], preferred_element_type=F32)
        b = jnp.dot(x, w3b[...], preferred_element_type=F32)
        hid = (a * jax.nn.sigmoid(a)) * b
        res = jnp.dot(hid.astype(BF16), w2b[...], preferred_element_type=F32)
        rows = lax.broadcasted_iota(jnp.int32, res.shape, 0)
        ys_ref[...] = jnp.where((rows >= lo) & (rows < hi), res, ys_ref[...])


def _experts(items, xs, w1, w3, w2):
    nw = items[0].shape[0]
    row = lambda w, wt, we, lo, hi, fi: (wt[w], 0)
    wsel = lambda w, wt, we, lo, hi, fi: (we[w], 0, 0)
    return pl.pallas_call(
        _expert_kernel,
        grid_spec=pltpu.PrefetchScalarGridSpec(
            num_scalar_prefetch=5,
            grid=(nw,),
            in_specs=[
                pl.BlockSpec((EXPERT_TILE, D_MODEL), row),
                pl.BlockSpec((1, D_MODEL, D_EXPERT), wsel),
                pl.BlockSpec((1, D_MODEL, D_EXPERT), wsel),
                pl.BlockSpec((1, D_EXPERT, D_MODEL), wsel),
            ],
            out_specs=pl.BlockSpec((EXPERT_TILE, D_MODEL), row),
            scratch_shapes=[
                pltpu.VMEM((D_MODEL, D_EXPERT), BF16),
                pltpu.VMEM((D_MODEL, D_EXPERT), BF16),
                pltpu.VMEM((D_EXPERT, D_MODEL), BF16),
                pltpu.SMEM((1,), jnp.int32),
            ],
        ),
        out_shape=jax.ShapeDtypeStruct(xs.shape, F32),
        compiler_params=_cparams(("arbitrary",)),
        name="moe_experts",
    )(*items, xs, w1, w3, w2)


def _combine_kernel(pos_ref, x1_ref, ri_ref, mod_ref, gf_ref, ys_ref, o_ref, y1buf, y2buf, sem,
                    *, row_base, row_step, tiles_per_seq, final):
    tm = x1_ref.shape[0]

    for r in range(tm):
        _row_copy(ys_ref, pos_ref[0, 0, r], y1buf, r, sem).start(priority=0)
        _row_copy(ys_ref, pos_ref[0, 1, r], y2buf, r, sem).start(priority=1)
    pltpu.make_async_copy(ys_ref.at[pl.ds(0, tm)], y1buf, sem).wait()
    pltpu.make_async_copy(ys_ref.at[pl.ds(0, tm)], y2buf, sem).wait()
    row = row_base + (pl.program_id(0) // tiles_per_seq) * row_step
    g2 = mod_ref[pl.ds(row, 1), 5 * D_MODEL:6 * D_MODEL]
    ri = ri_ref[...]
    y = ri[:, 2:3] * y1buf[...] + ri[:, 3:4] * y2buf[...]
    x2 = x1_ref[...] + g2 * y
    if final:
        ms = jnp.mean(x2 * x2, axis=-1, keepdims=True)
        o_ref[...] = (x2 * lax.rsqrt(ms + EPS)) * gf_ref[...]
    else:
        o_ref[...] = x2


def _combine(pos, x1, ri, mod, g_final, ys, *, row_base, row_step, tiles_per_seq, final):
    ntok = x1.shape[0]
    tm = ROW_TILE
    tok = pl.BlockSpec((tm, D_MODEL), lambda i: (i, 0))
    return pl.pallas_call(
        functools.partial(_combine_kernel, row_base=row_base, row_step=row_step, tiles_per_seq=tiles_per_seq,
                          final=final),
        grid=(ntok // tm,),
        in_specs=[
            pl.BlockSpec((1, 2, tm), lambda i: (i, 0, 0), memory_space=pltpu.SMEM),
            tok,
            pl.BlockSpec((tm, ROUTER_LANES), lambda i: (i, 0)),
            pl.BlockSpec((SUBLANES, N_MOD * D_MODEL), lambda i: (0, 0)),
            pl.BlockSpec((1, D_MODEL), lambda i: (0, 0)),
            pl.BlockSpec(memory_space=pl.ANY),
        ],
        out_specs=tok,
        out_shape=jax.ShapeDtypeStruct((ntok, D_MODEL), F32),
        scratch_shapes=[
            pltpu.VMEM((tm, D_MODEL), F32),
            pltpu.VMEM((tm, D_MODEL), F32),
            pltpu.SemaphoreType.DMA(()),
        ],
        compiler_params=_cparams(("arbitrary",)),
        name=f"moe_combine_{ntok}",
    )(pos, x1, ri, mod, g_final, ys)


def _rank_kernel(rp_ref, rs_ref, rank_ref, cnt_ref, carry, *, n_first):
    i = pl.program_id(0)

    @pl.when(i == 0)
    def _():
        carry[...] = jnp.zeros_like(carry)

    ri = jnp.where(i < n_first, rp_ref[...], rs_ref[...])
    tm = ri.shape[0]
    lane = lax.broadcasted_iota(jnp.int32, (tm, ROUTER_LANES), 1).astype(F32)
    m1 = lane == ri[:, 0:1]
    m2 = lane == ri[:, 1:2]
    oh = jnp.where(m1 | m2, 1.0, 0.0)
    r = lax.broadcasted_iota(jnp.int32, (tm, tm), 0)
    c = lax.broadcasted_iota(jnp.int32, (tm, tm), 1)
    tri = jnp.where(c < r, 1.0, 0.0).astype(BF16)
    before = jnp.dot(tri, oh.astype(BF16), preferred_element_type=F32) + carry[0:1, :]
    rank1 = jnp.sum(jnp.where(m1, before, 0.0), axis=-1, keepdims=True)
    rank2 = jnp.sum(jnp.where(m2, before, 0.0), axis=-1, keepdims=True)
    lane_i = lax.broadcasted_iota(jnp.int32, (tm, ROUTER_LANES), 1)
    rank_ref[...] = jnp.where(lane_i == 0, rank1, jnp.where(lane_i == 1, rank2, 0.0))
    total = carry[0:1, :] + jnp.sum(oh, axis=0, keepdims=True)
    carry[0:1, :] = total
    cnt_ref[...] = jnp.broadcast_to(total, cnt_ref.shape)


def _ranks(ri_first, ri_second):
    tm = ROW_TILE
    n_first = ri_first.shape[0] // tm
    n_second = ri_second.shape[0] // tm
    return pl.pallas_call(
        functools.partial(_rank_kernel, n_first=n_first),
        grid=(n_first + n_second,),
        in_specs=[
            pl.BlockSpec((tm, ROUTER_LANES), lambda i: (jnp.minimum(i, n_first - 1), 0)),
            pl.BlockSpec((tm, ROUTER_LANES), lambda i: (jnp.maximum(i - n_first, 0), 0)),
        ],
        out_specs=[
            pl.BlockSpec((tm, ROUTER_LANES), lambda i: (i, 0)),
            pl.BlockSpec((SUBLANES, ROUTER_LANES), lambda i: (0, 0)),
        ],
        out_shape=[
            jax.ShapeDtypeStruct(((n_first + n_second) * tm, ROUTER_LANES), F32),
            jax.ShapeDtypeStruct((SUBLANES, ROUTER_LANES), F32),
        ],
        scratch_shapes=[pltpu.VMEM((SUBLANES, ROUTER_LANES), F32)],
        compiler_params=_cparams(("arbitrary",)),
        name="moe_ranks",
    )(ri_first, ri_second)


def _route(ri_first, ri_second):
    ranks, cnt = _ranks(ri_first, ri_second)
    counts = cnt[0, 0:N_EXPERTS].astype(jnp.int32)
    offs = jnp.cumsum(counts) - counts
    e = jnp.concatenate([ri_first[:, 0:2], ri_second[:, 0:2]], axis=0).astype(jnp.int32)
    oh = (e[:, :, None] == jnp.arange(N_EXPERTS, dtype=jnp.int32)).astype(jnp.int32)
    pos = jnp.sum(oh * offs[None, None, :], axis=-1) + ranks[:, 0:2].astype(jnp.int32)
    return pos.astype(jnp.int32), counts, offs


def _work_items(counts, offs, nw, layer):
    tm = EXPERT_TILE
    first_tile = offs // tm
    last_tile = (offs + counts - 1) // tm
    n_e = jnp.where(counts > 0, last_tile - first_tile + 1, 0)
    w_end = jnp.cumsum(n_e)
    w_start = w_end - n_e
    total = w_end[-1]
    w = jnp.arange(nw, dtype=jnp.int32)
    wc = jnp.minimum(w, total - 1)
    e_w = jnp.sum((wc[:, None] >= w_end[None, :]).astype(jnp.int32), axis=-1)
    tile_w = first_tile[e_w] + (wc - w_start[e_w])
    lo = jnp.clip(offs[e_w] - tile_w * tm, 0, tm)
    hi = jnp.clip(offs[e_w] + counts[e_w] - tile_w * tm, 0, tm)
    valid = w < total
    lo = jnp.where(valid, lo, 0)
    hi = jnp.where(valid, hi, 0)
    first = (w == 0) | (tile_w != jnp.roll(tile_w, 1))
    i32 = lambda a: a.astype(jnp.int32)
    return (i32(tile_w), i32(e_w + layer * N_EXPERTS), i32(lo), i32(hi), i32(first))


def _grid_pos_embed(rows, d):
    t = jnp.arange(rows * GRID_W)
    row = (t // GRID_W).astype(F32)
    col = (t % GRID_W).astype(F32)
    nf = d // 4
    freq = 1.0 / (10000.0 ** (jnp.arange(nf, dtype=F32) / nf))
    er = row[:, None] * freq
    ec = col[:, None] * freq
    return jnp.concatenate([jnp.sin(er), jnp.cos(er), jnp.sin(ec), jnp.cos(ec)], axis=-1)


def _block_diag(w, nblk):
    *lead, H, d, _ = w.shape
    w = w.reshape(*lead, H // nblk, nblk, d, d)
    eye = jnp.eye(nblk, dtype=w.dtype)
    out = jnp.einsum('...gij,gh->...gihj', w, eye)
    return out.reshape(*lead, H // nblk, nblk * d, nblk * d)


def kernel(x_prompt, x_sample, state_lru, c, c_ctx, w_ada, b_ada, g_mix, g_ffn, g_final, w_in, w_out, conv_w, conv_b, lru_wa, lru_ba, lru_wx, lru_bx, lru_lambda, tmlp_ws, tmlp_b, fnet_w, router_g, router_g_b, router_e, router_e_b, e_w1, e_w3, e_w2):
    bp, lp, _ = x_prompt.shape
    bs, ls, _ = x_sample.shape
    n_ctx = bp * lp
    n_lat = bs * ls
    n_tok = n_ctx + n_lat

    cond8 = jnp.concatenate([c_ctx[None, :], c, jnp.zeros((SUBLANES - 1 - bs, D_MODEL), F32)], axis=0)
    pos = _grid_pos_embed(ls // GRID_W, D_MODEL)
    w_in_b = w_in.astype(BF16)
    w_out_b = w_out.astype(BF16)
    heads_per_cb = LRU_CB // RNN_HEAD_DIM
    wa_bd = (0.5 * _block_diag(lru_wa, heads_per_cb)).astype(BF16)
    wx_bd = (0.5 * _block_diag(lru_wx, heads_per_cb)).astype(BF16)
    ws_b = tmlp_ws.astype(BF16)
    bt = jnp.swapaxes(tmlp_b, 1, 2)
    fnet_bd = _block_diag(fnet_w, D_FNET // FNET_GROUP_DIM)[:, 0]
    wr = jnp.concatenate([router_g, router_e,
                          jnp.zeros((DEPTH, D_MODEL, ROUTER_LANES - N_EGROUPS - N_EXPERTS), F32)], axis=-1)
    br = jnp.concatenate([router_g_b, router_e_b,
                          jnp.zeros((DEPTH, ROUTER_LANES - N_EGROUPS - N_EXPERTS), F32)], axis=-1)
    ew1 = e_w1.reshape(DEPTH * N_EXPERTS, D_MODEL, D_EXPERT)
    ew3 = e_w3.reshape(DEPTH * N_EXPERTS, D_MODEL, D_EXPERT)
    ew2 = e_w2.reshape(DEPTH * N_EXPERTS, D_EXPERT, D_MODEL)

    mods = _modulation(cond8, w_ada, b_ada)
    fa, fb = _fnet_prep(fnet_bd)
    ct_p, st_p = _dft_tables(lp)
    ct_s, st_s = _dft_tables(ls)

    nw = 2 * n_tok // EXPERT_TILE + N_EXPERTS
    zeros_p = jnp.zeros((bp, D_RNN), F32)

    xp, xs = x_prompt, x_sample
    states = []
    for l in range(DEPTH):
        mod = mods[l]
        g_mix_l = g_mix[l][None, :]
        g_ffn_l = g_ffn[l][None, :]
        paths = []
        for (x, ct, st, h0, row_base, row_step, is_lat) in (
                (xp, ct_p, st_p, jnp.stack([zeros_p, zeros_p], axis=1), 0, 0, False),
                (xs, ct_s, st_s, state_lru[:, l], 1, 1, True)):
            pe = pos if (is_lat and l == 0) else None
            zr, zuv, va, vb = _in_proj(x, pe, mod, g_mix_l, w_in_b[l], fa[l], fb[l],
                                       row_base=row_base, row_step=row_step)
            yr, st_new = _lru_mixer(zr, conv_w[l], conv_b[l][None, :], wa_bd[l], wx_bd[l],
                                    lru_ba[l], lru_bx[l], lru_lambda[l], h0)
            yf = _dft_apply(ct, st, va, vb)
            x1, h2, ri = _out_proj(x, pe, yr, zuv, yf, mod, ws_b[l], bt[l], w_out_b[l], g_ffn_l,
                                   wr[l], br[l][None, :], row_base=row_base, row_step=row_step)
            paths.append((x1, h2, ri, st_new))
        states.append(paths[0][3])

        pos_all, counts, offs = _route(paths[0][2].reshape(n_ctx, ROUTER_LANES),
                                       paths[1][2].reshape(n_lat, ROUTER_LANES))
        items = _work_items(counts, offs, nw, l)
        pos_tiles = jnp.swapaxes(pos_all.reshape(n_tok // ROW_TILE, ROW_TILE, 2), 1, 2)
        nct = n_ctx // ROW_TILE
        xsorted = _dispatch(pos_tiles, paths[0][1].reshape(n_ctx, D_MODEL), paths[1][1].reshape(n_lat, D_MODEL))
        ysorted = _experts(items, xsorted, ew1, ew3, ew2)
        gfin = g_final[None, :]
        final = l == DEPTH - 1
        xp = _combine(pos_tiles[:nct], paths[0][0].reshape(n_ctx, D_MODEL),
                      paths[0][2].reshape(n_ctx, ROUTER_LANES), mod, gfin, ysorted,
                      row_base=0, row_step=0, tiles_per_seq=lp // ROW_TILE, final=final).reshape(bp, lp, D_MODEL)
        xs = _combine(pos_tiles[nct:], paths[1][0].reshape(n_lat, D_MODEL),
                      paths[1][2].reshape(n_lat, ROUTER_LANES), mod, gfin, ysorted,
                      row_base=1, row_step=1, tiles_per_seq=ls // ROW_TILE, final=final).reshape(bs, ls, D_MODEL)

    new_state = jnp.stack(states, axis=1).astype(x_prompt.dtype)
    return (xp, xs, new_state)
```

```python
import functools
import math

import jax
import jax.numpy as jnp
from jax import lax
from jax.experimental import pallas as pl
from jax.experimental.pallas import tpu as pltpu

F32 = jnp.float32
BF16 = jnp.bfloat16

D_MODEL = 1024
DEPTH = 2
GRID_W = 64
D_RNN = 512
RNN_HEAD_DIM = 64
CONV_W = 4
LRU_C = 8.0
D_TMLP = 256
TMLP_HEADS = 4
CHUNK = 128
D_FNET = 256
FNET_GROUP_DIM = 64
D_IN = 2 * D_RNN + 2 * D_TMLP + D_FNET
N_EGROUPS = 4
N_EPG = 8
N_EXPERTS = N_EGROUPS * N_EPG
D_EXPERT = 512
N_MOD = 6
EPS = 1e-6

LANES = 128
SUBLANES = 8
LRU_CB = LANES
ROW_TILE = 256
EXPERT_TILE = 256
ROUTER_LANES = LANES
NEG_BIG = -1e30
VMEM_LIMIT = 56 * 1024 * 1024


def _cparams(sem):
    return pltpu.CompilerParams(dimension_semantics=sem, vmem_limit_bytes=VMEM_LIMIT)


def _mod_kernel(c_ref, w_ref, b_ref, o_ref):
    c = c_ref[...]
    s = c * jax.nn.sigmoid(c)
    o_ref[0] = jnp.dot(s.astype(BF16), w_ref[0].astype(BF16), preferred_element_type=F32) + b_ref[0]


def _modulation(cond8, w_ada, b_ada):
    tn = 1536
    return pl.pallas_call(
        _mod_kernel,
        grid=(DEPTH, N_MOD * D_MODEL // tn),
        in_specs=[
            pl.BlockSpec((SUBLANES, D_MODEL), lambda l, j: (0, 0)),
            pl.BlockSpec((1, D_MODEL, tn), lambda l, j: (l, 0, j)),
            pl.BlockSpec((1, 1, tn), lambda l, j: (l, 0, j)),
        ],
        out_specs=pl.BlockSpec((1, SUBLANES, tn), lambda l, j: (l, 0, j)),
        out_shape=jax.ShapeDtypeStruct((DEPTH, SUBLANES, N_MOD * D_MODEL), F32),
        compiler_params=_cparams(("arbitrary", "arbitrary")),
        name="modulation",
    )(cond8, w_ada, b_ada.reshape(DEPTH, 1, N_MOD * D_MODEL))


def _fprep_kernel(w_ref, a_ref, b_ref):
    r = lax.broadcasted_iota(jnp.int32, (D_FNET, D_FNET), 0)
    c = lax.broadcasted_iota(jnp.int32, (D_FNET, D_FNET), 1)
    same = (r >> 6) == (c >> 6)
    ph = ((r & 63) * (c & 63)) & 63
    ang = ph.astype(F32) * (2.0 * math.pi / FNET_GROUP_DIM)
    scale = 1.0 / math.sqrt(FNET_GROUP_DIM)
    cm = jnp.where(same, jnp.cos(ang) * scale, 0.0)
    sm = jnp.where(same, jnp.sin(ang) * scale, 0.0)
    w = w_ref[0]
    a_ref[0] = jnp.dot(cm, w, precision=lax.Precision.HIGHEST, preferred_element_type=F32).astype(BF16)
    b_ref[0] = jnp.dot(sm, w, precision=lax.Precision.HIGHEST, preferred_element_type=F32).astype(BF16)


def _fnet_prep(wbd):
    spec = pl.BlockSpec((1, D_FNET, D_FNET), lambda l: (l, 0, 0))
    return pl.pallas_call(
        _fprep_kernel,
        grid=(DEPTH,),
        in_specs=[spec],
        out_specs=[spec, spec],
        out_shape=[jax.ShapeDtypeStruct((DEPTH, D_FNET, D_FNET), BF16)] * 2,
        compiler_params=_cparams(("arbitrary",)),
        name="fnet_prep",
    )(wbd)


TABLE_ROWS = 64


def _base_table_kernel(cj_ref, sj_ref, cm_ref, sm_ref, *, L):
    j = lax.broadcasted_iota(jnp.int32, (TABLE_ROWS, L), 0)
    n = lax.broadcasted_iota(jnp.int32, (TABLE_ROWS, L), 1)
    w = 2.0 * math.pi / L
    fine = ((j * n) & (L - 1)).astype(F32) * w
    coarse = ((j * TABLE_ROWS * n) & (L - 1)).astype(F32) * w
    scale = 1.0 / math.sqrt(L)
    cj_ref[...] = jnp.cos(fine)
    sj_ref[...] = jnp.sin(fine)
    cm_ref[...] = jnp.cos(coarse) * scale
    sm_ref[...] = jnp.sin(coarse) * scale


def _table_kernel(cj_ref, sj_ref, cm_ref, sm_ref, c_ref, s_ref):
    m = pl.program_id(0)
    c0 = cm_ref[pl.ds(m, 1), :]
    s0 = sm_ref[pl.ds(m, 1), :]
    cj = cj_ref[...]
    sj = sj_ref[...]
    c_ref[...] = (cj * c0 - sj * s0).astype(BF16)
    s_ref[...] = (-(sj * c0 + cj * s0)).astype(BF16)


def _dft_tables(L):
    small = pl.BlockSpec((TABLE_ROWS, L), lambda *_: (0, 0))
    base = pl.pallas_call(
        functools.partial(_base_table_kernel, L=L),
        out_specs=[small] * 4,
        out_shape=[jax.ShapeDtypeStruct((TABLE_ROWS, L), F32)] * 4,
        compiler_params=pltpu.CompilerParams(vmem_limit_bytes=VMEM_LIMIT),
        name=f"dft_base_tables_{L}",
    )()
    spec = pl.BlockSpec((TABLE_ROWS, L), lambda i: (i, 0))
    return pl.pallas_call(
        _table_kernel,
        grid=(L // TABLE_ROWS,),
        in_specs=[small] * 4,
        out_specs=[spec, spec],
        out_shape=[jax.ShapeDtypeStruct((L, L), BF16)] * 2,
        compiler_params=_cparams(("arbitrary",)),
        name=f"dft_tables_{L}",
    )(*base)


def _dft_kernel(c_ref, s_ref, va_ref, vb_ref, o_ref):
    o_ref[...] = (jnp.dot(c_ref[...], va_ref[...], preferred_element_type=F32)
                  + jnp.dot(s_ref[...], vb_ref[...], preferred_element_type=F32))


def _dft_apply(ct, st, va, vb):
    L, ncols = va.shape
    tk, tc = 256, 1024
    return pl.pallas_call(
        _dft_kernel,
        grid=(L // tk, ncols // tc),
        in_specs=[
            pl.BlockSpec((tk, L), lambda i, j: (i, 0)),
            pl.BlockSpec((tk, L), lambda i, j: (i, 0)),
            pl.BlockSpec((L, tc), lambda i, j: (0, j)),
            pl.BlockSpec((L, tc), lambda i, j: (0, j)),
        ],
        out_specs=pl.BlockSpec((tk, tc), lambda i, j: (i, j)),
        out_shape=jax.ShapeDtypeStruct((L, ncols), F32),
        compiler_params=_cparams(("arbitrary", "arbitrary")),
        name=f"dft_apply_{L}",
    )(ct, st, va, vb)


def _rms_mod(x, g, scale, shift):
    ms = jnp.mean(x * x, axis=-1, keepdims=True)
    return (x * lax.rsqrt(ms + EPS)) * g * (1.0 + scale) + shift


def _in_kernel(*refs, add_pos, row_base, row_step):
    if add_pos:
        x_ref, pos_ref, mod_ref, g_ref, w_ref, a_ref, b_ref, zr_ref, zuv_ref, va_ref, vb_ref = refs
    else:
        x_ref, mod_ref, g_ref, w_ref, a_ref, b_ref, zr_ref, zuv_ref, va_ref, vb_ref = refs
    x = x_ref[0]
    if add_pos:
        x = x + pos_ref[...]
    row = row_base + pl.program_id(0) * row_step
    m = mod_ref[pl.ds(row, 1), :]
    h = _rms_mod(x, g_ref[...], m[:, D_MODEL:2 * D_MODEL], m[:, 0:D_MODEL])
    z = jnp.dot(h.astype(BF16), w_ref[...], preferred_element_type=F32)
    zr_ref[0] = z[:, 0:2 * D_RNN]
    zuv_ref[0] = z[:, 2 * D_RNN:2 * D_RNN + 2 * D_TMLP]
    zf = z[:, 2 * D_RNN + 2 * D_TMLP:D_IN].astype(BF16)
    va_ref[...] = jnp.dot(zf, a_ref[...], preferred_element_type=F32).astype(BF16)
    vb_ref[...] = jnp.dot(zf, b_ref[...], preferred_element_type=F32).astype(BF16)


def _in_proj(x, pos, mods, g, w_in_b, fa, fb, *, layer, row_base, row_step):
    nseq, L, _ = x.shape
    tm = ROW_TILE
    add_pos = pos is not None
    in_specs = [pl.BlockSpec((1, tm, D_MODEL), lambda b, t: (b, t, 0))]
    args = [x]
    if add_pos:
        in_specs.append(pl.BlockSpec((tm, D_MODEL), lambda b, t: (t, 0)))
        args.append(pos)
    in_specs += [
        _layer_spec((SUBLANES, N_MOD * D_MODEL), layer),
        _layer_spec((1, D_MODEL), layer),
        _layer_spec((D_MODEL, D_IN), layer),
        _layer_spec((D_FNET, D_FNET), layer),
        _layer_spec((D_FNET, D_FNET), layer),
    ]
    args += [mods, g, w_in_b, fa, fb]
    return pl.pallas_call(
        functools.partial(_in_kernel, add_pos=add_pos, row_base=row_base, row_step=row_step),
        grid=(nseq, L // tm),
        in_specs=in_specs,
        out_specs=[
            pl.BlockSpec((1, tm, 2 * D_RNN), lambda b, t: (b, t, 0)),
            pl.BlockSpec((1, tm, 2 * D_TMLP), lambda b, t: (b, t, 0)),
            pl.BlockSpec((tm, D_FNET), lambda b, t: (t, b)),
            pl.BlockSpec((tm, D_FNET), lambda b, t: (t, b)),
        ],
        out_shape=[
            jax.ShapeDtypeStruct((nseq, L, 2 * D_RNN), F32),
            jax.ShapeDtypeStruct((nseq, L, 2 * D_TMLP), F32),
            jax.ShapeDtypeStruct((L, nseq * D_FNET), BF16),
            jax.ShapeDtypeStruct((L, nseq * D_FNET), BF16),
        ],
        compiler_params=_cparams(("arbitrary", "arbitrary")),
        name=f"in_proj_{L}",
    )(*args)


def _gelu_tanh(x):
    return 0.5 * x * (1.0 + jnp.tanh(math.sqrt(2.0 / math.pi) * (x + 0.044715 * (x * x * x))))


def _rows_to_tile(rows):
    sub = lax.broadcasted_iota(jnp.int32, (SUBLANES, LANES), 0)
    out = jnp.zeros((SUBLANES, LANES), F32)
    for s, r in enumerate(rows):
        out = jnp.where(sub == s, jnp.broadcast_to(r, (SUBLANES, LANES)), out)
    return out


def _lru_kernel(xr_ref, gr_ref, cw_ref, cb_ref, wa_ref, wx_ref, ba_ref, bx_ref, lam_ref, h0_ref,
                y_ref, st_ref, xnat, pext, af, bf, ab, bb, hnat, *, L):
    S = L // SUBLANES
    pitch = S + SUBLANES
    n = S * SUBLANES
    chunk = 256

    for s in range(SUBLANES):
        xnat[s * pitch:s * pitch + S, :] = xr_ref[0, s * S:(s + 1) * S, :]

    def perm_in(j, c):
        dst = pl.multiple_of((j + 2) * SUBLANES, SUBLANES)
        pext[pl.ds(dst, SUBLANES), :] = xnat[pl.ds(j, SUBLANES, stride=pitch), :]
        return c

    lax.fori_loop(0, S, perm_in, 0, unroll=8)

    sub = lax.broadcasted_iota(jnp.int32, (SUBLANES, LANES), 0)

    def from_prev_segment(v):
        return jnp.where(sub == 0, 0.0, pltpu.roll(v, 1, axis=0))

    def from_next_segment(v):
        return jnp.where(sub == SUBLANES - 1, 0.0, pltpu.roll(v, SUBLANES - 1, axis=0))

    pext[0:8, :] = from_prev_segment(pext[S * 8:(S + 1) * 8, :])
    pext[8:16, :] = from_prev_segment(pext[(S + 1) * 8:(S + 2) * 8, :])
    pext[(S + 2) * 8:(S + 3) * 8, :] = from_next_segment(pext[16:24, :])

    lam = lam_ref[...]
    nl = -lam
    sp = jnp.maximum(nl, 0.0) + jnp.log1p(jnp.exp(-jnp.abs(nl)))
    c_la = (-0.5 * LRU_C) * sp
    ba_h = 0.5 * ba_ref[...]
    bx_h = 0.5 * bx_ref[...]
    a_refs = (af, ab)
    b_refs = (bf, bb)

    def gates(i, c):
        base = pl.multiple_of(i * chunk, chunk)
        xc = (cw_ref[0:1, :] * pext[pl.ds(base, chunk), :]
              + cw_ref[1:2, :] * pext[pl.ds(base + 8, chunk), :]
              + cw_ref[2:3, :] * pext[pl.ds(base + 16, chunk), :]
              + cw_ref[3:4, :] * pext[pl.ds(base + 24, chunk), :]
              + cb_ref[...])
        xcb = xc.astype(BF16)
        xh = 0.5 * xc
        for d in range(2):
            tr = jnp.tanh(jnp.dot(xcb, wa_ref[d, 0], preferred_element_type=F32) + ba_h[d:d + 1, :])
            ti = jnp.tanh(jnp.dot(xcb, wx_ref[d, 0], preferred_element_type=F32) + bx_h[d:d + 1, :])
            log_a = c_la[d:d + 1, :] * (1.0 + tr)
            a = jnp.exp(log_a)
            v = jnp.tanh(log_a) * (-1.0 - a * a)
            coef = jnp.where(v > 0.0, v * lax.rsqrt(v), 0.0)
            a_refs[d][pl.ds(base, chunk), :] = a
            b_refs[d][pl.ds(base, chunk), :] = coef * ((1.0 + ti) * xh)
        return c

    lax.fori_loop(0, n // chunk, gates, 0)

    def scan(j, carry):
        hf, pf, hb, pb = carry
        jf = pl.multiple_of(j * SUBLANES, SUBLANES)
        jb = pl.multiple_of((S - 1 - j) * SUBLANES, SUBLANES)
        a1 = af[pl.ds(jf, SUBLANES), :]
        hf = a1 * hf + bf[pl.ds(jf, SUBLANES), :]
        pf = a1 * pf
        bf[pl.ds(jf, SUBLANES), :] = hf
        af[pl.ds(jf, SUBLANES), :] = pf
        a2 = ab[pl.ds(jb, SUBLANES), :]
        hb = a2 * hb + bb[pl.ds(jb, SUBLANES), :]
        pb = a2 * pb
        bb[pl.ds(jb, SUBLANES), :] = hb
        ab[pl.ds(jb, SUBLANES), :] = pb
        return hf, pf, hb, pb

    zero = jnp.zeros((SUBLANES, LANES), F32)
    one = jnp.ones((SUBLANES, LANES), F32)
    hf, pf, hb, pb = lax.fori_loop(0, S, scan, (zero, one, zero, one), unroll=8)

    rows_f = [h0_ref[0, 0:1, :]]
    for s in range(1, SUBLANES):
        rows_f.append(pf[s - 1:s, :] * rows_f[-1] + hf[s - 1:s, :])
    st_ref[0, 0:1, :] = pf[7:8, :] * rows_f[7] + hf[7:8, :]
    rows_b = [None] * SUBLANES
    rows_b[7] = h0_ref[0, 1:2, :]
    for s in range(SUBLANES - 2, -1, -1):
        rows_b[s] = pb[s + 1:s + 2, :] * rows_b[s + 1] + hb[s + 1:s + 2, :]
    st_ref[0, 1:2, :] = pb[0:1, :] * rows_b[0] + hb[0:1, :]
    init_f = _rows_to_tile(rows_f)
    init_b = _rows_to_tile(rows_b)

    def perm_out(j, c):
        src = pl.multiple_of(j * SUBLANES, SUBLANES)
        v = (bf[pl.ds(src, SUBLANES), :] + af[pl.ds(src, SUBLANES), :] * init_f
             + bb[pl.ds(src, SUBLANES), :] + ab[pl.ds(src, SUBLANES), :] * init_b)
        hnat[pl.ds(j, SUBLANES, stride=pitch), :] = v
        return c

    lax.fori_loop(0, S, perm_out, 0, unroll=8)

    for s in range(SUBLANES):
        g = gr_ref[0, s * S:(s + 1) * S, :]
        y_ref[0, s * S:(s + 1) * S, :] = hnat[s * pitch:s * pitch + S, :] * _gelu_tanh(g)


def _lru_mixer(zr, conv_w, conv_b, wa_bd, wx_bd, ba, bx, lam, h0, *, layer):
    nseq, L, _ = zr.shape
    ncb = D_RNN // LRU_CB
    S = L // SUBLANES
    pitch = S + SUBLANES
    vec2 = pl.BlockSpec((None, 2, LRU_CB), lambda b, c: (layer, 0, c))
    wspec = pl.BlockSpec((None, 2, 1, LRU_CB, LRU_CB), lambda b, c: (layer, 0, c, 0, 0))
    return pl.pallas_call(
        functools.partial(_lru_kernel, L=L),
        grid=(nseq, ncb),
        in_specs=[
            pl.BlockSpec((1, L, LRU_CB), lambda b, c: (b, 0, c)),
            pl.BlockSpec((1, L, LRU_CB), lambda b, c: (b, 0, c + ncb)),
            pl.BlockSpec((None, CONV_W, LRU_CB), lambda b, c: (layer, 0, c)),
            pl.BlockSpec((None, 1, LRU_CB), lambda b, c: (layer, 0, c)),
            wspec, wspec, vec2, vec2, vec2,
            pl.BlockSpec((1, 2, LRU_CB), lambda b, c: (b, 0, c)),
        ],
        out_specs=[
            pl.BlockSpec((1, L, LRU_CB), lambda b, c: (b, 0, c)),
            pl.BlockSpec((1, 2, LRU_CB), lambda b, c: (b, 0, c)),
        ],
        out_shape=[
            jax.ShapeDtypeStruct((nseq, L, D_RNN), F32),
            jax.ShapeDtypeStruct((nseq, 2, D_RNN), F32),
        ],
        scratch_shapes=[
            pltpu.VMEM((SUBLANES * pitch, LANES), F32),
            pltpu.VMEM(((S + 3) * SUBLANES, LANES), F32),
            pltpu.VMEM((L, LANES), F32),
            pltpu.VMEM((L, LANES), F32),
            pltpu.VMEM((L, LANES), F32),
            pltpu.VMEM((L, LANES), F32),
            pltpu.VMEM((SUBLANES * pitch, LANES), F32),
        ],
        compiler_params=_cparams(("arbitrary", "arbitrary")),
        name=f"lru_mixer_{L}",
    )(zr, zr, conv_w, conv_b, wa_bd, wx_bd, ba, bx, lam, h0)


def _out_kernel(*refs, add_pos, row_base, row_step):
    if add_pos:
        (x_ref, pos_ref, yr_ref, zuv_ref, yf_ref, mod_ref, ws_ref, bt_ref, wo_ref, g_ref, wr_ref, br_ref, cin_ref,
         x1_ref, h2_ref, ri_ref, cnt_ref, carry) = refs
    else:
        (x_ref, yr_ref, zuv_ref, yf_ref, mod_ref, ws_ref, bt_ref, wo_ref, g_ref, wr_ref, br_ref, cin_ref,
         x1_ref, h2_ref, ri_ref, cnt_ref, carry) = refs
    tm = x_ref.shape[1]
    x = x_ref[0]
    if add_pos:
        x = x + pos_ref[...]
    row = row_base + pl.program_id(0) * row_step
    m = mod_ref[pl.ds(row, 1), :]
    g1 = m[:, 2 * D_MODEL:3 * D_MODEL]
    sh2 = m[:, 3 * D_MODEL:4 * D_MODEL]
    sc2 = m[:, 4 * D_MODEL:5 * D_MODEL]

    head = lax.broadcasted_iota(jnp.int32, (CHUNK, D_TMLP), 1) >> 6
    yt_parts = []
    for ci in range(tm // CHUNK):
        u = zuv_ref[0, ci * CHUNK:(ci + 1) * CHUNK, 0:D_TMLP]
        v = zuv_ref[0, ci * CHUNK:(ci + 1) * CHUNK, D_TMLP:2 * D_TMLP].astype(BF16)
        s = jnp.zeros((CHUNK, D_TMLP), F32)
        for h in range(TMLP_HEADS):
            sh = jnp.dot(ws_ref[h], v, preferred_element_type=F32) + bt_ref[:, h:h + 1]
            s = jnp.where(head == h, sh, s)
        yt_parts.append(u * s)
    yt = jnp.concatenate(yt_parts, axis=0) if len(yt_parts) > 1 else yt_parts[0]

    y = (jnp.dot(yr_ref[0].astype(BF16), wo_ref[0:D_RNN, :], preferred_element_type=F32)
         + jnp.dot(yt.astype(BF16), wo_ref[D_RNN:D_RNN + D_TMLP, :], preferred_element_type=F32)
         + jnp.dot(yf_ref[...].astype(BF16), wo_ref[D_RNN + D_TMLP:D_MODEL, :], preferred_element_type=F32))
    x1 = x + g1 * y
    x1_ref[0] = x1
    h2 = _rms_mod(x1, g_ref[...], sc2, sh2)
    h2_ref[0] = h2

    wr = wr_ref[...]
    w_hi = wr.astype(BF16)
    w_lo = (wr - w_hi.astype(F32)).astype(BF16)
    h_hi = h2.astype(BF16)
    h_lo = (h2 - h_hi.astype(F32)).astype(BF16)
    p_hi = jnp.dot(h_hi, jnp.concatenate([w_hi, w_lo], axis=-1), preferred_element_type=F32)
    p_lo = jnp.dot(h_lo, w_hi, preferred_element_type=F32)
    logits = p_hi[:, 0:ROUTER_LANES] + p_hi[:, ROUTER_LANES:2 * ROUTER_LANES] + p_lo + br_ref[...]
    lane = lax.broadcasted_iota(jnp.int32, (tm, ROUTER_LANES), 1)
    lane_f = lane.astype(F32)
    is_g = lane < N_EGROUPS
    gl = jnp.where(is_g, logits, NEG_BIG)
    gmax = jnp.max(gl, axis=-1, keepdims=True)
    gsel = jnp.min(jnp.where(gl == gmax, lane_f, 1e4), axis=-1, keepdims=True)
    pg = 1.0 / jnp.sum(jnp.where(is_g, jnp.exp(logits - gmax), 0.0), axis=-1, keepdims=True)
    grp_f = ((lane - N_EGROUPS) >> 3).astype(F32)
    emask = (lane >= N_EGROUPS) & (lane < N_EGROUPS + N_EXPERTS) & (grp_f == gsel)
    el = jnp.where(emask, logits, NEG_BIG)
    v1 = jnp.max(el, axis=-1, keepdims=True)
    i1 = jnp.min(jnp.where(el == v1, lane_f, 1e4), axis=-1, keepdims=True)
    el2 = jnp.where(lane_f == i1, NEG_BIG, el)
    v2 = jnp.max(el2, axis=-1, keepdims=True)
    i2 = jnp.min(jnp.where(el2 == v2, lane_f, 1e4), axis=-1, keepdims=True)
    e2x = jnp.exp(v2 - v1)
    fw1 = 1.0 / (1.0 + e2x)
    fw2 = e2x * fw1
    @pl.when((pl.program_id(0) == 0) & (pl.program_id(1) == 0))
    def _():
        carry[...] = cin_ref[...]

    e1 = i1 - N_EGROUPS
    e2 = i2 - N_EGROUPS
    m1 = lane_f == e1
    m2 = lane_f == e2
    oh = jnp.where(m1 | m2, 1.0, 0.0)
    r_i = lax.broadcasted_iota(jnp.int32, (tm, tm), 0)
    c_i = lax.broadcasted_iota(jnp.int32, (tm, tm), 1)
    tri = jnp.where(c_i < r_i, 1.0, 0.0).astype(BF16)
    before = jnp.dot(tri, oh.astype(BF16), preferred_element_type=F32) + carry[0:1, :]
    rank1 = jnp.sum(jnp.where(m1, before, 0.0), axis=-1, keepdims=True)
    rank2 = jnp.sum(jnp.where(m2, before, 0.0), axis=-1, keepdims=True)
    total = carry[0:1, :] + jnp.sum(oh, axis=0, keepdims=True)
    carry[0:1, :] = total
    cnt_ref[...] = jnp.broadcast_to(total, cnt_ref.shape)

    vals = (e1, e2, pg * fw1, pg * fw2, rank1, rank2)
    ri = jnp.zeros((tm, ROUTER_LANES), F32)
    for k, v in enumerate(vals):
        ri = jnp.where(lane == k, v, ri)
    ri_ref[0] = ri


def _layer_spec(shape, layer):
    zeros = (0,) * len(shape)
    return pl.BlockSpec((None, *shape), lambda *_: (layer, *zeros))


def _out_proj(x, pos, yr, zuv, yf, mods, ws_b, bt, wo_b, g, wr, br, cin, *, layer, row_base, row_step):
    nseq, L, _ = x.shape
    tm = ROW_TILE
    add_pos = pos is not None
    in_specs = [pl.BlockSpec((1, tm, D_MODEL), lambda b, t: (b, t, 0))]
    args = [x]
    if add_pos:
        in_specs.append(pl.BlockSpec((tm, D_MODEL), lambda b, t: (t, 0)))
        args.append(pos)
    in_specs += [
        pl.BlockSpec((1, tm, D_RNN), lambda b, t: (b, t, 0)),
        pl.BlockSpec((1, tm, 2 * D_TMLP), lambda b, t: (b, t, 0)),
        pl.BlockSpec((tm, D_FNET), lambda b, t: (t, b)),
        _layer_spec((SUBLANES, N_MOD * D_MODEL), layer),
        _layer_spec((TMLP_HEADS, CHUNK, CHUNK), layer),
        _layer_spec((CHUNK, TMLP_HEADS), layer),
        _layer_spec((D_MODEL, D_MODEL), layer),
        _layer_spec((1, D_MODEL), layer),
        _layer_spec((D_MODEL, ROUTER_LANES), layer),
        _layer_spec((1, ROUTER_LANES), layer),
        pl.BlockSpec((SUBLANES, ROUTER_LANES), lambda b, t: (0, 0)),
    ]
    args += [yr, zuv, yf, mods, ws_b, bt, wo_b, g, wr, br, cin]
    tok = pl.BlockSpec((1, tm, D_MODEL), lambda b, t: (b, t, 0))
    return pl.pallas_call(
        functools.partial(_out_kernel, add_pos=add_pos, row_base=row_base, row_step=row_step),
        grid=(nseq, L // tm),
        in_specs=in_specs,
        out_specs=[tok, tok, pl.BlockSpec((1, tm, ROUTER_LANES), lambda b, t: (b, t, 0)),
                   pl.BlockSpec((SUBLANES, ROUTER_LANES), lambda b, t: (0, 0))],
        out_shape=[
            jax.ShapeDtypeStruct((nseq, L, D_MODEL), F32),
            jax.ShapeDtypeStruct((nseq, L, D_MODEL), F32),
            jax.ShapeDtypeStruct((nseq, L, ROUTER_LANES), F32),
            jax.ShapeDtypeStruct((SUBLANES, ROUTER_LANES), F32),
        ],
        scratch_shapes=[pltpu.VMEM((SUBLANES, ROUTER_LANES), F32)],
        compiler_params=_cparams(("arbitrary", "arbitrary")),
        name=f"out_proj_{L}",
    )(*args)


def _row_copy(src_ref, src_row, dst_ref, dst_row, sem):
    return pltpu.make_async_copy(src_ref.at[pl.ds(src_row, 1)], dst_ref.at[pl.ds(dst_row, 1)], sem)


def _scatter_rows(pos_ref, h_ref, xs_ref, sem):
    tm = h_ref.shape[0]

    for r in range(tm):
        _row_copy(h_ref, r, xs_ref, pos_ref[0, 0, r], sem).start(priority=0)
        _row_copy(h_ref, r, xs_ref, pos_ref[0, 1, r], sem).start(priority=1)
    for _ in range(2):
        pltpu.make_async_copy(h_ref, xs_ref.at[pl.ds(0, tm)], sem).wait()


def _dispatch_kernel(pos_ref, hp_ref, hs_ref, xs_ref, sem, *, n_first):
    i = pl.program_id(0)

    @pl.when(i < n_first)
    def _():
        _scatter_rows(pos_ref, hp_ref, xs_ref, sem)

    @pl.when(i >= n_first)
    def _():
        _scatter_rows(pos_ref, hs_ref, xs_ref, sem)


def _dispatch(pos, h_first, h_second):
    tm = ROW_TILE
    n_first = h_first.shape[0] // tm
    n_second = h_second.shape[0] // tm
    nrows = 2 * (h_first.shape[0] + h_second.shape[0])
    return pl.pallas_call(
        functools.partial(_dispatch_kernel, n_first=n_first),
        grid=(n_first + n_second,),
        in_specs=[
            pl.BlockSpec((1, 2, tm), lambda i: (i, 0, 0), memory_space=pltpu.SMEM),
            pl.BlockSpec((tm, D_MODEL), lambda i: (jnp.minimum(i, n_first - 1), 0)),
            pl.BlockSpec((tm, D_MODEL), lambda i: (jnp.maximum(i - n_first, 0), 0)),
        ],
        out_specs=pl.BlockSpec(memory_space=pl.ANY),
        out_shape=jax.ShapeDtypeStruct((nrows, D_MODEL), F32),
        scratch_shapes=[pltpu.SemaphoreType.DMA(())],
        compiler_params=_cparams(("arbitrary",)),
        name="moe_dispatch",
    )(pos, h_first, h_second)


def _expert_kernel(wt_ref, we_ref, lo_ref, hi_ref, first_ref, xs_ref, w1_ref, w3_ref, w2_ref, ys_ref,
                   w1b, w3b, w2b, prev):
    w = pl.program_id(0)

    @pl.when(w == 0)
    def _():
        prev[0] = -1

    e = we_ref[w]

    @pl.when(e != prev[0])
    def _():
        w1b[...] = w1_ref[0].astype(BF16)
        w3b[...] = w3_ref[0].astype(BF16)
        w2b[...] = w2_ref[0].astype(BF16)
        prev[0] = e

    @pl.when(first_ref[w] == 1)
    def _():
        ys_ref[...] = jnp.zeros_like(ys_ref)

    lo = lo_ref[w]
    hi = hi_ref[w]

    @pl.when(hi > lo)
    def _():
        x = xs_ref[...].astype(BF16)
        a = jnp.dot(x, w1b[...], preferred_element_type=F32)
        b = jnp.dot(x, w3b[...], preferred_element_type=F32)
        hid = (a * jax.nn.sigmoid(a)) * b
        res = jnp.dot(hid.astype(BF16), w2b[...], preferred_element_type=F32)
        rows = lax.broadcasted_iota(jnp.int32, res.shape, 0)
        ys_ref[...] = jnp.where((rows >= lo) & (rows < hi), res, ys_ref[...])


def _experts(items, xs, w1, w3, w2):
    nw = items[0].shape[0]
    row = lambda w, wt, we, lo, hi, fi: (wt[w], 0)
    wsel = lambda w, wt, we, lo, hi, fi: (we[w], 0, 0)
    return pl.pallas_call(
        _expert_kernel,
        grid_spec=pltpu.PrefetchScalarGridSpec(
            num_scalar_prefetch=5,
            grid=(nw,),
            in_specs=[
                pl.BlockSpec((EXPERT_TILE, D_MODEL), row),
                pl.BlockSpec((1, D_MODEL, D_EXPERT), wsel),
                pl.BlockSpec((1, D_MODEL, D_EXPERT), wsel),
                pl.BlockSpec((1, D_EXPERT, D_MODEL), wsel),
            ],
            out_specs=pl.BlockSpec((EXPERT_TILE, D_MODEL), row),
            scratch_shapes=[
                pltpu.VMEM((D_MODEL, D_EXPERT), BF16),
                pltpu.VMEM((D_MODEL, D_EXPERT), BF16),
                pltpu.VMEM((D_EXPERT, D_MODEL), BF16),
                pltpu.SMEM((1,), jnp.int32),
            ],
        ),
        out_shape=jax.ShapeDtypeStruct(xs.shape, F32),
        compiler_params=_cparams(("arbitrary",)),
        name="moe_experts",
    )(*items, xs, w1, w3, w2)


def _combine_kernel(pos_ref, x1_ref, ri_ref, mod_ref, gf_ref, ys_ref, o_ref, y1buf, y2buf, sem,
                    *, row_base, row_step, tiles_per_seq, final):
    tm = x1_ref.shape[0]

    for r in range(tm):
        _row_copy(ys_ref, pos_ref[0, 0, r], y1buf, r, sem).start(priority=0)
        _row_copy(ys_ref, pos_ref[0, 1, r], y2buf, r, sem).start(priority=1)
    pltpu.make_async_copy(ys_ref.at[pl.ds(0, tm)], y1buf, sem).wait()
    pltpu.make_async_copy(ys_ref.at[pl.ds(0, tm)], y2buf, sem).wait()
    row = row_base + (pl.program_id(0) // tiles_per_seq) * row_step
    g2 = mod_ref[pl.ds(row, 1), 5 * D_MODEL:6 * D_MODEL]
    ri = ri_ref[...]
    y = ri[:, 2:3] * y1buf[...] + ri[:, 3:4] * y2buf[...]
    x2 = x1_ref[...] + g2 * y
    if final:
        ms = jnp.mean(x2 * x2, axis=-1, keepdims=True)
        o_ref[...] = (x2 * lax.rsqrt(ms + EPS)) * gf_ref[...]
    else:
        o_ref[...] = x2


def _combine(pos, x1, ri, mods, g_final, ys, *, layer, row_base, row_step, tiles_per_seq, final):
    ntok = x1.shape[0]
    tm = ROW_TILE
    tok = pl.BlockSpec((tm, D_MODEL), lambda i: (i, 0))
    return pl.pallas_call(
        functools.partial(_combine_kernel, row_base=row_base, row_step=row_step, tiles_per_seq=tiles_per_seq,
                          final=final),
        grid=(ntok // tm,),
        in_specs=[
            pl.BlockSpec((1, 2, tm), lambda i: (i, 0, 0), memory_space=pltpu.SMEM),
            tok,
            pl.BlockSpec((tm, ROUTER_LANES), lambda i: (i, 0)),
            _layer_spec((SUBLANES, N_MOD * D_MODEL), layer),
            pl.BlockSpec((1, D_MODEL), lambda i: (0, 0)),
            pl.BlockSpec(memory_space=pl.ANY),
        ],
        out_specs=tok,
        out_shape=jax.ShapeDtypeStruct((ntok, D_MODEL), F32),
        scratch_shapes=[
            pltpu.VMEM((tm, D_MODEL), F32),
            pltpu.VMEM((tm, D_MODEL), F32),
            pltpu.SemaphoreType.DMA(()),
        ],
        compiler_params=_cparams(("arbitrary",)),
        name=f"moe_combine_{ntok}",
    )(pos, x1, ri, mods, g_final, ys)


def _route(ri_first, ri_second, cnt):
    counts = cnt[0, 0:N_EXPERTS].astype(jnp.int32)
    offs = jnp.cumsum(counts) - counts
    info = jnp.concatenate([ri_first, ri_second], axis=0)
    e = info[:, 0:2].astype(jnp.int32)
    oh = (e[:, :, None] == jnp.arange(N_EXPERTS, dtype=jnp.int32)).astype(jnp.int32)
    pos = jnp.sum(oh * offs[None, None, :], axis=-1) + info[:, 4:6].astype(jnp.int32)
    return pos.astype(jnp.int32), counts, offs


def _work_items(counts, offs, nw, layer):
    tm = EXPERT_TILE
    first_tile = offs // tm
    last_tile = (offs + counts - 1) // tm
    n_e = jnp.where(counts > 0, last_tile - first_tile + 1, 0)
    w_end = jnp.cumsum(n_e)
    w_start = w_end - n_e
    total = w_end[-1]
    w = jnp.arange(nw, dtype=jnp.int32)
    wc = jnp.minimum(w, total - 1)
    e_w = jnp.sum((wc[:, None] >= w_end[None, :]).astype(jnp.int32), axis=-1)
    sel = (e_w[:, None] == jnp.arange(N_EXPERTS, dtype=jnp.int32)).astype(jnp.int32)
    pick = lambda v: jnp.sum(sel * v[None, :], axis=-1)
    off_w = pick(offs)
    tile_w = pick(first_tile) + (wc - pick(w_start))
    lo = jnp.clip(off_w - tile_w * tm, 0, tm)
    hi = jnp.clip(off_w + pick(counts) - tile_w * tm, 0, tm)
    valid = w < total
    lo = jnp.where(valid, lo, 0)
    hi = jnp.where(valid, hi, 0)
    prev_tile = jnp.concatenate([jnp.full((1,), -1, tile_w.dtype), tile_w[:-1]])
    first = tile_w != prev_tile
    i32 = lambda a: a.astype(jnp.int32)
    return (i32(tile_w), i32(e_w + layer * N_EXPERTS), i32(lo), i32(hi), i32(first))


def _grid_pos_embed(rows, d):
    t = jnp.arange(rows * GRID_W)
    row = (t // GRID_W).astype(F32)
    col = (t % GRID_W).astype(F32)
    nf = d // 4
    freq = 1.0 / (10000.0 ** (jnp.arange(nf, dtype=F32) / nf))
    er = row[:, None] * freq
    ec = col[:, None] * freq
    return jnp.concatenate([jnp.sin(er), jnp.cos(er), jnp.sin(ec), jnp.cos(ec)], axis=-1)


def _block_diag(w, nblk):
    *lead, H, d, _ = w.shape
    w = w.reshape(*lead, H // nblk, nblk, d, d)
    eye = jnp.eye(nblk, dtype=w.dtype)
    out = jnp.einsum('...gij,gh->...gihj', w, eye)
    return out.reshape(*lead, H // nblk, nblk * d, nblk * d)


def kernel(x_prompt, x_sample, state_lru, c, c_ctx, w_ada, b_ada, g_mix, g_ffn, g_final, w_in, w_out, conv_w, conv_b, lru_wa, lru_ba, lru_wx, lru_bx, lru_lambda, tmlp_ws, tmlp_b, fnet_w, router_g, router_g_b, router_e, router_e_b, e_w1, e_w3, e_w2):
    bp, lp, _ = x_prompt.shape
    bs, ls, _ = x_sample.shape
    n_ctx = bp * lp
    n_lat = bs * ls
    n_tok = n_ctx + n_lat

    cond8 = jnp.concatenate([c_ctx[None, :], c, jnp.zeros((SUBLANES - 1 - bs, D_MODEL), F32)], axis=0)
    pos = _grid_pos_embed(ls // GRID_W, D_MODEL)
    w_in_b = w_in.astype(BF16)
    w_out_b = w_out.astype(BF16)
    heads_per_cb = LRU_CB // RNN_HEAD_DIM
    wa_bd = (0.5 * _block_diag(lru_wa, heads_per_cb)).astype(BF16)
    wx_bd = (0.5 * _block_diag(lru_wx, heads_per_cb)).astype(BF16)
    ws_b = tmlp_ws.astype(BF16)
    bt = jnp.swapaxes(tmlp_b, 1, 2)
    fnet_bd = _block_diag(fnet_w, D_FNET // FNET_GROUP_DIM)[:, 0]
    wr = jnp.concatenate([router_g, router_e,
                          jnp.zeros((DEPTH, D_MODEL, ROUTER_LANES - N_EGROUPS - N_EXPERTS), F32)], axis=-1)
    br = jnp.concatenate([router_g_b, router_e_b,
                          jnp.zeros((DEPTH, ROUTER_LANES - N_EGROUPS - N_EXPERTS), F32)], axis=-1)
    ew1 = e_w1.reshape(DEPTH * N_EXPERTS, D_MODEL, D_EXPERT)
    ew3 = e_w3.reshape(DEPTH * N_EXPERTS, D_MODEL, D_EXPERT)
    ew2 = e_w2.reshape(DEPTH * N_EXPERTS, D_EXPERT, D_MODEL)

    mods = _modulation(cond8, w_ada, b_ada)
    fa, fb = _fnet_prep(fnet_bd)
    ct_p, st_p = _dft_tables(lp)
    ct_s, st_s = _dft_tables(ls)

    nw = 2 * n_tok // EXPERT_TILE + N_EXPERTS
    h0_p = jnp.zeros((bp, 2, D_RNN), F32)
    g_mix3 = g_mix[:, None, :]
    g_ffn3 = g_ffn[:, None, :]
    conv_b3 = conv_b[:, None, :]
    br3 = br[:, None, :]

    xp, xs = x_prompt, x_sample
    states = []
    for l in range(DEPTH):
        paths = []
        cnt = jnp.zeros((SUBLANES, ROUTER_LANES), F32)
        for (x, ct, st, h0, row_base, row_step, is_lat) in (
                (xp, ct_p, st_p, h0_p, 0, 0, False),
                (xs, ct_s, st_s, state_lru[:, l], 1, 1, True)):
            pe = pos if (is_lat and l == 0) else None
            zr, zuv, va, vb = _in_proj(x, pe, mods, g_mix3, w_in_b, fa, fb,
                                       layer=l, row_base=row_base, row_step=row_step)
            yr, st_new = _lru_mixer(zr, conv_w, conv_b3, wa_bd, wx_bd, lru_ba, lru_bx, lru_lambda, h0, layer=l)
            yf = _dft_apply(ct, st, va, vb)
            x1, h2, ri, cnt = _out_proj(x, pe, yr, zuv, yf, mods, ws_b, bt, w_out_b, g_ffn3, wr, br3, cnt,
                                        layer=l, row_base=row_base, row_step=row_step)
            paths.append((x1, h2, ri, st_new))
        states.append(paths[0][3])

        pos_all, counts, offs = _route(paths[0][2].reshape(n_ctx, ROUTER_LANES),
                                       paths[1][2].reshape(n_lat, ROUTER_LANES), cnt)
        items = _work_items(counts, offs, nw, l)
        pos_tiles = jnp.swapaxes(pos_all.reshape(n_tok // ROW_TILE, ROW_TILE, 2), 1, 2)
        nct = n_ctx // ROW_TILE
        xsorted = _dispatch(pos_tiles, paths[0][1].reshape(n_ctx, D_MODEL), paths[1][1].reshape(n_lat, D_MODEL))
        ysorted = _experts(items, xsorted, ew1, ew3, ew2)
        gfin = g_final[None, :]
        final = l == DEPTH - 1
        xp = _combine(pos_tiles[:nct], paths[0][0].reshape(n_ctx, D_MODEL),
                      paths[0][2].reshape(n_ctx, ROUTER_LANES), mods, gfin, ysorted, layer=l,
                      row_base=0, row_step=0, tiles_per_seq=lp // ROW_TILE, final=final).reshape(bp, lp, D_MODEL)
        xs = _combine(pos_tiles[nct:], paths[1][0].reshape(n_lat, D_MODEL),
                      paths[1][2].reshape(n_lat, ROUTER_LANES), mods, gfin, ysorted, layer=l,
                      row_base=1, row_step=1, tiles_per_seq=ls // ROW_TILE, final=final).reshape(bs, ls, D_MODEL)

    new_state = jnp.stack(states, axis=1).astype(x_prompt.dtype)
    return (xp, xs, new_state)
```

```python
import functools
import math

import jax
import jax.numpy as jnp
from jax import lax
from jax.experimental import pallas as pl
from jax.experimental.pallas import tpu as pltpu

F32 = jnp.float32
BF16 = jnp.bfloat16

D_MODEL = 1024
DEPTH = 2
GRID_W = 64
D_RNN = 512
RNN_HEAD_DIM = 64
CONV_W = 4
LRU_C = 8.0
D_TMLP = 256
TMLP_HEADS = 4
CHUNK = 128
D_FNET = 256
FNET_GROUP_DIM = 64
D_IN = 2 * D_RNN + 2 * D_TMLP + D_FNET
N_EGROUPS = 4
N_EPG = 8
N_EXPERTS = N_EGROUPS * N_EPG
D_EXPERT = 512
N_MOD = 6
EPS = 1e-6

LANES = 128
SUBLANES = 8
LANE_TILES = D_MODEL // LANES
assert LANE_TILES == SUBLANES
LRU_CB = LANES
ROW_TILE = 256
EXPERT_TILE = 256
ROUTER_LANES = LANES
NEG_BIG = -1e30
VMEM_LIMIT = 56 * 1024 * 1024


def _cparams(sem):
    return pltpu.CompilerParams(dimension_semantics=sem, vmem_limit_bytes=VMEM_LIMIT)


def _mod_kernel(c_ref, w_ref, b_ref, o_ref):
    c = c_ref[...]
    s = c * jax.nn.sigmoid(c)
    o_ref[0] = jnp.dot(s.astype(BF16), w_ref[0].astype(BF16), preferred_element_type=F32) + b_ref[0]


def _modulation(cond8, w_ada, b_ada):
    tn = 1536
    return pl.pallas_call(
        _mod_kernel,
        grid=(DEPTH, N_MOD * D_MODEL // tn),
        in_specs=[
            pl.BlockSpec((SUBLANES, D_MODEL), lambda l, j: (0, 0)),
            pl.BlockSpec((1, D_MODEL, tn), lambda l, j: (l, 0, j)),
            pl.BlockSpec((1, 1, tn), lambda l, j: (l, 0, j)),
        ],
        out_specs=pl.BlockSpec((1, SUBLANES, tn), lambda l, j: (l, 0, j)),
        out_shape=jax.ShapeDtypeStruct((DEPTH, SUBLANES, N_MOD * D_MODEL), F32),
        compiler_params=_cparams(("arbitrary", "arbitrary")),
        name="modulation",
    )(cond8, w_ada, b_ada.reshape(DEPTH, 1, N_MOD * D_MODEL))


def _fprep_kernel(w_ref, a_ref, b_ref):
    r = lax.broadcasted_iota(jnp.int32, (D_FNET, D_FNET), 0)
    c = lax.broadcasted_iota(jnp.int32, (D_FNET, D_FNET), 1)
    same = (r >> 6) == (c >> 6)
    ph = ((r & 63) * (c & 63)) & 63
    ang = ph.astype(F32) * (2.0 * math.pi / FNET_GROUP_DIM)
    scale = 1.0 / math.sqrt(FNET_GROUP_DIM)
    cm = jnp.where(same, jnp.cos(ang) * scale, 0.0)
    sm = jnp.where(same, jnp.sin(ang) * scale, 0.0)
    w = w_ref[0]
    a_ref[0] = jnp.dot(cm, w, precision=lax.Precision.HIGHEST, preferred_element_type=F32).astype(BF16)
    b_ref[0] = jnp.dot(sm, w, precision=lax.Precision.HIGHEST, preferred_element_type=F32).astype(BF16)


def _fnet_prep(wbd):
    spec = pl.BlockSpec((1, D_FNET, D_FNET), lambda l: (l, 0, 0))
    return pl.pallas_call(
        _fprep_kernel,
        grid=(DEPTH,),
        in_specs=[spec],
        out_specs=[spec, spec],
        out_shape=[jax.ShapeDtypeStruct((DEPTH, D_FNET, D_FNET), BF16)] * 2,
        compiler_params=_cparams(("arbitrary",)),
        name="fnet_prep",
    )(wbd)


TABLE_ROWS = 64


def _base_table_kernel(cj_ref, sj_ref, cm_ref, sm_ref, *, L):
    j = lax.broadcasted_iota(jnp.int32, (TABLE_ROWS, L), 0)
    n = lax.broadcasted_iota(jnp.int32, (TABLE_ROWS, L), 1)
    w = 2.0 * math.pi / L
    fine = ((j * n) & (L - 1)).astype(F32) * w
    coarse = ((j * TABLE_ROWS * n) & (L - 1)).astype(F32) * w
    scale = 1.0 / math.sqrt(L)
    cj_ref[...] = jnp.cos(fine)
    sj_ref[...] = jnp.sin(fine)
    cm_ref[...] = jnp.cos(coarse) * scale
    sm_ref[...] = jnp.sin(coarse) * scale


def _table_kernel(cj_ref, sj_ref, cm_ref, sm_ref, c_ref, s_ref):
    m = pl.program_id(0)
    c0 = cm_ref[pl.ds(m, 1), :]
    s0 = sm_ref[pl.ds(m, 1), :]
    cj = cj_ref[...]
    sj = sj_ref[...]
    c_ref[...] = (cj * c0 - sj * s0).astype(BF16)
    s_ref[...] = (-(sj * c0 + cj * s0)).astype(BF16)


def _dft_tables(L):
    small = pl.BlockSpec((TABLE_ROWS, L), lambda *_: (0, 0))
    base = pl.pallas_call(
        functools.partial(_base_table_kernel, L=L),
        out_specs=[small] * 4,
        out_shape=[jax.ShapeDtypeStruct((TABLE_ROWS, L), F32)] * 4,
        compiler_params=pltpu.CompilerParams(vmem_limit_bytes=VMEM_LIMIT),
        name=f"dft_base_tables_{L}",
    )()
    spec = pl.BlockSpec((TABLE_ROWS, L), lambda i: (i, 0))
    return pl.pallas_call(
        _table_kernel,
        grid=(L // TABLE_ROWS,),
        in_specs=[small] * 4,
        out_specs=[spec, spec],
        out_shape=[jax.ShapeDtypeStruct((L, L), BF16)] * 2,
        compiler_params=_cparams(("arbitrary",)),
        name=f"dft_tables_{L}",
    )(*base)


def _dft_kernel(c_ref, s_ref, va_ref, vb_ref, o_ref):
    o_ref[...] = (jnp.dot(c_ref[...], va_ref[...], preferred_element_type=F32)
                  + jnp.dot(s_ref[...], vb_ref[...], preferred_element_type=F32))


def _dft_apply(ct, st, va, vb):
    L, ncols = va.shape
    tk, tc = 256, 1024
    return pl.pallas_call(
        _dft_kernel,
        grid=(L // tk, ncols // tc),
        in_specs=[
            pl.BlockSpec((tk, L), lambda i, j: (i, 0)),
            pl.BlockSpec((tk, L), lambda i, j: (i, 0)),
            pl.BlockSpec((L, tc), lambda i, j: (0, j)),
            pl.BlockSpec((L, tc), lambda i, j: (0, j)),
        ],
        out_specs=pl.BlockSpec((tk, tc), lambda i, j: (i, j)),
        out_shape=jax.ShapeDtypeStruct((L, ncols), F32),
        compiler_params=_cparams(("arbitrary", "arbitrary")),
        name=f"dft_apply_{L}",
    )(ct, st, va, vb)


def _rms_mod(x, g, scale, shift):
    ms = jnp.mean(x * x, axis=-1, keepdims=True)
    return (x * lax.rsqrt(ms + EPS)) * g * (1.0 + scale) + shift


def _in_kernel(*refs, add_pos, row_base, row_step):
    if add_pos:
        x_ref, pos_ref, mod_ref, g_ref, w_ref, a_ref, b_ref, zr_ref, zuv_ref, va_ref, vb_ref = refs
    else:
        x_ref, mod_ref, g_ref, w_ref, a_ref, b_ref, zr_ref, zuv_ref, va_ref, vb_ref = refs
    x = x_ref[0]
    if add_pos:
        x = x + pos_ref[...]
    row = row_base + pl.program_id(0) * row_step
    m = mod_ref[pl.ds(row, 1), :]
    h = _rms_mod(x, g_ref[...], m[:, D_MODEL:2 * D_MODEL], m[:, 0:D_MODEL])
    z = jnp.dot(h.astype(BF16), w_ref[...], preferred_element_type=F32)
    zr_ref[0] = z[:, 0:2 * D_RNN]
    zuv_ref[0] = z[:, 2 * D_RNN:2 * D_RNN + 2 * D_TMLP]
    zf = z[:, 2 * D_RNN + 2 * D_TMLP:D_IN].astype(BF16)
    va_ref[...] = jnp.dot(zf, a_ref[...], preferred_element_type=F32).astype(BF16)
    vb_ref[...] = jnp.dot(zf, b_ref[...], preferred_element_type=F32).astype(BF16)


def _in_proj(x, pos, mods, g, w_in_b, fa, fb, *, layer, row_base, row_step):
    nseq, L, _ = x.shape
    tm = ROW_TILE
    add_pos = pos is not None
    in_specs = [pl.BlockSpec((1, tm, D_MODEL), lambda b, t: (b, t, 0))]
    args = [x]
    if add_pos:
        in_specs.append(pl.BlockSpec((tm, D_MODEL), lambda b, t: (t, 0)))
        args.append(pos)
    in_specs += [
        _layer_spec((SUBLANES, N_MOD * D_MODEL), layer),
        _layer_spec((1, D_MODEL), layer),
        _layer_spec((D_MODEL, D_IN), layer),
        _layer_spec((D_FNET, D_FNET), layer),
        _layer_spec((D_FNET, D_FNET), layer),
    ]
    args += [mods, g, w_in_b, fa, fb]
    return pl.pallas_call(
        functools.partial(_in_kernel, add_pos=add_pos, row_base=row_base, row_step=row_step),
        grid=(nseq, L // tm),
        in_specs=in_specs,
        out_specs=[
            pl.BlockSpec((1, tm, 2 * D_RNN), lambda b, t: (b, t, 0)),
            pl.BlockSpec((1, tm, 2 * D_TMLP), lambda b, t: (b, t, 0)),
            pl.BlockSpec((tm, D_FNET), lambda b, t: (t, b)),
            pl.BlockSpec((tm, D_FNET), lambda b, t: (t, b)),
        ],
        out_shape=[
            jax.ShapeDtypeStruct((nseq, L, 2 * D_RNN), F32),
            jax.ShapeDtypeStruct((nseq, L, 2 * D_TMLP), F32),
            jax.ShapeDtypeStruct((L, nseq * D_FNET), BF16),
            jax.ShapeDtypeStruct((L, nseq * D_FNET), BF16),
        ],
        compiler_params=_cparams(("arbitrary", "arbitrary")),
        name=f"in_proj_{L}",
    )(*args)


def _gelu_tanh(x):
    return 0.5 * x * (1.0 + jnp.tanh(math.sqrt(2.0 / math.pi) * (x + 0.044715 * (x * x * x))))


def _rows_to_tile(rows):
    sub = lax.broadcasted_iota(jnp.int32, (SUBLANES, LANES), 0)
    out = jnp.zeros((SUBLANES, LANES), F32)
    for s, r in enumerate(rows):
        out = jnp.where(sub == s, jnp.broadcast_to(r, (SUBLANES, LANES)), out)
    return out


def _lru_kernel(xr_ref, gr_ref, cw_ref, cb_ref, wa_ref, wx_ref, ba_ref, bx_ref, lam_ref, h0_ref,
                y_ref, st_ref, xnat, pext, af, bf, ab, bb, hnat, *, L):
    S = L // SUBLANES
    pitch = S + SUBLANES
    n = S * SUBLANES
    chunk = 256

    for s in range(SUBLANES):
        xnat[s * pitch:s * pitch + S, :] = xr_ref[0, s * S:(s + 1) * S, :]

    def perm_in(j, c):
        dst = pl.multiple_of((j + 2) * SUBLANES, SUBLANES)
        pext[pl.ds(dst, SUBLANES), :] = xnat[pl.ds(j, SUBLANES, stride=pitch), :]
        return c

    lax.fori_loop(0, S, perm_in, 0, unroll=8)

    sub = lax.broadcasted_iota(jnp.int32, (SUBLANES, LANES), 0)

    def from_prev_segment(v):
        return jnp.where(sub == 0, 0.0, pltpu.roll(v, 1, axis=0))

    def from_next_segment(v):
        return jnp.where(sub == SUBLANES - 1, 0.0, pltpu.roll(v, SUBLANES - 1, axis=0))

    pext[0:8, :] = from_prev_segment(pext[S * 8:(S + 1) * 8, :])
    pext[8:16, :] = from_prev_segment(pext[(S + 1) * 8:(S + 2) * 8, :])
    pext[(S + 2) * 8:(S + 3) * 8, :] = from_next_segment(pext[16:24, :])

    lam = lam_ref[...]
    nl = -lam
    sp = jnp.maximum(nl, 0.0) + jnp.log1p(jnp.exp(-jnp.abs(nl)))
    c_la = (-0.5 * LRU_C) * sp
    ba_h = 0.5 * ba_ref[...]
    bx_h = 0.5 * bx_ref[...]
    a_refs = (af, ab)
    b_refs = (bf, bb)

    def gates(i, c):
        base = pl.multiple_of(i * chunk, chunk)
        xc = (cw_ref[0:1, :] * pext[pl.ds(base, chunk), :]
              + cw_ref[1:2, :] * pext[pl.ds(base + 8, chunk), :]
              + cw_ref[2:3, :] * pext[pl.ds(base + 16, chunk), :]
              + cw_ref[3:4, :] * pext[pl.ds(base + 24, chunk), :]
              + cb_ref[...])
        xcb = xc.astype(BF16)
        xh = 0.5 * xc
        for d in range(2):
            tr = jnp.tanh(jnp.dot(xcb, wa_ref[d, 0], preferred_element_type=F32) + ba_h[d:d + 1, :])
            ti = jnp.tanh(jnp.dot(xcb, wx_ref[d, 0], preferred_element_type=F32) + bx_h[d:d + 1, :])
            log_a = c_la[d:d + 1, :] * (1.0 + tr)
            a = jnp.exp(log_a)
            v = jnp.tanh(log_a) * (-1.0 - a * a)
            coef = jnp.where(v > 0.0, v * lax.rsqrt(v), 0.0)
            a_refs[d][pl.ds(base, chunk), :] = a
            b_refs[d][pl.ds(base, chunk), :] = coef * ((1.0 + ti) * xh)
        return c

    lax.fori_loop(0, n // chunk, gates, 0)

    def scan(j, carry):
        hf, pf, hb, pb = carry
        jf = pl.multiple_of(j * SUBLANES, SUBLANES)
        jb = pl.multiple_of((S - 1 - j) * SUBLANES, SUBLANES)
        a1 = af[pl.ds(jf, SUBLANES), :]
        hf = a1 * hf + bf[pl.ds(jf, SUBLANES), :]
        pf = a1 * pf
        bf[pl.ds(jf, SUBLANES), :] = hf
        af[pl.ds(jf, SUBLANES), :] = pf
        a2 = ab[pl.ds(jb, SUBLANES), :]
        hb = a2 * hb + bb[pl.ds(jb, SUBLANES), :]
        pb = a2 * pb
        bb[pl.ds(jb, SUBLANES), :] = hb
        ab[pl.ds(jb, SUBLANES), :] = pb
        return hf, pf, hb, pb

    zero = jnp.zeros((SUBLANES, LANES), F32)
    one = jnp.ones((SUBLANES, LANES), F32)
    hf, pf, hb, pb = lax.fori_loop(0, S, scan, (zero, one, zero, one), unroll=8)

    rows_f = [h0_ref[0, 0:1, :]]
    for s in range(1, SUBLANES):
        rows_f.append(pf[s - 1:s, :] * rows_f[-1] + hf[s - 1:s, :])
    st_ref[0, 0:1, :] = pf[7:8, :] * rows_f[7] + hf[7:8, :]
    rows_b = [None] * SUBLANES
    rows_b[7] = h0_ref[0, 1:2, :]
    for s in range(SUBLANES - 2, -1, -1):
        rows_b[s] = pb[s + 1:s + 2, :] * rows_b[s + 1] + hb[s + 1:s + 2, :]
    st_ref[0, 1:2, :] = pb[0:1, :] * rows_b[0] + hb[0:1, :]
    init_f = _rows_to_tile(rows_f)
    init_b = _rows_to_tile(rows_b)

    def perm_out(j, c):
        src = pl.multiple_of(j * SUBLANES, SUBLANES)
        v = (bf[pl.ds(src, SUBLANES), :] + af[pl.ds(src, SUBLANES), :] * init_f
             + bb[pl.ds(src, SUBLANES), :] + ab[pl.ds(src, SUBLANES), :] * init_b)
        hnat[pl.ds(j, SUBLANES, stride=pitch), :] = v
        return c

    lax.fori_loop(0, S, perm_out, 0, unroll=8)

    for s in range(SUBLANES):
        g = gr_ref[0, s * S:(s + 1) * S, :]
        y_ref[0, s * S:(s + 1) * S, :] = hnat[s * pitch:s * pitch + S, :] * _gelu_tanh(g)


def _lru_mixer(zr, conv_w, conv_b, wa_bd, wx_bd, ba, bx, lam, h0, *, layer):
    nseq, L, _ = zr.shape
    ncb = D_RNN // LRU_CB
    S = L // SUBLANES
    pitch = S + SUBLANES
    vec2 = pl.BlockSpec((None, 2, LRU_CB), lambda b, c: (layer, 0, c))
    wspec = pl.BlockSpec((None, 2, 1, LRU_CB, LRU_CB), lambda b, c: (layer, 0, c, 0, 0))
    return pl.pallas_call(
        functools.partial(_lru_kernel, L=L),
        grid=(nseq, ncb),
        in_specs=[
            pl.BlockSpec((1, L, LRU_CB), lambda b, c: (b, 0, c)),
            pl.BlockSpec((1, L, LRU_CB), lambda b, c: (b, 0, c + ncb)),
            pl.BlockSpec((None, CONV_W, LRU_CB), lambda b, c: (layer, 0, c)),
            pl.BlockSpec((None, 1, LRU_CB), lambda b, c: (layer, 0, c)),
            wspec, wspec, vec2, vec2, vec2,
            pl.BlockSpec((1, 2, LRU_CB), lambda b, c: (b, 0, c)),
        ],
        out_specs=[
            pl.BlockSpec((1, L, LRU_CB), lambda b, c: (b, 0, c)),
            pl.BlockSpec((1, 2, LRU_CB), lambda b, c: (b, 0, c)),
        ],
        out_shape=[
            jax.ShapeDtypeStruct((nseq, L, D_RNN), F32),
            jax.ShapeDtypeStruct((nseq, 2, D_RNN), F32),
        ],
        scratch_shapes=[
            pltpu.VMEM((SUBLANES * pitch, LANES), F32),
            pltpu.VMEM(((S + 3) * SUBLANES, LANES), F32),
            pltpu.VMEM((L, LANES), F32),
            pltpu.VMEM((L, LANES), F32),
            pltpu.VMEM((L, LANES), F32),
            pltpu.VMEM((L, LANES), F32),
            pltpu.VMEM((SUBLANES * pitch, LANES), F32),
        ],
        compiler_params=_cparams(("arbitrary", "arbitrary")),
        name=f"lru_mixer_{L}",
    )(zr, zr, conv_w, conv_b, wa_bd, wx_bd, ba, bx, lam, h0)


def _out_kernel(*refs, add_pos, row_base, row_step):
    if add_pos:
        (x_ref, pos_ref, yr_ref, zuv_ref, yf_ref, mod_ref, ws_ref, bt_ref, wo_ref, g_ref, wr_ref, br_ref, cin_ref,
         x1_ref, h2_ref, ri_ref, cnt_ref, carry) = refs
    else:
        (x_ref, yr_ref, zuv_ref, yf_ref, mod_ref, ws_ref, bt_ref, wo_ref, g_ref, wr_ref, br_ref, cin_ref,
         x1_ref, h2_ref, ri_ref, cnt_ref, carry) = refs
    tm = x_ref.shape[1]
    x = x_ref[0]
    if add_pos:
        x = x + pos_ref[...]
    row = row_base + pl.program_id(0) * row_step
    m = mod_ref[pl.ds(row, 1), :]
    g1 = m[:, 2 * D_MODEL:3 * D_MODEL]
    sh2 = m[:, 3 * D_MODEL:4 * D_MODEL]
    sc2 = m[:, 4 * D_MODEL:5 * D_MODEL]

    head = lax.broadcasted_iota(jnp.int32, (CHUNK, D_TMLP), 1) >> 6
    yt_parts = []
    for ci in range(tm // CHUNK):
        u = zuv_ref[0, ci * CHUNK:(ci + 1) * CHUNK, 0:D_TMLP]
        v = zuv_ref[0, ci * CHUNK:(ci + 1) * CHUNK, D_TMLP:2 * D_TMLP].astype(BF16)
        s = jnp.zeros((CHUNK, D_TMLP), F32)
        for h in range(TMLP_HEADS):
            sh = jnp.dot(ws_ref[h], v, preferred_element_type=F32) + bt_ref[:, h:h + 1]
            s = jnp.where(head == h, sh, s)
        yt_parts.append(u * s)
    yt = jnp.concatenate(yt_parts, axis=0) if len(yt_parts) > 1 else yt_parts[0]

    y = (jnp.dot(yr_ref[0].astype(BF16), wo_ref[0:D_RNN, :], preferred_element_type=F32)
         + jnp.dot(yt.astype(BF16), wo_ref[D_RNN:D_RNN + D_TMLP, :], preferred_element_type=F32)
         + jnp.dot(yf_ref[...].astype(BF16), wo_ref[D_RNN + D_TMLP:D_MODEL, :], preferred_element_type=F32))
    x1 = x + g1 * y
    x1_ref[0] = x1
    h2 = _rms_mod(x1, g_ref[...], sc2, sh2)
    _store_token_major(h2_ref, h2)

    wr = wr_ref[...]
    w_hi = wr.astype(BF16)
    w_lo = (wr - w_hi.astype(F32)).astype(BF16)
    h_hi = h2.astype(BF16)
    h_lo = (h2 - h_hi.astype(F32)).astype(BF16)
    p_hi = jnp.dot(h_hi, jnp.concatenate([w_hi, w_lo], axis=-1), preferred_element_type=F32)
    p_lo = jnp.dot(h_lo, w_hi, preferred_element_type=F32)
    logits = p_hi[:, 0:ROUTER_LANES] + p_hi[:, ROUTER_LANES:2 * ROUTER_LANES] + p_lo + br_ref[...]
    lane = lax.broadcasted_iota(jnp.int32, (tm, ROUTER_LANES), 1)
    lane_f = lane.astype(F32)
    is_g = lane < N_EGROUPS
    gl = jnp.where(is_g, logits, NEG_BIG)
    gmax = jnp.max(gl, axis=-1, keepdims=True)
    gsel = jnp.min(jnp.where(gl == gmax, lane_f, 1e4), axis=-1, keepdims=True)
    pg = 1.0 / jnp.sum(jnp.where(is_g, jnp.exp(logits - gmax), 0.0), axis=-1, keepdims=True)
    grp_f = ((lane - N_EGROUPS) >> 3).astype(F32)
    emask = (lane >= N_EGROUPS) & (lane < N_EGROUPS + N_EXPERTS) & (grp_f == gsel)
    el = jnp.where(emask, logits, NEG_BIG)
    v1 = jnp.max(el, axis=-1, keepdims=True)
    i1 = jnp.min(jnp.where(el == v1, lane_f, 1e4), axis=-1, keepdims=True)
    el2 = jnp.where(lane_f == i1, NEG_BIG, el)
    v2 = jnp.max(el2, axis=-1, keepdims=True)
    i2 = jnp.min(jnp.where(el2 == v2, lane_f, 1e4), axis=-1, keepdims=True)
    e2x = jnp.exp(v2 - v1)
    fw1 = 1.0 / (1.0 + e2x)
    fw2 = e2x * fw1
    @pl.when((pl.program_id(0) == 0) & (pl.program_id(1) == 0))
    def _():
        carry[...] = cin_ref[...]

    e1 = i1 - N_EGROUPS
    e2 = i2 - N_EGROUPS
    m1 = lane_f == e1
    m2 = lane_f == e2
    oh = jnp.where(m1 | m2, 1.0, 0.0)
    r_i = lax.broadcasted_iota(jnp.int32, (tm, tm), 0)
    c_i = lax.broadcasted_iota(jnp.int32, (tm, tm), 1)
    tri = jnp.where(c_i < r_i, 1.0, 0.0).astype(BF16)
    before = jnp.dot(tri, oh.astype(BF16), preferred_element_type=F32) + carry[0:1, :]
    rank1 = jnp.sum(jnp.where(m1, before, 0.0), axis=-1, keepdims=True)
    rank2 = jnp.sum(jnp.where(m2, before, 0.0), axis=-1, keepdims=True)
    total = carry[0:1, :] + jnp.sum(oh, axis=0, keepdims=True)
    carry[0:1, :] = total
    cnt_ref[...] = jnp.broadcast_to(total, cnt_ref.shape)

    vals = (e1, e2, pg * fw1, pg * fw2, rank1, rank2)
    ri = jnp.zeros((tm, ROUTER_LANES), F32)
    for k, v in enumerate(vals):
        ri = jnp.where(lane == k, v, ri)
    ri_ref[0] = ri


def _layer_spec(shape, layer):
    zeros = (0,) * len(shape)
    return pl.BlockSpec((None, *shape), lambda *_: (layer, *zeros))


def _out_proj(x, pos, yr, zuv, yf, mods, ws_b, bt, wo_b, g, wr, br, cin, *, layer, row_base, row_step):
    nseq, L, _ = x.shape
    tm = ROW_TILE
    add_pos = pos is not None
    in_specs = [pl.BlockSpec((1, tm, D_MODEL), lambda b, t: (b, t, 0))]
    args = [x]
    if add_pos:
        in_specs.append(pl.BlockSpec((tm, D_MODEL), lambda b, t: (t, 0)))
        args.append(pos)
    in_specs += [
        pl.BlockSpec((1, tm, D_RNN), lambda b, t: (b, t, 0)),
        pl.BlockSpec((1, tm, 2 * D_TMLP), lambda b, t: (b, t, 0)),
        pl.BlockSpec((tm, D_FNET), lambda b, t: (t, b)),
        _layer_spec((SUBLANES, N_MOD * D_MODEL), layer),
        _layer_spec((TMLP_HEADS, CHUNK, CHUNK), layer),
        _layer_spec((CHUNK, TMLP_HEADS), layer),
        _layer_spec((D_MODEL, D_MODEL), layer),
        _layer_spec((1, D_MODEL), layer),
        _layer_spec((D_MODEL, ROUTER_LANES), layer),
        _layer_spec((1, ROUTER_LANES), layer),
        pl.BlockSpec((SUBLANES, ROUTER_LANES), lambda b, t: (0, 0)),
    ]
    args += [yr, zuv, yf, mods, ws_b, bt, wo_b, g, wr, br, cin]
    tok = pl.BlockSpec((1, tm, D_MODEL), lambda b, t: (b, t, 0))
    return pl.pallas_call(
        functools.partial(_out_kernel, add_pos=add_pos, row_base=row_base, row_step=row_step),
        grid=(nseq, L // tm),
        in_specs=in_specs,
        out_specs=[tok, pl.BlockSpec((tm * LANE_TILES, LANES), lambda b, t: (b * (L // tm) + t, 0)),
                   pl.BlockSpec((1, tm, ROUTER_LANES), lambda b, t: (b, t, 0)),
                   pl.BlockSpec((SUBLANES, ROUTER_LANES), lambda b, t: (0, 0))],
        out_shape=[
            jax.ShapeDtypeStruct((nseq, L, D_MODEL), F32),
            jax.ShapeDtypeStruct((nseq * L * LANE_TILES, LANES), F32),
            jax.ShapeDtypeStruct((nseq, L, ROUTER_LANES), F32),
            jax.ShapeDtypeStruct((SUBLANES, ROUTER_LANES), F32),
        ],
        scratch_shapes=[pltpu.VMEM((SUBLANES, ROUTER_LANES), F32)],
        compiler_params=_cparams(("arbitrary", "arbitrary")),
        name=f"out_proj_{L}",
    )(*args)


def _store_token_major(ref, x):
    tm = x.shape[0]
    for j in range(LANE_TILES):
        ref[pl.ds(j, tm, stride=LANE_TILES), :] = x[:, j * LANES:(j + 1) * LANES]


def _load_token_major(ref):
    tm = ref.shape[0] // LANE_TILES
    return jnp.concatenate([ref[pl.ds(j, tm, stride=LANE_TILES), :] for j in range(LANE_TILES)], axis=-1)


def _token_copy(src_ref, src_tok, dst_ref, dst_tok, sem):
    return pltpu.make_async_copy(src_ref.at[pl.ds(pl.multiple_of(src_tok, LANE_TILES), LANE_TILES)],
                                 dst_ref.at[pl.ds(pl.multiple_of(dst_tok, LANE_TILES), LANE_TILES)], sem)


def _scatter_rows(pos_ref, h_ref, xs_ref, sem):
    rows = h_ref.shape[0]

    for r in range(0, rows, LANE_TILES):
        _token_copy(h_ref, r, xs_ref, pos_ref[0, 0, r // LANE_TILES], sem).start(priority=0)
        _token_copy(h_ref, r, xs_ref, pos_ref[0, 1, r // LANE_TILES], sem).start(priority=1)
    for _ in range(2):
        pltpu.make_async_copy(h_ref, xs_ref.at[pl.ds(0, rows)], sem).wait()


def _dispatch_kernel(pos_ref, hp_ref, hs_ref, xs_ref, sem, *, n_first):
    i = pl.program_id(0)

    @pl.when(i < n_first)
    def _():
        _scatter_rows(pos_ref, hp_ref, xs_ref, sem)

    @pl.when(i >= n_first)
    def _():
        _scatter_rows(pos_ref, hs_ref, xs_ref, sem)


def _dispatch(pos, h_first, h_second):
    tm = ROW_TILE
    blk = tm * LANE_TILES
    n_first = h_first.shape[0] // blk
    n_second = h_second.shape[0] // blk
    nrows = 2 * (h_first.shape[0] + h_second.shape[0])
    return pl.pallas_call(
        functools.partial(_dispatch_kernel, n_first=n_first),
        grid=(n_first + n_second,),
        in_specs=[
            pl.BlockSpec((1, 2, tm), lambda i: (i, 0, 0), memory_space=pltpu.SMEM),
            pl.BlockSpec((blk, LANES), lambda i: (jnp.minimum(i, n_first - 1), 0)),
            pl.BlockSpec((blk, LANES), lambda i: (jnp.maximum(i - n_first, 0), 0)),
        ],
        out_specs=pl.BlockSpec(memory_space=pl.ANY),
        out_shape=jax.ShapeDtypeStruct((nrows, LANES), F32),
        scratch_shapes=[pltpu.SemaphoreType.DMA(())],
        compiler_params=_cparams(("arbitrary",)),
        name="moe_dispatch",
    )(pos, h_first, h_second)


def _expert_kernel(wt_ref, we_ref, lo_ref, hi_ref, first_ref, xs_ref, w1_ref, w3_ref, w2_ref, ys_ref,
                   w1b, w3b, w2b, prev):
    w = pl.program_id(0)

    @pl.when(w == 0)
    def _():
        prev[0] = -1

    e = we_ref[w]

    @pl.when(e != prev[0])
    def _():
        w1b[...] = w1_ref[0].astype(BF16)
        w3b[...] = w3_ref[0].astype(BF16)
        w2b[...] = w2_ref[0].astype(BF16)
        prev[0] = e

    @pl.when(first_ref[w] == 1)
    def _():
        ys_ref[...] = jnp.zeros_like(ys_ref)

    lo = lo_ref[w]
    hi = hi_ref[w]

    @pl.when(hi > lo)
    def _():
        x = _load_token_major(xs_ref).astype(BF16)
        a = jnp.dot(x, w1b[...], preferred_element_type=F32)
        b = jnp.dot(x, w3b[...], preferred_element_type=F32)
        hid = (a * jax.nn.sigmoid(a)) * b
        res = jnp.dot(hid.astype(BF16), w2b[...], preferred_element_type=F32)
        rows = lax.broadcasted_iota(jnp.int32, res.shape, 0)
        _store_token_major(ys_ref, jnp.where((rows >= lo) & (rows < hi), res, _load_token_major(ys_ref)))


def _experts(items, xs, w1, w3, w2):
    nw = items[0].shape[0]
    row = lambda w, wt, we, lo, hi, fi: (wt[w], 0)
    wsel = lambda w, wt, we, lo, hi, fi: (we[w], 0, 0)
    return pl.pallas_call(
        _expert_kernel,
        grid_spec=pltpu.PrefetchScalarGridSpec(
            num_scalar_prefetch=5,
            grid=(nw,),
            in_specs=[
                pl.BlockSpec((EXPERT_TILE * LANE_TILES, LANES), row),
                pl.BlockSpec((1, D_MODEL, D_EXPERT), wsel),
                pl.BlockSpec((1, D_MODEL, D_EXPERT), wsel),
                pl.BlockSpec((1, D_EXPERT, D_MODEL), wsel),
            ],
            out_specs=pl.BlockSpec((EXPERT_TILE * LANE_TILES, LANES), row),
            scratch_shapes=[
                pltpu.VMEM((D_MODEL, D_EXPERT), BF16),
                pltpu.VMEM((D_MODEL, D_EXPERT), BF16),
                pltpu.VMEM((D_EXPERT, D_MODEL), BF16),
                pltpu.SMEM((1,), jnp.int32),
            ],
        ),
        out_shape=jax.ShapeDtypeStruct(xs.shape, F32),
        compiler_params=_cparams(("arbitrary",)),
        name="moe_experts",
    )(*items, xs, w1, w3, w2)


def _combine_kernel(pos_ref, x1_ref, ri_ref, mod_ref, gf_ref, ys_ref, o_ref, y1buf, y2buf, sem,
                    *, row_base, row_step, tiles_per_seq, final):
    tm = x1_ref.shape[0]

    for r in range(tm):
        _token_copy(ys_ref, pos_ref[0, 0, r], y1buf, r * LANE_TILES, sem).start(priority=0)
        _token_copy(ys_ref, pos_ref[0, 1, r], y2buf, r * LANE_TILES, sem).start(priority=1)
    pltpu.make_async_copy(ys_ref.at[pl.ds(0, tm * LANE_TILES)], y1buf, sem).wait()
    pltpu.make_async_copy(ys_ref.at[pl.ds(0, tm * LANE_TILES)], y2buf, sem).wait()
    row = row_base + (pl.program_id(0) // tiles_per_seq) * row_step
    g2 = mod_ref[pl.ds(row, 1), 5 * D_MODEL:6 * D_MODEL]
    ri = ri_ref[...]
    y = ri[:, 2:3] * _load_token_major(y1buf) + ri[:, 3:4] * _load_token_major(y2buf)
    x2 = x1_ref[...] + g2 * y
    if final:
        ms = jnp.mean(x2 * x2, axis=-1, keepdims=True)
        o_ref[...] = (x2 * lax.rsqrt(ms + EPS)) * gf_ref[...]
    else:
        o_ref[...] = x2


def _combine(pos, x1, ri, mods, g_final, ys, *, layer, row_base, row_step, tiles_per_seq, final):
    ntok = x1.shape[0]
    tm = ROW_TILE
    tok = pl.BlockSpec((tm, D_MODEL), lambda i: (i, 0))
    return pl.pallas_call(
        functools.partial(_combine_kernel, row_base=row_base, row_step=row_step, tiles_per_seq=tiles_per_seq,
                          final=final),
        grid=(ntok // tm,),
        in_specs=[
            pl.BlockSpec((1, 2, tm), lambda i: (i, 0, 0), memory_space=pltpu.SMEM),
            tok,
            pl.BlockSpec((tm, ROUTER_LANES), lambda i: (i, 0)),
            _layer_spec((SUBLANES, N_MOD * D_MODEL), layer),
            pl.BlockSpec((1, D_MODEL), lambda i: (0, 0)),
            pl.BlockSpec(memory_space=pl.ANY),
        ],
        out_specs=tok,
        out_shape=jax.ShapeDtypeStruct((ntok, D_MODEL), F32),
        scratch_shapes=[
            pltpu.VMEM((tm * LANE_TILES, LANES), F32),
            pltpu.VMEM((tm * LANE_TILES, LANES), F32),
            pltpu.SemaphoreType.DMA(()),
        ],
        compiler_params=_cparams(("arbitrary",)),
        name=f"moe_combine_{ntok}",
    )(pos, x1, ri, mods, g_final, ys)


def _route(ri_first, ri_second, cnt):
    counts = cnt[0, 0:N_EXPERTS].astype(jnp.int32)
    offs = jnp.cumsum(counts) - counts
    info = jnp.concatenate([ri_first, ri_second], axis=0)
    e = info[:, 0:2].astype(jnp.int32)
    oh = (e[:, :, None] == jnp.arange(N_EXPERTS, dtype=jnp.int32)).astype(jnp.int32)
    pos = jnp.sum(oh * offs[None, None, :], axis=-1) + info[:, 4:6].astype(jnp.int32)
    return pos.astype(jnp.int32), counts, offs


def _work_items(counts, offs, nw, layer):
    tm = EXPERT_TILE
    first_tile = offs // tm
    last_tile = (offs + counts - 1) // tm
    n_e = jnp.where(counts > 0, last_tile - first_tile + 1, 0)
    w_end = jnp.cumsum(n_e)
    w_start = w_end - n_e
    total = w_end[-1]
    w = jnp.arange(nw, dtype=jnp.int32)
    wc = jnp.minimum(w, total - 1)
    e_w = jnp.sum((wc[:, None] >= w_end[None, :]).astype(jnp.int32), axis=-1)
    sel = (e_w[:, None] == jnp.arange(N_EXPERTS, dtype=jnp.int32)).astype(jnp.int32)
    pick = lambda v: jnp.sum(sel * v[None, :], axis=-1)
    off_w = pick(offs)
    tile_w = pick(first_tile) + (wc - pick(w_start))
    lo = jnp.clip(off_w - tile_w * tm, 0, tm)
    hi = jnp.clip(off_w + pick(counts) - tile_w * tm, 0, tm)
    valid = w < total
    lo = jnp.where(valid, lo, 0)
    hi = jnp.where(valid, hi, 0)
    prev_tile = jnp.concatenate([jnp.full((1,), -1, tile_w.dtype), tile_w[:-1]])
    first = tile_w != prev_tile
    i32 = lambda a: a.astype(jnp.int32)
    return (i32(tile_w), i32(e_w + layer * N_EXPERTS), i32(lo), i32(hi), i32(first))


def _grid_pos_embed(rows, d):
    t = jnp.arange(rows * GRID_W)
    row = (t // GRID_W).astype(F32)
    col = (t % GRID_W).astype(F32)
    nf = d // 4
    freq = 1.0 / (10000.0 ** (jnp.arange(nf, dtype=F32) / nf))
    er = row[:, None] * freq
    ec = col[:, None] * freq
    return jnp.concatenate([jnp.sin(er), jnp.cos(er), jnp.sin(ec), jnp.cos(ec)], axis=-1)


def _block_diag(w, nblk):
    *lead, H, d, _ = w.shape
    w = w.reshape(*lead, H // nblk, nblk, d, d)
    eye = jnp.eye(nblk, dtype=w.dtype)
    out = jnp.einsum('...gij,gh->...gihj', w, eye)
    return out.reshape(*lead, H // nblk, nblk * d, nblk * d)


def kernel(x_prompt, x_sample, state_lru, c, c_ctx, w_ada, b_ada, g_mix, g_ffn, g_final, w_in, w_out, conv_w, conv_b, lru_wa, lru_ba, lru_wx, lru_bx, lru_lambda, tmlp_ws, tmlp_b, fnet_w, router_g, router_g_b, router_e, router_e_b, e_w1, e_w3, e_w2):
    bp, lp, _ = x_prompt.shape
    bs, ls, _ = x_sample.shape
    n_ctx = bp * lp
    n_lat = bs * ls
    n_tok = n_ctx + n_lat

    cond8 = jnp.concatenate([c_ctx[None, :], c, jnp.zeros((SUBLANES - 1 - bs, D_MODEL), F32)], axis=0)
    pos = _grid_pos_embed(ls // GRID_W, D_MODEL)
    w_in_b = w_in.astype(BF16)
    w_out_b = w_out.astype(BF16)
    heads_per_cb = LRU_CB // RNN_HEAD_DIM
    wa_bd = (0.5 * _block_diag(lru_wa, heads_per_cb)).astype(BF16)
    wx_bd = (0.5 * _block_diag(lru_wx, heads_per_cb)).astype(BF16)
    ws_b = tmlp_ws.astype(BF16)
    bt = jnp.swapaxes(tmlp_b, 1, 2)
    fnet_bd = _block_diag(fnet_w, D_FNET // FNET_GROUP_DIM)[:, 0]
    wr = jnp.concatenate([router_g, router_e,
                          jnp.zeros((DEPTH, D_MODEL, ROUTER_LANES - N_EGROUPS - N_EXPERTS), F32)], axis=-1)
    br = jnp.concatenate([router_g_b, router_e_b,
                          jnp.zeros((DEPTH, ROUTER_LANES - N_EGROUPS - N_EXPERTS), F32)], axis=-1)
    ew1 = e_w1.reshape(DEPTH * N_EXPERTS, D_MODEL, D_EXPERT)
    ew3 = e_w3.reshape(DEPTH * N_EXPERTS, D_MODEL, D_EXPERT)
    ew2 = e_w2.reshape(DEPTH * N_EXPERTS, D_EXPERT, D_MODEL)

    mods = _modulation(cond8, w_ada, b_ada)
    fa, fb = _fnet_prep(fnet_bd)
    ct_p, st_p = _dft_tables(lp)
    ct_s, st_s = _dft_tables(ls)

    nw = 2 * n_tok // EXPERT_TILE + N_EXPERTS
    h0_p = jnp.zeros((bp, 2, D_RNN), F32)
    g_mix3 = g_mix[:, None, :]
    g_ffn3 = g_ffn[:, None, :]
    conv_b3 = conv_b[:, None, :]
    br3 = br[:, None, :]

    xp, xs = x_prompt, x_sample
    states = []
    for l in range(DEPTH):
        paths = []
        cnt = jnp.zeros((SUBLANES, ROUTER_LANES), F32)
        for (x, ct, st, h0, row_base, row_step, is_lat) in (
                (xp, ct_p, st_p, h0_p, 0, 0, False),
                (xs, ct_s, st_s, state_lru[:, l], 1, 1, True)):
            pe = pos if (is_lat and l == 0) else None
            zr, zuv, va, vb = _in_proj(x, pe, mods, g_mix3, w_in_b, fa, fb,
                                       layer=l, row_base=row_base, row_step=row_step)
            yr, st_new = _lru_mixer(zr, conv_w, conv_b3, wa_bd, wx_bd, lru_ba, lru_bx, lru_lambda, h0, layer=l)
            yf = _dft_apply(ct, st, va, vb)
            x1, h2, ri, cnt = _out_proj(x, pe, yr, zuv, yf, mods, ws_b, bt, w_out_b, g_ffn3, wr, br3, cnt,
                                        layer=l, row_base=row_base, row_step=row_step)
            paths.append((x1, h2, ri, st_new))
        states.append(paths[0][3])

        pos_all, counts, offs = _route(paths[0][2].reshape(n_ctx, ROUTER_LANES),
                                       paths[1][2].reshape(n_lat, ROUTER_LANES), cnt)
        items = _work_items(counts, offs, nw, l)
        pos_tiles = jnp.swapaxes((pos_all * LANE_TILES).reshape(n_tok // ROW_TILE, ROW_TILE, 2), 1, 2)
        nct = n_ctx // ROW_TILE
        xsorted = _dispatch(pos_tiles, paths[0][1], paths[1][1])
        ysorted = _experts(items, xsorted, ew1, ew3, ew2)
        gfin = g_final[None, :]
        final = l == DEPTH - 1
        xp = _combine(pos_tiles[:nct], paths[0][0].reshape(n_ctx, D_MODEL),
                      paths[0][2].reshape(n_ctx, ROUTER_LANES), mods, gfin, ysorted, layer=l,
                      row_base=0, row_step=0, tiles_per_seq=lp // ROW_TILE, final=final).reshape(bp, lp, D_MODEL)
        xs = _combine(pos_tiles[nct:], paths[1][0].reshape(n_lat, D_MODEL),
                      paths[1][2].reshape(n_lat, ROUTER_LANES), mods, gfin, ysorted, layer=l,
                      row_base=1, row_step=1, tiles_per_seq=ls // ROW_TILE, final=final).reshape(bs, ls, D_MODEL)

    new_state = jnp.stack(states, axis=1).astype(x_prompt.dtype)
    return (xp, xs, new_state)
```

```python
import functools
import math

import jax
import jax.numpy as jnp
from jax import lax
from jax.experimental import pallas as pl
from jax.experimental.pallas import tpu as pltpu

F32 = jnp.float32
BF16 = jnp.bfloat16

D_MODEL = 1024
DEPTH = 2
GRID_W = 64
D_RNN = 512
RNN_HEAD_DIM = 64
CONV_W = 4
LRU_C = 8.0
D_TMLP = 256
TMLP_HEADS = 4
CHUNK = 128
D_FNET = 256
FNET_GROUP_DIM = 64
D_IN = 2 * D_RNN + 2 * D_TMLP + D_FNET
N_EGROUPS = 4
N_EPG = 8
N_EXPERTS = N_EGROUPS * N_EPG
D_EXPERT = 512
N_MOD = 6
EPS = 1e-6

LANES = 128
SUBLANES = 8
LANE_TILES = D_MODEL // LANES
assert LANE_TILES == SUBLANES
LRU_CB = LANES
ROW_TILE = 256
EXPERT_TILE = 256
ROUTER_LANES = LANES
NEG_BIG = -1e30
VMEM_LIMIT = 56 * 1024 * 1024


def _cparams(sem):
    return pltpu.CompilerParams(dimension_semantics=sem, vmem_limit_bytes=VMEM_LIMIT)


def _mod_kernel(c_ref, w_ref, b_ref, o_ref):
    c = c_ref[...]
    s = c * jax.nn.sigmoid(c)
    o_ref[0] = jnp.dot(s.astype(BF16), w_ref[0].astype(BF16), preferred_element_type=F32) + b_ref[0]


def _modulation(cond8, w_ada, b_ada):
    tn = 1536
    return pl.pallas_call(
        _mod_kernel,
        grid=(DEPTH, N_MOD * D_MODEL // tn),
        in_specs=[
            pl.BlockSpec((SUBLANES, D_MODEL), lambda l, j: (0, 0)),
            pl.BlockSpec((1, D_MODEL, tn), lambda l, j: (l, 0, j)),
            pl.BlockSpec((1, 1, tn), lambda l, j: (l, 0, j)),
        ],
        out_specs=pl.BlockSpec((1, SUBLANES, tn), lambda l, j: (l, 0, j)),
        out_shape=jax.ShapeDtypeStruct((DEPTH, SUBLANES, N_MOD * D_MODEL), F32),
        compiler_params=_cparams(("arbitrary", "arbitrary")),
        name="modulation",
    )(cond8, w_ada, b_ada.reshape(DEPTH, 1, N_MOD * D_MODEL))


def _fprep_kernel(w_ref, a_ref, b_ref):
    r = lax.broadcasted_iota(jnp.int32, (D_FNET, D_FNET), 0)
    c = lax.broadcasted_iota(jnp.int32, (D_FNET, D_FNET), 1)
    same = (r >> 6) == (c >> 6)
    ph = ((r & 63) * (c & 63)) & 63
    ang = ph.astype(F32) * (2.0 * math.pi / FNET_GROUP_DIM)
    scale = 1.0 / math.sqrt(FNET_GROUP_DIM)
    cm = jnp.where(same, jnp.cos(ang) * scale, 0.0)
    sm = jnp.where(same, jnp.sin(ang) * scale, 0.0)
    w = w_ref[0]
    a_ref[0] = jnp.dot(cm, w, precision=lax.Precision.HIGHEST, preferred_element_type=F32).astype(BF16)
    b_ref[0] = jnp.dot(sm, w, precision=lax.Precision.HIGHEST, preferred_element_type=F32).astype(BF16)


def _fnet_prep(wbd):
    spec = pl.BlockSpec((1, D_FNET, D_FNET), lambda l: (l, 0, 0))
    return pl.pallas_call(
        _fprep_kernel,
        grid=(DEPTH,),
        in_specs=[spec],
        out_specs=[spec, spec],
        out_shape=[jax.ShapeDtypeStruct((DEPTH, D_FNET, D_FNET), BF16)] * 2,
        compiler_params=_cparams(("arbitrary",)),
        name="fnet_prep",
    )(wbd)


TABLE_ROWS = 64


def _base_table_kernel(cj_ref, sj_ref, cm_ref, sm_ref, *, L):
    j = lax.broadcasted_iota(jnp.int32, (TABLE_ROWS, L), 0)
    n = lax.broadcasted_iota(jnp.int32, (TABLE_ROWS, L), 1)
    w = 2.0 * math.pi / L
    fine = ((j * n) & (L - 1)).astype(F32) * w
    coarse = ((j * TABLE_ROWS * n) & (L - 1)).astype(F32) * w
    scale = 1.0 / math.sqrt(L)
    cj_ref[...] = jnp.cos(fine)
    sj_ref[...] = jnp.sin(fine)
    cm_ref[...] = jnp.cos(coarse) * scale
    sm_ref[...] = jnp.sin(coarse) * scale


def _table_kernel(cj_ref, sj_ref, cm_ref, sm_ref, c_ref, s_ref):
    m = pl.program_id(0)
    c0 = cm_ref[pl.ds(m, 1), :]
    s0 = sm_ref[pl.ds(m, 1), :]
    cj = cj_ref[...]
    sj = sj_ref[...]
    c_ref[...] = (cj * c0 - sj * s0).astype(BF16)
    s_ref[...] = (-(sj * c0 + cj * s0)).astype(BF16)


def _dft_tables(L):
    small = pl.BlockSpec((TABLE_ROWS, L), lambda *_: (0, 0))
    base = pl.pallas_call(
        functools.partial(_base_table_kernel, L=L),
        out_specs=[small] * 4,
        out_shape=[jax.ShapeDtypeStruct((TABLE_ROWS, L), F32)] * 4,
        compiler_params=pltpu.CompilerParams(vmem_limit_bytes=VMEM_LIMIT),
        name=f"dft_base_tables_{L}",
    )()
    spec = pl.BlockSpec((TABLE_ROWS, L), lambda i: (i, 0))
    return pl.pallas_call(
        _table_kernel,
        grid=(L // TABLE_ROWS,),
        in_specs=[small] * 4,
        out_specs=[spec, spec],
        out_shape=[jax.ShapeDtypeStruct((L, L), BF16)] * 2,
        compiler_params=_cparams(("arbitrary",)),
        name=f"dft_tables_{L}",
    )(*base)


def _dft_kernel(c_ref, s_ref, va_ref, vb_ref, o_ref):
    o_ref[...] = (jnp.dot(c_ref[...], va_ref[...], preferred_element_type=F32)
                  + jnp.dot(s_ref[...], vb_ref[...], preferred_element_type=F32))


def _dft_apply(ct, st, va, vb):
    L, ncols = va.shape
    tk, tc = 256, 1024
    return pl.pallas_call(
        _dft_kernel,
        grid=(L // tk, ncols // tc),
        in_specs=[
            pl.BlockSpec((tk, L), lambda i, j: (i, 0)),
            pl.BlockSpec((tk, L), lambda i, j: (i, 0)),
            pl.BlockSpec((L, tc), lambda i, j: (0, j)),
            pl.BlockSpec((L, tc), lambda i, j: (0, j)),
        ],
        out_specs=pl.BlockSpec((tk, tc), lambda i, j: (i, j)),
        out_shape=jax.ShapeDtypeStruct((L, ncols), F32),
        compiler_params=_cparams(("arbitrary", "arbitrary")),
        name=f"dft_apply_{L}",
    )(ct, st, va, vb)


def _rms_mod(x, g, scale, shift):
    ms = jnp.mean(x * x, axis=-1, keepdims=True)
    return (x * lax.rsqrt(ms + EPS)) * g * (1.0 + scale) + shift


def _in_kernel(*refs, add_pos, row_base, row_step):
    if add_pos:
        x_ref, pos_ref, mod_ref, g_ref, w_ref, a_ref, b_ref, zr_ref, zuv_ref, va_ref, vb_ref = refs
    else:
        x_ref, mod_ref, g_ref, w_ref, a_ref, b_ref, zr_ref, zuv_ref, va_ref, vb_ref = refs
    x = x_ref[0]
    if add_pos:
        x = x + pos_ref[...]
    row = row_base + pl.program_id(0) * row_step
    m = mod_ref[pl.ds(row, 1), :]
    h = _rms_mod(x, g_ref[...], m[:, D_MODEL:2 * D_MODEL], m[:, 0:D_MODEL])
    z = jnp.dot(h.astype(BF16), w_ref[...], preferred_element_type=F32)
    zr_ref[0] = z[:, 0:2 * D_RNN]
    zuv_ref[0] = z[:, 2 * D_RNN:2 * D_RNN + 2 * D_TMLP]
    zf = z[:, 2 * D_RNN + 2 * D_TMLP:D_IN].astype(BF16)
    va_ref[...] = jnp.dot(zf, a_ref[...], preferred_element_type=F32).astype(BF16)
    vb_ref[...] = jnp.dot(zf, b_ref[...], preferred_element_type=F32).astype(BF16)


def _in_proj(x, pos, mods, g, w_in_b, fa, fb, *, layer, row_base, row_step):
    nseq, L, _ = x.shape
    tm = ROW_TILE
    add_pos = pos is not None
    in_specs = [pl.BlockSpec((1, tm, D_MODEL), lambda b, t: (b, t, 0))]
    args = [x]
    if add_pos:
        in_specs.append(pl.BlockSpec((tm, D_MODEL), lambda b, t: (t, 0)))
        args.append(pos)
    in_specs += [
        _layer_spec((SUBLANES, N_MOD * D_MODEL), layer),
        _layer_spec((1, D_MODEL), layer),
        _layer_spec((D_MODEL, D_IN), layer),
        _layer_spec((D_FNET, D_FNET), layer),
        _layer_spec((D_FNET, D_FNET), layer),
    ]
    args += [mods, g, w_in_b, fa, fb]
    return pl.pallas_call(
        functools.partial(_in_kernel, add_pos=add_pos, row_base=row_base, row_step=row_step),
        grid=(nseq, L // tm),
        in_specs=in_specs,
        out_specs=[
            pl.BlockSpec((1, tm, 2 * D_RNN), lambda b, t: (b, t, 0)),
            pl.BlockSpec((1, tm, 2 * D_TMLP), lambda b, t: (b, t, 0)),
            pl.BlockSpec((tm, D_FNET), lambda b, t: (t, b)),
            pl.BlockSpec((tm, D_FNET), lambda b, t: (t, b)),
        ],
        out_shape=[
            jax.ShapeDtypeStruct((nseq, L, 2 * D_RNN), F32),
            jax.ShapeDtypeStruct((nseq, L, 2 * D_TMLP), F32),
            jax.ShapeDtypeStruct((L, nseq * D_FNET), BF16),
            jax.ShapeDtypeStruct((L, nseq * D_FNET), BF16),
        ],
        compiler_params=_cparams(("arbitrary", "arbitrary")),
        name=f"in_proj_{L}",
    )(*args)


def _gelu_tanh(x):
    return 0.5 * x * (1.0 + jnp.tanh(math.sqrt(2.0 / math.pi) * (x + 0.044715 * (x * x * x))))


def _rows_to_tile(rows):
    sub = lax.broadcasted_iota(jnp.int32, (SUBLANES, LANES), 0)
    out = jnp.zeros((SUBLANES, LANES), F32)
    for s, r in enumerate(rows):
        out = jnp.where(sub == s, jnp.broadcast_to(r, (SUBLANES, LANES)), out)
    return out


def _lru_kernel(xr_ref, gr_ref, cw_ref, cb_ref, wa_ref, wx_ref, ba_ref, bx_ref, lam_ref, h0_ref,
                y_ref, st_ref, xnat, pext, af, bf, ab, bb, hnat, *, L):
    S = L // SUBLANES
    pitch = S + SUBLANES
    n = S * SUBLANES
    chunk = 256

    for s in range(SUBLANES):
        xnat[s * pitch:s * pitch + S, :] = xr_ref[0, s * S:(s + 1) * S, :]

    def perm_in(j, c):
        dst = pl.multiple_of((j + 2) * SUBLANES, SUBLANES)
        pext[pl.ds(dst, SUBLANES), :] = xnat[pl.ds(j, SUBLANES, stride=pitch), :]
        return c

    lax.fori_loop(0, S, perm_in, 0, unroll=8)

    sub = lax.broadcasted_iota(jnp.int32, (SUBLANES, LANES), 0)

    def from_prev_segment(v):
        return jnp.where(sub == 0, 0.0, pltpu.roll(v, 1, axis=0))

    def from_next_segment(v):
        return jnp.where(sub == SUBLANES - 1, 0.0, pltpu.roll(v, SUBLANES - 1, axis=0))

    pext[0:8, :] = from_prev_segment(pext[S * 8:(S + 1) * 8, :])
    pext[8:16, :] = from_prev_segment(pext[(S + 1) * 8:(S + 2) * 8, :])
    pext[(S + 2) * 8:(S + 3) * 8, :] = from_next_segment(pext[16:24, :])

    lam = lam_ref[...]
    nl = -lam
    sp = jnp.maximum(nl, 0.0) + jnp.log1p(jnp.exp(-jnp.abs(nl)))
    c_la = (-0.5 * LRU_C) * sp
    ba_h = 0.5 * ba_ref[...]
    bx_h = 0.5 * bx_ref[...]
    a_refs = (af, ab)
    b_refs = (bf, bb)

    def gates(i, c):
        base = pl.multiple_of(i * chunk, chunk)
        xc = (cw_ref[0:1, :] * pext[pl.ds(base, chunk), :]
              + cw_ref[1:2, :] * pext[pl.ds(base + 8, chunk), :]
              + cw_ref[2:3, :] * pext[pl.ds(base + 16, chunk), :]
              + cw_ref[3:4, :] * pext[pl.ds(base + 24, chunk), :]
              + cb_ref[...])
        xcb = xc.astype(BF16)
        xh = 0.5 * xc
        for d in range(2):
            tr = jnp.tanh(jnp.dot(xcb, wa_ref[d, 0], preferred_element_type=F32) + ba_h[d:d + 1, :])
            ti = jnp.tanh(jnp.dot(xcb, wx_ref[d, 0], preferred_element_type=F32) + bx_h[d:d + 1, :])
            log_a = c_la[d:d + 1, :] * (1.0 + tr)
            a = jnp.exp(log_a)
            v = jnp.tanh(log_a) * (-1.0 - a * a)
            coef = jnp.where(v > 0.0, v * lax.rsqrt(v), 0.0)
            a_refs[d][pl.ds(base, chunk), :] = a
            b_refs[d][pl.ds(base, chunk), :] = coef * ((1.0 + ti) * xh)
        return c

    lax.fori_loop(0, n // chunk, gates, 0)

    def scan(j, carry):
        hf, pf, hb, pb = carry
        jf = pl.multiple_of(j * SUBLANES, SUBLANES)
        jb = pl.multiple_of((S - 1 - j) * SUBLANES, SUBLANES)
        a1 = af[pl.ds(jf, SUBLANES), :]
        hf = a1 * hf + bf[pl.ds(jf, SUBLANES), :]
        pf = a1 * pf
        bf[pl.ds(jf, SUBLANES), :] = hf
        af[pl.ds(jf, SUBLANES), :] = pf
        a2 = ab[pl.ds(jb, SUBLANES), :]
        hb = a2 * hb + bb[pl.ds(jb, SUBLANES), :]
        pb = a2 * pb
        bb[pl.ds(jb, SUBLANES), :] = hb
        ab[pl.ds(jb, SUBLANES), :] = pb
        return hf, pf, hb, pb

    zero = jnp.zeros((SUBLANES, LANES), F32)
    one = jnp.ones((SUBLANES, LANES), F32)
    hf, pf, hb, pb = lax.fori_loop(0, S, scan, (zero, one, zero, one), unroll=8)

    rows_f = [h0_ref[0, 0:1, :]]
    for s in range(1, SUBLANES):
        rows_f.append(pf[s - 1:s, :] * rows_f[-1] + hf[s - 1:s, :])
    st_ref[0, 0:1, :] = pf[7:8, :] * rows_f[7] + hf[7:8, :]
    rows_b = [None] * SUBLANES
    rows_b[7] = h0_ref[0, 1:2, :]
    for s in range(SUBLANES - 2, -1, -1):
        rows_b[s] = pb[s + 1:s + 2, :] * rows_b[s + 1] + hb[s + 1:s + 2, :]
    st_ref[0, 1:2, :] = pb[0:1, :] * rows_b[0] + hb[0:1, :]
    init_f = _rows_to_tile(rows_f)
    init_b = _rows_to_tile(rows_b)

    def perm_out(j, c):
        src = pl.multiple_of(j * SUBLANES, SUBLANES)
        v = (bf[pl.ds(src, SUBLANES), :] + af[pl.ds(src, SUBLANES), :] * init_f
             + bb[pl.ds(src, SUBLANES), :] + ab[pl.ds(src, SUBLANES), :] * init_b)
        hnat[pl.ds(j, SUBLANES, stride=pitch), :] = v
        return c

    lax.fori_loop(0, S, perm_out, 0, unroll=8)

    for s in range(SUBLANES):
        g = gr_ref[0, s * S:(s + 1) * S, :]
        y_ref[0, s * S:(s + 1) * S, :] = hnat[s * pitch:s * pitch + S, :] * _gelu_tanh(g)


def _lru_mixer(zr, conv_w, conv_b, wa_bd, wx_bd, ba, bx, lam, h0, *, layer):
    nseq, L, _ = zr.shape
    ncb = D_RNN // LRU_CB
    S = L // SUBLANES
    pitch = S + SUBLANES
    vec2 = pl.BlockSpec((None, 2, LRU_CB), lambda b, c: (layer, 0, c))
    wspec = pl.BlockSpec((None, 2, 1, LRU_CB, LRU_CB), lambda b, c: (layer, 0, c, 0, 0))
    return pl.pallas_call(
        functools.partial(_lru_kernel, L=L),
        grid=(nseq, ncb),
        in_specs=[
            pl.BlockSpec((1, L, LRU_CB), lambda b, c: (b, 0, c)),
            pl.BlockSpec((1, L, LRU_CB), lambda b, c: (b, 0, c + ncb)),
            pl.BlockSpec((None, CONV_W, LRU_CB), lambda b, c: (layer, 0, c)),
            pl.BlockSpec((None, 1, LRU_CB), lambda b, c: (layer, 0, c)),
            wspec, wspec, vec2, vec2, vec2,
            pl.BlockSpec((1, 2, LRU_CB), lambda b, c: (b, 0, c)),
        ],
        out_specs=[
            pl.BlockSpec((1, L, LRU_CB), lambda b, c: (b, 0, c)),
            pl.BlockSpec((1, 2, LRU_CB), lambda b, c: (b, 0, c)),
        ],
        out_shape=[
            jax.ShapeDtypeStruct((nseq, L, D_RNN), F32),
            jax.ShapeDtypeStruct((nseq, 2, D_RNN), F32),
        ],
        scratch_shapes=[
            pltpu.VMEM((SUBLANES * pitch, LANES), F32),
            pltpu.VMEM(((S + 3) * SUBLANES, LANES), F32),
            pltpu.VMEM((L, LANES), F32),
            pltpu.VMEM((L, LANES), F32),
            pltpu.VMEM((L, LANES), F32),
            pltpu.VMEM((L, LANES), F32),
            pltpu.VMEM((SUBLANES * pitch, LANES), F32),
        ],
        compiler_params=_cparams(("arbitrary", "arbitrary")),
        name=f"lru_mixer_{L}",
    )(zr, zr, conv_w, conv_b, wa_bd, wx_bd, ba, bx, lam, h0)


def _out_kernel(*refs, add_pos, row_base, row_step):
    if add_pos:
        (x_ref, pos_ref, yr_ref, zuv_ref, yf_ref, mod_ref, ws_ref, bt_ref, wo_ref, g_ref, wr_ref, br_ref, cin_ref,
         x1_ref, h2_ref, ri_ref, cnt_ref, carry) = refs
    else:
        (x_ref, yr_ref, zuv_ref, yf_ref, mod_ref, ws_ref, bt_ref, wo_ref, g_ref, wr_ref, br_ref, cin_ref,
         x1_ref, h2_ref, ri_ref, cnt_ref, carry) = refs
    tm = x_ref.shape[1]
    x = x_ref[0]
    if add_pos:
        x = x + pos_ref[...]
    row = row_base + pl.program_id(0) * row_step
    m = mod_ref[pl.ds(row, 1), :]
    g1 = m[:, 2 * D_MODEL:3 * D_MODEL]
    sh2 = m[:, 3 * D_MODEL:4 * D_MODEL]
    sc2 = m[:, 4 * D_MODEL:5 * D_MODEL]

    head = lax.broadcasted_iota(jnp.int32, (CHUNK, D_TMLP), 1) >> 6
    yt_parts = []
    for ci in range(tm // CHUNK):
        u = zuv_ref[0, ci * CHUNK:(ci + 1) * CHUNK, 0:D_TMLP]
        v = zuv_ref[0, ci * CHUNK:(ci + 1) * CHUNK, D_TMLP:2 * D_TMLP].astype(BF16)
        s = jnp.zeros((CHUNK, D_TMLP), F32)
        for h in range(TMLP_HEADS):
            sh = jnp.dot(ws_ref[h], v, preferred_element_type=F32) + bt_ref[:, h:h + 1]
            s = jnp.where(head == h, sh, s)
        yt_parts.append(u * s)
    yt = jnp.concatenate(yt_parts, axis=0) if len(yt_parts) > 1 else yt_parts[0]

    y = (jnp.dot(yr_ref[0].astype(BF16), wo_ref[0:D_RNN, :], preferred_element_type=F32)
         + jnp.dot(yt.astype(BF16), wo_ref[D_RNN:D_RNN + D_TMLP, :], preferred_element_type=F32)
         + jnp.dot(yf_ref[...].astype(BF16), wo_ref[D_RNN + D_TMLP:D_MODEL, :], preferred_element_type=F32))
    x1 = x + g1 * y
    x1_ref[0] = x1
    h2 = _rms_mod(x1, g_ref[...], sc2, sh2)
    _store_token_major(h2_ref, h2)

    wr = wr_ref[...]
    w_hi = wr.astype(BF16)
    w_lo = (wr - w_hi.astype(F32)).astype(BF16)
    h_hi = h2.astype(BF16)
    h_lo = (h2 - h_hi.astype(F32)).astype(BF16)
    p_hi = jnp.dot(h_hi, jnp.concatenate([w_hi, w_lo], axis=-1), preferred_element_type=F32)
    p_lo = jnp.dot(h_lo, w_hi, preferred_element_type=F32)
    logits = p_hi[:, 0:ROUTER_LANES] + p_hi[:, ROUTER_LANES:2 * ROUTER_LANES] + p_lo + br_ref[...]
    lane = lax.broadcasted_iota(jnp.int32, (tm, ROUTER_LANES), 1)
    lane_f = lane.astype(F32)
    is_g = lane < N_EGROUPS
    gl = jnp.where(is_g, logits, NEG_BIG)
    gmax = jnp.max(gl, axis=-1, keepdims=True)
    gsel = jnp.min(jnp.where(gl == gmax, lane_f, 1e4), axis=-1, keepdims=True)
    pg = 1.0 / jnp.sum(jnp.where(is_g, jnp.exp(logits - gmax), 0.0), axis=-1, keepdims=True)
    grp_f = ((lane - N_EGROUPS) >> 3).astype(F32)
    emask = (lane >= N_EGROUPS) & (lane < N_EGROUPS + N_EXPERTS) & (grp_f == gsel)
    el = jnp.where(emask, logits, NEG_BIG)
    v1 = jnp.max(el, axis=-1, keepdims=True)
    i1 = jnp.min(jnp.where(el == v1, lane_f, 1e4), axis=-1, keepdims=True)
    el2 = jnp.where(lane_f == i1, NEG_BIG, el)
    v2 = jnp.max(el2, axis=-1, keepdims=True)
    i2 = jnp.min(jnp.where(el2 == v2, lane_f, 1e4), axis=-1, keepdims=True)
    e2x = jnp.exp(v2 - v1)
    fw1 = 1.0 / (1.0 + e2x)
    fw2 = e2x * fw1
    @pl.when((pl.program_id(0) == 0) & (pl.program_id(1) == 0))
    def _():
        carry[...] = cin_ref[...]

    e1 = i1 - N_EGROUPS
    e2 = i2 - N_EGROUPS
    m1 = lane_f == e1
    m2 = lane_f == e2
    oh = jnp.where(m1 | m2, 1.0, 0.0)
    r_i = lax.broadcasted_iota(jnp.int32, (tm, tm), 0)
    c_i = lax.broadcasted_iota(jnp.int32, (tm, tm), 1)
    tri = jnp.where(c_i < r_i, 1.0, 0.0).astype(BF16)
    before = jnp.dot(tri, oh.astype(BF16), preferred_element_type=F32) + carry[0:1, :]
    rank1 = jnp.sum(jnp.where(m1, before, 0.0), axis=-1, keepdims=True)
    rank2 = jnp.sum(jnp.where(m2, before, 0.0), axis=-1, keepdims=True)
    total = carry[0:1, :] + jnp.sum(oh, axis=0, keepdims=True)
    carry[0:1, :] = total
    cnt_ref[...] = jnp.broadcast_to(total, cnt_ref.shape)

    vals = (e1, e2, pg * fw1, pg * fw2, rank1, rank2)
    ri = jnp.zeros((tm, ROUTER_LANES), F32)
    for k, v in enumerate(vals):
        ri = jnp.where(lane == k, v, ri)
    ri_ref[0] = ri


def _layer_spec(shape, layer):
    zeros = (0,) * len(shape)
    return pl.BlockSpec((None, *shape), lambda *_: (layer, *zeros))


def _out_proj(x, pos, yr, zuv, yf, mods, ws_b, bt, wo_b, g, wr, br, cin, *, layer, row_base, row_step):
    nseq, L, _ = x.shape
    tm = ROW_TILE
    add_pos = pos is not None
    in_specs = [pl.BlockSpec((1, tm, D_MODEL), lambda b, t: (b, t, 0))]
    args = [x]
    if add_pos:
        in_specs.append(pl.BlockSpec((tm, D_MODEL), lambda b, t: (t, 0)))
        args.append(pos)
    in_specs += [
        pl.BlockSpec((1, tm, D_RNN), lambda b, t: (b, t, 0)),
        pl.BlockSpec((1, tm, 2 * D_TMLP), lambda b, t: (b, t, 0)),
        pl.BlockSpec((tm, D_FNET), lambda b, t: (t, b)),
        _layer_spec((SUBLANES, N_MOD * D_MODEL), layer),
        _layer_spec((TMLP_HEADS, CHUNK, CHUNK), layer),
        _layer_spec((CHUNK, TMLP_HEADS), layer),
        _layer_spec((D_MODEL, D_MODEL), layer),
        _layer_spec((1, D_MODEL), layer),
        _layer_spec((D_MODEL, ROUTER_LANES), layer),
        _layer_spec((1, ROUTER_LANES), layer),
        pl.BlockSpec((SUBLANES, ROUTER_LANES), lambda b, t: (0, 0)),
    ]
    args += [yr, zuv, yf, mods, ws_b, bt, wo_b, g, wr, br, cin]
    tok = pl.BlockSpec((1, tm, D_MODEL), lambda b, t: (b, t, 0))
    return pl.pallas_call(
        functools.partial(_out_kernel, add_pos=add_pos, row_base=row_base, row_step=row_step),
        grid=(nseq, L // tm),
        in_specs=in_specs,
        out_specs=[tok, pl.BlockSpec((tm * LANE_TILES, LANES), lambda b, t: (b * (L // tm) + t, 0)),
                   pl.BlockSpec((1, tm, ROUTER_LANES), lambda b, t: (b, t, 0)),
                   pl.BlockSpec((SUBLANES, ROUTER_LANES), lambda b, t: (0, 0))],
        out_shape=[
            jax.ShapeDtypeStruct((nseq, L, D_MODEL), F32),
            jax.ShapeDtypeStruct((nseq * L * LANE_TILES, LANES), F32),
            jax.ShapeDtypeStruct((nseq, L, ROUTER_LANES), F32),
            jax.ShapeDtypeStruct((SUBLANES, ROUTER_LANES), F32),
        ],
        scratch_shapes=[pltpu.VMEM((SUBLANES, ROUTER_LANES), F32)],
        compiler_params=_cparams(("arbitrary", "arbitrary")),
        name=f"out_proj_{L}",
    )(*args)


def _store_token_major(ref, x):
    tm = x.shape[0]
    for j in range(LANE_TILES):
        ref[pl.ds(j, tm, stride=LANE_TILES), :] = x[:, j * LANES:(j + 1) * LANES]


def _load_token_major(ref):
    tm = ref.shape[0] // LANE_TILES
    return jnp.concatenate([ref[pl.ds(j, tm, stride=LANE_TILES), :] for j in range(LANE_TILES)], axis=-1)


def _token_copy(src_ref, src_tok, dst_ref, dst_tok, sem):
    return pltpu.make_async_copy(src_ref.at[pl.ds(pl.multiple_of(src_tok, LANE_TILES), LANE_TILES)],
                                 dst_ref.at[pl.ds(pl.multiple_of(dst_tok, LANE_TILES), LANE_TILES)], sem)


def _scatter_rows(pos_ref, h_ref, xs_ref, sem):
    rows = h_ref.shape[0]

    for r in range(0, rows, LANE_TILES):
        _token_copy(h_ref, r, xs_ref, pos_ref[0, 0, r // LANE_TILES], sem).start(priority=0)
        _token_copy(h_ref, r, xs_ref, pos_ref[0, 1, r // LANE_TILES], sem).start(priority=1)
    for _ in range(2):
        pltpu.make_async_copy(h_ref, xs_ref.at[pl.ds(0, rows)], sem).wait()


def _dispatch_kernel(pos_ref, hp_ref, hs_ref, xs_ref, sem, *, n_first):
    i = pl.program_id(0)

    @pl.when(i < n_first)
    def _():
        _scatter_rows(pos_ref, hp_ref, xs_ref, sem)

    @pl.when(i >= n_first)
    def _():
        _scatter_rows(pos_ref, hs_ref, xs_ref, sem)


def _dispatch(pos, h_first, h_second):
    tm = ROW_TILE
    blk = tm * LANE_TILES
    n_first = h_first.shape[0] // blk
    n_second = h_second.shape[0] // blk
    nrows = 2 * (h_first.shape[0] + h_second.shape[0])
    return pl.pallas_call(
        functools.partial(_dispatch_kernel, n_first=n_first),
        grid=(n_first + n_second,),
        in_specs=[
            pl.BlockSpec((1, 2, tm), lambda i: (i, 0, 0), memory_space=pltpu.SMEM),
            pl.BlockSpec((blk, LANES), lambda i: (jnp.minimum(i, n_first - 1), 0)),
            pl.BlockSpec((blk, LANES), lambda i: (jnp.maximum(i - n_first, 0), 0)),
        ],
        out_specs=pl.BlockSpec(memory_space=pl.ANY),
        out_shape=jax.ShapeDtypeStruct((nrows, LANES), F32),
        scratch_shapes=[pltpu.SemaphoreType.DMA(())],
        compiler_params=_cparams(("arbitrary",)),
        name="moe_dispatch",
    )(pos, h_first, h_second)


WEIGHT_LEAD = (3, 2, 1)
WEIGHT_SLOTS = 4
LEAD_ITEMS = max(WEIGHT_LEAD)


def _expert_kernel(wt_ref, we_ref, lo_ref, hi_ref, first_ref, ord_ref, xs_ref, w1_ref, w3_ref, w2_ref, ys_ref,
                   w1b, w3b, w2b):
    s = pl.program_id(0)
    last = pl.num_programs(0) - 1
    tm = EXPERT_TILE

    def stage(lead, src, dst):
        cur = jnp.minimum(s + lead, last)
        prv = jnp.minimum(s + lead - 1, last)

        @pl.when((s == 0) | (we_ref[cur] != we_ref[prv]))
        def _():
            dst[ord_ref[cur] & (WEIGHT_SLOTS - 1)] = src[0].astype(BF16)

    for lead, src, dst in zip(WEIGHT_LEAD, (w1_ref, w3_ref, w2_ref), (w1b, w3b, w2b)):
        stage(lead, src, dst)

    lo = lo_ref[s]
    hi = hi_ref[s]
    full = (lo == 0) & (hi == tm)

    @pl.when((first_ref[s] == 1) & jnp.logical_not(full))
    def _():
        ys_ref[...] = jnp.zeros_like(ys_ref)

    @pl.when(hi > lo)
    def _():
        slot = ord_ref[s] & (WEIGHT_SLOTS - 1)
        x = _load_token_major(xs_ref).astype(BF16)
        a = jnp.dot(x, w1b[slot], preferred_element_type=F32)
        b = jnp.dot(x, w3b[slot], preferred_element_type=F32)
        hid = (a * jax.nn.sigmoid(a)) * b
        res = jnp.dot(hid.astype(BF16), w2b[slot], preferred_element_type=F32)

        @pl.when(full)
        def _():
            _store_token_major(ys_ref, res)

        @pl.when(jnp.logical_not(full))
        def _():
            rows = lax.broadcasted_iota(jnp.int32, res.shape, 0)
            _store_token_major(ys_ref, jnp.where((rows >= lo) & (rows < hi), res, _load_token_major(ys_ref)))


def _experts(items, xs, w1, w3, w2):
    nw = items[0].shape[0]

    def row(s, wt, we, lo, hi, fi, od):
        return (wt[s], 0)

    def weight(lead):
        return lambda s, wt, we, lo, hi, fi, od: (we[jnp.minimum(s + lead, nw - 1)], 0, 0)

    return pl.pallas_call(
        _expert_kernel,
        grid_spec=pltpu.PrefetchScalarGridSpec(
            num_scalar_prefetch=6,
            grid=(nw,),
            in_specs=[
                pl.BlockSpec((EXPERT_TILE * LANE_TILES, LANES), row),
                pl.BlockSpec((1, D_MODEL, D_EXPERT), weight(WEIGHT_LEAD[0])),
                pl.BlockSpec((1, D_MODEL, D_EXPERT), weight(WEIGHT_LEAD[1])),
                pl.BlockSpec((1, D_EXPERT, D_MODEL), weight(WEIGHT_LEAD[2])),
            ],
            out_specs=pl.BlockSpec((EXPERT_TILE * LANE_TILES, LANES), row),
            scratch_shapes=[
                pltpu.VMEM((WEIGHT_SLOTS, D_MODEL, D_EXPERT), BF16),
                pltpu.VMEM((WEIGHT_SLOTS, D_MODEL, D_EXPERT), BF16),
                pltpu.VMEM((WEIGHT_SLOTS, D_EXPERT, D_MODEL), BF16),
            ],
        ),
        out_shape=jax.ShapeDtypeStruct(xs.shape, F32),
        compiler_params=_cparams(("arbitrary",)),
        name="moe_experts",
    )(*items, xs, w1, w3, w2)


def _combine_kernel(pos_ref, posn_ref, x1_ref, ri_ref, mod_ref, gf_ref, ys_ref, o_ref, ybuf, sem,
                    *, row_base, row_step, tiles_per_seq, final):
    tm = x1_ref.shape[0]
    i = pl.program_id(0)
    n = pl.num_programs(0)

    def gather(p_ref, slot):
        for r in range(tm):
            _token_copy(ys_ref, p_ref[0, 0, r], ybuf.at[slot, 0], r * LANE_TILES, sem.at[slot]).start(priority=0)
            _token_copy(ys_ref, p_ref[0, 1, r], ybuf.at[slot, 1], r * LANE_TILES, sem.at[slot]).start(priority=1)

    def drain(slot):
        for k in range(2):
            pltpu.make_async_copy(ys_ref.at[pl.ds(0, tm * LANE_TILES)], ybuf.at[slot, k], sem.at[slot]).wait()

    def tile(slot):
        if slot == 0:
            @pl.when(i == 0)
            def _():
                gather(pos_ref, 0)

        @pl.when(i + 1 < n)
        def _():
            gather(posn_ref, 1 - slot)

        drain(slot)
        row = row_base + (i // tiles_per_seq) * row_step
        g2 = mod_ref[pl.ds(row, 1), 5 * D_MODEL:6 * D_MODEL]
        ri = ri_ref[...]
        y = (ri[:, 2:3] * _load_token_major(ybuf.at[slot, 0])
             + ri[:, 3:4] * _load_token_major(ybuf.at[slot, 1]))
        x2 = x1_ref[...] + g2 * y
        if final:
            ms = jnp.mean(x2 * x2, axis=-1, keepdims=True)
            o_ref[...] = (x2 * lax.rsqrt(ms + EPS)) * gf_ref[...]
        else:
            o_ref[...] = x2

    for slot in range(2):
        pl.when((i & 1) == slot)(functools.partial(tile, slot))


def _combine(pos, x1, ri, mods, g_final, ys, *, layer, row_base, row_step, tiles_per_seq, final):
    ntok = x1.shape[0]
    tm = ROW_TILE
    tok = pl.BlockSpec((tm, D_MODEL), lambda i: (i, 0))
    return pl.pallas_call(
        functools.partial(_combine_kernel, row_base=row_base, row_step=row_step, tiles_per_seq=tiles_per_seq,
                          final=final),
        grid=(ntok // tm,),
        in_specs=[
            pl.BlockSpec((1, 2, tm), lambda i: (i, 0, 0), memory_space=pltpu.SMEM),
            pl.BlockSpec((1, 2, tm), lambda i: (jnp.minimum(i + 1, ntok // tm - 1), 0, 0), memory_space=pltpu.SMEM),
            tok,
            pl.BlockSpec((tm, ROUTER_LANES), lambda i: (i, 0)),
            _layer_spec((SUBLANES, N_MOD * D_MODEL), layer),
            pl.BlockSpec((1, D_MODEL), lambda i: (0, 0)),
            pl.BlockSpec(memory_space=pl.ANY),
        ],
        out_specs=tok,
        out_shape=jax.ShapeDtypeStruct((ntok, D_MODEL), F32),
        scratch_shapes=[
            pltpu.VMEM((2, 2, tm * LANE_TILES, LANES), F32),
            pltpu.SemaphoreType.DMA((2,)),
        ],
        compiler_params=_cparams(("arbitrary",)),
        name=f"moe_combine_{ntok}",
    )(pos, pos, x1, ri, mods, g_final, ys)


def _route(ri_first, ri_second, cnt):
    counts = cnt[0, 0:N_EXPERTS].astype(jnp.int32)
    offs = jnp.cumsum(counts) - counts
    info = jnp.concatenate([ri_first, ri_second], axis=0)
    e = info[:, 0:2].astype(jnp.int32)
    oh = (e[:, :, None] == jnp.arange(N_EXPERTS, dtype=jnp.int32)).astype(jnp.int32)
    pos = jnp.sum(oh * offs[None, None, :], axis=-1) + info[:, 4:6].astype(jnp.int32)
    return pos.astype(jnp.int32), counts, offs


def _work_items(counts, offs, nw, layer):
    tm = EXPERT_TILE
    first_tile = offs // tm
    last_tile = (offs + counts - 1) // tm
    n_e = jnp.where(counts > 0, last_tile - first_tile + 1, 0)
    w_end = jnp.cumsum(n_e)
    w_start = w_end - n_e
    total = w_end[-1]
    w = jnp.arange(nw, dtype=jnp.int32)
    wc = jnp.minimum(w, total - 1)
    e_w = jnp.sum((wc[:, None] >= w_end[None, :]).astype(jnp.int32), axis=-1)
    sel = (e_w[:, None] == jnp.arange(N_EXPERTS, dtype=jnp.int32)).astype(jnp.int32)
    pick = lambda v: jnp.sum(sel * v[None, :], axis=-1)
    off_w = pick(offs)
    tile_w = pick(first_tile) + (wc - pick(w_start))
    lo = jnp.clip(off_w - tile_w * tm, 0, tm)
    hi = jnp.clip(off_w + pick(counts) - tile_w * tm, 0, tm)
    valid = w < total
    lo = jnp.where(valid, lo, 0)
    hi = jnp.where(valid, hi, 0)
    prev_tile = jnp.concatenate([jnp.full((1,), -1, tile_w.dtype), tile_w[:-1]])
    first = tile_w != prev_tile
    ordinal = pick(jnp.cumsum((counts > 0).astype(jnp.int32)) - 1)

    def lead(a, fill=None):
        head = jnp.broadcast_to(a[0] if fill is None else jnp.asarray(fill, a.dtype), (LEAD_ITEMS,))
        return jnp.concatenate([head, a]).astype(jnp.int32)

    return (lead(tile_w), lead(e_w + layer * N_EXPERTS), lead(lo, 0), lead(hi, 0), lead(first, 0), lead(ordinal))


def _grid_pos_embed(rows, d):
    t = jnp.arange(rows * GRID_W)
    row = (t // GRID_W).astype(F32)
    col = (t % GRID_W).astype(F32)
    nf = d // 4
    freq = 1.0 / (10000.0 ** (jnp.arange(nf, dtype=F32) / nf))
    er = row[:, None] * freq
    ec = col[:, None] * freq
    return jnp.concatenate([jnp.sin(er), jnp.cos(er), jnp.sin(ec), jnp.cos(ec)], axis=-1)


def _block_diag(w, nblk):
    *lead, H, d, _ = w.shape
    w = w.reshape(*lead, H // nblk, nblk, d, d)
    eye = jnp.eye(nblk, dtype=w.dtype)
    out = jnp.einsum('...gij,gh->...gihj', w, eye)
    return out.reshape(*lead, H // nblk, nblk * d, nblk * d)


def kernel(x_prompt, x_sample, state_lru, c, c_ctx, w_ada, b_ada, g_mix, g_ffn, g_final, w_in, w_out, conv_w, conv_b, lru_wa, lru_ba, lru_wx, lru_bx, lru_lambda, tmlp_ws, tmlp_b, fnet_w, router_g, router_g_b, router_e, router_e_b, e_w1, e_w3, e_w2):
    bp, lp, _ = x_prompt.shape
    bs, ls, _ = x_sample.shape
    n_ctx = bp * lp
    n_lat = bs * ls
    n_tok = n_ctx + n_lat

    cond8 = jnp.concatenate([c_ctx[None, :], c, jnp.zeros((SUBLANES - 1 - bs, D_MODEL), F32)], axis=0)
    pos = _grid_pos_embed(ls // GRID_W, D_MODEL)
    w_in_b = w_in.astype(BF16)
    w_out_b = w_out.astype(BF16)
    heads_per_cb = LRU_CB // RNN_HEAD_DIM
    wa_bd = (0.5 * _block_diag(lru_wa, heads_per_cb)).astype(BF16)
    wx_bd = (0.5 * _block_diag(lru_wx, heads_per_cb)).astype(BF16)
    ws_b = tmlp_ws.astype(BF16)
    bt = jnp.swapaxes(tmlp_b, 1, 2)
    fnet_bd = _block_diag(fnet_w, D_FNET // FNET_GROUP_DIM)[:, 0]
    wr = jnp.concatenate([router_g, router_e,
                          jnp.zeros((DEPTH, D_MODEL, ROUTER_LANES - N_EGROUPS - N_EXPERTS), F32)], axis=-1)
    br = jnp.concatenate([router_g_b, router_e_b,
                          jnp.zeros((DEPTH, ROUTER_LANES - N_EGROUPS - N_EXPERTS), F32)], axis=-1)
    ew1 = e_w1.reshape(DEPTH * N_EXPERTS, D_MODEL, D_EXPERT)
    ew3 = e_w3.reshape(DEPTH * N_EXPERTS, D_MODEL, D_EXPERT)
    ew2 = e_w2.reshape(DEPTH * N_EXPERTS, D_EXPERT, D_MODEL)

    mods = _modulation(cond8, w_ada, b_ada)
    fa, fb = _fnet_prep(fnet_bd)
    ct_p, st_p = _dft_tables(lp)
    ct_s, st_s = _dft_tables(ls)

    nw = 2 * n_tok // EXPERT_TILE + N_EXPERTS
    h0_p = jnp.zeros((bp, 2, D_RNN), F32)
    g_mix3 = g_mix[:, None, :]
    g_ffn3 = g_ffn[:, None, :]
    conv_b3 = conv_b[:, None, :]
    br3 = br[:, None, :]

    xp, xs = x_prompt, x_sample
    states = []
    for l in range(DEPTH):
        paths = []
        cnt = jnp.zeros((SUBLANES, ROUTER_LANES), F32)
        for (x, ct, st, h0, row_base, row_step, is_lat) in (
                (xp, ct_p, st_p, h0_p, 0, 0, False),
                (xs, ct_s, st_s, state_lru[:, l], 1, 1, True)):
            pe = pos if (is_lat and l == 0) else None
            zr, zuv, va, vb = _in_proj(x, pe, mods, g_mix3, w_in_b, fa, fb,
                                       layer=l, row_base=row_base, row_step=row_step)
            yr, st_new = _lru_mixer(zr, conv_w, conv_b3, wa_bd, wx_bd, lru_ba, lru_bx, lru_lambda, h0, layer=l)
            yf = _dft_apply(ct, st, va, vb)
            x1, h2, ri, cnt = _out_proj(x, pe, yr, zuv, yf, mods, ws_b, bt, w_out_b, g_ffn3, wr, br3, cnt,
                                        layer=l, row_base=row_base, row_step=row_step)
            paths.append((x1, h2, ri, st_new))
        states.append(paths[0][3])

        pos_all, counts, offs = _route(paths[0][2].reshape(n_ctx, ROUTER_LANES),
                                       paths[1][2].reshape(n_lat, ROUTER_LANES), cnt)
        items = _work_items(counts, offs, nw, l)
        pos_tiles = jnp.swapaxes((pos_all * LANE_TILES).reshape(n_tok // ROW_TILE, ROW_TILE, 2), 1, 2)
        nct = n_ctx // ROW_TILE
        xsorted = _dispatch(pos_tiles, paths[0][1], paths[1][1])
        ysorted = _experts(items, xsorted, ew1, ew3, ew2)
        gfin = g_final[None, :]
        final = l == DEPTH - 1
        xp = _combine(pos_tiles[:nct], paths[0][0].reshape(n_ctx, D_MODEL),
                      paths[0][2].reshape(n_ctx, ROUTER_LANES), mods, gfin, ysorted, layer=l,
                      row_base=0, row_step=0, tiles_per_seq=lp // ROW_TILE, final=final).reshape(bp, lp, D_MODEL)
        xs = _combine(pos_tiles[nct:], paths[1][0].reshape(n_lat, D_MODEL),
                      paths[1][2].reshape(n_lat, ROUTER_LANES), mods, gfin, ysorted, layer=l,
                      row_base=1, row_step=1, tiles_per_seq=ls // ROW_TILE, final=final).reshape(bs, ls, D_MODEL)

    new_state = jnp.stack(states, axis=1).astype(x_prompt.dtype)
    return (xp, xs, new_state)
```

```python
import functools
import math

import jax
import jax.numpy as jnp
from jax import lax
from jax.experimental import pallas as pl
from jax.experimental.pallas import tpu as pltpu

F32 = jnp.float32
BF16 = jnp.bfloat16

D_MODEL = 1024
DEPTH = 2
GRID_W = 64
D_RNN = 512
RNN_HEAD_DIM = 64
CONV_W = 4
LRU_C = 8.0
D_TMLP = 256
TMLP_HEADS = 4
CHUNK = 128
D_FNET = 256
FNET_GROUP_DIM = 64
D_IN = 2 * D_RNN + 2 * D_TMLP + D_FNET
N_EGROUPS = 4
N_EPG = 8
N_EXPERTS = N_EGROUPS * N_EPG
D_EXPERT = 512
N_MOD = 6
EPS = 1e-6

LANES = 128
SUBLANES = 8
LANE_TILES = D_MODEL // LANES
assert LANE_TILES == SUBLANES
LRU_CB = LANES
ROW_TILE = 256
PROJ_TILE = 512
EXPERT_TILE = 256
ROUTER_LANES = LANES
NEG_BIG = -1e30
VMEM_LIMIT = 56 * 1024 * 1024


def _cparams(sem):
    return pltpu.CompilerParams(dimension_semantics=sem, vmem_limit_bytes=VMEM_LIMIT)


def _mod_kernel(c_ref, w_ref, b_ref, o_ref):
    c = c_ref[...]
    s = c * jax.nn.sigmoid(c)
    o_ref[0] = jnp.dot(s.astype(BF16), w_ref[0].astype(BF16), preferred_element_type=F32) + b_ref[0]


def _modulation(cond8, w_ada, b_ada):
    tn = 1536
    return pl.pallas_call(
        _mod_kernel,
        grid=(DEPTH, N_MOD * D_MODEL // tn),
        in_specs=[
            pl.BlockSpec((SUBLANES, D_MODEL), lambda l, j: (0, 0)),
            pl.BlockSpec((1, D_MODEL, tn), lambda l, j: (l, 0, j)),
            pl.BlockSpec((1, 1, tn), lambda l, j: (l, 0, j)),
        ],
        out_specs=pl.BlockSpec((1, SUBLANES, tn), lambda l, j: (l, 0, j)),
        out_shape=jax.ShapeDtypeStruct((DEPTH, SUBLANES, N_MOD * D_MODEL), F32),
        compiler_params=_cparams(("arbitrary", "arbitrary")),
        name="modulation",
    )(cond8, w_ada, b_ada.reshape(DEPTH, 1, N_MOD * D_MODEL))


def _fprep_kernel(w_ref, a_ref, b_ref):
    r = lax.broadcasted_iota(jnp.int32, (D_FNET, D_FNET), 0)
    c = lax.broadcasted_iota(jnp.int32, (D_FNET, D_FNET), 1)
    same = (r >> 6) == (c >> 6)
    ph = ((r & 63) * (c & 63)) & 63
    ang = ph.astype(F32) * (2.0 * math.pi / FNET_GROUP_DIM)
    scale = 1.0 / math.sqrt(FNET_GROUP_DIM)
    cm = jnp.where(same, jnp.cos(ang) * scale, 0.0)
    sm = jnp.where(same, jnp.sin(ang) * scale, 0.0)
    w = w_ref[0]
    a_ref[0] = jnp.dot(cm, w, precision=lax.Precision.HIGHEST, preferred_element_type=F32).astype(BF16)
    b_ref[0] = jnp.dot(sm, w, precision=lax.Precision.HIGHEST, preferred_element_type=F32).astype(BF16)


def _fnet_prep(wbd):
    spec = pl.BlockSpec((1, D_FNET, D_FNET), lambda l: (l, 0, 0))
    return pl.pallas_call(
        _fprep_kernel,
        grid=(DEPTH,),
        in_specs=[spec],
        out_specs=[spec, spec],
        out_shape=[jax.ShapeDtypeStruct((DEPTH, D_FNET, D_FNET), BF16)] * 2,
        compiler_params=_cparams(("arbitrary",)),
        name="fnet_prep",
    )(wbd)


TABLE_ROWS = 64


def _base_table_kernel(cj_ref, sj_ref, cm_ref, sm_ref, *, L):
    j = lax.broadcasted_iota(jnp.int32, (TABLE_ROWS, L), 0)
    n = lax.broadcasted_iota(jnp.int32, (TABLE_ROWS, L), 1)
    w = 2.0 * math.pi / L
    fine = ((j * n) & (L - 1)).astype(F32) * w
    coarse = ((j * TABLE_ROWS * n) & (L - 1)).astype(F32) * w
    scale = 1.0 / math.sqrt(L)
    cj_ref[...] = jnp.cos(fine)
    sj_ref[...] = jnp.sin(fine)
    cm_ref[...] = jnp.cos(coarse) * scale
    sm_ref[...] = jnp.sin(coarse) * scale


def _table_kernel(cj_ref, sj_ref, cm_ref, sm_ref, c_ref, s_ref):
    m = pl.program_id(0)
    c0 = cm_ref[pl.ds(m, 1), :]
    s0 = sm_ref[pl.ds(m, 1), :]
    cj = cj_ref[...]
    sj = sj_ref[...]
    c_ref[...] = (cj * c0 - sj * s0).astype(BF16)
    s_ref[...] = (-(sj * c0 + cj * s0)).astype(BF16)


def _dft_tables(L):
    small = pl.BlockSpec((TABLE_ROWS, L), lambda *_: (0, 0))
    base = pl.pallas_call(
        functools.partial(_base_table_kernel, L=L),
        out_specs=[small] * 4,
        out_shape=[jax.ShapeDtypeStruct((TABLE_ROWS, L), F32)] * 4,
        compiler_params=pltpu.CompilerParams(vmem_limit_bytes=VMEM_LIMIT),
        name=f"dft_base_tables_{L}",
    )()
    spec = pl.BlockSpec((TABLE_ROWS, L), lambda i: (i, 0))
    return pl.pallas_call(
        _table_kernel,
        grid=(L // TABLE_ROWS,),
        in_specs=[small] * 4,
        out_specs=[spec, spec],
        out_shape=[jax.ShapeDtypeStruct((L, L), BF16)] * 2,
        compiler_params=_cparams(("arbitrary",)),
        name=f"dft_tables_{L}",
    )(*base)


def _dft_kernel(c_ref, s_ref, va_ref, vb_ref, o_ref):
    o_ref[...] = (jnp.dot(c_ref[...], va_ref[...], preferred_element_type=F32)
                  + jnp.dot(s_ref[...], vb_ref[...], preferred_element_type=F32))


def _dft_apply(ct, st, va, vb):
    L, ncols = va.shape
    tk, tc = 256, 1024
    return pl.pallas_call(
        _dft_kernel,
        grid=(L // tk, ncols // tc),
        in_specs=[
            pl.BlockSpec((tk, L), lambda i, j: (i, 0)),
            pl.BlockSpec((tk, L), lambda i, j: (i, 0)),
            pl.BlockSpec((L, tc), lambda i, j: (0, j)),
            pl.BlockSpec((L, tc), lambda i, j: (0, j)),
        ],
        out_specs=pl.BlockSpec((tk, tc), lambda i, j: (i, j)),
        out_shape=jax.ShapeDtypeStruct((L, ncols), F32),
        compiler_params=_cparams(("arbitrary", "arbitrary")),
        name=f"dft_apply_{L}",
    )(ct, st, va, vb)


def _rms_mod(x, g, scale, shift):
    ms = jnp.mean(x * x, axis=-1, keepdims=True)
    return (x * lax.rsqrt(ms + EPS)) * g * (1.0 + scale) + shift


def _pos_table_kernel(s_ref, c_ref):
    n, nf = s_ref.shape
    j = lax.broadcasted_iota(jnp.int32, (n, nf), 0).astype(F32)
    k = lax.broadcasted_iota(jnp.int32, (n, nf), 1).astype(F32)
    ang = j * jnp.exp(k * (-math.log(10000.0) / nf))
    s_ref[...] = jnp.sin(ang)
    c_ref[...] = jnp.cos(ang)


def _pos_tables(n):
    nf = D_MODEL // 4
    return pl.pallas_call(
        _pos_table_kernel,
        out_shape=[jax.ShapeDtypeStruct((n, nf), F32)] * 2,
        name="pos_tables",
    )()


def _pos_tile(ps_ref, pc_ref, t, tm):
    nrow = tm // GRID_W
    nf = D_MODEL // 4

    def rows(tab):
        return jnp.concatenate(
            [jnp.broadcast_to(tab[pl.ds(t * nrow + k, 1), :], (GRID_W, nf)) for k in range(nrow)], axis=0)

    def cols(tab):
        return jnp.concatenate([tab[0:GRID_W, :]] * nrow, axis=0)

    return jnp.concatenate([rows(ps_ref), rows(pc_ref), cols(ps_ref), cols(pc_ref)], axis=-1)


def _in_kernel(*refs, add_pos, row_base, row_step):
    if add_pos:
        x_ref, ps_ref, pc_ref, mod_ref, g_ref, w_ref, a_ref, b_ref, zr_ref, zuv_ref, va_ref, vb_ref = refs
    else:
        x_ref, mod_ref, g_ref, w_ref, a_ref, b_ref, zr_ref, zuv_ref, va_ref, vb_ref = refs
    x = x_ref[0]
    if add_pos:
        x = x + _pos_tile(ps_ref, pc_ref, pl.program_id(1), x.shape[0])
    row = row_base + pl.program_id(0) * row_step
    m = mod_ref[pl.ds(row, 1), :]
    h = _rms_mod(x, g_ref[...], m[:, D_MODEL:2 * D_MODEL], m[:, 0:D_MODEL])
    z = jnp.dot(h.astype(BF16), w_ref[...], preferred_element_type=F32)
    zr_ref[0] = z[:, 0:2 * D_RNN]
    zuv_ref[0] = z[:, 2 * D_RNN:2 * D_RNN + 2 * D_TMLP]
    zf = z[:, 2 * D_RNN + 2 * D_TMLP:D_IN].astype(BF16)
    va_ref[...] = jnp.dot(zf, a_ref[...], preferred_element_type=F32).astype(BF16)
    vb_ref[...] = jnp.dot(zf, b_ref[...], preferred_element_type=F32).astype(BF16)


def _in_proj(x, pos, mods, g, w_in_b, fa, fb, *, layer, row_base, row_step):
    nseq, L, _ = x.shape
    tm = min(L, PROJ_TILE)
    add_pos = pos is not None
    in_specs = [pl.BlockSpec((1, tm, D_MODEL), lambda b, t: (b, t, 0))]
    args = [x]
    if add_pos:
        in_specs += [pl.BlockSpec(p.shape, lambda b, t: (0, 0)) for p in pos]
        args += list(pos)
    in_specs += [
        _layer_spec((SUBLANES, N_MOD * D_MODEL), layer),
        _layer_spec((1, D_MODEL), layer),
        _layer_spec((D_MODEL, D_IN), layer),
        _layer_spec((D_FNET, D_FNET), layer),
        _layer_spec((D_FNET, D_FNET), layer),
    ]
    args += [mods, g, w_in_b, fa, fb]
    return pl.pallas_call(
        functools.partial(_in_kernel, add_pos=add_pos, row_base=row_base, row_step=row_step),
        grid=(nseq, L // tm),
        in_specs=in_specs,
        out_specs=[
            pl.BlockSpec((1, tm, 2 * D_RNN), lambda b, t: (b, t, 0)),
            pl.BlockSpec((1, tm, 2 * D_TMLP), lambda b, t: (b, t, 0)),
            pl.BlockSpec((tm, D_FNET), lambda b, t: (t, b)),
            pl.BlockSpec((tm, D_FNET), lambda b, t: (t, b)),
        ],
        out_shape=[
            jax.ShapeDtypeStruct((nseq, L, 2 * D_RNN), F32),
            jax.ShapeDtypeStruct((nseq, L, 2 * D_TMLP), F32),
            jax.ShapeDtypeStruct((L, nseq * D_FNET), BF16),
            jax.ShapeDtypeStruct((L, nseq * D_FNET), BF16),
        ],
        compiler_params=_cparams(("arbitrary", "arbitrary")),
        name=f"in_proj_{L}",
    )(*args)


def _gelu_tanh(x):
    return 0.5 * x * (1.0 + jnp.tanh(math.sqrt(2.0 / math.pi) * (x + 0.044715 * (x * x * x))))


def _rows_to_tile(rows):
    sub = lax.broadcasted_iota(jnp.int32, (SUBLANES, LANES), 0)
    out = jnp.zeros((SUBLANES, LANES), F32)
    for s, r in enumerate(rows):
        out = jnp.where(sub == s, jnp.broadcast_to(r, (SUBLANES, LANES)), out)
    return out


def _lru_kernel(xr_ref, gr_ref, cw_ref, cb_ref, wa_ref, wx_ref, ba_ref, bx_ref, lam_ref, h0_ref,
                y_ref, st_ref, xnat, pext, af, bf, ab, bb, hnat, *, L):
    S = L // SUBLANES
    pitch = S + SUBLANES
    n = S * SUBLANES
    chunk = 256

    for s in range(SUBLANES):
        xnat[s * pitch:s * pitch + S, :] = xr_ref[0, s * S:(s + 1) * S, :]

    def perm_in(j, c):
        dst = pl.multiple_of((j + 2) * SUBLANES, SUBLANES)
        pext[pl.ds(dst, SUBLANES), :] = xnat[pl.ds(j, SUBLANES, stride=pitch), :]
        return c

    lax.fori_loop(0, S, perm_in, 0, unroll=8)

    sub = lax.broadcasted_iota(jnp.int32, (SUBLANES, LANES), 0)

    def from_prev_segment(v):
        return jnp.where(sub == 0, 0.0, pltpu.roll(v, 1, axis=0))

    def from_next_segment(v):
        return jnp.where(sub == SUBLANES - 1, 0.0, pltpu.roll(v, SUBLANES - 1, axis=0))

    pext[0:8, :] = from_prev_segment(pext[S * 8:(S + 1) * 8, :])
    pext[8:16, :] = from_prev_segment(pext[(S + 1) * 8:(S + 2) * 8, :])
    pext[(S + 2) * 8:(S + 3) * 8, :] = from_next_segment(pext[16:24, :])

    lam = lam_ref[...]
    nl = -lam
    sp = jnp.maximum(nl, 0.0) + jnp.log1p(jnp.exp(-jnp.abs(nl)))
    c_la = (-0.5 * LRU_C) * sp
    ba_h = 0.5 * ba_ref[...]
    bx_h = 0.5 * bx_ref[...]
    a_refs = (af, ab)
    b_refs = (bf, bb)

    def gates(i, c):
        base = pl.multiple_of(i * chunk, chunk)
        xc = (cw_ref[0:1, :] * pext[pl.ds(base, chunk), :]
              + cw_ref[1:2, :] * pext[pl.ds(base + 8, chunk), :]
              + cw_ref[2:3, :] * pext[pl.ds(base + 16, chunk), :]
              + cw_ref[3:4, :] * pext[pl.ds(base + 24, chunk), :]
              + cb_ref[...])
        xcb = xc.astype(BF16)
        xh = 0.5 * xc
        for d in range(2):
            tr = jnp.tanh(jnp.dot(xcb, wa_ref[d, 0], preferred_element_type=F32) + ba_h[d:d + 1, :])
            ti = jnp.tanh(jnp.dot(xcb, wx_ref[d, 0], preferred_element_type=F32) + bx_h[d:d + 1, :])
            log_a = c_la[d:d + 1, :] * (1.0 + tr)
            a = jnp.exp(log_a)
            v = jnp.tanh(log_a) * (-1.0 - a * a)
            coef = jnp.where(v > 0.0, v * lax.rsqrt(v), 0.0)
            a_refs[d][pl.ds(base, chunk), :] = a
            b_refs[d][pl.ds(base, chunk), :] = coef * ((1.0 + ti) * xh)
        return c

    lax.fori_loop(0, n // chunk, gates, 0)

    def scan(j, carry):
        hf, pf, hb, pb = carry
        jf = pl.multiple_of(j * SUBLANES, SUBLANES)
        jb = pl.multiple_of((S - 1 - j) * SUBLANES, SUBLANES)
        a1 = af[pl.ds(jf, SUBLANES), :]
        hf = a1 * hf + bf[pl.ds(jf, SUBLANES), :]
        pf = a1 * pf
        bf[pl.ds(jf, SUBLANES), :] = hf
        af[pl.ds(jf, SUBLANES), :] = pf
        a2 = ab[pl.ds(jb, SUBLANES), :]
        hb = a2 * hb + bb[pl.ds(jb, SUBLANES), :]
        pb = a2 * pb
        bb[pl.ds(jb, SUBLANES), :] = hb
        ab[pl.ds(jb, SUBLANES), :] = pb
        return hf, pf, hb, pb

    zero = jnp.zeros((SUBLANES, LANES), F32)
    one = jnp.ones((SUBLANES, LANES), F32)
    hf, pf, hb, pb = lax.fori_loop(0, S, scan, (zero, one, zero, one), unroll=8)

    rows_f = [h0_ref[0, 0:1, :]]
    for s in range(1, SUBLANES):
        rows_f.append(pf[s - 1:s, :] * rows_f[-1] + hf[s - 1:s, :])
    st_ref[0, 0:1, :] = pf[7:8, :] * rows_f[7] + hf[7:8, :]
    rows_b = [None] * SUBLANES
    rows_b[7] = h0_ref[0, 1:2, :]
    for s in range(SUBLANES - 2, -1, -1):
        rows_b[s] = pb[s + 1:s + 2, :] * rows_b[s + 1] + hb[s + 1:s + 2, :]
    st_ref[0, 1:2, :] = pb[0:1, :] * rows_b[0] + hb[0:1, :]
    init_f = _rows_to_tile(rows_f)
    init_b = _rows_to_tile(rows_b)

    def perm_out(j, c):
        src = pl.multiple_of(j * SUBLANES, SUBLANES)
        v = (bf[pl.ds(src, SUBLANES), :] + af[pl.ds(src, SUBLANES), :] * init_f
             + bb[pl.ds(src, SUBLANES), :] + ab[pl.ds(src, SUBLANES), :] * init_b)
        hnat[pl.ds(j, SUBLANES, stride=pitch), :] = v
        return c

    lax.fori_loop(0, S, perm_out, 0, unroll=8)

    for s in range(SUBLANES):
        g = gr_ref[0, s * S:(s + 1) * S, :]
        y_ref[0, s * S:(s + 1) * S, :] = hnat[s * pitch:s * pitch + S, :] * _gelu_tanh(g)


def _lru_mixer(zr, conv_w, conv_b, wa_bd, wx_bd, ba, bx, lam, h0, *, layer):
    nseq, L, _ = zr.shape
    ncb = D_RNN // LRU_CB
    S = L // SUBLANES
    pitch = S + SUBLANES
    vec2 = pl.BlockSpec((None, 2, LRU_CB), lambda b, c: (layer, 0, c))
    wspec = pl.BlockSpec((None, 2, 1, LRU_CB, LRU_CB), lambda b, c: (layer, 0, c, 0, 0))
    return pl.pallas_call(
        functools.partial(_lru_kernel, L=L),
        grid=(nseq, ncb),
        in_specs=[
            pl.BlockSpec((1, L, LRU_CB), lambda b, c: (b, 0, c)),
            pl.BlockSpec((1, L, LRU_CB), lambda b, c: (b, 0, c + ncb)),
            pl.BlockSpec((None, CONV_W, LRU_CB), lambda b, c: (layer, 0, c)),
            pl.BlockSpec((None, 1, LRU_CB), lambda b, c: (layer, 0, c)),
            wspec, wspec, vec2, vec2, vec2,
            pl.BlockSpec((1, 2, LRU_CB), lambda b, c: (b, 0, c)),
        ],
        out_specs=[
            pl.BlockSpec((1, L, LRU_CB), lambda b, c: (b, 0, c)),
            pl.BlockSpec((1, 2, LRU_CB), lambda b, c: (b, 0, c)),
        ],
        out_shape=[
            jax.ShapeDtypeStruct((nseq, L, D_RNN), F32),
            jax.ShapeDtypeStruct((nseq, 2, D_RNN), F32),
        ],
        scratch_shapes=[
            pltpu.VMEM((SUBLANES * pitch, LANES), F32),
            pltpu.VMEM(((S + 3) * SUBLANES, LANES), F32),
            pltpu.VMEM((L, LANES), F32),
            pltpu.VMEM((L, LANES), F32),
            pltpu.VMEM((L, LANES), F32),
            pltpu.VMEM((L, LANES), F32),
            pltpu.VMEM((SUBLANES * pitch, LANES), F32),
        ],
        compiler_params=_cparams(("arbitrary", "arbitrary")),
        name=f"lru_mixer_{L}",
    )(zr, zr, conv_w, conv_b, wa_bd, wx_bd, ba, bx, lam, h0)


def _out_kernel(*refs, add_pos, row_base, row_step):
    if add_pos:
        (x_ref, ps_ref, pc_ref, yr_ref, zuv_ref, yf_ref, mod_ref, ws_ref, bt_ref, wo_ref, g_ref, wr_ref, br_ref,
         cin_ref, x1_ref, h2_ref, ri_ref, cnt_ref, carry) = refs
    else:
        (x_ref, yr_ref, zuv_ref, yf_ref, mod_ref, ws_ref, bt_ref, wo_ref, g_ref, wr_ref, br_ref, cin_ref,
         x1_ref, h2_ref, ri_ref, cnt_ref, carry) = refs
    tm = x_ref.shape[1]
    x = x_ref[0]
    if add_pos:
        x = x + _pos_tile(ps_ref, pc_ref, pl.program_id(1), tm)
    row = row_base + pl.program_id(0) * row_step
    m = mod_ref[pl.ds(row, 1), :]
    g1 = m[:, 2 * D_MODEL:3 * D_MODEL]
    sh2 = m[:, 3 * D_MODEL:4 * D_MODEL]
    sc2 = m[:, 4 * D_MODEL:5 * D_MODEL]

    head = lax.broadcasted_iota(jnp.int32, (CHUNK, D_TMLP), 1) >> 6
    yt_parts = []
    for ci in range(tm // CHUNK):
        u = zuv_ref[0, ci * CHUNK:(ci + 1) * CHUNK, 0:D_TMLP]
        v = zuv_ref[0, ci * CHUNK:(ci + 1) * CHUNK, D_TMLP:2 * D_TMLP].astype(BF16)
        s = jnp.zeros((CHUNK, D_TMLP), F32)
        for h in range(TMLP_HEADS):
            sh = jnp.dot(ws_ref[h], v, preferred_element_type=F32) + bt_ref[:, h:h + 1]
            s = jnp.where(head == h, sh, s)
        yt_parts.append(u * s)
    yt = jnp.concatenate(yt_parts, axis=0) if len(yt_parts) > 1 else yt_parts[0]

    y = (jnp.dot(yr_ref[0].astype(BF16), wo_ref[0:D_RNN, :], preferred_element_type=F32)
         + jnp.dot(yt.astype(BF16), wo_ref[D_RNN:D_RNN + D_TMLP, :], preferred_element_type=F32)
         + jnp.dot(yf_ref[...].astype(BF16), wo_ref[D_RNN + D_TMLP:D_MODEL, :], preferred_element_type=F32))
    x1 = x + g1 * y
    x1_ref[0] = x1
    h2 = _rms_mod(x1, g_ref[...], sc2, sh2)
    _store_token_major(h2_ref, h2)

    wr = wr_ref[...]
    w_hi = wr.astype(BF16)
    w_lo = (wr - w_hi.astype(F32)).astype(BF16)
    h_hi = h2.astype(BF16)
    h_lo = (h2 - h_hi.astype(F32)).astype(BF16)
    p_hi = jnp.dot(h_hi, jnp.concatenate([w_hi, w_lo], axis=-1), preferred_element_type=F32)
    p_lo = jnp.dot(h_lo, w_hi, preferred_element_type=F32)
    logits = p_hi[:, 0:ROUTER_LANES] + p_hi[:, ROUTER_LANES:2 * ROUTER_LANES] + p_lo + br_ref[...]
    lane = lax.broadcasted_iota(jnp.int32, (tm, ROUTER_LANES), 1)
    lane_f = lane.astype(F32)
    is_g = lane < N_EGROUPS
    gl = jnp.where(is_g, logits, NEG_BIG)
    gmax = jnp.max(gl, axis=-1, keepdims=True)
    gsel = jnp.min(jnp.where(gl == gmax, lane_f, 1e4), axis=-1, keepdims=True)
    pg = 1.0 / jnp.sum(jnp.where(is_g, jnp.exp(logits - gmax), 0.0), axis=-1, keepdims=True)
    grp_f = ((lane - N_EGROUPS) >> 3).astype(F32)
    emask = (lane >= N_EGROUPS) & (lane < N_EGROUPS + N_EXPERTS) & (grp_f == gsel)
    el = jnp.where(emask, logits, NEG_BIG)
    v1 = jnp.max(el, axis=-1, keepdims=True)
    i1 = jnp.min(jnp.where(el == v1, lane_f, 1e4), axis=-1, keepdims=True)
    el2 = jnp.where(lane_f == i1, NEG_BIG, el)
    v2 = jnp.max(el2, axis=-1, keepdims=True)
    i2 = jnp.min(jnp.where(el2 == v2, lane_f, 1e4), axis=-1, keepdims=True)
    e2x = jnp.exp(v2 - v1)
    fw1 = 1.0 / (1.0 + e2x)
    fw2 = e2x * fw1
    @pl.when((pl.program_id(0) == 0) & (pl.program_id(1) == 0))
    def _():
        carry[...] = cin_ref[...]

    e1 = i1 - N_EGROUPS
    e2 = i2 - N_EGROUPS
    m1 = lane_f == e1
    m2 = lane_f == e2
    oh = jnp.where(m1 | m2, 1.0, 0.0)
    r_i = lax.broadcasted_iota(jnp.int32, (tm, tm), 0)
    c_i = lax.broadcasted_iota(jnp.int32, (tm, tm), 1)
    tri = jnp.where(c_i < r_i, 1.0, 0.0).astype(BF16)
    before = jnp.dot(tri, oh.astype(BF16), preferred_element_type=F32) + carry[0:1, :]
    rank1 = jnp.sum(jnp.where(m1, before, 0.0), axis=-1, keepdims=True)
    rank2 = jnp.sum(jnp.where(m2, before, 0.0), axis=-1, keepdims=True)
    total = carry[0:1, :] + jnp.sum(oh, axis=0, keepdims=True)
    carry[0:1, :] = total
    cnt_ref[...] = jnp.broadcast_to(total, cnt_ref.shape)

    vals = (e1, e2, pg * fw1, pg * fw2, rank1, rank2)
    ri = jnp.zeros((tm, ROUTER_LANES), F32)
    for k, v in enumerate(vals):
        ri = jnp.where(lane == k, v, ri)
    ri_ref[0] = ri


def _layer_spec(shape, layer):
    zeros = (0,) * len(shape)
    return pl.BlockSpec((None, *shape), lambda *_: (layer, *zeros))


def _out_proj(x, pos, yr, zuv, yf, mods, ws_b, bt, wo_b, g, wr, br, cin, *, layer, row_base, row_step):
    nseq, L, _ = x.shape
    tm = min(L, PROJ_TILE)
    add_pos = pos is not None
    in_specs = [pl.BlockSpec((1, tm, D_MODEL), lambda b, t: (b, t, 0))]
    args = [x]
    if add_pos:
        in_specs += [pl.BlockSpec(p.shape, lambda b, t: (0, 0)) for p in pos]
        args += list(pos)
    in_specs += [
        pl.BlockSpec((1, tm, D_RNN), lambda b, t: (b, t, 0)),
        pl.BlockSpec((1, tm, 2 * D_TMLP), lambda b, t: (b, t, 0)),
        pl.BlockSpec((tm, D_FNET), lambda b, t: (t, b)),
        _layer_spec((SUBLANES, N_MOD * D_MODEL), layer),
        _layer_spec((TMLP_HEADS, CHUNK, CHUNK), layer),
        _layer_spec((CHUNK, TMLP_HEADS), layer),
        _layer_spec((D_MODEL, D_MODEL), layer),
        _layer_spec((1, D_MODEL), layer),
        _layer_spec((D_MODEL, ROUTER_LANES), layer),
        _layer_spec((1, ROUTER_LANES), layer),
        pl.BlockSpec((SUBLANES, ROUTER_LANES), lambda b, t: (0, 0)),
    ]
    args += [yr, zuv, yf, mods, ws_b, bt, wo_b, g, wr, br, cin]
    tok = pl.BlockSpec((1, tm, D_MODEL), lambda b, t: (b, t, 0))
    return pl.pallas_call(
        functools.partial(_out_kernel, add_pos=add_pos, row_base=row_base, row_step=row_step),
        grid=(nseq, L // tm),
        in_specs=in_specs,
        out_specs=[tok, pl.BlockSpec((tm * LANE_TILES, LANES), lambda b, t: (b * (L // tm) + t, 0)),
                   pl.BlockSpec((1, tm, ROUTER_LANES), lambda b, t: (b, t, 0)),
                   pl.BlockSpec((SUBLANES, ROUTER_LANES), lambda b, t: (0, 0))],
        out_shape=[
            jax.ShapeDtypeStruct((nseq, L, D_MODEL), F32),
            jax.ShapeDtypeStruct((nseq * L * LANE_TILES, LANES), F32),
            jax.ShapeDtypeStruct((nseq, L, ROUTER_LANES), F32),
            jax.ShapeDtypeStruct((SUBLANES, ROUTER_LANES), F32),
        ],
        scratch_shapes=[pltpu.VMEM((SUBLANES, ROUTER_LANES), F32)],
        compiler_params=_cparams(("arbitrary", "arbitrary")),
        name=f"out_proj_{L}",
    )(*args)


def _store_token_major(ref, x):
    tm = x.shape[0]
    for j in range(LANE_TILES):
        ref[pl.ds(j, tm, stride=LANE_TILES), :] = x[:, j * LANES:(j + 1) * LANES]


def _load_token_major(ref):
    tm = ref.shape[0] // LANE_TILES
    return jnp.concatenate([ref[pl.ds(j, tm, stride=LANE_TILES), :] for j in range(LANE_TILES)], axis=-1)


def _token_copy(src_ref, src_tok, dst_ref, dst_tok, sem):
    return pltpu.make_async_copy(src_ref.at[pl.ds(pl.multiple_of(src_tok, LANE_TILES), LANE_TILES)],
                                 dst_ref.at[pl.ds(pl.multiple_of(dst_tok, LANE_TILES), LANE_TILES)], sem)


def _scatter_rows(pos_ref, h_ref, xs_ref, sem):
    rows = h_ref.shape[0]

    for r in range(0, rows, LANE_TILES):
        _token_copy(h_ref, r, xs_ref, pos_ref[0, 0, r // LANE_TILES], sem).start(priority=0)
        _token_copy(h_ref, r, xs_ref, pos_ref[0, 1, r // LANE_TILES], sem).start(priority=1)
    for _ in range(2):
        pltpu.make_async_copy(h_ref, xs_ref.at[pl.ds(0, rows)], sem).wait()


def _dispatch_kernel(pos_ref, hp_ref, hs_ref, xs_ref, sem, *, n_first):
    i = pl.program_id(0)

    @pl.when(i < n_first)
    def _():
        _scatter_rows(pos_ref, hp_ref, xs_ref, sem)

    @pl.when(i >= n_first)
    def _():
        _scatter_rows(pos_ref, hs_ref, xs_ref, sem)


def _dispatch(pos, h_first, h_second):
    tm = ROW_TILE
    blk = tm * LANE_TILES
    n_first = h_first.shape[0] // blk
    n_second = h_second.shape[0] // blk
    nrows = 2 * (h_first.shape[0] + h_second.shape[0])
    return pl.pallas_call(
        functools.partial(_dispatch_kernel, n_first=n_first),
        grid=(n_first + n_second,),
        in_specs=[
            pl.BlockSpec((1, 2, tm), lambda i: (i, 0, 0), memory_space=pltpu.SMEM),
            pl.BlockSpec((blk, LANES), lambda i: (jnp.minimum(i, n_first - 1), 0)),
            pl.BlockSpec((blk, LANES), lambda i: (jnp.maximum(i - n_first, 0), 0)),
        ],
        out_specs=pl.BlockSpec(memory_space=pl.ANY),
        out_shape=jax.ShapeDtypeStruct((nrows, LANES), F32),
        scratch_shapes=[pltpu.SemaphoreType.DMA(())],
        compiler_params=_cparams(("arbitrary",)),
        name="moe_dispatch",
    )(pos, h_first, h_second)


WEIGHT_LEAD = (3, 2, 1)
WEIGHT_SLOTS = 4
LEAD_ITEMS = max(WEIGHT_LEAD)


def _expert_kernel(wt_ref, we_ref, lo_ref, hi_ref, first_ref, ord_ref, xs_ref, w1_ref, w3_ref, w2_ref, ys_ref,
                   w1b, w3b, w2b):
    s = pl.program_id(0)
    last = pl.num_programs(0) - 1
    tm = EXPERT_TILE

    def stage(lead, src, dst):
        cur = jnp.minimum(s + lead, last)
        prv = jnp.minimum(s + lead - 1, last)

        @pl.when((s == 0) | (we_ref[cur] != we_ref[prv]))
        def _():
            dst[ord_ref[cur] & (WEIGHT_SLOTS - 1)] = src[0].astype(BF16)

    for lead, src, dst in zip(WEIGHT_LEAD, (w1_ref, w3_ref, w2_ref), (w1b, w3b, w2b)):
        stage(lead, src, dst)

    lo = lo_ref[s]
    hi = hi_ref[s]
    full = (lo == 0) & (hi == tm)

    @pl.when((first_ref[s] == 1) & jnp.logical_not(full))
    def _():
        ys_ref[...] = jnp.zeros_like(ys_ref)

    @pl.when(hi > lo)
    def _():
        slot = ord_ref[s] & (WEIGHT_SLOTS - 1)
        x = _load_token_major(xs_ref).astype(BF16)
        a = jnp.dot(x, w1b[slot], preferred_element_type=F32)
        b = jnp.dot(x, w3b[slot], preferred_element_type=F32)
        hid = (a * jax.nn.sigmoid(a)) * b
        res = jnp.dot(hid.astype(BF16), w2b[slot], preferred_element_type=F32)

        @pl.when(full)
        def _():
            _store_token_major(ys_ref, res)

        @pl.when(jnp.logical_not(full))
        def _():
            rows = lax.broadcasted_iota(jnp.int32, res.shape, 0)
            _store_token_major(ys_ref, jnp.where((rows >= lo) & (rows < hi), res, _load_token_major(ys_ref)))


def _experts(items, xs, w1, w3, w2):
    nw = items[0].shape[0]

    def row(s, wt, we, lo, hi, fi, od):
        return (wt[s], 0)

    def weight(lead):
        return lambda s, wt, we, lo, hi, fi, od: (we[jnp.minimum(s + lead, nw - 1)], 0, 0)

    return pl.pallas_call(
        _expert_kernel,
        grid_spec=pltpu.PrefetchScalarGridSpec(
            num_scalar_prefetch=6,
            grid=(nw,),
            in_specs=[
                pl.BlockSpec((EXPERT_TILE * LANE_TILES, LANES), row),
                pl.BlockSpec((1, D_MODEL, D_EXPERT), weight(WEIGHT_LEAD[0])),
                pl.BlockSpec((1, D_MODEL, D_EXPERT), weight(WEIGHT_LEAD[1])),
                pl.BlockSpec((1, D_EXPERT, D_MODEL), weight(WEIGHT_LEAD[2])),
            ],
            out_specs=pl.BlockSpec((EXPERT_TILE * LANE_TILES, LANES), row),
            scratch_shapes=[
                pltpu.VMEM((WEIGHT_SLOTS, D_MODEL, D_EXPERT), BF16),
                pltpu.VMEM((WEIGHT_SLOTS, D_MODEL, D_EXPERT), BF16),
                pltpu.VMEM((WEIGHT_SLOTS, D_EXPERT, D_MODEL), BF16),
            ],
        ),
        out_shape=jax.ShapeDtypeStruct(xs.shape, F32),
        compiler_params=_cparams(("arbitrary",)),
        name="moe_experts",
    )(*items, xs, w1, w3, w2)


def _combine_kernel(pos_ref, posn_ref, x1_ref, ri_ref, mod_ref, gf_ref, ys_ref, o_ref, ybuf, sem,
                    *, row_base, row_step, tiles_per_seq, final):
    tm = x1_ref.shape[0]
    i = pl.program_id(0)
    n = pl.num_programs(0)

    def gather(p_ref, slot):
        for r in range(tm):
            _token_copy(ys_ref, p_ref[0, 0, r], ybuf.at[slot, 0], r * LANE_TILES, sem.at[slot]).start(priority=0)
            _token_copy(ys_ref, p_ref[0, 1, r], ybuf.at[slot, 1], r * LANE_TILES, sem.at[slot]).start(priority=1)

    def drain(slot):
        for k in range(2):
            pltpu.make_async_copy(ys_ref.at[pl.ds(0, tm * LANE_TILES)], ybuf.at[slot, k], sem.at[slot]).wait()

    def tile(slot):
        if slot == 0:
            @pl.when(i == 0)
            def _():
                gather(pos_ref, 0)

        @pl.when(i + 1 < n)
        def _():
            gather(posn_ref, 1 - slot)

        drain(slot)
        row = row_base + (i // tiles_per_seq) * row_step
        g2 = mod_ref[pl.ds(row, 1), 5 * D_MODEL:6 * D_MODEL]
        ri = ri_ref[...]
        y = (ri[:, 2:3] * _load_token_major(ybuf.at[slot, 0])
             + ri[:, 3:4] * _load_token_major(ybuf.at[slot, 1]))
        x2 = x1_ref[...] + g2 * y
        if final:
            ms = jnp.mean(x2 * x2, axis=-1, keepdims=True)
            o_ref[...] = (x2 * lax.rsqrt(ms + EPS)) * gf_ref[...]
        else:
            o_ref[...] = x2

    for slot in range(2):
        pl.when((i & 1) == slot)(functools.partial(tile, slot))


def _combine(pos, x1, ri, mods, g_final, ys, *, layer, row_base, row_step, tiles_per_seq, final):
    ntok = x1.shape[0]
    tm = ROW_TILE
    tok = pl.BlockSpec((tm, D_MODEL), lambda i: (i, 0))
    return pl.pallas_call(
        functools.partial(_combine_kernel, row_base=row_base, row_step=row_step, tiles_per_seq=tiles_per_seq,
                          final=final),
        grid=(ntok // tm,),
        in_specs=[
            pl.BlockSpec((1, 2, tm), lambda i: (i, 0, 0), memory_space=pltpu.SMEM),
            pl.BlockSpec((1, 2, tm), lambda i: (jnp.minimum(i + 1, ntok // tm - 1), 0, 0), memory_space=pltpu.SMEM),
            tok,
            pl.BlockSpec((tm, ROUTER_LANES), lambda i: (i, 0)),
            _layer_spec((SUBLANES, N_MOD * D_MODEL), layer),
            pl.BlockSpec((1, D_MODEL), lambda i: (0, 0)),
            pl.BlockSpec(memory_space=pl.ANY),
        ],
        out_specs=tok,
        out_shape=jax.ShapeDtypeStruct((ntok, D_MODEL), F32),
        scratch_shapes=[
            pltpu.VMEM((2, 2, tm * LANE_TILES, LANES), F32),
            pltpu.SemaphoreType.DMA((2,)),
        ],
        compiler_params=_cparams(("arbitrary",)),
        name=f"moe_combine_{ntok}",
    )(pos, pos, x1, ri, mods, g_final, ys)


def _route(ri_first, ri_second, cnt):
    counts = cnt[0, 0:N_EXPERTS].astype(jnp.int32)
    offs = jnp.cumsum(counts) - counts
    info = jnp.concatenate([ri_first, ri_second], axis=0)
    e = info[:, 0:2].astype(jnp.int32)
    oh = (e[:, :, None] == jnp.arange(N_EXPERTS, dtype=jnp.int32)).astype(jnp.int32)
    pos = jnp.sum(oh * offs[None, None, :], axis=-1) + info[:, 4:6].astype(jnp.int32)
    return pos.astype(jnp.int32), counts, offs


def _work_items(counts, offs, nw, layer):
    tm = EXPERT_TILE
    first_tile = offs // tm
    last_tile = (offs + counts - 1) // tm
    n_e = jnp.where(counts > 0, last_tile - first_tile + 1, 0)
    w_end = jnp.cumsum(n_e)
    w_start = w_end - n_e
    total = w_end[-1]
    w = jnp.arange(nw, dtype=jnp.int32)
    wc = jnp.minimum(w, total - 1)
    e_w = jnp.sum((wc[:, None] >= w_end[None, :]).astype(jnp.int32), axis=-1)
    sel = (e_w[:, None] == jnp.arange(N_EXPERTS, dtype=jnp.int32)).astype(jnp.int32)
    pick = lambda v: jnp.sum(sel * v[None, :], axis=-1)
    off_w = pick(offs)
    tile_w = pick(first_tile) + (wc - pick(w_start))
    lo = jnp.clip(off_w - tile_w * tm, 0, tm)
    hi = jnp.clip(off_w + pick(counts) - tile_w * tm, 0, tm)
    valid = w < total
    lo = jnp.where(valid, lo, 0)
    hi = jnp.where(valid, hi, 0)
    prev_tile = jnp.concatenate([jnp.full((1,), -1, tile_w.dtype), tile_w[:-1]])
    first = tile_w != prev_tile
    ordinal = pick(jnp.cumsum((counts > 0).astype(jnp.int32)) - 1)

    def lead(a, fill=None):
        head = jnp.broadcast_to(a[0] if fill is None else jnp.asarray(fill, a.dtype), (LEAD_ITEMS,))
        return jnp.concatenate([head, a]).astype(jnp.int32)

    return (lead(tile_w), lead(e_w + layer * N_EXPERTS), lead(lo, 0), lead(hi, 0), lead(first, 0), lead(ordinal))


def _block_diag(w, nblk):
    *lead, H, d, _ = w.shape
    w = w.reshape(*lead, H // nblk, nblk, d, d)
    eye = jnp.eye(nblk, dtype=w.dtype)
    out = jnp.einsum('...gij,gh->...gihj', w, eye)
    return out.reshape(*lead, H // nblk, nblk * d, nblk * d)


def kernel(x_prompt, x_sample, state_lru, c, c_ctx, w_ada, b_ada, g_mix, g_ffn, g_final, w_in, w_out, conv_w, conv_b, lru_wa, lru_ba, lru_wx, lru_bx, lru_lambda, tmlp_ws, tmlp_b, fnet_w, router_g, router_g_b, router_e, router_e_b, e_w1, e_w3, e_w2):
    bp, lp, _ = x_prompt.shape
    bs, ls, _ = x_sample.shape
    n_ctx = bp * lp
    n_lat = bs * ls
    n_tok = n_ctx + n_lat

    cond8 = jnp.concatenate([c_ctx[None, :], c, jnp.zeros((SUBLANES - 1 - bs, D_MODEL), F32)], axis=0)
    pos = _pos_tables(max(ls // GRID_W, GRID_W))
    w_in_b = w_in.astype(BF16)
    w_out_b = w_out.astype(BF16)
    heads_per_cb = LRU_CB // RNN_HEAD_DIM
    wa_bd = (0.5 * _block_diag(lru_wa, heads_per_cb)).astype(BF16)
    wx_bd = (0.5 * _block_diag(lru_wx, heads_per_cb)).astype(BF16)
    ws_b = tmlp_ws.astype(BF16)
    bt = jnp.swapaxes(tmlp_b, 1, 2)
    fnet_bd = _block_diag(fnet_w, D_FNET // FNET_GROUP_DIM)[:, 0]
    wr = jnp.concatenate([router_g, router_e,
                          jnp.zeros((DEPTH, D_MODEL, ROUTER_LANES - N_EGROUPS - N_EXPERTS), F32)], axis=-1)
    br = jnp.concatenate([router_g_b, router_e_b,
                          jnp.zeros((DEPTH, ROUTER_LANES - N_EGROUPS - N_EXPERTS), F32)], axis=-1)
    ew1 = e_w1.reshape(DEPTH * N_EXPERTS, D_MODEL, D_EXPERT)
    ew3 = e_w3.reshape(DEPTH * N_EXPERTS, D_MODEL, D_EXPERT)
    ew2 = e_w2.reshape(DEPTH * N_EXPERTS, D_EXPERT, D_MODEL)

    mods = _modulation(cond8, w_ada, b_ada)
    fa, fb = _fnet_prep(fnet_bd)
    ct_p, st_p = _dft_tables(lp)
    ct_s, st_s = _dft_tables(ls)

    nw = 2 * n_tok // EXPERT_TILE + N_EXPERTS
    h0_p = jnp.zeros((bp, 2, D_RNN), F32)
    g_mix3 = g_mix[:, None, :]
    g_ffn3 = g_ffn[:, None, :]
    conv_b3 = conv_b[:, None, :]
    br3 = br[:, None, :]

    xp, xs = x_prompt, x_sample
    states = []
    for l in range(DEPTH):
        paths = []
        cnt = jnp.zeros((SUBLANES, ROUTER_LANES), F32)
        for (x, ct, st, h0, row_base, row_step, is_lat) in (
                (xp, ct_p, st_p, h0_p, 0, 0, False),
                (xs, ct_s, st_s, state_lru[:, l], 1, 1, True)):
            pe = pos if (is_lat and l == 0) else None
            zr, zuv, va, vb = _in_proj(x, pe, mods, g_mix3, w_in_b, fa, fb,
                                       layer=l, row_base=row_base, row_step=row_step)
            yr, st_new = _lru_mixer(zr, conv_w, conv_b3, wa_bd, wx_bd, lru_ba, lru_bx, lru_lambda, h0, layer=l)
            yf = _dft_apply(ct, st, va, vb)
            x1, h2, ri, cnt = _out_proj(x, pe, yr, zuv, yf, mods, ws_b, bt, w_out_b, g_ffn3, wr, br3, cnt,
                                        layer=l, row_base=row_base, row_step=row_step)
            paths.append((x1, h2, ri, st_new))
        states.append(paths[0][3])

        pos_all, counts, offs = _route(paths[0][2].reshape(n_ctx, ROUTER_LANES),
                                       paths[1][2].reshape(n_lat, ROUTER_LANES), cnt)
        items = _work_items(counts, offs, nw, l)
        pos_tiles = jnp.swapaxes((pos_all * LANE_TILES).reshape(n_tok // ROW_TILE, ROW_TILE, 2), 1, 2)
        nct = n_ctx // ROW_TILE
        xsorted = _dispatch(pos_tiles, paths[0][1], paths[1][1])
        ysorted = _experts(items, xsorted, ew1, ew3, ew2)
        gfin = g_final[None, :]
        final = l == DEPTH - 1
        xp = _combine(pos_tiles[:nct], paths[0][0].reshape(n_ctx, D_MODEL),
                      paths[0][2].reshape(n_ctx, ROUTER_LANES), mods, gfin, ysorted, layer=l,
                      row_base=0, row_step=0, tiles_per_seq=lp // ROW_TILE, final=final).reshape(bp, lp, D_MODEL)
        xs = _combine(pos_tiles[nct:], paths[1][0].reshape(n_lat, D_MODEL),
                      paths[1][2].reshape(n_lat, ROUTER_LANES), mods, gfin, ysorted, layer=l,
                      row_base=1, row_step=1, tiles_per_seq=ls // ROW_TILE, final=final).reshape(bs, ls, D_MODEL)

    new_state = jnp.stack(states, axis=1).astype(x_prompt.dtype)
    return (xp, xs, new_state)
```

```python
import functools
import math

import jax
import jax.numpy as jnp
from jax import lax
from jax.experimental import pallas as pl
from jax.experimental.pallas import tpu as pltpu

F32 = jnp.float32
BF16 = jnp.bfloat16

D_MODEL = 1024
DEPTH = 2
GRID_W = 64
D_RNN = 512
RNN_HEAD_DIM = 64
CONV_W = 4
LRU_C = 8.0
D_TMLP = 256
TMLP_HEADS = 4
CHUNK = 128
D_FNET = 256
FNET_GROUP_DIM = 64
D_IN = 2 * D_RNN + 2 * D_TMLP + D_FNET
N_EGROUPS = 4
N_EPG = 8
N_EXPERTS = N_EGROUPS * N_EPG
D_EXPERT = 512
N_MOD = 6
EPS = 1e-6

LANES = 128
SUBLANES = 8
LANE_TILES = D_MODEL // LANES
assert LANE_TILES == SUBLANES
LRU_CB = LANES
LRU_SUB = 4
ROW_TILE = 256
PROJ_TILE = 512
EXPERT_TILE = 256
ROUTER_LANES = LANES
NEG_BIG = -1e30
VMEM_LIMIT = 56 * 1024 * 1024


def _cparams(sem):
    return pltpu.CompilerParams(dimension_semantics=sem, vmem_limit_bytes=VMEM_LIMIT)


def _mod_kernel(c_ref, w_ref, b_ref, o_ref):
    c = c_ref[...]
    s = c * jax.nn.sigmoid(c)
    o_ref[0] = jnp.dot(s.astype(BF16), w_ref[0].astype(BF16), preferred_element_type=F32) + b_ref[0]


def _modulation(cond8, w_ada, b_ada):
    tn = 1536
    return pl.pallas_call(
        _mod_kernel,
        grid=(DEPTH, N_MOD * D_MODEL // tn),
        in_specs=[
            pl.BlockSpec((SUBLANES, D_MODEL), lambda l, j: (0, 0)),
            pl.BlockSpec((1, D_MODEL, tn), lambda l, j: (l, 0, j)),
            pl.BlockSpec((1, 1, tn), lambda l, j: (l, 0, j)),
        ],
        out_specs=pl.BlockSpec((1, SUBLANES, tn), lambda l, j: (l, 0, j)),
        out_shape=jax.ShapeDtypeStruct((DEPTH, SUBLANES, N_MOD * D_MODEL), F32),
        compiler_params=_cparams(("arbitrary", "arbitrary")),
        name="modulation",
    )(cond8, w_ada, b_ada.reshape(DEPTH, 1, N_MOD * D_MODEL))


def _fprep_kernel(w_ref, a_ref, b_ref):
    r = lax.broadcasted_iota(jnp.int32, (D_FNET, D_FNET), 0)
    c = lax.broadcasted_iota(jnp.int32, (D_FNET, D_FNET), 1)
    same = (r >> 6) == (c >> 6)
    ph = ((r & 63) * (c & 63)) & 63
    ang = ph.astype(F32) * (2.0 * math.pi / FNET_GROUP_DIM)
    scale = 1.0 / math.sqrt(FNET_GROUP_DIM)
    cm = jnp.where(same, jnp.cos(ang) * scale, 0.0)
    sm = jnp.where(same, jnp.sin(ang) * scale, 0.0)
    w = w_ref[0]
    a_ref[0] = jnp.dot(cm, w, precision=lax.Precision.HIGHEST, preferred_element_type=F32).astype(BF16)
    b_ref[0] = jnp.dot(sm, w, precision=lax.Precision.HIGHEST, preferred_element_type=F32).astype(BF16)


def _fnet_prep(wbd):
    spec = pl.BlockSpec((1, D_FNET, D_FNET), lambda l: (l, 0, 0))
    return pl.pallas_call(
        _fprep_kernel,
        grid=(DEPTH,),
        in_specs=[spec],
        out_specs=[spec, spec],
        out_shape=[jax.ShapeDtypeStruct((DEPTH, D_FNET, D_FNET), BF16)] * 2,
        compiler_params=_cparams(("arbitrary",)),
        name="fnet_prep",
    )(wbd)


TABLE_ROWS = 64


def _base_table_kernel(cj_ref, sj_ref, cm_ref, sm_ref, *, L):
    j = lax.broadcasted_iota(jnp.int32, (TABLE_ROWS, L), 0)
    n = lax.broadcasted_iota(jnp.int32, (TABLE_ROWS, L), 1)
    w = 2.0 * math.pi / L
    fine = ((j * n) & (L - 1)).astype(F32) * w
    coarse = ((j * TABLE_ROWS * n) & (L - 1)).astype(F32) * w
    scale = 1.0 / math.sqrt(L)
    cj_ref[...] = jnp.cos(fine)
    sj_ref[...] = jnp.sin(fine)
    cm_ref[...] = jnp.cos(coarse) * scale
    sm_ref[...] = jnp.sin(coarse) * scale


def _table_kernel(cj_ref, sj_ref, cm_ref, sm_ref, c_ref, s_ref):
    m = pl.program_id(0)
    c0 = cm_ref[pl.ds(m, 1), :]
    s0 = sm_ref[pl.ds(m, 1), :]
    cj = cj_ref[...]
    sj = sj_ref[...]
    c_ref[...] = (cj * c0 - sj * s0).astype(BF16)
    s_ref[...] = (-(sj * c0 + cj * s0)).astype(BF16)


def _dft_tables(L):
    small = pl.BlockSpec((TABLE_ROWS, L), lambda *_: (0, 0))
    base = pl.pallas_call(
        functools.partial(_base_table_kernel, L=L),
        out_specs=[small] * 4,
        out_shape=[jax.ShapeDtypeStruct((TABLE_ROWS, L), F32)] * 4,
        compiler_params=pltpu.CompilerParams(vmem_limit_bytes=VMEM_LIMIT),
        name=f"dft_base_tables_{L}",
    )()
    spec = pl.BlockSpec((TABLE_ROWS, L), lambda i: (i, 0))
    return pl.pallas_call(
        _table_kernel,
        grid=(L // TABLE_ROWS,),
        in_specs=[small] * 4,
        out_specs=[spec, spec],
        out_shape=[jax.ShapeDtypeStruct((L, L), BF16)] * 2,
        compiler_params=_cparams(("arbitrary",)),
        name=f"dft_tables_{L}",
    )(*base)


def _dft_kernel(c_ref, s_ref, va_ref, vb_ref, o_ref):
    o_ref[...] = (jnp.dot(c_ref[...], va_ref[...], preferred_element_type=F32)
                  + jnp.dot(s_ref[...], vb_ref[...], preferred_element_type=F32))


def _dft_apply(ct, st, va, vb):
    L, ncols = va.shape
    tk, tc = 256, 1024
    return pl.pallas_call(
        _dft_kernel,
        grid=(L // tk, ncols // tc),
        in_specs=[
            pl.BlockSpec((tk, L), lambda i, j: (i, 0)),
            pl.BlockSpec((tk, L), lambda i, j: (i, 0)),
            pl.BlockSpec((L, tc), lambda i, j: (0, j)),
            pl.BlockSpec((L, tc), lambda i, j: (0, j)),
        ],
        out_specs=pl.BlockSpec((tk, tc), lambda i, j: (i, j)),
        out_shape=jax.ShapeDtypeStruct((L, ncols), F32),
        compiler_params=_cparams(("arbitrary", "arbitrary")),
        name=f"dft_apply_{L}",
    )(ct, st, va, vb)


def _rms_mod(x, g, scale, shift):
    ms = jnp.mean(x * x, axis=-1, keepdims=True)
    return (x * lax.rsqrt(ms + EPS)) * g * (1.0 + scale) + shift


def _pos_table_kernel(s_ref, c_ref):
    n, nf = s_ref.shape
    j = lax.broadcasted_iota(jnp.int32, (n, nf), 0).astype(F32)
    k = lax.broadcasted_iota(jnp.int32, (n, nf), 1).astype(F32)
    ang = j * jnp.exp(k * (-math.log(10000.0) / nf))
    s_ref[...] = jnp.sin(ang)
    c_ref[...] = jnp.cos(ang)


def _pos_tables(n):
    nf = D_MODEL // 4
    return pl.pallas_call(
        _pos_table_kernel,
        out_shape=[jax.ShapeDtypeStruct((n, nf), F32)] * 2,
        name="pos_tables",
    )()


def _pos_tile(ps_ref, pc_ref, t, tm):
    nrow = tm // GRID_W
    nf = D_MODEL // 4

    def rows(tab):
        return jnp.concatenate(
            [jnp.broadcast_to(tab[pl.ds(t * nrow + k, 1), :], (GRID_W, nf)) for k in range(nrow)], axis=0)

    def cols(tab):
        return jnp.concatenate([tab[0:GRID_W, :]] * nrow, axis=0)

    return jnp.concatenate([rows(ps_ref), rows(pc_ref), cols(ps_ref), cols(pc_ref)], axis=-1)


def _in_kernel(*refs, add_pos, row_base, row_step):
    if add_pos:
        x_ref, ps_ref, pc_ref, mod_ref, g_ref, w_ref, a_ref, b_ref, zr_ref, zuv_ref, va_ref, vb_ref = refs
    else:
        x_ref, mod_ref, g_ref, w_ref, a_ref, b_ref, zr_ref, zuv_ref, va_ref, vb_ref = refs
    x = x_ref[0]
    if add_pos:
        x = x + _pos_tile(ps_ref, pc_ref, pl.program_id(1), x.shape[0])
    row = row_base + pl.program_id(0) * row_step
    m = mod_ref[pl.ds(row, 1), :]
    h = _rms_mod(x, g_ref[...], m[:, D_MODEL:2 * D_MODEL], m[:, 0:D_MODEL])
    z = jnp.dot(h.astype(BF16), w_ref[...], preferred_element_type=F32)
    zr_ref[0] = z[:, 0:2 * D_RNN]
    zuv_ref[0] = z[:, 2 * D_RNN:2 * D_RNN + 2 * D_TMLP]
    zf = z[:, 2 * D_RNN + 2 * D_TMLP:D_IN].astype(BF16)
    va_ref[...] = jnp.dot(zf, a_ref[...], preferred_element_type=F32).astype(BF16)
    vb_ref[...] = jnp.dot(zf, b_ref[...], preferred_element_type=F32).astype(BF16)


def _in_proj(x, pos, mods, g, w_in_b, fa, fb, *, layer, row_base, row_step):
    nseq, L, _ = x.shape
    tm = min(L, PROJ_TILE)
    add_pos = pos is not None
    in_specs = [pl.BlockSpec((1, tm, D_MODEL), lambda b, t: (b, t, 0))]
    args = [x]
    if add_pos:
        in_specs += [pl.BlockSpec(p.shape, lambda b, t: (0, 0)) for p in pos]
        args += list(pos)
    in_specs += [
        _layer_spec((SUBLANES, N_MOD * D_MODEL), layer),
        _layer_spec((1, D_MODEL), layer),
        _layer_spec((D_MODEL, D_IN), layer),
        _layer_spec((D_FNET, D_FNET), layer),
        _layer_spec((D_FNET, D_FNET), layer),
    ]
    args += [mods, g, w_in_b, fa, fb]
    return pl.pallas_call(
        functools.partial(_in_kernel, add_pos=add_pos, row_base=row_base, row_step=row_step),
        grid=(nseq, L // tm),
        in_specs=in_specs,
        out_specs=[
            pl.BlockSpec((1, tm, 2 * D_RNN), lambda b, t: (b, t, 0)),
            pl.BlockSpec((1, tm, 2 * D_TMLP), lambda b, t: (b, t, 0)),
            pl.BlockSpec((tm, D_FNET), lambda b, t: (t, b)),
            pl.BlockSpec((tm, D_FNET), lambda b, t: (t, b)),
        ],
        out_shape=[
            jax.ShapeDtypeStruct((nseq, L, 2 * D_RNN), F32),
            jax.ShapeDtypeStruct((nseq, L, 2 * D_TMLP), F32),
            jax.ShapeDtypeStruct((L, nseq * D_FNET), BF16),
            jax.ShapeDtypeStruct((L, nseq * D_FNET), BF16),
        ],
        compiler_params=_cparams(("arbitrary", "arbitrary")),
        name=f"in_proj_{L}",
    )(*args)


def _gelu_tanh(x):
    return 0.5 * x * (1.0 + jnp.tanh(math.sqrt(2.0 / math.pi) * (x + 0.044715 * (x * x * x))))


def _rows_to_tile(rows):
    sub = lax.broadcasted_iota(jnp.int32, (SUBLANES, LANES), 0)
    out = jnp.zeros((SUBLANES, LANES), F32)
    for s, r in enumerate(rows):
        out = jnp.where(sub == s, jnp.broadcast_to(r, (SUBLANES, LANES)), out)
    return out


def _lru_kernel(xr_ref, gr_ref, cw_ref, cb_ref, wa_ref, wx_ref, ba_ref, bx_ref, lam_ref, h0_ref,
                y_ref, st_ref, xnat, pext, af, bf, ab, bb, hfo, pfo, hbo, pbo, hnat, *, L):
    S = L // SUBLANES
    pitch = S + SUBLANES
    n = S * SUBLANES
    chunk = 256

    for s in range(SUBLANES):
        xnat[s * pitch:s * pitch + S, :] = xr_ref[0, s * S:(s + 1) * S, :]

    def perm_in(j, c):
        dst = pl.multiple_of((j + 2) * SUBLANES, SUBLANES)
        pext[pl.ds(dst, SUBLANES), :] = xnat[pl.ds(j, SUBLANES, stride=pitch), :]
        return c

    lax.fori_loop(0, S, perm_in, 0, unroll=8)

    sub = lax.broadcasted_iota(jnp.int32, (SUBLANES, LANES), 0)

    def from_prev_segment(v):
        return jnp.where(sub == 0, 0.0, pltpu.roll(v, 1, axis=0))

    def from_next_segment(v):
        return jnp.where(sub == SUBLANES - 1, 0.0, pltpu.roll(v, SUBLANES - 1, axis=0))

    pext[0:8, :] = from_prev_segment(pext[S * 8:(S + 1) * 8, :])
    pext[8:16, :] = from_prev_segment(pext[(S + 1) * 8:(S + 2) * 8, :])
    pext[(S + 2) * 8:(S + 3) * 8, :] = from_next_segment(pext[16:24, :])

    lam = lam_ref[...]
    nl = -lam
    sp = jnp.maximum(nl, 0.0) + jnp.log1p(jnp.exp(-jnp.abs(nl)))
    c_la = (-0.5 * LRU_C) * sp
    ba_h = 0.5 * ba_ref[...]
    bx_h = 0.5 * bx_ref[...]
    a_refs = (af, ab)
    b_refs = (bf, bb)

    def gates(i, c):
        base = pl.multiple_of(i * chunk, chunk)
        xc = (cw_ref[0:1, :] * pext[pl.ds(base, chunk), :]
              + cw_ref[1:2, :] * pext[pl.ds(base + 8, chunk), :]
              + cw_ref[2:3, :] * pext[pl.ds(base + 16, chunk), :]
              + cw_ref[3:4, :] * pext[pl.ds(base + 24, chunk), :]
              + cb_ref[...])
        xcb = xc.astype(BF16)
        xh = 0.5 * xc
        for d in range(2):
            tr = jnp.tanh(jnp.dot(xcb, wa_ref[d, 0], preferred_element_type=F32) + ba_h[d:d + 1, :])
            ti = jnp.tanh(jnp.dot(xcb, wx_ref[d, 0], preferred_element_type=F32) + bx_h[d:d + 1, :])
            log_a = c_la[d:d + 1, :] * (1.0 + tr)
            a = jnp.exp(log_a)
            v = jnp.tanh(log_a) * (-1.0 - a * a)
            coef = jnp.where(v > 0.0, v * lax.rsqrt(v), 0.0)
            a_refs[d][pl.ds(base, chunk), :] = a
            b_refs[d][pl.ds(base, chunk), :] = coef * ((1.0 + ti) * xh)
        return c

    lax.fori_loop(0, n // chunk, gates, 0)

    nq = LRU_SUB
    sq = S // nq

    def scan(i, carry):
        out = []
        for q in range(nq):
            hf, pf, hb, pb = carry[4 * q:4 * q + 4]
            jf = pl.multiple_of((q * sq + i) * SUBLANES, SUBLANES)
            jb = pl.multiple_of(((q + 1) * sq - 1 - i) * SUBLANES, SUBLANES)
            a1 = af[pl.ds(jf, SUBLANES), :]
            hf = a1 * hf + bf[pl.ds(jf, SUBLANES), :]
            pf = a1 * pf
            hfo[pl.ds(jf, SUBLANES), :] = hf
            pfo[pl.ds(jf, SUBLANES), :] = pf
            a2 = ab[pl.ds(jb, SUBLANES), :]
            hb = a2 * hb + bb[pl.ds(jb, SUBLANES), :]
            pb = a2 * pb
            hbo[pl.ds(jb, SUBLANES), :] = hb
            pbo[pl.ds(jb, SUBLANES), :] = pb
            out += [hf, pf, hb, pb]
        return tuple(out)

    zero = jnp.zeros((SUBLANES, LANES), F32)
    one = jnp.ones((SUBLANES, LANES), F32)
    ends = lax.fori_loop(0, sq, scan, (zero, one, zero, one) * nq, unroll=min(sq, 4))
    end_f = [(ends[4 * q], ends[4 * q + 1]) for q in range(nq)]
    end_b = [(ends[4 * q + 2], ends[4 * q + 3]) for q in range(nq)]

    hf, pf = end_f[0]
    for q in range(1, nq):
        hf, pf = end_f[q][1] * hf + end_f[q][0], end_f[q][1] * pf
    hb, pb = end_b[nq - 1]
    for q in range(nq - 2, -1, -1):
        hb, pb = end_b[q][1] * hb + end_b[q][0], end_b[q][1] * pb

    rows_f = [h0_ref[0, 0:1, :]]
    for s in range(1, SUBLANES):
        rows_f.append(pf[s - 1:s, :] * rows_f[-1] + hf[s - 1:s, :])
    st_ref[0, 0:1, :] = pf[7:8, :] * rows_f[7] + hf[7:8, :]
    rows_b = [None] * SUBLANES
    rows_b[7] = h0_ref[0, 1:2, :]
    for s in range(SUBLANES - 2, -1, -1):
        rows_b[s] = pb[s + 1:s + 2, :] * rows_b[s + 1] + hb[s + 1:s + 2, :]
    st_ref[0, 1:2, :] = pb[0:1, :] * rows_b[0] + hb[0:1, :]

    init_f = [_rows_to_tile(rows_f)]
    for q in range(nq - 1):
        init_f.append(end_f[q][1] * init_f[q] + end_f[q][0])
    init_b = [None] * nq
    init_b[nq - 1] = _rows_to_tile(rows_b)
    for q in range(nq - 1, 0, -1):
        init_b[q - 1] = end_b[q][1] * init_b[q] + end_b[q][0]

    for q in range(nq):
        def perm_out(j, c, q=q):
            src = pl.multiple_of(j * SUBLANES, SUBLANES)
            v = (hfo[pl.ds(src, SUBLANES), :] + pfo[pl.ds(src, SUBLANES), :] * init_f[q]
                 + hbo[pl.ds(src, SUBLANES), :] + pbo[pl.ds(src, SUBLANES), :] * init_b[q])
            hnat[pl.ds(j, SUBLANES, stride=pitch), :] = v
            return c

        lax.fori_loop(q * sq, (q + 1) * sq, perm_out, 0, unroll=min(sq, 8))

    for s in range(SUBLANES):
        g = gr_ref[0, s * S:(s + 1) * S, :]
        y_ref[0, s * S:(s + 1) * S, :] = hnat[s * pitch:s * pitch + S, :] * _gelu_tanh(g)


def _lru_mixer(zr, conv_w, conv_b, wa_bd, wx_bd, ba, bx, lam, h0, *, layer):
    nseq, L, _ = zr.shape
    ncb = D_RNN // LRU_CB
    S = L // SUBLANES
    pitch = S + SUBLANES
    vec2 = pl.BlockSpec((None, 2, LRU_CB), lambda b, c: (layer, 0, c))
    wspec = pl.BlockSpec((None, 2, 1, LRU_CB, LRU_CB), lambda b, c: (layer, 0, c, 0, 0))
    return pl.pallas_call(
        functools.partial(_lru_kernel, L=L),
        grid=(nseq, ncb),
        in_specs=[
            pl.BlockSpec((1, L, LRU_CB), lambda b, c: (b, 0, c)),
            pl.BlockSpec((1, L, LRU_CB), lambda b, c: (b, 0, c + ncb)),
            pl.BlockSpec((None, CONV_W, LRU_CB), lambda b, c: (layer, 0, c)),
            pl.BlockSpec((None, 1, LRU_CB), lambda b, c: (layer, 0, c)),
            wspec, wspec, vec2, vec2, vec2,
            pl.BlockSpec((1, 2, LRU_CB), lambda b, c: (b, 0, c)),
        ],
        out_specs=[
            pl.BlockSpec((1, L, LRU_CB), lambda b, c: (b, 0, c)),
            pl.BlockSpec((1, 2, LRU_CB), lambda b, c: (b, 0, c)),
        ],
        out_shape=[
            jax.ShapeDtypeStruct((nseq, L, D_RNN), F32),
            jax.ShapeDtypeStruct((nseq, 2, D_RNN), F32),
        ],
        scratch_shapes=[
            pltpu.VMEM((SUBLANES * pitch, LANES), F32),
            pltpu.VMEM(((S + 3) * SUBLANES, LANES), F32),
            *([pltpu.VMEM((L, LANES), F32)] * 8),
            pltpu.VMEM((SUBLANES * pitch, LANES), F32),
        ],
        compiler_params=_cparams(("arbitrary", "arbitrary")),
        name=f"lru_mixer_{L}",
    )(zr, zr, conv_w, conv_b, wa_bd, wx_bd, ba, bx, lam, h0)


def _out_kernel(*refs, add_pos, row_base, row_step):
    if add_pos:
        (x_ref, ps_ref, pc_ref, yr_ref, zuv_ref, yf_ref, mod_ref, ws_ref, bt_ref, wo_ref, g_ref, wr_ref, br_ref,
         cin_ref, x1_ref, h2_ref, ri_ref, cnt_ref, carry) = refs
    else:
        (x_ref, yr_ref, zuv_ref, yf_ref, mod_ref, ws_ref, bt_ref, wo_ref, g_ref, wr_ref, br_ref, cin_ref,
         x1_ref, h2_ref, ri_ref, cnt_ref, carry) = refs
    tm = x_ref.shape[1]
    x = x_ref[0]
    if add_pos:
        x = x + _pos_tile(ps_ref, pc_ref, pl.program_id(1), tm)
    row = row_base + pl.program_id(0) * row_step
    m = mod_ref[pl.ds(row, 1), :]
    g1 = m[:, 2 * D_MODEL:3 * D_MODEL]
    sh2 = m[:, 3 * D_MODEL:4 * D_MODEL]
    sc2 = m[:, 4 * D_MODEL:5 * D_MODEL]

    head = lax.broadcasted_iota(jnp.int32, (CHUNK, D_TMLP), 1) >> 6
    yt_parts = []
    for ci in range(tm // CHUNK):
        u = zuv_ref[0, ci * CHUNK:(ci + 1) * CHUNK, 0:D_TMLP]
        v = zuv_ref[0, ci * CHUNK:(ci + 1) * CHUNK, D_TMLP:2 * D_TMLP].astype(BF16)
        s = jnp.zeros((CHUNK, D_TMLP), F32)
        for h in range(TMLP_HEADS):
            sh = jnp.dot(ws_ref[h], v, preferred_element_type=F32) + bt_ref[:, h:h + 1]
            s = jnp.where(head == h, sh, s)
        yt_parts.append(u * s)
    yt = jnp.concatenate(yt_parts, axis=0) if len(yt_parts) > 1 else yt_parts[0]

    y = (jnp.dot(yr_ref[0].astype(BF16), wo_ref[0:D_RNN, :], preferred_element_type=F32)
         + jnp.dot(yt.astype(BF16), wo_ref[D_RNN:D_RNN + D_TMLP, :], preferred_element_type=F32)
         + jnp.dot(yf_ref[...].astype(BF16), wo_ref[D_RNN + D_TMLP:D_MODEL, :], preferred_element_type=F32))
    x1 = x + g1 * y
    x1_ref[0] = x1
    h2 = _rms_mod(x1, g_ref[...], sc2, sh2)
    _store_token_major(h2_ref, h2)

    wr = wr_ref[...]
    w_hi = wr.astype(BF16)
    w_lo = (wr - w_hi.astype(F32)).astype(BF16)
    h_hi = h2.astype(BF16)
    h_lo = (h2 - h_hi.astype(F32)).astype(BF16)
    p_hi = jnp.dot(h_hi, jnp.concatenate([w_hi, w_lo], axis=-1), preferred_element_type=F32)
    p_lo = jnp.dot(h_lo, w_hi, preferred_element_type=F32)
    logits = p_hi[:, 0:ROUTER_LANES] + p_hi[:, ROUTER_LANES:2 * ROUTER_LANES] + p_lo + br_ref[...]
    lane = lax.broadcasted_iota(jnp.int32, (tm, ROUTER_LANES), 1)
    lane_f = lane.astype(F32)
    is_g = lane < N_EGROUPS
    gl = jnp.where(is_g, logits, NEG_BIG)
    gmax = jnp.max(gl, axis=-1, keepdims=True)
    gsel = jnp.min(jnp.where(gl == gmax, lane_f, 1e4), axis=-1, keepdims=True)
    pg = 1.0 / jnp.sum(jnp.where(is_g, jnp.exp(logits - gmax), 0.0), axis=-1, keepdims=True)
    grp_f = ((lane - N_EGROUPS) >> 3).astype(F32)
    emask = (lane >= N_EGROUPS) & (lane < N_EGROUPS + N_EXPERTS) & (grp_f == gsel)
    el = jnp.where(emask, logits, NEG_BIG)
    v1 = jnp.max(el, axis=-1, keepdims=True)
    i1 = jnp.min(jnp.where(el == v1, lane_f, 1e4), axis=-1, keepdims=True)
    el2 = jnp.where(lane_f == i1, NEG_BIG, el)
    v2 = jnp.max(el2, axis=-1, keepdims=True)
    i2 = jnp.min(jnp.where(el2 == v2, lane_f, 1e4), axis=-1, keepdims=True)
    e2x = jnp.exp(v2 - v1)
    fw1 = 1.0 / (1.0 + e2x)
    fw2 = e2x * fw1
    @pl.when((pl.program_id(0) == 0) & (pl.program_id(1) == 0))
    def _():
        carry[...] = cin_ref[...]

    e1 = i1 - N_EGROUPS
    e2 = i2 - N_EGROUPS
    m1 = lane_f == e1
    m2 = lane_f == e2
    oh = jnp.where(m1 | m2, 1.0, 0.0)
    r_i = lax.broadcasted_iota(jnp.int32, (tm, tm), 0)
    c_i = lax.broadcasted_iota(jnp.int32, (tm, tm), 1)
    tri = jnp.where(c_i < r_i, 1.0, 0.0).astype(BF16)
    before = jnp.dot(tri, oh.astype(BF16), preferred_element_type=F32) + carry[0:1, :]
    rank1 = jnp.sum(jnp.where(m1, before, 0.0), axis=-1, keepdims=True)
    rank2 = jnp.sum(jnp.where(m2, before, 0.0), axis=-1, keepdims=True)
    total = carry[0:1, :] + jnp.sum(oh, axis=0, keepdims=True)
    carry[0:1, :] = total
    cnt_ref[...] = jnp.broadcast_to(total, cnt_ref.shape)

    vals = (e1, e2, pg * fw1, pg * fw2, rank1, rank2)
    ri = jnp.zeros((tm, ROUTER_LANES), F32)
    for k, v in enumerate(vals):
        ri = jnp.where(lane == k, v, ri)
    ri_ref[0] = ri


def _layer_spec(shape, layer):
    zeros = (0,) * len(shape)
    return pl.BlockSpec((None, *shape), lambda *_: (layer, *zeros))


def _out_proj(x, pos, yr, zuv, yf, mods, ws_b, bt, wo_b, g, wr, br, cin, *, layer, row_base, row_step):
    nseq, L, _ = x.shape
    tm = min(L, PROJ_TILE)
    add_pos = pos is not None
    in_specs = [pl.BlockSpec((1, tm, D_MODEL), lambda b, t: (b, t, 0))]
    args = [x]
    if add_pos:
        in_specs += [pl.BlockSpec(p.shape, lambda b, t: (0, 0)) for p in pos]
        args += list(pos)
    in_specs += [
        pl.BlockSpec((1, tm, D_RNN), lambda b, t: (b, t, 0)),
        pl.BlockSpec((1, tm, 2 * D_TMLP), lambda b, t: (b, t, 0)),
        pl.BlockSpec((tm, D_FNET), lambda b, t: (t, b)),
        _layer_spec((SUBLANES, N_MOD * D_MODEL), layer),
        _layer_spec((TMLP_HEADS, CHUNK, CHUNK), layer),
        _layer_spec((CHUNK, TMLP_HEADS), layer),
        _layer_spec((D_MODEL, D_MODEL), layer),
        _layer_spec((1, D_MODEL), layer),
        _layer_spec((D_MODEL, ROUTER_LANES), layer),
        _layer_spec((1, ROUTER_LANES), layer),
        pl.BlockSpec((SUBLANES, ROUTER_LANES), lambda b, t: (0, 0)),
    ]
    args += [yr, zuv, yf, mods, ws_b, bt, wo_b, g, wr, br, cin]
    tok = pl.BlockSpec((1, tm, D_MODEL), lambda b, t: (b, t, 0))
    return pl.pallas_call(
        functools.partial(_out_kernel, add_pos=add_pos, row_base=row_base, row_step=row_step),
        grid=(nseq, L // tm),
        in_specs=in_specs,
        out_specs=[tok, pl.BlockSpec((tm * LANE_TILES, LANES), lambda b, t: (b * (L // tm) + t, 0)),
                   pl.BlockSpec((1, tm, ROUTER_LANES), lambda b, t: (b, t, 0)),
                   pl.BlockSpec((SUBLANES, ROUTER_LANES), lambda b, t: (0, 0))],
        out_shape=[
            jax.ShapeDtypeStruct((nseq, L, D_MODEL), F32),
            jax.ShapeDtypeStruct((nseq * L * LANE_TILES, LANES), F32),
            jax.ShapeDtypeStruct((nseq, L, ROUTER_LANES), F32),
            jax.ShapeDtypeStruct((SUBLANES, ROUTER_LANES), F32),
        ],
        scratch_shapes=[pltpu.VMEM((SUBLANES, ROUTER_LANES), F32)],
        compiler_params=_cparams(("arbitrary", "arbitrary")),
        name=f"out_proj_{L}",
    )(*args)


def _store_token_major(ref, x):
    tm = x.shape[0]
    for j in range(LANE_TILES):
        ref[pl.ds(j, tm, stride=LANE_TILES), :] = x[:, j * LANES:(j + 1) * LANES]


def _load_token_major(ref):
    tm = ref.shape[0] // LANE_TILES
    return jnp.concatenate([ref[pl.ds(j, tm, stride=LANE_TILES), :] for j in range(LANE_TILES)], axis=-1)


def _token_copy(src_ref, src_tok, dst_ref, dst_tok, sem):
    return pltpu.make_async_copy(src_ref.at[pl.ds(pl.multiple_of(src_tok, LANE_TILES), LANE_TILES)],
                                 dst_ref.at[pl.ds(pl.multiple_of(dst_tok, LANE_TILES), LANE_TILES)], sem)


def _scatter_rows(pos_ref, h_ref, xs_ref, sem):
    rows = h_ref.shape[0]

    for r in range(0, rows, LANE_TILES):
        _token_copy(h_ref, r, xs_ref, pos_ref[0, 0, r // LANE_TILES], sem).start(priority=0)
        _token_copy(h_ref, r, xs_ref, pos_ref[0, 1, r // LANE_TILES], sem).start(priority=1)
    for _ in range(2):
        pltpu.make_async_copy(h_ref, xs_ref.at[pl.ds(0, rows)], sem).wait()


def _dispatch_kernel(pos_ref, hp_ref, hs_ref, xs_ref, sem, *, n_first):
    i = pl.program_id(0)

    @pl.when(i < n_first)
    def _():
        _scatter_rows(pos_ref, hp_ref, xs_ref, sem)

    @pl.when(i >= n_first)
    def _():
        _scatter_rows(pos_ref, hs_ref, xs_ref, sem)


def _dispatch(pos, h_first, h_second):
    tm = ROW_TILE
    blk = tm * LANE_TILES
    n_first = h_first.shape[0] // blk
    n_second = h_second.shape[0] // blk
    nrows = 2 * (h_first.shape[0] + h_second.shape[0])
    return pl.pallas_call(
        functools.partial(_dispatch_kernel, n_first=n_first),
        grid=(n_first + n_second,),
        in_specs=[
            pl.BlockSpec((1, 2, tm), lambda i: (i, 0, 0), memory_space=pltpu.SMEM),
            pl.BlockSpec((blk, LANES), lambda i: (jnp.minimum(i, n_first - 1), 0)),
            pl.BlockSpec((blk, LANES), lambda i: (jnp.maximum(i - n_first, 0), 0)),
        ],
        out_specs=pl.BlockSpec(memory_space=pl.ANY),
        out_shape=jax.ShapeDtypeStruct((nrows, LANES), F32),
        scratch_shapes=[pltpu.SemaphoreType.DMA(())],
        compiler_params=_cparams(("arbitrary",)),
        name="moe_dispatch",
    )(pos, h_first, h_second)


WEIGHT_LEAD = (3, 2, 1)
WEIGHT_SLOTS = 4
LEAD_ITEMS = max(WEIGHT_LEAD)


def _expert_kernel(wt_ref, we_ref, lo_ref, hi_ref, first_ref, ord_ref, xs_ref, w1_ref, w3_ref, w2_ref, ys_ref,
                   w1b, w3b, w2b):
    s = pl.program_id(0)
    last = pl.num_programs(0) - 1
    tm = EXPERT_TILE

    def stage(lead, src, dst):
        cur = jnp.minimum(s + lead, last)
        prv = jnp.minimum(s + lead - 1, last)

        @pl.when((s == 0) | (we_ref[cur] != we_ref[prv]))
        def _():
            dst[ord_ref[cur] & (WEIGHT_SLOTS - 1)] = src[0].astype(BF16)

    for lead, src, dst in zip(WEIGHT_LEAD, (w1_ref, w3_ref, w2_ref), (w1b, w3b, w2b)):
        stage(lead, src, dst)

    lo = lo_ref[s]
    hi = hi_ref[s]
    full = (lo == 0) & (hi == tm)

    @pl.when((first_ref[s] == 1) & jnp.logical_not(full))
    def _():
        ys_ref[...] = jnp.zeros_like(ys_ref)

    @pl.when(hi > lo)
    def _():
        slot = ord_ref[s] & (WEIGHT_SLOTS - 1)
        x = _load_token_major(xs_ref).astype(BF16)
        a = jnp.dot(x, w1b[slot], preferred_element_type=F32)
        b = jnp.dot(x, w3b[slot], preferred_element_type=F32)
        hid = (a * jax.nn.sigmoid(a)) * b
        res = jnp.dot(hid.astype(BF16), w2b[slot], preferred_element_type=F32)

        @pl.when(full)
        def _():
            _store_token_major(ys_ref, res)

        @pl.when(jnp.logical_not(full))
        def _():
            rows = lax.broadcasted_iota(jnp.int32, res.shape, 0)
            _store_token_major(ys_ref, jnp.where((rows >= lo) & (rows < hi), res, _load_token_major(ys_ref)))


def _experts(items, xs, w1, w3, w2):
    nw = items[0].shape[0]

    def row(s, wt, we, lo, hi, fi, od):
        return (wt[s], 0)

    def weight(lead):
        return lambda s, wt, we, lo, hi, fi, od: (we[jnp.minimum(s + lead, nw - 1)], 0, 0)

    return pl.pallas_call(
        _expert_kernel,
        grid_spec=pltpu.PrefetchScalarGridSpec(
            num_scalar_prefetch=6,
            grid=(nw,),
            in_specs=[
                pl.BlockSpec((EXPERT_TILE * LANE_TILES, LANES), row),
                pl.BlockSpec((1, D_MODEL, D_EXPERT), weight(WEIGHT_LEAD[0])),
                pl.BlockSpec((1, D_MODEL, D_EXPERT), weight(WEIGHT_LEAD[1])),
                pl.BlockSpec((1, D_EXPERT, D_MODEL), weight(WEIGHT_LEAD[2])),
            ],
            out_specs=pl.BlockSpec((EXPERT_TILE * LANE_TILES, LANES), row),
            scratch_shapes=[
                pltpu.VMEM((WEIGHT_SLOTS, D_MODEL, D_EXPERT), BF16),
                pltpu.VMEM((WEIGHT_SLOTS, D_MODEL, D_EXPERT), BF16),
                pltpu.VMEM((WEIGHT_SLOTS, D_EXPERT, D_MODEL), BF16),
            ],
        ),
        out_shape=jax.ShapeDtypeStruct(xs.shape, F32),
        compiler_params=_cparams(("arbitrary",)),
        name="moe_experts",
    )(*items, xs, w1, w3, w2)


def _combine_kernel(pos_ref, posn_ref, x1_ref, ri_ref, mod_ref, gf_ref, ys_ref, o_ref, ybuf, sem,
                    *, row_base, row_step, tiles_per_seq, final):
    tm = x1_ref.shape[0]
    i = pl.program_id(0)
    n = pl.num_programs(0)

    def gather(p_ref, slot):
        for r in range(tm):
            _token_copy(ys_ref, p_ref[0, 0, r], ybuf.at[slot, 0], r * LANE_TILES, sem.at[slot]).start(priority=0)
            _token_copy(ys_ref, p_ref[0, 1, r], ybuf.at[slot, 1], r * LANE_TILES, sem.at[slot]).start(priority=1)

    def drain(slot):
        for k in range(2):
            pltpu.make_async_copy(ys_ref.at[pl.ds(0, tm * LANE_TILES)], ybuf.at[slot, k], sem.at[slot]).wait()

    def tile(slot):
        if slot == 0:
            @pl.when(i == 0)
            def _():
                gather(pos_ref, 0)

        @pl.when(i + 1 < n)
        def _():
            gather(posn_ref, 1 - slot)

        drain(slot)
        row = row_base + (i // tiles_per_seq) * row_step
        g2 = mod_ref[pl.ds(row, 1), 5 * D_MODEL:6 * D_MODEL]
        ri = ri_ref[...]
        y = (ri[:, 2:3] * _load_token_major(ybuf.at[slot, 0])
             + ri[:, 3:4] * _load_token_major(ybuf.at[slot, 1]))
        x2 = x1_ref[...] + g2 * y
        if final:
            ms = jnp.mean(x2 * x2, axis=-1, keepdims=True)
            o_ref[...] = (x2 * lax.rsqrt(ms + EPS)) * gf_ref[...]
        else:
            o_ref[...] = x2

    for slot in range(2):
        pl.when((i & 1) == slot)(functools.partial(tile, slot))


def _combine(pos, x1, ri, mods, g_final, ys, *, layer, row_base, row_step, tiles_per_seq, final):
    ntok = x1.shape[0]
    tm = ROW_TILE
    tok = pl.BlockSpec((tm, D_MODEL), lambda i: (i, 0))
    return pl.pallas_call(
        functools.partial(_combine_kernel, row_base=row_base, row_step=row_step, tiles_per_seq=tiles_per_seq,
                          final=final),
        grid=(ntok // tm,),
        in_specs=[
            pl.BlockSpec((1, 2, tm), lambda i: (i, 0, 0), memory_space=pltpu.SMEM),
            pl.BlockSpec((1, 2, tm), lambda i: (jnp.minimum(i + 1, ntok // tm - 1), 0, 0), memory_space=pltpu.SMEM),
            tok,
            pl.BlockSpec((tm, ROUTER_LANES), lambda i: (i, 0)),
            _layer_spec((SUBLANES, N_MOD * D_MODEL), layer),
            pl.BlockSpec((1, D_MODEL), lambda i: (0, 0)),
            pl.BlockSpec(memory_space=pl.ANY),
        ],
        out_specs=tok,
        out_shape=jax.ShapeDtypeStruct((ntok, D_MODEL), F32),
        scratch_shapes=[
            pltpu.VMEM((2, 2, tm * LANE_TILES, LANES), F32),
            pltpu.SemaphoreType.DMA((2,)),
        ],
        compiler_params=_cparams(("arbitrary",)),
        name=f"moe_combine_{ntok}",
    )(pos, pos, x1, ri, mods, g_final, ys)


def _route(ri_first, ri_second, cnt):
    counts = cnt[0, 0:N_EXPERTS].astype(jnp.int32)
    offs = jnp.cumsum(counts) - counts
    info = jnp.concatenate([ri_first[:, 0:SUBLANES], ri_second[:, 0:SUBLANES]], axis=0).T
    e = info[0:2].astype(jnp.int32)
    base = jnp.zeros_like(e)
    for k in range(N_EXPERTS):
        base = jnp.where(e == k, offs[k], base)
    slot = (base + info[4:6].astype(jnp.int32)) * LANE_TILES
    n_tiles = slot.shape[1] // ROW_TILE
    pos_tiles = jnp.swapaxes(slot.reshape(2, n_tiles, ROW_TILE), 0, 1)
    return pos_tiles, counts, offs


def _work_items(counts, offs, nw, layer):
    tm = EXPERT_TILE
    first_tile = offs // tm
    last_tile = (offs + counts - 1) // tm
    n_e = jnp.where(counts > 0, last_tile - first_tile + 1, 0)
    w_end = jnp.cumsum(n_e)
    w_start = w_end - n_e
    total = w_end[-1]
    w = jnp.arange(nw, dtype=jnp.int32)
    wc = jnp.minimum(w, total - 1)
    e_w = jnp.sum((wc[:, None] >= w_end[None, :]).astype(jnp.int32), axis=-1)
    sel = (e_w[:, None] == jnp.arange(N_EXPERTS, dtype=jnp.int32)).astype(jnp.int32)
    pick = lambda v: jnp.sum(sel * v[None, :], axis=-1)
    off_w = pick(offs)
    tile_w = pick(first_tile) + (wc - pick(w_start))
    lo = jnp.clip(off_w - tile_w * tm, 0, tm)
    hi = jnp.clip(off_w + pick(counts) - tile_w * tm, 0, tm)
    valid = w < total
    lo = jnp.where(valid, lo, 0)
    hi = jnp.where(valid, hi, 0)
    prev_tile = jnp.concatenate([jnp.full((1,), -1, tile_w.dtype), tile_w[:-1]])
    first = tile_w != prev_tile
    ordinal = pick(jnp.cumsum((counts > 0).astype(jnp.int32)) - 1)

    def lead(a, fill=None):
        head = jnp.broadcast_to(a[0] if fill is None else jnp.asarray(fill, a.dtype), (LEAD_ITEMS,))
        return jnp.concatenate([head, a]).astype(jnp.int32)

    return (lead(tile_w), lead(e_w + layer * N_EXPERTS), lead(lo, 0), lead(hi, 0), lead(first, 0), lead(ordinal))


def _block_diag(w, nblk):
    *lead, H, d, _ = w.shape
    w = w.reshape(*lead, H // nblk, nblk, d, d)
    eye = jnp.eye(nblk, dtype=w.dtype)
    out = jnp.einsum('...gij,gh->...gihj', w, eye)
    return out.reshape(*lead, H // nblk, nblk * d, nblk * d)


def kernel(x_prompt, x_sample, state_lru, c, c_ctx, w_ada, b_ada, g_mix, g_ffn, g_final, w_in, w_out, conv_w, conv_b, lru_wa, lru_ba, lru_wx, lru_bx, lru_lambda, tmlp_ws, tmlp_b, fnet_w, router_g, router_g_b, router_e, router_e_b, e_w1, e_w3, e_w2):
    bp, lp, _ = x_prompt.shape
    bs, ls, _ = x_sample.shape
    n_ctx = bp * lp
    n_lat = bs * ls
    n_tok = n_ctx + n_lat

    cond8 = jnp.concatenate([c_ctx[None, :], c, jnp.zeros((SUBLANES - 1 - bs, D_MODEL), F32)], axis=0)
    pos = _pos_tables(max(ls // GRID_W, GRID_W))
    w_in_b = w_in.astype(BF16)
    w_out_b = w_out.astype(BF16)
    heads_per_cb = LRU_CB // RNN_HEAD_DIM
    wa_bd = (0.5 * _block_diag(lru_wa, heads_per_cb)).astype(BF16)
    wx_bd = (0.5 * _block_diag(lru_wx, heads_per_cb)).astype(BF16)
    ws_b = tmlp_ws.astype(BF16)
    bt = jnp.swapaxes(tmlp_b, 1, 2)
    fnet_bd = _block_diag(fnet_w, D_FNET // FNET_GROUP_DIM)[:, 0]
    wr = jnp.concatenate([router_g, router_e,
                          jnp.zeros((DEPTH, D_MODEL, ROUTER_LANES - N_EGROUPS - N_EXPERTS), F32)], axis=-1)
    br = jnp.concatenate([router_g_b, router_e_b,
                          jnp.zeros((DEPTH, ROUTER_LANES - N_EGROUPS - N_EXPERTS), F32)], axis=-1)
    ew1 = e_w1.reshape(DEPTH * N_EXPERTS, D_MODEL, D_EXPERT)
    ew3 = e_w3.reshape(DEPTH * N_EXPERTS, D_MODEL, D_EXPERT)
    ew2 = e_w2.reshape(DEPTH * N_EXPERTS, D_EXPERT, D_MODEL)

    mods = _modulation(cond8, w_ada, b_ada)
    fa, fb = _fnet_prep(fnet_bd)
    ct_p, st_p = _dft_tables(lp)
    ct_s, st_s = _dft_tables(ls)

    nw = 2 * n_tok // EXPERT_TILE + N_EXPERTS
    h0_p = jnp.zeros((bp, 2, D_RNN), F32)
    g_mix3 = g_mix[:, None, :]
    g_ffn3 = g_ffn[:, None, :]
    conv_b3 = conv_b[:, None, :]
    br3 = br[:, None, :]

    xp, xs = x_prompt, x_sample
    states = []
    for l in range(DEPTH):
        paths = []
        cnt = jnp.zeros((SUBLANES, ROUTER_LANES), F32)
        for (x, ct, st, h0, row_base, row_step, is_lat) in (
                (xp, ct_p, st_p, h0_p, 0, 0, False),
                (xs, ct_s, st_s, state_lru[:, l], 1, 1, True)):
            pe = pos if (is_lat and l == 0) else None
            zr, zuv, va, vb = _in_proj(x, pe, mods, g_mix3, w_in_b, fa, fb,
                                       layer=l, row_base=row_base, row_step=row_step)
            yr, st_new = _lru_mixer(zr, conv_w, conv_b3, wa_bd, wx_bd, lru_ba, lru_bx, lru_lambda, h0, layer=l)
            yf = _dft_apply(ct, st, va, vb)
            x1, h2, ri, cnt = _out_proj(x, pe, yr, zuv, yf, mods, ws_b, bt, w_out_b, g_ffn3, wr, br3, cnt,
                                        layer=l, row_base=row_base, row_step=row_step)
            paths.append((x1, h2, ri, st_new))
        states.append(paths[0][3])

        pos_tiles, counts, offs = _route(paths[0][2].reshape(n_ctx, ROUTER_LANES),
                                         paths[1][2].reshape(n_lat, ROUTER_LANES), cnt)
        items = _work_items(counts, offs, nw, l)
        nct = n_ctx // ROW_TILE
        xsorted = _dispatch(pos_tiles, paths[0][1], paths[1][1])
        ysorted = _experts(items, xsorted, ew1, ew3, ew2)
        gfin = g_final[None, :]
        final = l == DEPTH - 1
        xp = _combine(pos_tiles[:nct], paths[0][0].reshape(n_ctx, D_MODEL),
                      paths[0][2].reshape(n_ctx, ROUTER_LANES), mods, gfin, ysorted, layer=l,
                      row_base=0, row_step=0, tiles_per_seq=lp // ROW_TILE, final=final).reshape(bp, lp, D_MODEL)
        xs = _combine(pos_tiles[nct:], paths[1][0].reshape(n_lat, D_MODEL),
                      paths[1][2].reshape(n_lat, ROUTER_LANES), mods, gfin, ysorted, layer=l,
                      row_base=1, row_step=1, tiles_per_seq=ls // ROW_TILE, final=final).reshape(bs, ls, D_MODEL)

    new_state = jnp.stack(states, axis=1).astype(x_prompt.dtype)
    return (xp, xs, new_state)
```

```python
import functools
import math

import jax
import jax.numpy as jnp
from jax import lax
from jax.experimental import pallas as pl
from jax.experimental.pallas import tpu as pltpu

F32 = jnp.float32
BF16 = jnp.bfloat16

D_MODEL = 1024
DEPTH = 2
GRID_W = 64
D_RNN = 512
RNN_HEAD_DIM = 64
CONV_W = 4
LRU_C = 8.0
D_TMLP = 256
TMLP_HEADS = 4
CHUNK = 128
D_FNET = 256
FNET_GROUP_DIM = 64
D_IN = 2 * D_RNN + 2 * D_TMLP + D_FNET
N_EGROUPS = 4
N_EPG = 8
N_EXPERTS = N_EGROUPS * N_EPG
D_EXPERT = 512
N_MOD = 6
EPS = 1e-6

LANES = 128
SUBLANES = 8
LANE_TILES = D_MODEL // LANES
assert LANE_TILES == SUBLANES
LRU_CB = LANES
LRU_SUB = 4
ROW_TILE = 256
PROJ_TILE = 512
EXPERT_TILE = 256
ROUTER_LANES = LANES
NEG_BIG = -1e30
VMEM_LIMIT = 56 * 1024 * 1024


def _cparams(sem):
    return pltpu.CompilerParams(dimension_semantics=sem, vmem_limit_bytes=VMEM_LIMIT)


def _mod_kernel(c_ref, w_ref, b_ref, o_ref):
    c = c_ref[...]
    s = c * jax.nn.sigmoid(c)
    o_ref[0] = jnp.dot(s.astype(BF16), w_ref[0].astype(BF16), preferred_element_type=F32) + b_ref[0]


def _modulation(cond8, w_ada, b_ada):
    tn = 1536
    return pl.pallas_call(
        _mod_kernel,
        grid=(DEPTH, N_MOD * D_MODEL // tn),
        in_specs=[
            pl.BlockSpec((SUBLANES, D_MODEL), lambda l, j: (0, 0)),
            pl.BlockSpec((1, D_MODEL, tn), lambda l, j: (l, 0, j)),
            pl.BlockSpec((1, 1, tn), lambda l, j: (l, 0, j)),
        ],
        out_specs=pl.BlockSpec((1, SUBLANES, tn), lambda l, j: (l, 0, j)),
        out_shape=jax.ShapeDtypeStruct((DEPTH, SUBLANES, N_MOD * D_MODEL), F32),
        compiler_params=_cparams(("arbitrary", "arbitrary")),
        name="modulation",
    )(cond8, w_ada, b_ada.reshape(DEPTH, 1, N_MOD * D_MODEL))


def _fprep_kernel(w_ref, a_ref, b_ref):
    r = lax.broadcasted_iota(jnp.int32, (D_FNET, D_FNET), 0)
    c = lax.broadcasted_iota(jnp.int32, (D_FNET, D_FNET), 1)
    same = (r >> 6) == (c >> 6)
    ph = ((r & 63) * (c & 63)) & 63
    ang = ph.astype(F32) * (2.0 * math.pi / FNET_GROUP_DIM)
    scale = 1.0 / math.sqrt(FNET_GROUP_DIM)
    cm = jnp.where(same, jnp.cos(ang) * scale, 0.0)
    sm = jnp.where(same, jnp.sin(ang) * scale, 0.0)
    w = w_ref[0]
    a_ref[0] = jnp.dot(cm, w, precision=lax.Precision.HIGHEST, preferred_element_type=F32).astype(BF16)
    b_ref[0] = jnp.dot(sm, w, precision=lax.Precision.HIGHEST, preferred_element_type=F32).astype(BF16)


def _fnet_prep(wbd):
    spec = pl.BlockSpec((1, D_FNET, D_FNET), lambda l: (l, 0, 0))
    return pl.pallas_call(
        _fprep_kernel,
        grid=(DEPTH,),
        in_specs=[spec],
        out_specs=[spec, spec],
        out_shape=[jax.ShapeDtypeStruct((DEPTH, D_FNET, D_FNET), BF16)] * 2,
        compiler_params=_cparams(("arbitrary",)),
        name="fnet_prep",
    )(wbd)


TABLE_ROWS = 64


def _base_table_kernel(cj_ref, sj_ref, cm_ref, sm_ref, *, L):
    j = lax.broadcasted_iota(jnp.int32, (TABLE_ROWS, L // 2), 0)
    n = lax.broadcasted_iota(jnp.int32, (TABLE_ROWS, L // 2), 1)
    w = 2.0 * math.pi / L
    fine = ((j * n) & (L - 1)).astype(F32) * w
    coarse = ((j * TABLE_ROWS * n) & (L - 1)).astype(F32) * w
    scale = 1.0 / math.sqrt(L)
    cj_ref[...] = jnp.cos(fine)
    sj_ref[...] = jnp.sin(fine)
    cm_ref[...] = jnp.cos(coarse) * scale
    sm_ref[...] = jnp.sin(coarse) * scale


def _table_kernel(cj_ref, sj_ref, cm_ref, sm_ref, c_ref, s_ref):
    m = pl.program_id(0)
    c0 = cm_ref[pl.ds(m, 1), :]
    s0 = sm_ref[pl.ds(m, 1), :]
    cj = cj_ref[...]
    sj = sj_ref[...]
    c_ref[...] = (cj * c0 - sj * s0).astype(BF16)
    s_ref[...] = (-(sj * c0 + cj * s0)).astype(BF16)


def _dft_tables(L):
    half = L // 2
    small = pl.BlockSpec((TABLE_ROWS, half), lambda *_: (0, 0))
    base = pl.pallas_call(
        functools.partial(_base_table_kernel, L=L),
        out_specs=[small] * 4,
        out_shape=[jax.ShapeDtypeStruct((TABLE_ROWS, half), F32)] * 4,
        compiler_params=pltpu.CompilerParams(vmem_limit_bytes=VMEM_LIMIT),
        name=f"dft_base_tables_{L}",
    )()
    spec = pl.BlockSpec((TABLE_ROWS, half), lambda i: (i, 0))
    return pl.pallas_call(
        _table_kernel,
        grid=(L // TABLE_ROWS,),
        in_specs=[small] * 4,
        out_specs=[spec, spec],
        out_shape=[jax.ShapeDtypeStruct((L, half), BF16)] * 2,
        compiler_params=_cparams(("arbitrary",)),
        name=f"dft_tables_{L}",
    )(*base)


def _dft_kernel(c_ref, s_ref, va_ref, vb_ref, o_ref, vaf, vbf, *, L):
    tk, tc = o_ref.shape
    half = L // 2
    fb = min(256, half)

    @pl.when(pl.program_id(1) == 0)
    def _():
        r = lax.broadcasted_iota(jnp.int32, (fb, fb), 0)
        c = lax.broadcasted_iota(jnp.int32, (fb, fb), 1)
        flip = jnp.where(r + c == fb, 1.0, 0.0).astype(BF16)
        first_row = lax.broadcasted_iota(jnp.int32, (fb, tc), 0) == 0
        for m in range(half // fb):
            pm = L // fb - m - 1
            for src, dst, sign in ((va_ref, vaf, 1.0), (vb_ref, vbf, -1.0)):
                partner = jnp.dot(flip, src[pm * fb:(pm + 1) * fb, :], preferred_element_type=F32)
                if m >= 1:
                    partner = partner + jnp.where(
                        first_row, src[(pm + 1) * fb:(pm + 1) * fb + 1, :].astype(F32), 0.0)
                dst[m * fb:(m + 1) * fb, :] = (src[m * fb:(m + 1) * fb, :].astype(F32) + sign * partner).astype(BF16)

    k = lax.broadcasted_iota(jnp.int32, (tk, 1), 0) + pl.program_id(1) * tk
    alt = (1 - 2 * (k & 1)).astype(F32) * (1.0 / math.sqrt(L))
    o_ref[...] = (jnp.dot(c_ref[...], vaf[...], preferred_element_type=F32)
                  + jnp.dot(s_ref[...], vbf[...], preferred_element_type=F32)
                  + alt * va_ref[half:half + 1, :].astype(F32))


def _dft_apply(ct, st, va, vb):
    L, ncols = va.shape
    half = L // 2
    tk, tc = 256, 1024
    return pl.pallas_call(
        functools.partial(_dft_kernel, L=L),
        grid=(ncols // tc, L // tk),
        in_specs=[
            pl.BlockSpec((tk, half), lambda j, i: (i, 0)),
            pl.BlockSpec((tk, half), lambda j, i: (i, 0)),
            pl.BlockSpec((L, tc), lambda j, i: (0, j)),
            pl.BlockSpec((L, tc), lambda j, i: (0, j)),
        ],
        out_specs=pl.BlockSpec((tk, tc), lambda j, i: (i, j)),
        out_shape=jax.ShapeDtypeStruct((L, ncols), F32),
        scratch_shapes=[pltpu.VMEM((half, tc), BF16), pltpu.VMEM((half, tc), BF16)],
        compiler_params=_cparams(("arbitrary", "arbitrary")),
        name=f"dft_apply_{L}",
    )(ct, st, va, vb)


def _rms_mod(x, g, scale, shift):
    ms = jnp.mean(x * x, axis=-1, keepdims=True)
    return (x * lax.rsqrt(ms + EPS)) * g * (1.0 + scale) + shift


def _pos_table_kernel(s_ref, c_ref):
    n, nf = s_ref.shape
    j = lax.broadcasted_iota(jnp.int32, (n, nf), 0).astype(F32)
    k = lax.broadcasted_iota(jnp.int32, (n, nf), 1).astype(F32)
    ang = j * jnp.exp(k * (-math.log(10000.0) / nf))
    s_ref[...] = jnp.sin(ang)
    c_ref[...] = jnp.cos(ang)


def _pos_tables(n):
    nf = D_MODEL // 4
    return pl.pallas_call(
        _pos_table_kernel,
        out_shape=[jax.ShapeDtypeStruct((n, nf), F32)] * 2,
        name="pos_tables",
    )()


def _pos_tile(ps_ref, pc_ref, t, tm):
    nrow = tm // GRID_W
    nf = D_MODEL // 4

    def rows(tab):
        return jnp.concatenate(
            [jnp.broadcast_to(tab[pl.ds(t * nrow + k, 1), :], (GRID_W, nf)) for k in range(nrow)], axis=0)

    def cols(tab):
        return jnp.concatenate([tab[0:GRID_W, :]] * nrow, axis=0)

    return jnp.concatenate([rows(ps_ref), rows(pc_ref), cols(ps_ref), cols(pc_ref)], axis=-1)


def _in_kernel(*refs, add_pos, row_base, row_step):
    if add_pos:
        x_ref, ps_ref, pc_ref, mod_ref, g_ref, w_ref, a_ref, b_ref, zr_ref, zuv_ref, va_ref, vb_ref = refs
    else:
        x_ref, mod_ref, g_ref, w_ref, a_ref, b_ref, zr_ref, zuv_ref, va_ref, vb_ref = refs
    x = x_ref[0]
    if add_pos:
        x = x + _pos_tile(ps_ref, pc_ref, pl.program_id(1), x.shape[0])
    row = row_base + pl.program_id(0) * row_step
    m = mod_ref[pl.ds(row, 1), :]
    h = _rms_mod(x, g_ref[...], m[:, D_MODEL:2 * D_MODEL], m[:, 0:D_MODEL])
    z = jnp.dot(h.astype(BF16), w_ref[...], preferred_element_type=F32)
    zr_ref[0] = z[:, 0:2 * D_RNN]
    zuv_ref[0] = z[:, 2 * D_RNN:2 * D_RNN + 2 * D_TMLP]
    zf = z[:, 2 * D_RNN + 2 * D_TMLP:D_IN].astype(BF16)
    va_ref[...] = jnp.dot(zf, a_ref[...], preferred_element_type=F32).astype(BF16)
    vb_ref[...] = jnp.dot(zf, b_ref[...], preferred_element_type=F32).astype(BF16)


def _in_proj(x, pos, mods, g, w_in_b, fa, fb, *, layer, row_base, row_step):
    nseq, L, _ = x.shape
    tm = min(L, PROJ_TILE)
    add_pos = pos is not None
    in_specs = [pl.BlockSpec((1, tm, D_MODEL), lambda b, t: (b, t, 0))]
    args = [x]
    if add_pos:
        in_specs += [pl.BlockSpec(p.shape, lambda b, t: (0, 0)) for p in pos]
        args += list(pos)
    in_specs += [
        _layer_spec((SUBLANES, N_MOD * D_MODEL), layer),
        _layer_spec((1, D_MODEL), layer),
        _layer_spec((D_MODEL, D_IN), layer),
        _layer_spec((D_FNET, D_FNET), layer),
        _layer_spec((D_FNET, D_FNET), layer),
    ]
    args += [mods, g, w_in_b, fa, fb]
    return pl.pallas_call(
        functools.partial(_in_kernel, add_pos=add_pos, row_base=row_base, row_step=row_step),
        grid=(nseq, L // tm),
        in_specs=in_specs,
        out_specs=[
            pl.BlockSpec((1, tm, 2 * D_RNN), lambda b, t: (b, t, 0)),
            pl.BlockSpec((1, tm, 2 * D_TMLP), lambda b, t: (b, t, 0)),
            pl.BlockSpec((tm, D_FNET), lambda b, t: (t, b)),
            pl.BlockSpec((tm, D_FNET), lambda b, t: (t, b)),
        ],
        out_shape=[
            jax.ShapeDtypeStruct((nseq, L, 2 * D_RNN), F32),
            jax.ShapeDtypeStruct((nseq, L, 2 * D_TMLP), F32),
            jax.ShapeDtypeStruct((L, nseq * D_FNET), BF16),
            jax.ShapeDtypeStruct((L, nseq * D_FNET), BF16),
        ],
        compiler_params=_cparams(("arbitrary", "arbitrary")),
        name=f"in_proj_{L}",
    )(*args)


def _gelu_tanh(x):
    return 0.5 * x * (1.0 + jnp.tanh(math.sqrt(2.0 / math.pi) * (x + 0.044715 * (x * x * x))))


def _rows_to_tile(rows):
    sub = lax.broadcasted_iota(jnp.int32, (SUBLANES, LANES), 0)
    out = jnp.zeros((SUBLANES, LANES), F32)
    for s, r in enumerate(rows):
        out = jnp.where(sub == s, jnp.broadcast_to(r, (SUBLANES, LANES)), out)
    return out


def _lru_kernel(xr_ref, gr_ref, cw_ref, cb_ref, wa_ref, wx_ref, ba_ref, bx_ref, lam_ref, h0_ref,
                y_ref, st_ref, xnat, pext, af, bf, ab, bb, hfo, pfo, hbo, pbo, hnat, *, L):
    S = L // SUBLANES
    pitch = S + SUBLANES
    n = S * SUBLANES
    chunk = 256

    for s in range(SUBLANES):
        xnat[s * pitch:s * pitch + S, :] = xr_ref[0, s * S:(s + 1) * S, :]

    def perm_in(j, c):
        dst = pl.multiple_of((j + 2) * SUBLANES, SUBLANES)
        pext[pl.ds(dst, SUBLANES), :] = xnat[pl.ds(j, SUBLANES, stride=pitch), :]
        return c

    lax.fori_loop(0, S, perm_in, 0, unroll=8)

    sub = lax.broadcasted_iota(jnp.int32, (SUBLANES, LANES), 0)

    def from_prev_segment(v):
        return jnp.where(sub == 0, 0.0, pltpu.roll(v, 1, axis=0))

    def from_next_segment(v):
        return jnp.where(sub == SUBLANES - 1, 0.0, pltpu.roll(v, SUBLANES - 1, axis=0))

    pext[0:8, :] = from_prev_segment(pext[S * 8:(S + 1) * 8, :])
    pext[8:16, :] = from_prev_segment(pext[(S + 1) * 8:(S + 2) * 8, :])
    pext[(S + 2) * 8:(S + 3) * 8, :] = from_next_segment(pext[16:24, :])

    lam = lam_ref[...]
    nl = -lam
    sp = jnp.maximum(nl, 0.0) + jnp.log1p(jnp.exp(-jnp.abs(nl)))
    c_la = (-0.5 * LRU_C) * sp
    ba_h = 0.5 * ba_ref[...]
    bx_h = 0.5 * bx_ref[...]
    a_refs = (af, ab)
    b_refs = (bf, bb)

    def gates(i, c):
        base = pl.multiple_of(i * chunk, chunk)
        xc = (cw_ref[0:1, :] * pext[pl.ds(base, chunk), :]
              + cw_ref[1:2, :] * pext[pl.ds(base + 8, chunk), :]
              + cw_ref[2:3, :] * pext[pl.ds(base + 16, chunk), :]
              + cw_ref[3:4, :] * pext[pl.ds(base + 24, chunk), :]
              + cb_ref[...])
        xcb = xc.astype(BF16)
        xh = 0.5 * xc
        for d in range(2):
            tr = jnp.tanh(jnp.dot(xcb, wa_ref[d, 0], preferred_element_type=F32) + ba_h[d:d + 1, :])
            ti = jnp.tanh(jnp.dot(xcb, wx_ref[d, 0], preferred_element_type=F32) + bx_h[d:d + 1, :])
            log_a = c_la[d:d + 1, :] * (1.0 + tr)
            a = jnp.exp(log_a)
            v = jnp.tanh(log_a) * (-1.0 - a * a)
            coef = jnp.where(v > 0.0, v * lax.rsqrt(v), 0.0)
            a_refs[d][pl.ds(base, chunk), :] = a
            b_refs[d][pl.ds(base, chunk), :] = coef * ((1.0 + ti) * xh)
        return c

    lax.fori_loop(0, n // chunk, gates, 0, unroll=min(4, n // chunk))

    nq = LRU_SUB
    sq = S // nq

    def scan(i, carry):
        out = []
        for q in range(nq):
            hf, pf, hb, pb = carry[4 * q:4 * q + 4]
            jf = pl.multiple_of((q * sq + i) * SUBLANES, SUBLANES)
            jb = pl.multiple_of(((q + 1) * sq - 1 - i) * SUBLANES, SUBLANES)
            a1 = af[pl.ds(jf, SUBLANES), :]
            hf = a1 * hf + bf[pl.ds(jf, SUBLANES), :]
            pf = a1 * pf
            hfo[pl.ds(jf, SUBLANES), :] = hf
            pfo[pl.ds(jf, SUBLANES), :] = pf
            a2 = ab[pl.ds(jb, SUBLANES), :]
            hb = a2 * hb + bb[pl.ds(jb, SUBLANES), :]
            pb = a2 * pb
            hbo[pl.ds(jb, SUBLANES), :] = hb
            pbo[pl.ds(jb, SUBLANES), :] = pb
            out += [hf, pf, hb, pb]
        return tuple(out)

    zero = jnp.zeros((SUBLANES, LANES), F32)
    one = jnp.ones((SUBLANES, LANES), F32)
    ends = lax.fori_loop(0, sq, scan, (zero, one, zero, one) * nq, unroll=min(sq, 4))
    end_f = [(ends[4 * q], ends[4 * q + 1]) for q in range(nq)]
    end_b = [(ends[4 * q + 2], ends[4 * q + 3]) for q in range(nq)]

    hf, pf = end_f[0]
    for q in range(1, nq):
        hf, pf = end_f[q][1] * hf + end_f[q][0], end_f[q][1] * pf
    hb, pb = end_b[nq - 1]
    for q in range(nq - 2, -1, -1):
        hb, pb = end_b[q][1] * hb + end_b[q][0], end_b[q][1] * pb

    rows_f = [h0_ref[0, 0:1, :]]
    for s in range(1, SUBLANES):
        rows_f.append(pf[s - 1:s, :] * rows_f[-1] + hf[s - 1:s, :])
    st_ref[0, 0:1, :] = pf[7:8, :] * rows_f[7] + hf[7:8, :]
    rows_b = [None] * SUBLANES
    rows_b[7] = h0_ref[0, 1:2, :]
    for s in range(SUBLANES - 2, -1, -1):
        rows_b[s] = pb[s + 1:s + 2, :] * rows_b[s + 1] + hb[s + 1:s + 2, :]
    st_ref[0, 1:2, :] = pb[0:1, :] * rows_b[0] + hb[0:1, :]

    init_f = [_rows_to_tile(rows_f)]
    for q in range(nq - 1):
        init_f.append(end_f[q][1] * init_f[q] + end_f[q][0])
    init_b = [None] * nq
    init_b[nq - 1] = _rows_to_tile(rows_b)
    for q in range(nq - 1, 0, -1):
        init_b[q - 1] = end_b[q][1] * init_b[q] + end_b[q][0]

    for q in range(nq):
        def perm_out(j, c, q=q):
            src = pl.multiple_of(j * SUBLANES, SUBLANES)
            v = (hfo[pl.ds(src, SUBLANES), :] + pfo[pl.ds(src, SUBLANES), :] * init_f[q]
                 + hbo[pl.ds(src, SUBLANES), :] + pbo[pl.ds(src, SUBLANES), :] * init_b[q])
            hnat[pl.ds(j, SUBLANES, stride=pitch), :] = v
            return c

        lax.fori_loop(q * sq, (q + 1) * sq, perm_out, 0, unroll=min(sq, 8))

    for s in range(SUBLANES):
        g = gr_ref[0, s * S:(s + 1) * S, :]
        y_ref[0, s * S:(s + 1) * S, :] = hnat[s * pitch:s * pitch + S, :] * _gelu_tanh(g)


def _lru_mixer(zr, conv_w, conv_b, wa_bd, wx_bd, ba, bx, lam, h0, *, layer):
    nseq, L, _ = zr.shape
    ncb = D_RNN // LRU_CB
    S = L // SUBLANES
    pitch = S + SUBLANES
    vec2 = pl.BlockSpec((None, 2, LRU_CB), lambda b, c: (layer, 0, c))
    wspec = pl.BlockSpec((None, 2, 1, LRU_CB, LRU_CB), lambda b, c: (layer, 0, c, 0, 0))
    return pl.pallas_call(
        functools.partial(_lru_kernel, L=L),
        grid=(nseq, ncb),
        in_specs=[
            pl.BlockSpec((1, L, LRU_CB), lambda b, c: (b, 0, c)),
            pl.BlockSpec((1, L, LRU_CB), lambda b, c: (b, 0, c + ncb)),
            pl.BlockSpec((None, CONV_W, LRU_CB), lambda b, c: (layer, 0, c)),
            pl.BlockSpec((None, 1, LRU_CB), lambda b, c: (layer, 0, c)),
            wspec, wspec, vec2, vec2, vec2,
            pl.BlockSpec((1, 2, LRU_CB), lambda b, c: (b, 0, c)),
        ],
        out_specs=[
            pl.BlockSpec((1, L, LRU_CB), lambda b, c: (b, 0, c)),
            pl.BlockSpec((1, 2, LRU_CB), lambda b, c: (b, 0, c)),
        ],
        out_shape=[
            jax.ShapeDtypeStruct((nseq, L, D_RNN), F32),
            jax.ShapeDtypeStruct((nseq, 2, D_RNN), F32),
        ],
        scratch_shapes=[
            pltpu.VMEM((SUBLANES * pitch, LANES), F32),
            pltpu.VMEM(((S + 3) * SUBLANES, LANES), F32),
            *([pltpu.VMEM((L, LANES), F32)] * 8),
            pltpu.VMEM((SUBLANES * pitch, LANES), F32),
        ],
        compiler_params=_cparams(("arbitrary", "arbitrary")),
        name=f"lru_mixer_{L}",
    )(zr, zr, conv_w, conv_b, wa_bd, wx_bd, ba, bx, lam, h0)


def _out_kernel(*refs, add_pos, row_base, row_step):
    if add_pos:
        (x_ref, ps_ref, pc_ref, yr_ref, zuv_ref, yf_ref, mod_ref, ws_ref, bt_ref, wo_ref, g_ref, wr_ref, br_ref,
         cin_ref, x1_ref, h2_ref, ri_ref, cnt_ref, carry) = refs
    else:
        (x_ref, yr_ref, zuv_ref, yf_ref, mod_ref, ws_ref, bt_ref, wo_ref, g_ref, wr_ref, br_ref, cin_ref,
         x1_ref, h2_ref, ri_ref, cnt_ref, carry) = refs
    tm = x_ref.shape[1]
    x = x_ref[0]
    if add_pos:
        x = x + _pos_tile(ps_ref, pc_ref, pl.program_id(1), tm)
    row = row_base + pl.program_id(0) * row_step
    m = mod_ref[pl.ds(row, 1), :]
    g1 = m[:, 2 * D_MODEL:3 * D_MODEL]
    sh2 = m[:, 3 * D_MODEL:4 * D_MODEL]
    sc2 = m[:, 4 * D_MODEL:5 * D_MODEL]

    head = lax.broadcasted_iota(jnp.int32, (CHUNK, D_TMLP), 1) >> 6
    yt_parts = []
    for ci in range(tm // CHUNK):
        u = zuv_ref[0, ci * CHUNK:(ci + 1) * CHUNK, 0:D_TMLP]
        v = zuv_ref[0, ci * CHUNK:(ci + 1) * CHUNK, D_TMLP:2 * D_TMLP].astype(BF16)
        s = jnp.zeros((CHUNK, D_TMLP), F32)
        for h in range(TMLP_HEADS):
            sh = jnp.dot(ws_ref[h], v, preferred_element_type=F32) + bt_ref[:, h:h + 1]
            s = jnp.where(head == h, sh, s)
        yt_parts.append(u * s)
    yt = jnp.concatenate(yt_parts, axis=0) if len(yt_parts) > 1 else yt_parts[0]

    y = (jnp.dot(yr_ref[0].astype(BF16), wo_ref[0:D_RNN, :], preferred_element_type=F32)
         + jnp.dot(yt.astype(BF16), wo_ref[D_RNN:D_RNN + D_TMLP, :], preferred_element_type=F32)
         + jnp.dot(yf_ref[...].astype(BF16), wo_ref[D_RNN + D_TMLP:D_MODEL, :], preferred_element_type=F32))
    x1 = x + g1 * y
    x1_ref[0] = x1
    h2 = _rms_mod(x1, g_ref[...], sc2, sh2)
    _store_token_major(h2_ref, h2)

    wr = wr_ref[...]
    w_hi = wr.astype(BF16)
    w_lo = (wr - w_hi.astype(F32)).astype(BF16)
    h_hi = h2.astype(BF16)
    h_lo = (h2 - h_hi.astype(F32)).astype(BF16)
    p_hi = jnp.dot(h_hi, jnp.concatenate([w_hi, w_lo], axis=-1), preferred_element_type=F32)
    p_lo = jnp.dot(h_lo, w_hi, preferred_element_type=F32)
    logits = p_hi[:, 0:ROUTER_LANES] + p_hi[:, ROUTER_LANES:2 * ROUTER_LANES] + p_lo + br_ref[...]
    lane = lax.broadcasted_iota(jnp.int32, (tm, ROUTER_LANES), 1)
    lane_f = lane.astype(F32)
    is_g = lane < N_EGROUPS
    gl = jnp.where(is_g, logits, NEG_BIG)
    gmax = jnp.max(gl, axis=-1, keepdims=True)
    gsel = jnp.min(jnp.where(gl == gmax, lane_f, 1e4), axis=-1, keepdims=True)
    pg = 1.0 / jnp.sum(jnp.where(is_g, jnp.exp(logits - gmax), 0.0), axis=-1, keepdims=True)
    grp_f = ((lane - N_EGROUPS) >> 3).astype(F32)
    emask = (lane >= N_EGROUPS) & (lane < N_EGROUPS + N_EXPERTS) & (grp_f == gsel)
    el = jnp.where(emask, logits, NEG_BIG)
    v1 = jnp.max(el, axis=-1, keepdims=True)
    i1 = jnp.min(jnp.where(el == v1, lane_f, 1e4), axis=-1, keepdims=True)
    el2 = jnp.where(lane_f == i1, NEG_BIG, el)
    v2 = jnp.max(el2, axis=-1, keepdims=True)
    i2 = jnp.min(jnp.where(el2 == v2, lane_f, 1e4), axis=-1, keepdims=True)
    e2x = jnp.exp(v2 - v1)
    fw1 = 1.0 / (1.0 + e2x)
    fw2 = e2x * fw1
    @pl.when((pl.program_id(0) == 0) & (pl.program_id(1) == 0))
    def _():
        carry[...] = cin_ref[...]

    e1 = i1 - N_EGROUPS
    e2 = i2 - N_EGROUPS
    m1 = lane_f == e1
    m2 = lane_f == e2
    oh = jnp.where(m1 | m2, 1.0, 0.0)
    r_i = lax.broadcasted_iota(jnp.int32, (tm, tm), 0)
    c_i = lax.broadcasted_iota(jnp.int32, (tm, tm), 1)
    tri = jnp.where(c_i < r_i, 1.0, 0.0).astype(BF16)
    before = jnp.dot(tri, oh.astype(BF16), preferred_element_type=F32) + carry[0:1, :]
    rank1 = jnp.sum(jnp.where(m1, before, 0.0), axis=-1, keepdims=True)
    rank2 = jnp.sum(jnp.where(m2, before, 0.0), axis=-1, keepdims=True)
    total = carry[0:1, :] + jnp.sum(oh, axis=0, keepdims=True)
    carry[0:1, :] = total
    cnt_ref[...] = jnp.broadcast_to(total, cnt_ref.shape)

    vals = (e1, e2, pg * fw1, pg * fw2, rank1, rank2)
    ri = jnp.zeros((tm, ROUTER_LANES), F32)
    for k, v in enumerate(vals):
        ri = jnp.where(lane == k, v, ri)
    ri_ref[0] = ri


def _layer_spec(shape, layer):
    zeros = (0,) * len(shape)
    return pl.BlockSpec((None, *shape), lambda *_: (layer, *zeros))


def _out_proj(x, pos, yr, zuv, yf, mods, ws_b, bt, wo_b, g, wr, br, cin, *, layer, row_base, row_step):
    nseq, L, _ = x.shape
    tm = min(L, PROJ_TILE)
    add_pos = pos is not None
    in_specs = [pl.BlockSpec((1, tm, D_MODEL), lambda b, t: (b, t, 0))]
    args = [x]
    if add_pos:
        in_specs += [pl.BlockSpec(p.shape, lambda b, t: (0, 0)) for p in pos]
        args += list(pos)
    in_specs += [
        pl.BlockSpec((1, tm, D_RNN), lambda b, t: (b, t, 0)),
        pl.BlockSpec((1, tm, 2 * D_TMLP), lambda b, t: (b, t, 0)),
        pl.BlockSpec((tm, D_FNET), lambda b, t: (t, b)),
        _layer_spec((SUBLANES, N_MOD * D_MODEL), layer),
        _layer_spec((TMLP_HEADS, CHUNK, CHUNK), layer),
        _layer_spec((CHUNK, TMLP_HEADS), layer),
        _layer_spec((D_MODEL, D_MODEL), layer),
        _layer_spec((1, D_MODEL), layer),
        _layer_spec((D_MODEL, ROUTER_LANES), layer),
        _layer_spec((1, ROUTER_LANES), layer),
        pl.BlockSpec((SUBLANES, ROUTER_LANES), lambda b, t: (0, 0)),
    ]
    args += [yr, zuv, yf, mods, ws_b, bt, wo_b, g, wr, br, cin]
    tok = pl.BlockSpec((1, tm, D_MODEL), lambda b, t: (b, t, 0))
    return pl.pallas_call(
        functools.partial(_out_kernel, add_pos=add_pos, row_base=row_base, row_step=row_step),
        grid=(nseq, L // tm),
        in_specs=in_specs,
        out_specs=[tok, pl.BlockSpec((tm * LANE_TILES, LANES), lambda b, t: (b * (L // tm) + t, 0)),
                   pl.BlockSpec((1, tm, ROUTER_LANES), lambda b, t: (b, t, 0)),
                   pl.BlockSpec((SUBLANES, ROUTER_LANES), lambda b, t: (0, 0))],
        out_shape=[
            jax.ShapeDtypeStruct((nseq, L, D_MODEL), F32),
            jax.ShapeDtypeStruct((nseq * L * LANE_TILES, LANES), F32),
            jax.ShapeDtypeStruct((nseq, L, ROUTER_LANES), F32),
            jax.ShapeDtypeStruct((SUBLANES, ROUTER_LANES), F32),
        ],
        scratch_shapes=[pltpu.VMEM((SUBLANES, ROUTER_LANES), F32)],
        compiler_params=_cparams(("arbitrary", "arbitrary")),
        name=f"out_proj_{L}",
    )(*args)


def _store_token_major(ref, x):
    tm = x.shape[0]
    for j in range(LANE_TILES):
        ref[pl.ds(j, tm, stride=LANE_TILES), :] = x[:, j * LANES:(j + 1) * LANES]


def _load_token_major(ref):
    tm = ref.shape[0] // LANE_TILES
    return jnp.concatenate([ref[pl.ds(j, tm, stride=LANE_TILES), :] for j in range(LANE_TILES)], axis=-1)


def _token_copy(src_ref, src_tok, dst_ref, dst_tok, sem):
    return pltpu.make_async_copy(src_ref.at[pl.ds(pl.multiple_of(src_tok, LANE_TILES), LANE_TILES)],
                                 dst_ref.at[pl.ds(pl.multiple_of(dst_tok, LANE_TILES), LANE_TILES)], sem)


def _scatter_rows(pos_ref, h_ref, xs_ref, sem):
    rows = h_ref.shape[0]

    for r in range(0, rows, LANE_TILES):
        _token_copy(h_ref, r, xs_ref, pos_ref[0, 0, r // LANE_TILES], sem).start(priority=0)
        _token_copy(h_ref, r, xs_ref, pos_ref[0, 1, r // LANE_TILES], sem).start(priority=1)
    for _ in range(2):
        pltpu.make_async_copy(h_ref, xs_ref.at[pl.ds(0, rows)], sem).wait()


def _dispatch_kernel(pos_ref, hp_ref, hs_ref, xs_ref, sem, *, n_first):
    i = pl.program_id(0)

    @pl.when(i < n_first)
    def _():
        _scatter_rows(pos_ref, hp_ref, xs_ref, sem)

    @pl.when(i >= n_first)
    def _():
        _scatter_rows(pos_ref, hs_ref, xs_ref, sem)


def _dispatch(pos, h_first, h_second):
    tm = ROW_TILE
    blk = tm * LANE_TILES
    n_first = h_first.shape[0] // blk
    n_second = h_second.shape[0] // blk
    nrows = 2 * (h_first.shape[0] + h_second.shape[0])
    return pl.pallas_call(
        functools.partial(_dispatch_kernel, n_first=n_first),
        grid=(n_first + n_second,),
        in_specs=[
            pl.BlockSpec((1, 2, tm), lambda i: (i, 0, 0), memory_space=pltpu.SMEM),
            pl.BlockSpec((blk, LANES), lambda i: (jnp.minimum(i, n_first - 1), 0)),
            pl.BlockSpec((blk, LANES), lambda i: (jnp.maximum(i - n_first, 0), 0)),
        ],
        out_specs=pl.BlockSpec(memory_space=pl.ANY),
        out_shape=jax.ShapeDtypeStruct((nrows, LANES), F32),
        scratch_shapes=[pltpu.SemaphoreType.DMA(())],
        compiler_params=_cparams(("arbitrary",)),
        name="moe_dispatch",
    )(pos, h_first, h_second)


WEIGHT_LEAD = (3, 2, 1)
WEIGHT_SLOTS = 4
LEAD_ITEMS = max(WEIGHT_LEAD)


def _expert_kernel(wt_ref, we_ref, lo_ref, hi_ref, first_ref, ord_ref, xs_ref, w1_ref, w3_ref, w2_ref, ys_ref,
                   w1b, w3b, w2b):
    s = pl.program_id(0)
    last = pl.num_programs(0) - 1
    tm = EXPERT_TILE

    def stage(lead, src, dst):
        cur = jnp.minimum(s + lead, last)
        prv = jnp.minimum(s + lead - 1, last)

        @pl.when((s == 0) | (we_ref[cur] != we_ref[prv]))
        def _():
            dst[ord_ref[cur] & (WEIGHT_SLOTS - 1)] = src[0].astype(BF16)

    for lead, src, dst in zip(WEIGHT_LEAD, (w1_ref, w3_ref, w2_ref), (w1b, w3b, w2b)):
        stage(lead, src, dst)

    lo = lo_ref[s]
    hi = hi_ref[s]
    full = (lo == 0) & (hi == tm)

    @pl.when((first_ref[s] == 1) & jnp.logical_not(full))
    def _():
        ys_ref[...] = jnp.zeros_like(ys_ref)

    @pl.when(hi > lo)
    def _():
        slot = ord_ref[s] & (WEIGHT_SLOTS - 1)
        x = _load_token_major(xs_ref).astype(BF16)
        a = jnp.dot(x, w1b[slot], preferred_element_type=F32)
        b = jnp.dot(x, w3b[slot], preferred_element_type=F32)
        hid = (a * jax.nn.sigmoid(a)) * b
        res = jnp.dot(hid.astype(BF16), w2b[slot], preferred_element_type=F32)

        @pl.when(full)
        def _():
            _store_token_major(ys_ref, res)

        @pl.when(jnp.logical_not(full))
        def _():
            rows = lax.broadcasted_iota(jnp.int32, res.shape, 0)
            _store_token_major(ys_ref, jnp.where((rows >= lo) & (rows < hi), res, _load_token_major(ys_ref)))


def _experts(items, xs, w1, w3, w2):
    nw = items[0].shape[0]

    def row(s, wt, we, lo, hi, fi, od):
        return (wt[s], 0)

    def weight(lead):
        return lambda s, wt, we, lo, hi, fi, od: (we[jnp.minimum(s + lead, nw - 1)], 0, 0)

    return pl.pallas_call(
        _expert_kernel,
        grid_spec=pltpu.PrefetchScalarGridSpec(
            num_scalar_prefetch=6,
            grid=(nw,),
            in_specs=[
                pl.BlockSpec((EXPERT_TILE * LANE_TILES, LANES), row),
                pl.BlockSpec((1, D_MODEL, D_EXPERT), weight(WEIGHT_LEAD[0])),
                pl.BlockSpec((1, D_MODEL, D_EXPERT), weight(WEIGHT_LEAD[1])),
                pl.BlockSpec((1, D_EXPERT, D_MODEL), weight(WEIGHT_LEAD[2])),
            ],
            out_specs=pl.BlockSpec((EXPERT_TILE * LANE_TILES, LANES), row),
            scratch_shapes=[
                pltpu.VMEM((WEIGHT_SLOTS, D_MODEL, D_EXPERT), BF16),
                pltpu.VMEM((WEIGHT_SLOTS, D_MODEL, D_EXPERT), BF16),
                pltpu.VMEM((WEIGHT_SLOTS, D_EXPERT, D_MODEL), BF16),
            ],
        ),
        out_shape=jax.ShapeDtypeStruct(xs.shape, F32),
        compiler_params=_cparams(("arbitrary",)),
        name="moe_experts",
    )(*items, xs, w1, w3, w2)


def _combine_kernel(pos_ref, posn_ref, x1_ref, ri_ref, mod_ref, gf_ref, ys_ref, o_ref, ybuf, sem,
                    *, row_base, row_step, tiles_per_seq, final):
    tm = x1_ref.shape[0]
    i = pl.program_id(0)
    n = pl.num_programs(0)

    def gather(p_ref, slot):
        for r in range(tm):
            _token_copy(ys_ref, p_ref[0, 0, r], ybuf.at[slot, 0], r * LANE_TILES, sem.at[slot]).start(priority=0)
            _token_copy(ys_ref, p_ref[0, 1, r], ybuf.at[slot, 1], r * LANE_TILES, sem.at[slot]).start(priority=1)

    def drain(slot):
        for k in range(2):
            pltpu.make_async_copy(ys_ref.at[pl.ds(0, tm * LANE_TILES)], ybuf.at[slot, k], sem.at[slot]).wait()

    def tile(slot):
        if slot == 0:
            @pl.when(i == 0)
            def _():
                gather(pos_ref, 0)

        @pl.when(i + 1 < n)
        def _():
            gather(posn_ref, 1 - slot)

        drain(slot)
        row = row_base + (i // tiles_per_seq) * row_step
        g2 = mod_ref[pl.ds(row, 1), 5 * D_MODEL:6 * D_MODEL]
        ri = ri_ref[...]
        y = (ri[:, 2:3] * _load_token_major(ybuf.at[slot, 0])
             + ri[:, 3:4] * _load_token_major(ybuf.at[slot, 1]))
        x2 = x1_ref[...] + g2 * y
        if final:
            ms = jnp.mean(x2 * x2, axis=-1, keepdims=True)
            o_ref[...] = (x2 * lax.rsqrt(ms + EPS)) * gf_ref[...]
        else:
            o_ref[...] = x2

    for slot in range(2):
        pl.when((i & 1) == slot)(functools.partial(tile, slot))


def _combine(pos, x1, ri, mods, g_final, ys, *, layer, row_base, row_step, tiles_per_seq, final):
    ntok = x1.shape[0]
    tm = ROW_TILE
    tok = pl.BlockSpec((tm, D_MODEL), lambda i: (i, 0))
    return pl.pallas_call(
        functools.partial(_combine_kernel, row_base=row_base, row_step=row_step, tiles_per_seq=tiles_per_seq,
                          final=final),
        grid=(ntok // tm,),
        in_specs=[
            pl.BlockSpec((1, 2, tm), lambda i: (i, 0, 0), memory_space=pltpu.SMEM),
            pl.BlockSpec((1, 2, tm), lambda i: (jnp.minimum(i + 1, ntok // tm - 1), 0, 0), memory_space=pltpu.SMEM),
            tok,
            pl.BlockSpec((tm, ROUTER_LANES), lambda i: (i, 0)),
            _layer_spec((SUBLANES, N_MOD * D_MODEL), layer),
            pl.BlockSpec((1, D_MODEL), lambda i: (0, 0)),
            pl.BlockSpec(memory_space=pl.ANY),
        ],
        out_specs=tok,
        out_shape=jax.ShapeDtypeStruct((ntok, D_MODEL), F32),
        scratch_shapes=[
            pltpu.VMEM((2, 2, tm * LANE_TILES, LANES), F32),
            pltpu.SemaphoreType.DMA((2,)),
        ],
        compiler_params=_cparams(("arbitrary",)),
        name=f"moe_combine_{ntok}",
    )(pos, pos, x1, ri, mods, g_final, ys)


def _route(ri_first, ri_second, cnt):
    counts = cnt[0, 0:N_EXPERTS].astype(jnp.int32)
    offs = jnp.cumsum(counts) - counts
    info = jnp.concatenate([ri_first[:, 0:SUBLANES], ri_second[:, 0:SUBLANES]], axis=0).T
    e = info[0:2].astype(jnp.int32)
    base = jnp.zeros_like(e)
    for k in range(N_EXPERTS):
        base = jnp.where(e == k, offs[k], base)
    slot = (base + info[4:6].astype(jnp.int32)) * LANE_TILES
    n_tiles = slot.shape[1] // ROW_TILE
    pos_tiles = jnp.swapaxes(slot.reshape(2, n_tiles, ROW_TILE), 0, 1)
    return pos_tiles, counts, offs


def _work_items(counts, offs, nw, layer):
    tm = EXPERT_TILE
    first_tile = offs // tm
    last_tile = (offs + counts - 1) // tm
    n_e = jnp.where(counts > 0, last_tile - first_tile + 1, 0)
    w_end = jnp.cumsum(n_e)
    w_start = w_end - n_e
    total = w_end[-1]
    w = jnp.arange(nw, dtype=jnp.int32)
    wc = jnp.minimum(w, total - 1)
    e_w = jnp.sum((wc[:, None] >= w_end[None, :]).astype(jnp.int32), axis=-1)
    sel = (e_w[:, None] == jnp.arange(N_EXPERTS, dtype=jnp.int32)).astype(jnp.int32)
    pick = lambda v: jnp.sum(sel * v[None, :], axis=-1)
    off_w = pick(offs)
    tile_w = pick(first_tile) + (wc - pick(w_start))
    lo = jnp.clip(off_w - tile_w * tm, 0, tm)
    hi = jnp.clip(off_w + pick(counts) - tile_w * tm, 0, tm)
    valid = w < total
    lo = jnp.where(valid, lo, 0)
    hi = jnp.where(valid, hi, 0)
    prev_tile = jnp.concatenate([jnp.full((1,), -1, tile_w.dtype), tile_w[:-1]])
    first = tile_w != prev_tile
    ordinal = pick(jnp.cumsum((counts > 0).astype(jnp.int32)) - 1)

    def lead(a, fill=None):
        head = jnp.broadcast_to(a[0] if fill is None else jnp.asarray(fill, a.dtype), (LEAD_ITEMS,))
        return jnp.concatenate([head, a]).astype(jnp.int32)

    return (lead(tile_w), lead(e_w + layer * N_EXPERTS), lead(lo, 0), lead(hi, 0), lead(first, 0), lead(ordinal))


def _block_diag(w, nblk):
    *lead, H, d, _ = w.shape
    w = w.reshape(*lead, H // nblk, nblk, d, d)
    eye = jnp.eye(nblk, dtype=w.dtype)
    out = jnp.einsum('...gij,gh->...gihj', w, eye)
    return out.reshape(*lead, H // nblk, nblk * d, nblk * d)


def kernel(x_prompt, x_sample, state_lru, c, c_ctx, w_ada, b_ada, g_mix, g_ffn, g_final, w_in, w_out, conv_w, conv_b, lru_wa, lru_ba, lru_wx, lru_bx, lru_lambda, tmlp_ws, tmlp_b, fnet_w, router_g, router_g_b, router_e, router_e_b, e_w1, e_w3, e_w2):
    bp, lp, _ = x_prompt.shape
    bs, ls, _ = x_sample.shape
    n_ctx = bp * lp
    n_lat = bs * ls
    n_tok = n_ctx + n_lat

    cond8 = jnp.concatenate([c_ctx[None, :], c, jnp.zeros((SUBLANES - 1 - bs, D_MODEL), F32)], axis=0)
    pos = _pos_tables(max(ls // GRID_W, GRID_W))
    w_in_b = w_in.astype(BF16)
    w_out_b = w_out.astype(BF16)
    heads_per_cb = LRU_CB // RNN_HEAD_DIM
    wa_bd = (0.5 * _block_diag(lru_wa, heads_per_cb)).astype(BF16)
    wx_bd = (0.5 * _block_diag(lru_wx, heads_per_cb)).astype(BF16)
    ws_b = tmlp_ws.astype(BF16)
    bt = jnp.swapaxes(tmlp_b, 1, 2)
    fnet_bd = _block_diag(fnet_w, D_FNET // FNET_GROUP_DIM)[:, 0]
    wr = jnp.concatenate([router_g, router_e,
                          jnp.zeros((DEPTH, D_MODEL, ROUTER_LANES - N_EGROUPS - N_EXPERTS), F32)], axis=-1)
    br = jnp.concatenate([router_g_b, router_e_b,
                          jnp.zeros((DEPTH, ROUTER_LANES - N_EGROUPS - N_EXPERTS), F32)], axis=-1)
    ew1 = e_w1.reshape(DEPTH * N_EXPERTS, D_MODEL, D_EXPERT)
    ew3 = e_w3.reshape(DEPTH * N_EXPERTS, D_MODEL, D_EXPERT)
    ew2 = e_w2.reshape(DEPTH * N_EXPERTS, D_EXPERT, D_MODEL)

    mods = _modulation(cond8, w_ada, b_ada)
    fa, fb = _fnet_prep(fnet_bd)
    ct_p, st_p = _dft_tables(lp)
    ct_s, st_s = _dft_tables(ls)

    nw = 2 * n_tok // EXPERT_TILE + N_EXPERTS
    h0_p = jnp.zeros((bp, 2, D_RNN), F32)
    g_mix3 = g_mix[:, None, :]
    g_ffn3 = g_ffn[:, None, :]
    conv_b3 = conv_b[:, None, :]
    br3 = br[:, None, :]

    xp, xs = x_prompt, x_sample
    states = []
    for l in range(DEPTH):
        paths = []
        cnt = jnp.zeros((SUBLANES, ROUTER_LANES), F32)
        for (x, ct, st, h0, row_base, row_step, is_lat) in (
                (xp, ct_p, st_p, h0_p, 0, 0, False),
                (xs, ct_s, st_s, state_lru[:, l], 1, 1, True)):
            pe = pos if (is_lat and l == 0) else None
            zr, zuv, va, vb = _in_proj(x, pe, mods, g_mix3, w_in_b, fa, fb,
                                       layer=l, row_base=row_base, row_step=row_step)
            yr, st_new = _lru_mixer(zr, conv_w, conv_b3, wa_bd, wx_bd, lru_ba, lru_bx, lru_lambda, h0, layer=l)
            yf = _dft_apply(ct, st, va, vb)
            x1, h2, ri, cnt = _out_proj(x, pe, yr, zuv, yf, mods, ws_b, bt, w_out_b, g_ffn3, wr, br3, cnt,
                                        layer=l, row_base=row_base, row_step=row_step)
            paths.append((x1, h2, ri, st_new))
        states.append(paths[0][3])

        pos_tiles, counts, offs = _route(paths[0][2].reshape(n_ctx, ROUTER_LANES),
                                         paths[1][2].reshape(n_lat, ROUTER_LANES), cnt)
        items = _work_items(counts, offs, nw, l)
        nct = n_ctx // ROW_TILE
        xsorted = _dispatch(pos_tiles, paths[0][1], paths[1][1])
        ysorted = _experts(items, xsorted, ew1, ew3, ew2)
        gfin = g_final[None, :]
        final = l == DEPTH - 1
        xp = _combine(pos_tiles[:nct], paths[0][0].reshape(n_ctx, D_MODEL),
                      paths[0][2].reshape(n_ctx, ROUTER_LANES), mods, gfin, ysorted, layer=l,
                      row_base=0, row_step=0, tiles_per_seq=lp // ROW_TILE, final=final).reshape(bp, lp, D_MODEL)
        xs = _combine(pos_tiles[nct:], paths[1][0].reshape(n_lat, D_MODEL),
                      paths[1][2].reshape(n_lat, ROUTER_LANES), mods, gfin, ysorted, layer=l,
                      row_base=1, row_step=1, tiles_per_seq=ls // ROW_TILE, final=final).reshape(bs, ls, D_MODEL)

    new_state = jnp.stack(states, axis=1).astype(x_prompt.dtype)
    return (xp, xs, new_state)
```

```python
import functools
import math

import jax
import jax.numpy as jnp
from jax import lax
from jax.experimental import pallas as pl
from jax.experimental.pallas import tpu as pltpu

F32 = jnp.float32
BF16 = jnp.bfloat16

D_MODEL = 1024
DEPTH = 2
GRID_W = 64
D_RNN = 512
RNN_HEAD_DIM = 64
CONV_W = 4
LRU_C = 8.0
D_TMLP = 256
TMLP_HEADS = 4
CHUNK = 128
D_FNET = 256
FNET_GROUP_DIM = 64
D_IN = 2 * D_RNN + 2 * D_TMLP + D_FNET
N_EGROUPS = 4
N_EPG = 8
N_EXPERTS = N_EGROUPS * N_EPG
D_EXPERT = 512
N_MOD = 6
EPS = 1e-6

LANES = 128
SUBLANES = 8
LANE_TILES = D_MODEL // LANES
assert LANE_TILES == SUBLANES
LRU_CB = LANES
LRU_SUB = 4
LRU_BATCH_MAX_LEN = 512
ROW_TILE = 256
PROJ_TILE = 512
EXPERT_TILE = 256
ROUTER_LANES = LANES
NEG_BIG = -1e30
VMEM_LIMIT = 56 * 1024 * 1024


def _cparams(sem):
    return pltpu.CompilerParams(dimension_semantics=sem, vmem_limit_bytes=VMEM_LIMIT)


def _mod_kernel(c_ref, w_ref, b_ref, o_ref):
    c = c_ref[...]
    s = c * jax.nn.sigmoid(c)
    o_ref[0] = jnp.dot(s.astype(BF16), w_ref[0].astype(BF16), preferred_element_type=F32) + b_ref[0]


def _modulation(cond8, w_ada, b_ada):
    tn = 1536
    return pl.pallas_call(
        _mod_kernel,
        grid=(DEPTH, N_MOD * D_MODEL // tn),
        in_specs=[
            pl.BlockSpec((SUBLANES, D_MODEL), lambda l, j: (0, 0)),
            pl.BlockSpec((1, D_MODEL, tn), lambda l, j: (l, 0, j)),
            pl.BlockSpec((1, 1, tn), lambda l, j: (l, 0, j)),
        ],
        out_specs=pl.BlockSpec((1, SUBLANES, tn), lambda l, j: (l, 0, j)),
        out_shape=jax.ShapeDtypeStruct((DEPTH, SUBLANES, N_MOD * D_MODEL), F32),
        compiler_params=_cparams(("arbitrary", "arbitrary")),
        name="modulation",
    )(cond8, w_ada, b_ada.reshape(DEPTH, 1, N_MOD * D_MODEL))


def _fprep_kernel(w_ref, a_ref, b_ref):
    r = lax.broadcasted_iota(jnp.int32, (D_FNET, D_FNET), 0)
    c = lax.broadcasted_iota(jnp.int32, (D_FNET, D_FNET), 1)
    same = (r >> 6) == (c >> 6)
    ph = ((r & 63) * (c & 63)) & 63
    ang = ph.astype(F32) * (2.0 * math.pi / FNET_GROUP_DIM)
    scale = 1.0 / math.sqrt(FNET_GROUP_DIM)
    cm = jnp.where(same, jnp.cos(ang) * scale, 0.0)
    sm = jnp.where(same, jnp.sin(ang) * scale, 0.0)
    w = w_ref[0]
    a_ref[0] = jnp.dot(cm, w, precision=lax.Precision.HIGHEST, preferred_element_type=F32).astype(BF16)
    b_ref[0] = jnp.dot(sm, w, precision=lax.Precision.HIGHEST, preferred_element_type=F32).astype(BF16)


def _fnet_prep(wbd):
    spec = pl.BlockSpec((1, D_FNET, D_FNET), lambda l: (l, 0, 0))
    return pl.pallas_call(
        _fprep_kernel,
        grid=(DEPTH,),
        in_specs=[spec],
        out_specs=[spec, spec],
        out_shape=[jax.ShapeDtypeStruct((DEPTH, D_FNET, D_FNET), BF16)] * 2,
        compiler_params=_cparams(("arbitrary",)),
        name="fnet_prep",
    )(wbd)


TABLE_ROWS = 64


def _base_table_kernel(cj_ref, sj_ref, cm_ref, sm_ref, *, L):
    j = lax.broadcasted_iota(jnp.int32, (TABLE_ROWS, L // 2), 0)
    n = lax.broadcasted_iota(jnp.int32, (TABLE_ROWS, L // 2), 1)
    w = 2.0 * math.pi / L
    fine = ((j * n) & (L - 1)).astype(F32) * w
    coarse = ((j * TABLE_ROWS * n) & (L - 1)).astype(F32) * w
    scale = 1.0 / math.sqrt(L)
    cj_ref[...] = jnp.cos(fine)
    sj_ref[...] = jnp.sin(fine)
    cm_ref[...] = jnp.cos(coarse) * scale
    sm_ref[...] = jnp.sin(coarse) * scale


def _table_kernel(cj_ref, sj_ref, cm_ref, sm_ref, c_ref, s_ref):
    m = pl.program_id(0)
    c0 = cm_ref[pl.ds(m, 1), :]
    s0 = sm_ref[pl.ds(m, 1), :]
    cj = cj_ref[...]
    sj = sj_ref[...]
    c_ref[...] = (cj * c0 - sj * s0).astype(BF16)
    s_ref[...] = (-(sj * c0 + cj * s0)).astype(BF16)


def _dft_tables(L):
    half = L // 2
    small = pl.BlockSpec((TABLE_ROWS, half), lambda *_: (0, 0))
    base = pl.pallas_call(
        functools.partial(_base_table_kernel, L=L),
        out_specs=[small] * 4,
        out_shape=[jax.ShapeDtypeStruct((TABLE_ROWS, half), F32)] * 4,
        compiler_params=pltpu.CompilerParams(vmem_limit_bytes=VMEM_LIMIT),
        name=f"dft_base_tables_{L}",
    )()
    spec = pl.BlockSpec((TABLE_ROWS, half), lambda i: (i, 0))
    return pl.pallas_call(
        _table_kernel,
        grid=(L // TABLE_ROWS,),
        in_specs=[small] * 4,
        out_specs=[spec, spec],
        out_shape=[jax.ShapeDtypeStruct((L, half), BF16)] * 2,
        compiler_params=_cparams(("arbitrary",)),
        name=f"dft_tables_{L}",
    )(*base)


def _dft_kernel(c_ref, s_ref, va_ref, vb_ref, o_ref, vaf, vbf, *, L):
    tk, tc = o_ref.shape
    half = L // 2
    fb = min(256, half)

    @pl.when(pl.program_id(1) == 0)
    def _():
        r = lax.broadcasted_iota(jnp.int32, (fb, fb), 0)
        c = lax.broadcasted_iota(jnp.int32, (fb, fb), 1)
        flip = jnp.where(r + c == fb, 1.0, 0.0).astype(BF16)
        first_row = lax.broadcasted_iota(jnp.int32, (fb, tc), 0) == 0
        for m in range(half // fb):
            pm = L // fb - m - 1
            for src, dst, sign in ((va_ref, vaf, 1.0), (vb_ref, vbf, -1.0)):
                partner = jnp.dot(flip, src[pm * fb:(pm + 1) * fb, :], preferred_element_type=F32)
                if m >= 1:
                    partner = partner + jnp.where(
                        first_row, src[(pm + 1) * fb:(pm + 1) * fb + 1, :].astype(F32), 0.0)
                dst[m * fb:(m + 1) * fb, :] = (src[m * fb:(m + 1) * fb, :].astype(F32) + sign * partner).astype(BF16)

    k = lax.broadcasted_iota(jnp.int32, (tk, 1), 0) + pl.program_id(1) * tk
    alt = (1 - 2 * (k & 1)).astype(F32) * (1.0 / math.sqrt(L))
    o_ref[...] = (jnp.dot(c_ref[...], vaf[...], preferred_element_type=F32)
                  + jnp.dot(s_ref[...], vbf[...], preferred_element_type=F32)
                  + alt * va_ref[half:half + 1, :].astype(F32))


def _dft_apply(ct, st, va, vb):
    L, ncols = va.shape
    half = L // 2
    tk, tc = 256, 1024
    return pl.pallas_call(
        functools.partial(_dft_kernel, L=L),
        grid=(ncols // tc, L // tk),
        in_specs=[
            pl.BlockSpec((tk, half), lambda j, i: (i, 0)),
            pl.BlockSpec((tk, half), lambda j, i: (i, 0)),
            pl.BlockSpec((L, tc), lambda j, i: (0, j)),
            pl.BlockSpec((L, tc), lambda j, i: (0, j)),
        ],
        out_specs=pl.BlockSpec((tk, tc), lambda j, i: (i, j)),
        out_shape=jax.ShapeDtypeStruct((L, ncols), F32),
        scratch_shapes=[pltpu.VMEM((half, tc), BF16), pltpu.VMEM((half, tc), BF16)],
        compiler_params=_cparams(("arbitrary", "arbitrary")),
        name=f"dft_apply_{L}",
    )(ct, st, va, vb)


def _rms_mod(x, g, scale, shift):
    ms = jnp.mean(x * x, axis=-1, keepdims=True)
    return (x * lax.rsqrt(ms + EPS)) * g * (1.0 + scale) + shift


def _pos_table_kernel(s_ref, c_ref):
    n, nf = s_ref.shape
    j = lax.broadcasted_iota(jnp.int32, (n, nf), 0).astype(F32)
    k = lax.broadcasted_iota(jnp.int32, (n, nf), 1).astype(F32)
    ang = j * jnp.exp(k * (-math.log(10000.0) / nf))
    s_ref[...] = jnp.sin(ang)
    c_ref[...] = jnp.cos(ang)


def _pos_tables(n):
    nf = D_MODEL // 4
    return pl.pallas_call(
        _pos_table_kernel,
        out_shape=[jax.ShapeDtypeStruct((n, nf), F32)] * 2,
        name="pos_tables",
    )()


def _pos_tile(ps_ref, pc_ref, t, tm):
    nrow = tm // GRID_W
    nf = D_MODEL // 4

    def rows(tab):
        return jnp.concatenate(
            [jnp.broadcast_to(tab[pl.ds(t * nrow + k, 1), :], (GRID_W, nf)) for k in range(nrow)], axis=0)

    def cols(tab):
        return jnp.concatenate([tab[0:GRID_W, :]] * nrow, axis=0)

    return jnp.concatenate([rows(ps_ref), rows(pc_ref), cols(ps_ref), cols(pc_ref)], axis=-1)


def _in_kernel(*refs, add_pos, row_base, row_step):
    if add_pos:
        x_ref, ps_ref, pc_ref, mod_ref, g_ref, w_ref, a_ref, b_ref, zr_ref, zuv_ref, va_ref, vb_ref = refs
    else:
        x_ref, mod_ref, g_ref, w_ref, a_ref, b_ref, zr_ref, zuv_ref, va_ref, vb_ref = refs
    x = x_ref[0]
    if add_pos:
        x = x + _pos_tile(ps_ref, pc_ref, pl.program_id(1), x.shape[0])
    row = row_base + pl.program_id(0) * row_step
    m = mod_ref[pl.ds(row, 1), :]
    h = _rms_mod(x, g_ref[...], m[:, D_MODEL:2 * D_MODEL], m[:, 0:D_MODEL])
    z = jnp.dot(h.astype(BF16), w_ref[...], preferred_element_type=F32)
    zr_ref[0] = z[:, 0:2 * D_RNN]
    zuv_ref[0] = z[:, 2 * D_RNN:2 * D_RNN + 2 * D_TMLP]
    zf = z[:, 2 * D_RNN + 2 * D_TMLP:D_IN].astype(BF16)
    va_ref[...] = jnp.dot(zf, a_ref[...], preferred_element_type=F32).astype(BF16)
    vb_ref[...] = jnp.dot(zf, b_ref[...], preferred_element_type=F32).astype(BF16)


def _in_proj(x, pos, mods, g, w_in_b, fa, fb, *, layer, row_base, row_step):
    nseq, L, _ = x.shape
    tm = min(L, PROJ_TILE)
    add_pos = pos is not None
    in_specs = [pl.BlockSpec((1, tm, D_MODEL), lambda b, t: (b, t, 0))]
    args = [x]
    if add_pos:
        in_specs += [pl.BlockSpec(p.shape, lambda b, t: (0, 0)) for p in pos]
        args += list(pos)
    in_specs += [
        _layer_spec((SUBLANES, N_MOD * D_MODEL), layer),
        _layer_spec((1, D_MODEL), layer),
        _layer_spec((D_MODEL, D_IN), layer),
        _layer_spec((D_FNET, D_FNET), layer),
        _layer_spec((D_FNET, D_FNET), layer),
    ]
    args += [mods, g, w_in_b, fa, fb]
    return pl.pallas_call(
        functools.partial(_in_kernel, add_pos=add_pos, row_base=row_base, row_step=row_step),
        grid=(nseq, L // tm),
        in_specs=in_specs,
        out_specs=[
            pl.BlockSpec((1, tm, 2 * D_RNN), lambda b, t: (b, t, 0)),
            pl.BlockSpec((1, tm, 2 * D_TMLP), lambda b, t: (b, t, 0)),
            pl.BlockSpec((tm, D_FNET), lambda b, t: (t, b)),
            pl.BlockSpec((tm, D_FNET), lambda b, t: (t, b)),
        ],
        out_shape=[
            jax.ShapeDtypeStruct((nseq, L, 2 * D_RNN), F32),
            jax.ShapeDtypeStruct((nseq, L, 2 * D_TMLP), F32),
            jax.ShapeDtypeStruct((L, nseq * D_FNET), BF16),
            jax.ShapeDtypeStruct((L, nseq * D_FNET), BF16),
        ],
        compiler_params=_cparams(("arbitrary", "arbitrary")),
        name=f"in_proj_{L}",
    )(*args)


def _gelu_tanh(x):
    return 0.5 * x * (1.0 + jnp.tanh(math.sqrt(2.0 / math.pi) * (x + 0.044715 * (x * x * x))))


def _rows_to_tile(rows):
    sub = lax.broadcasted_iota(jnp.int32, (SUBLANES, LANES), 0)
    out = jnp.zeros((SUBLANES, LANES), F32)
    for s, r in enumerate(rows):
        out = jnp.where(sub == s, jnp.broadcast_to(r, (SUBLANES, LANES)), out)
    return out


def _lru_kernel(xr_ref, gr_ref, cw_ref, cb_ref, wa_ref, wx_ref, ba_ref, bx_ref, lam_ref, h0_ref,
                y_ref, st_ref, xnat, pext, af, bf, ab, bb, hfo, pfo, hbo, pbo, hnat, *, L, batched):
    S = L if batched else L // SUBLANES
    pitch = S + SUBLANES
    n = S * SUBLANES
    chunk = 256

    def seg(ref, s):
        return ref.at[s] if batched else ref.at[0, s * S:(s + 1) * S]

    for s in range(SUBLANES):
        xnat[s * pitch:s * pitch + S, :] = seg(xr_ref, s)[...]

    def perm_in(j, c):
        dst = pl.multiple_of((j + 2) * SUBLANES, SUBLANES)
        pext[pl.ds(dst, SUBLANES), :] = xnat[pl.ds(j, SUBLANES, stride=pitch), :]
        return c

    lax.fori_loop(0, S, perm_in, 0, unroll=8)

    sub = lax.broadcasted_iota(jnp.int32, (SUBLANES, LANES), 0)

    def from_prev_segment(v):
        return jnp.where(sub == 0, 0.0, pltpu.roll(v, 1, axis=0))

    def from_next_segment(v):
        return jnp.where(sub == SUBLANES - 1, 0.0, pltpu.roll(v, SUBLANES - 1, axis=0))

    if batched:
        pext[0:16, :] = jnp.zeros((16, LANES), F32)
        pext[(S + 2) * 8:(S + 3) * 8, :] = jnp.zeros((8, LANES), F32)
    else:
        pext[0:8, :] = from_prev_segment(pext[S * 8:(S + 1) * 8, :])
        pext[8:16, :] = from_prev_segment(pext[(S + 1) * 8:(S + 2) * 8, :])
        pext[(S + 2) * 8:(S + 3) * 8, :] = from_next_segment(pext[16:24, :])

    lam = lam_ref[...]
    nl = -lam
    sp = jnp.maximum(nl, 0.0) + jnp.log1p(jnp.exp(-jnp.abs(nl)))
    c_la = (-0.5 * LRU_C) * sp
    ba_h = 0.5 * ba_ref[...]
    bx_h = 0.5 * bx_ref[...]
    a_refs = (af, ab)
    b_refs = (bf, bb)

    def gates(i, c):
        base = pl.multiple_of(i * chunk, chunk)
        xc = (cw_ref[0:1, :] * pext[pl.ds(base, chunk), :]
              + cw_ref[1:2, :] * pext[pl.ds(base + 8, chunk), :]
              + cw_ref[2:3, :] * pext[pl.ds(base + 16, chunk), :]
              + cw_ref[3:4, :] * pext[pl.ds(base + 24, chunk), :]
              + cb_ref[...])
        xcb = xc.astype(BF16)
        xh = 0.5 * xc
        for d in range(2):
            tr = jnp.tanh(jnp.dot(xcb, wa_ref[d, 0], preferred_element_type=F32) + ba_h[d:d + 1, :])
            ti = jnp.tanh(jnp.dot(xcb, wx_ref[d, 0], preferred_element_type=F32) + bx_h[d:d + 1, :])
            log_a = c_la[d:d + 1, :] * (1.0 + tr)
            a = jnp.exp(log_a)
            v = jnp.tanh(log_a) * (-1.0 - a * a)
            coef = jnp.where(v > 0.0, v * lax.rsqrt(v), 0.0)
            a_refs[d][pl.ds(base, chunk), :] = a
            b_refs[d][pl.ds(base, chunk), :] = coef * ((1.0 + ti) * xh)
        return c

    lax.fori_loop(0, n // chunk, gates, 0, unroll=min(4, n // chunk))

    nq = LRU_SUB
    sq = S // nq

    def scan(i, carry):
        out = []
        for q in range(nq):
            hf, pf, hb, pb = carry[4 * q:4 * q + 4]
            jf = pl.multiple_of((q * sq + i) * SUBLANES, SUBLANES)
            jb = pl.multiple_of(((q + 1) * sq - 1 - i) * SUBLANES, SUBLANES)
            a1 = af[pl.ds(jf, SUBLANES), :]
            hf = a1 * hf + bf[pl.ds(jf, SUBLANES), :]
            pf = a1 * pf
            hfo[pl.ds(jf, SUBLANES), :] = hf
            pfo[pl.ds(jf, SUBLANES), :] = pf
            a2 = ab[pl.ds(jb, SUBLANES), :]
            hb = a2 * hb + bb[pl.ds(jb, SUBLANES), :]
            pb = a2 * pb
            hbo[pl.ds(jb, SUBLANES), :] = hb
            pbo[pl.ds(jb, SUBLANES), :] = pb
            out += [hf, pf, hb, pb]
        return tuple(out)

    zero = jnp.zeros((SUBLANES, LANES), F32)
    one = jnp.ones((SUBLANES, LANES), F32)
    ends = lax.fori_loop(0, sq, scan, (zero, one, zero, one) * nq, unroll=min(sq, 4))
    end_f = [(ends[4 * q], ends[4 * q + 1]) for q in range(nq)]
    end_b = [(ends[4 * q + 2], ends[4 * q + 3]) for q in range(nq)]

    hf, pf = end_f[0]
    for q in range(1, nq):
        hf, pf = end_f[q][1] * hf + end_f[q][0], end_f[q][1] * pf
    hb, pb = end_b[nq - 1]
    for q in range(nq - 2, -1, -1):
        hb, pb = end_b[q][1] * hb + end_b[q][0], end_b[q][1] * pb

    if batched:
        start_f = h0_ref[0]
        start_b = h0_ref[1]
        st_ref[0] = pf * start_f + hf
        st_ref[1] = pb * start_b + hb
    else:
        rows_f = [h0_ref[0, 0:1, :]]
        for s in range(1, SUBLANES):
            rows_f.append(pf[s - 1:s, :] * rows_f[-1] + hf[s - 1:s, :])
        st_ref[0, 0:1, :] = pf[7:8, :] * rows_f[7] + hf[7:8, :]
        rows_b = [None] * SUBLANES
        rows_b[7] = h0_ref[0, 1:2, :]
        for s in range(SUBLANES - 2, -1, -1):
            rows_b[s] = pb[s + 1:s + 2, :] * rows_b[s + 1] + hb[s + 1:s + 2, :]
        st_ref[0, 1:2, :] = pb[0:1, :] * rows_b[0] + hb[0:1, :]
        start_f = _rows_to_tile(rows_f)
        start_b = _rows_to_tile(rows_b)

    init_f = [start_f]
    for q in range(nq - 1):
        init_f.append(end_f[q][1] * init_f[q] + end_f[q][0])
    init_b = [None] * nq
    init_b[nq - 1] = start_b
    for q in range(nq - 1, 0, -1):
        init_b[q - 1] = end_b[q][1] * init_b[q] + end_b[q][0]

    for q in range(nq):
        def perm_out(j, c, q=q):
            src = pl.multiple_of(j * SUBLANES, SUBLANES)
            v = (hfo[pl.ds(src, SUBLANES), :] + pfo[pl.ds(src, SUBLANES), :] * init_f[q]
                 + hbo[pl.ds(src, SUBLANES), :] + pbo[pl.ds(src, SUBLANES), :] * init_b[q])
            hnat[pl.ds(j, SUBLANES, stride=pitch), :] = v
            return c

        lax.fori_loop(q * sq, (q + 1) * sq, perm_out, 0, unroll=min(sq, 8))

    for s in range(SUBLANES):
        seg(y_ref, s)[...] = hnat[s * pitch:s * pitch + S, :] * _gelu_tanh(seg(gr_ref, s)[...])


def _lru_mixer(zr, conv_w, conv_b, wa_bd, wx_bd, ba, bx, lam, h0, *, layer):
    nseq, L, _ = zr.shape
    ncb = D_RNN // LRU_CB
    batched = L <= LRU_BATCH_MAX_LEN
    nb = SUBLANES if batched else 1
    S = L if batched else L // SUBLANES
    pitch = S + SUBLANES
    n = S * SUBLANES
    vec2 = pl.BlockSpec((None, 2, LRU_CB), lambda b, c: (layer, 0, c))
    wspec = pl.BlockSpec((None, 2, 1, LRU_CB, LRU_CB), lambda b, c: (layer, 0, c, 0, 0))
    if batched:
        state = pl.BlockSpec((2, nb, LRU_CB), lambda b, c: (0, b, c))
        state_shape = (2, nseq, D_RNN)
    else:
        state = pl.BlockSpec((1, 2, LRU_CB), lambda b, c: (b, 0, c))
        state_shape = (nseq, 2, D_RNN)
    return pl.pallas_call(
        functools.partial(_lru_kernel, L=L, batched=batched),
        grid=(nseq // nb, ncb),
        in_specs=[
            pl.BlockSpec((nb, L, LRU_CB), lambda b, c: (b, 0, c)),
            pl.BlockSpec((nb, L, LRU_CB), lambda b, c: (b, 0, c + ncb)),
            pl.BlockSpec((None, CONV_W, LRU_CB), lambda b, c: (layer, 0, c)),
            pl.BlockSpec((None, 1, LRU_CB), lambda b, c: (layer, 0, c)),
            wspec, wspec, vec2, vec2, vec2,
            state,
        ],
        out_specs=[pl.BlockSpec((nb, L, LRU_CB), lambda b, c: (b, 0, c)), state],
        out_shape=[
            jax.ShapeDtypeStruct((nseq, L, D_RNN), F32),
            jax.ShapeDtypeStruct(state_shape, F32),
        ],
        scratch_shapes=[
            pltpu.VMEM((SUBLANES * pitch, LANES), F32),
            pltpu.VMEM(((S + 3) * SUBLANES, LANES), F32),
            *([pltpu.VMEM((n, LANES), F32)] * 8),
            pltpu.VMEM((SUBLANES * pitch, LANES), F32),
        ],
        compiler_params=_cparams(("arbitrary", "arbitrary")),
        name=f"lru_mixer_{L}",
    )(zr, zr, conv_w, conv_b, wa_bd, wx_bd, ba, bx, lam, h0)


def _out_kernel(*refs, add_pos, row_base, row_step):
    if add_pos:
        (x_ref, ps_ref, pc_ref, yr_ref, zuv_ref, yf_ref, mod_ref, ws_ref, bt_ref, wo_ref, g_ref, wr_ref, br_ref,
         cin_ref, x1_ref, h2_ref, ri_ref, cnt_ref, carry) = refs
    else:
        (x_ref, yr_ref, zuv_ref, yf_ref, mod_ref, ws_ref, bt_ref, wo_ref, g_ref, wr_ref, br_ref, cin_ref,
         x1_ref, h2_ref, ri_ref, cnt_ref, carry) = refs
    tm = x_ref.shape[1]
    x = x_ref[0]
    if add_pos:
        x = x + _pos_tile(ps_ref, pc_ref, pl.program_id(1), tm)
    row = row_base + pl.program_id(0) * row_step
    m = mod_ref[pl.ds(row, 1), :]
    g1 = m[:, 2 * D_MODEL:3 * D_MODEL]
    sh2 = m[:, 3 * D_MODEL:4 * D_MODEL]
    sc2 = m[:, 4 * D_MODEL:5 * D_MODEL]

    head = lax.broadcasted_iota(jnp.int32, (CHUNK, D_TMLP), 1) >> 6
    yt_parts = []
    for ci in range(tm // CHUNK):
        u = zuv_ref[0, ci * CHUNK:(ci + 1) * CHUNK, 0:D_TMLP]
        v = zuv_ref[0, ci * CHUNK:(ci + 1) * CHUNK, D_TMLP:2 * D_TMLP].astype(BF16)
        s = jnp.zeros((CHUNK, D_TMLP), F32)
        for h in range(TMLP_HEADS):
            sh = jnp.dot(ws_ref[h], v, preferred_element_type=F32) + bt_ref[:, h:h + 1]
            s = jnp.where(head == h, sh, s)
        yt_parts.append(u * s)
    yt = jnp.concatenate(yt_parts, axis=0) if len(yt_parts) > 1 else yt_parts[0]

    y = (jnp.dot(yr_ref[0].astype(BF16), wo_ref[0:D_RNN, :], preferred_element_type=F32)
         + jnp.dot(yt.astype(BF16), wo_ref[D_RNN:D_RNN + D_TMLP, :], preferred_element_type=F32)
         + jnp.dot(yf_ref[...].astype(BF16), wo_ref[D_RNN + D_TMLP:D_MODEL, :], preferred_element_type=F32))
    x1 = x + g1 * y
    x1_ref[0] = x1
    h2 = _rms_mod(x1, g_ref[...], sc2, sh2)
    _store_token_major(h2_ref, h2)

    wr = wr_ref[...]
    w_hi = wr.astype(BF16)
    w_lo = (wr - w_hi.astype(F32)).astype(BF16)
    h_hi = h2.astype(BF16)
    h_lo = (h2 - h_hi.astype(F32)).astype(BF16)
    p_hi = jnp.dot(h_hi, jnp.concatenate([w_hi, w_lo], axis=-1), preferred_element_type=F32)
    p_lo = jnp.dot(h_lo, w_hi, preferred_element_type=F32)
    logits = p_hi[:, 0:ROUTER_LANES] + p_hi[:, ROUTER_LANES:2 * ROUTER_LANES] + p_lo + br_ref[...]
    lane = lax.broadcasted_iota(jnp.int32, (tm, ROUTER_LANES), 1)
    lane_f = lane.astype(F32)
    is_g = lane < N_EGROUPS
    gl = jnp.where(is_g, logits, NEG_BIG)
    gmax = jnp.max(gl, axis=-1, keepdims=True)
    gsel = jnp.min(jnp.where(gl == gmax, lane_f, 1e4), axis=-1, keepdims=True)
    pg = 1.0 / jnp.sum(jnp.where(is_g, jnp.exp(logits - gmax), 0.0), axis=-1, keepdims=True)
    grp_f = ((lane - N_EGROUPS) >> 3).astype(F32)
    emask = (lane >= N_EGROUPS) & (lane < N_EGROUPS + N_EXPERTS) & (grp_f == gsel)
    el = jnp.where(emask, logits, NEG_BIG)
    v1 = jnp.max(el, axis=-1, keepdims=True)
    i1 = jnp.min(jnp.where(el == v1, lane_f, 1e4), axis=-1, keepdims=True)
    el2 = jnp.where(lane_f == i1, NEG_BIG, el)
    v2 = jnp.max(el2, axis=-1, keepdims=True)
    i2 = jnp.min(jnp.where(el2 == v2, lane_f, 1e4), axis=-1, keepdims=True)
    e2x = jnp.exp(v2 - v1)
    fw1 = 1.0 / (1.0 + e2x)
    fw2 = e2x * fw1
    @pl.when((pl.program_id(0) == 0) & (pl.program_id(1) == 0))
    def _():
        carry[...] = cin_ref[...]

    e1 = i1 - N_EGROUPS
    e2 = i2 - N_EGROUPS
    m1 = lane_f == e1
    m2 = lane_f == e2
    oh = jnp.where(m1 | m2, 1.0, 0.0)
    r_i = lax.broadcasted_iota(jnp.int32, (tm, tm), 0)
    c_i = lax.broadcasted_iota(jnp.int32, (tm, tm), 1)
    tri = jnp.where(c_i < r_i, 1.0, 0.0).astype(BF16)
    before = jnp.dot(tri, oh.astype(BF16), preferred_element_type=F32) + carry[0:1, :]
    rank1 = jnp.sum(jnp.where(m1, before, 0.0), axis=-1, keepdims=True)
    rank2 = jnp.sum(jnp.where(m2, before, 0.0), axis=-1, keepdims=True)
    total = carry[0:1, :] + jnp.sum(oh, axis=0, keepdims=True)
    carry[0:1, :] = total
    cnt_ref[...] = jnp.broadcast_to(total, cnt_ref.shape)

    vals = (e1, e2, pg * fw1, pg * fw2, rank1, rank2)
    ri = jnp.zeros((tm, ROUTER_LANES), F32)
    for k, v in enumerate(vals):
        ri = jnp.where(lane == k, v, ri)
    ri_ref[0] = ri


def _layer_spec(shape, layer):
    zeros = (0,) * len(shape)
    return pl.BlockSpec((None, *shape), lambda *_: (layer, *zeros))


def _out_proj(x, pos, yr, zuv, yf, mods, ws_b, bt, wo_b, g, wr, br, cin, *, layer, row_base, row_step):
    nseq, L, _ = x.shape
    tm = min(L, PROJ_TILE)
    add_pos = pos is not None
    in_specs = [pl.BlockSpec((1, tm, D_MODEL), lambda b, t: (b, t, 0))]
    args = [x]
    if add_pos:
        in_specs += [pl.BlockSpec(p.shape, lambda b, t: (0, 0)) for p in pos]
        args += list(pos)
    in_specs += [
        pl.BlockSpec((1, tm, D_RNN), lambda b, t: (b, t, 0)),
        pl.BlockSpec((1, tm, 2 * D_TMLP), lambda b, t: (b, t, 0)),
        pl.BlockSpec((tm, D_FNET), lambda b, t: (t, b)),
        _layer_spec((SUBLANES, N_MOD * D_MODEL), layer),
        _layer_spec((TMLP_HEADS, CHUNK, CHUNK), layer),
        _layer_spec((CHUNK, TMLP_HEADS), layer),
        _layer_spec((D_MODEL, D_MODEL), layer),
        _layer_spec((1, D_MODEL), layer),
        _layer_spec((D_MODEL, ROUTER_LANES), layer),
        _layer_spec((1, ROUTER_LANES), layer),
        pl.BlockSpec((SUBLANES, ROUTER_LANES), lambda b, t: (0, 0)),
    ]
    args += [yr, zuv, yf, mods, ws_b, bt, wo_b, g, wr, br, cin]
    tok = pl.BlockSpec((1, tm, D_MODEL), lambda b, t: (b, t, 0))
    return pl.pallas_call(
        functools.partial(_out_kernel, add_pos=add_pos, row_base=row_base, row_step=row_step),
        grid=(nseq, L // tm),
        in_specs=in_specs,
        out_specs=[tok, pl.BlockSpec((tm * LANE_TILES, LANES), lambda b, t: (b * (L // tm) + t, 0)),
                   pl.BlockSpec((1, tm, ROUTER_LANES), lambda b, t: (b, t, 0)),
                   pl.BlockSpec((SUBLANES, ROUTER_LANES), lambda b, t: (0, 0))],
        out_shape=[
            jax.ShapeDtypeStruct((nseq, L, D_MODEL), F32),
            jax.ShapeDtypeStruct((nseq * L * LANE_TILES, LANES), F32),
            jax.ShapeDtypeStruct((nseq, L, ROUTER_LANES), F32),
            jax.ShapeDtypeStruct((SUBLANES, ROUTER_LANES), F32),
        ],
        scratch_shapes=[pltpu.VMEM((SUBLANES, ROUTER_LANES), F32)],
        compiler_params=_cparams(("arbitrary", "arbitrary")),
        name=f"out_proj_{L}",
    )(*args)


def _store_token_major(ref, x):
    tm = x.shape[0]
    for j in range(LANE_TILES):
        ref[pl.ds(j, tm, stride=LANE_TILES), :] = x[:, j * LANES:(j + 1) * LANES]


def _load_token_major(ref):
    tm = ref.shape[0] // LANE_TILES
    return jnp.concatenate([ref[pl.ds(j, tm, stride=LANE_TILES), :] for j in range(LANE_TILES)], axis=-1)


def _token_copy(src_ref, src_tok, dst_ref, dst_tok, sem):
    return pltpu.make_async_copy(src_ref.at[pl.ds(pl.multiple_of(src_tok, LANE_TILES), LANE_TILES)],
                                 dst_ref.at[pl.ds(pl.multiple_of(dst_tok, LANE_TILES), LANE_TILES)], sem)


def _scatter_rows(pos_ref, h_ref, xs_ref, sem):
    rows = h_ref.shape[0]

    for r in range(0, rows, LANE_TILES):
        _token_copy(h_ref, r, xs_ref, pos_ref[0, 0, r // LANE_TILES], sem).start(priority=0)
        _token_copy(h_ref, r, xs_ref, pos_ref[0, 1, r // LANE_TILES], sem).start(priority=1)
    for _ in range(2):
        pltpu.make_async_copy(h_ref, xs_ref.at[pl.ds(0, rows)], sem).wait()


def _dispatch_kernel(pos_ref, hp_ref, hs_ref, xs_ref, sem, *, n_first):
    i = pl.program_id(0)

    @pl.when(i < n_first)
    def _():
        _scatter_rows(pos_ref, hp_ref, xs_ref, sem)

    @pl.when(i >= n_first)
    def _():
        _scatter_rows(pos_ref, hs_ref, xs_ref, sem)


def _dispatch(pos, h_first, h_second):
    tm = ROW_TILE
    blk = tm * LANE_TILES
    n_first = h_first.shape[0] // blk
    n_second = h_second.shape[0] // blk
    nrows = 2 * (h_first.shape[0] + h_second.shape[0])
    return pl.pallas_call(
        functools.partial(_dispatch_kernel, n_first=n_first),
        grid=(n_first + n_second,),
        in_specs=[
            pl.BlockSpec((1, 2, tm), lambda i: (i, 0, 0), memory_space=pltpu.SMEM),
            pl.BlockSpec((blk, LANES), lambda i: (jnp.minimum(i, n_first - 1), 0)),
            pl.BlockSpec((blk, LANES), lambda i: (jnp.maximum(i - n_first, 0), 0)),
        ],
        out_specs=pl.BlockSpec(memory_space=pl.ANY),
        out_shape=jax.ShapeDtypeStruct((nrows, LANES), F32),
        scratch_shapes=[pltpu.SemaphoreType.DMA(())],
        compiler_params=_cparams(("arbitrary",)),
        name="moe_dispatch",
    )(pos, h_first, h_second)


WEIGHT_LEAD = (3, 2, 1)
WEIGHT_SLOTS = 4
LEAD_ITEMS = max(WEIGHT_LEAD)


def _expert_kernel(wt_ref, we_ref, lo_ref, hi_ref, first_ref, ord_ref, xs_ref, w1_ref, w3_ref, w2_ref, ys_ref,
                   w1b, w3b, w2b):
    s = pl.program_id(0)
    last = pl.num_programs(0) - 1
    tm = EXPERT_TILE

    def stage(lead, src, dst):
        cur = jnp.minimum(s + lead, last)
        prv = jnp.minimum(s + lead - 1, last)

        @pl.when((s == 0) | (we_ref[cur] != we_ref[prv]))
        def _():
            dst[ord_ref[cur] & (WEIGHT_SLOTS - 1)] = src[0].astype(BF16)

    for lead, src, dst in zip(WEIGHT_LEAD, (w1_ref, w3_ref, w2_ref), (w1b, w3b, w2b)):
        stage(lead, src, dst)

    lo = lo_ref[s]
    hi = hi_ref[s]
    full = (lo == 0) & (hi == tm)

    @pl.when((first_ref[s] == 1) & jnp.logical_not(full))
    def _():
        ys_ref[...] = jnp.zeros_like(ys_ref)

    @pl.when(hi > lo)
    def _():
        slot = ord_ref[s] & (WEIGHT_SLOTS - 1)
        x = _load_token_major(xs_ref).astype(BF16)
        a = jnp.dot(x, w1b[slot], preferred_element_type=F32)
        b = jnp.dot(x, w3b[slot], preferred_element_type=F32)
        hid = (a * jax.nn.sigmoid(a)) * b
        res = jnp.dot(hid.astype(BF16), w2b[slot], preferred_element_type=F32)

        @pl.when(full)
        def _():
            _store_token_major(ys_ref, res)

        @pl.when(jnp.logical_not(full))
        def _():
            rows = lax.broadcasted_iota(jnp.int32, res.shape, 0)
            _store_token_major(ys_ref, jnp.where((rows >= lo) & (rows < hi), res, _load_token_major(ys_ref)))


def _experts(items, xs, w1, w3, w2):
    nw = items[0].shape[0]

    def row(s, wt, we, lo, hi, fi, od):
        return (wt[s], 0)

    def weight(lead):
        return lambda s, wt, we, lo, hi, fi, od: (we[jnp.minimum(s + lead, nw - 1)], 0, 0)

    return pl.pallas_call(
        _expert_kernel,
        grid_spec=pltpu.PrefetchScalarGridSpec(
            num_scalar_prefetch=6,
            grid=(nw,),
            in_specs=[
                pl.BlockSpec((EXPERT_TILE * LANE_TILES, LANES), row),
                pl.BlockSpec((1, D_MODEL, D_EXPERT), weight(WEIGHT_LEAD[0])),
                pl.BlockSpec((1, D_MODEL, D_EXPERT), weight(WEIGHT_LEAD[1])),
                pl.BlockSpec((1, D_EXPERT, D_MODEL), weight(WEIGHT_LEAD[2])),
            ],
            out_specs=pl.BlockSpec((EXPERT_TILE * LANE_TILES, LANES), row),
            scratch_shapes=[
                pltpu.VMEM((WEIGHT_SLOTS, D_MODEL, D_EXPERT), BF16),
                pltpu.VMEM((WEIGHT_SLOTS, D_MODEL, D_EXPERT), BF16),
                pltpu.VMEM((WEIGHT_SLOTS, D_EXPERT, D_MODEL), BF16),
            ],
        ),
        out_shape=jax.ShapeDtypeStruct(xs.shape, F32),
        compiler_params=_cparams(("arbitrary",)),
        name="moe_experts",
    )(*items, xs, w1, w3, w2)


def _combine_kernel(pos_ref, posn_ref, x1_ref, ri_ref, mod_ref, gf_ref, ys_ref, o_ref, ybuf, sem,
                    *, row_base, row_step, tiles_per_seq, final):
    tm = x1_ref.shape[0]
    i = pl.program_id(0)
    n = pl.num_programs(0)

    def gather(p_ref, slot):
        for r in range(tm):
            _token_copy(ys_ref, p_ref[0, 0, r], ybuf.at[slot, 0], r * LANE_TILES, sem.at[slot]).start(priority=0)
            _token_copy(ys_ref, p_ref[0, 1, r], ybuf.at[slot, 1], r * LANE_TILES, sem.at[slot]).start(priority=1)

    def drain(slot):
        for k in range(2):
            pltpu.make_async_copy(ys_ref.at[pl.ds(0, tm * LANE_TILES)], ybuf.at[slot, k], sem.at[slot]).wait()

    def tile(slot):
        if slot == 0:
            @pl.when(i == 0)
            def _():
                gather(pos_ref, 0)

        @pl.when(i + 1 < n)
        def _():
            gather(posn_ref, 1 - slot)

        drain(slot)
        row = row_base + (i // tiles_per_seq) * row_step
        g2 = mod_ref[pl.ds(row, 1), 5 * D_MODEL:6 * D_MODEL]
        ri = ri_ref[...]
        y = (ri[:, 2:3] * _load_token_major(ybuf.at[slot, 0])
             + ri[:, 3:4] * _load_token_major(ybuf.at[slot, 1]))
        x2 = x1_ref[...] + g2 * y
        if final:
            ms = jnp.mean(x2 * x2, axis=-1, keepdims=True)
            o_ref[...] = (x2 * lax.rsqrt(ms + EPS)) * gf_ref[...]
        else:
            o_ref[...] = x2

    for slot in range(2):
        pl.when((i & 1) == slot)(functools.partial(tile, slot))


def _combine(pos, x1, ri, mods, g_final, ys, *, layer, row_base, row_step, tiles_per_seq, final):
    ntok = x1.shape[0]
    tm = ROW_TILE
    tok = pl.BlockSpec((tm, D_MODEL), lambda i: (i, 0))
    return pl.pallas_call(
        functools.partial(_combine_kernel, row_base=row_base, row_step=row_step, tiles_per_seq=tiles_per_seq,
                          final=final),
        grid=(ntok // tm,),
        in_specs=[
            pl.BlockSpec((1, 2, tm), lambda i: (i, 0, 0), memory_space=pltpu.SMEM),
            pl.BlockSpec((1, 2, tm), lambda i: (jnp.minimum(i + 1, ntok // tm - 1), 0, 0), memory_space=pltpu.SMEM),
            tok,
            pl.BlockSpec((tm, ROUTER_LANES), lambda i: (i, 0)),
            _layer_spec((SUBLANES, N_MOD * D_MODEL), layer),
            pl.BlockSpec((1, D_MODEL), lambda i: (0, 0)),
            pl.BlockSpec(memory_space=pl.ANY),
        ],
        out_specs=tok,
        out_shape=jax.ShapeDtypeStruct((ntok, D_MODEL), F32),
        scratch_shapes=[
            pltpu.VMEM((2, 2, tm * LANE_TILES, LANES), F32),
            pltpu.SemaphoreType.DMA((2,)),
        ],
        compiler_params=_cparams(("arbitrary",)),
        name=f"moe_combine_{ntok}",
    )(pos, pos, x1, ri, mods, g_final, ys)


def _route(ri_first, ri_second, cnt):
    counts = cnt[0, 0:N_EXPERTS].astype(jnp.int32)
    offs = jnp.cumsum(counts) - counts
    info = jnp.concatenate([ri_first[:, 0:SUBLANES], ri_second[:, 0:SUBLANES]], axis=0).T
    vals = info.astype(jnp.int32)
    base = jnp.zeros_like(vals)
    for k in range(N_EXPERTS):
        base = jnp.where(vals == k, offs[k], base)
    slot = (base[0:2] + vals[4:6]) * LANE_TILES
    n_tiles = slot.shape[1] // ROW_TILE
    pos_tiles = jnp.swapaxes(slot.reshape(2, n_tiles, ROW_TILE), 0, 1)
    return pos_tiles, counts, offs


def _work_items(counts, offs, nw, layer):
    tm = EXPERT_TILE
    first_tile = offs // tm
    last_tile = (offs + counts - 1) // tm
    n_e = jnp.where(counts > 0, last_tile - first_tile + 1, 0)
    w_end = jnp.cumsum(n_e)
    w_start = w_end - n_e
    total = w_end[-1]
    w = jnp.arange(nw, dtype=jnp.int32)
    wc = jnp.minimum(w, total - 1)
    e_w = jnp.sum((wc[:, None] >= w_end[None, :]).astype(jnp.int32), axis=-1)
    sel = (e_w[:, None] == jnp.arange(N_EXPERTS, dtype=jnp.int32)).astype(jnp.int32)
    pick = lambda v: jnp.sum(sel * v[None, :], axis=-1)
    off_w = pick(offs)
    tile_w = pick(first_tile) + (wc - pick(w_start))
    lo = jnp.clip(off_w - tile_w * tm, 0, tm)
    hi = jnp.clip(off_w + pick(counts) - tile_w * tm, 0, tm)
    valid = w < total
    lo = jnp.where(valid, lo, 0)
    hi = jnp.where(valid, hi, 0)
    prev_tile = jnp.concatenate([jnp.full((1,), -1, tile_w.dtype), tile_w[:-1]])
    first = tile_w != prev_tile
    ordinal = pick(jnp.cumsum((counts > 0).astype(jnp.int32)) - 1)

    def lead(a, fill=None):
        head = jnp.broadcast_to(a[0] if fill is None else jnp.asarray(fill, a.dtype), (LEAD_ITEMS,))
        return jnp.concatenate([head, a]).astype(jnp.int32)

    return (lead(tile_w), lead(e_w + layer * N_EXPERTS), lead(lo, 0), lead(hi, 0), lead(first, 0), lead(ordinal))


def _block_diag(w, nblk):
    *lead, H, d, _ = w.shape
    w = w.reshape(*lead, H // nblk, nblk, d, d)
    eye = jnp.eye(nblk, dtype=w.dtype)
    out = jnp.einsum('...gij,gh->...gihj', w, eye)
    return out.reshape(*lead, H // nblk, nblk * d, nblk * d)


def kernel(x_prompt, x_sample, state_lru, c, c_ctx, w_ada, b_ada, g_mix, g_ffn, g_final, w_in, w_out, conv_w, conv_b, lru_wa, lru_ba, lru_wx, lru_bx, lru_lambda, tmlp_ws, tmlp_b, fnet_w, router_g, router_g_b, router_e, router_e_b, e_w1, e_w3, e_w2):
    bp, lp, _ = x_prompt.shape
    bs, ls, _ = x_sample.shape
    n_ctx = bp * lp
    n_lat = bs * ls
    n_tok = n_ctx + n_lat

    cond8 = jnp.concatenate([c_ctx[None, :], c, jnp.zeros((SUBLANES - 1 - bs, D_MODEL), F32)], axis=0)
    pos = _pos_tables(max(ls // GRID_W, GRID_W))
    w_in_b = w_in.astype(BF16)
    w_out_b = w_out.astype(BF16)
    heads_per_cb = LRU_CB // RNN_HEAD_DIM
    wa_bd = (0.5 * _block_diag(lru_wa, heads_per_cb)).astype(BF16)
    wx_bd = (0.5 * _block_diag(lru_wx, heads_per_cb)).astype(BF16)
    ws_b = tmlp_ws.astype(BF16)
    bt = jnp.swapaxes(tmlp_b, 1, 2)
    fnet_bd = _block_diag(fnet_w, D_FNET // FNET_GROUP_DIM)[:, 0]
    wr = jnp.concatenate([router_g, router_e,
                          jnp.zeros((DEPTH, D_MODEL, ROUTER_LANES - N_EGROUPS - N_EXPERTS), F32)], axis=-1)
    br = jnp.concatenate([router_g_b, router_e_b,
                          jnp.zeros((DEPTH, ROUTER_LANES - N_EGROUPS - N_EXPERTS), F32)], axis=-1)
    ew1 = e_w1.reshape(DEPTH * N_EXPERTS, D_MODEL, D_EXPERT)
    ew3 = e_w3.reshape(DEPTH * N_EXPERTS, D_MODEL, D_EXPERT)
    ew2 = e_w2.reshape(DEPTH * N_EXPERTS, D_EXPERT, D_MODEL)

    mods = _modulation(cond8, w_ada, b_ada)
    fa, fb = _fnet_prep(fnet_bd)
    ct_p, st_p = _dft_tables(lp)
    ct_s, st_s = _dft_tables(ls)

    nw = 2 * n_tok // EXPERT_TILE + N_EXPERTS
    h0_p = jnp.zeros((bp, 2, D_RNN), F32)
    g_mix3 = g_mix[:, None, :]
    g_ffn3 = g_ffn[:, None, :]
    conv_b3 = conv_b[:, None, :]
    br3 = br[:, None, :]

    xp, xs = x_prompt, x_sample
    states = []
    for l in range(DEPTH):
        paths = []
        cnt = jnp.zeros((SUBLANES, ROUTER_LANES), F32)
        for (x, ct, st, h0, row_base, row_step, is_lat) in (
                (xp, ct_p, st_p, h0_p, 0, 0, False),
                (xs, ct_s, st_s, state_lru[:, l], 1, 1, True)):
            pe = pos if (is_lat and l == 0) else None
            zr, zuv, va, vb = _in_proj(x, pe, mods, g_mix3, w_in_b, fa, fb,
                                       layer=l, row_base=row_base, row_step=row_step)
            swap = x.shape[1] <= LRU_BATCH_MAX_LEN
            h0 = jnp.swapaxes(h0, 0, 1) if swap else h0
            yr, st_new = _lru_mixer(zr, conv_w, conv_b3, wa_bd, wx_bd, lru_ba, lru_bx, lru_lambda, h0, layer=l)
            st_new = jnp.swapaxes(st_new, 0, 1) if swap else st_new
            yf = _dft_apply(ct, st, va, vb)
            x1, h2, ri, cnt = _out_proj(x, pe, yr, zuv, yf, mods, ws_b, bt, w_out_b, g_ffn3, wr, br3, cnt,
                                        layer=l, row_base=row_base, row_step=row_step)
            paths.append((x1, h2, ri, st_new))
        states.append(paths[0][3])

        pos_tiles, counts, offs = _route(paths[0][2].reshape(n_ctx, ROUTER_LANES),
                                         paths[1][2].reshape(n_lat, ROUTER_LANES), cnt)
        items = _work_items(counts, offs, nw, l)
        nct = n_ctx // ROW_TILE
        xsorted = _dispatch(pos_tiles, paths[0][1], paths[1][1])
        ysorted = _experts(items, xsorted, ew1, ew3, ew2)
        gfin = g_final[None, :]
        final = l == DEPTH - 1
        xp = _combine(pos_tiles[:nct], paths[0][0].reshape(n_ctx, D_MODEL),
                      paths[0][2].reshape(n_ctx, ROUTER_LANES), mods, gfin, ysorted, layer=l,
                      row_base=0, row_step=0, tiles_per_seq=lp // ROW_TILE, final=final).reshape(bp, lp, D_MODEL)
        xs = _combine(pos_tiles[nct:], paths[1][0].reshape(n_lat, D_MODEL),
                      paths[1][2].reshape(n_lat, ROUTER_LANES), mods, gfin, ysorted, layer=l,
                      row_base=1, row_step=1, tiles_per_seq=ls // ROW_TILE, final=final).reshape(bs, ls, D_MODEL)

    new_state = jnp.stack(states, axis=1).astype(x_prompt.dtype)
    return (xp, xs, new_state)
```

```python
import functools
import math

import jax
import jax.numpy as jnp
from jax import lax
from jax.experimental import pallas as pl
from jax.experimental.pallas import tpu as pltpu

F32 = jnp.float32
BF16 = jnp.bfloat16

D_MODEL = 1024
DEPTH = 2
GRID_W = 64
D_RNN = 512
RNN_HEAD_DIM = 64
CONV_W = 4
LRU_C = 8.0
D_TMLP = 256
TMLP_HEADS = 4
CHUNK = 128
D_FNET = 256
FNET_GROUP_DIM = 64
D_IN = 2 * D_RNN + 2 * D_TMLP + D_FNET
N_EGROUPS = 4
N_EPG = 8
N_EXPERTS = N_EGROUPS * N_EPG
D_EXPERT = 512
N_MOD = 6
EPS = 1e-6

LANES = 128
SUBLANES = 8
LANE_TILES = D_MODEL // LANES
assert LANE_TILES == SUBLANES
LRU_CB = LANES
LRU_SUB = 4
LRU_BATCH_MAX_LEN = 512
ROW_TILE = 256
PROJ_TILE = 512
EXPERT_TILE = 256
ROUTER_LANES = LANES
NEG_BIG = -1e30
VMEM_LIMIT = 56 * 1024 * 1024


def _cparams(sem):
    return pltpu.CompilerParams(dimension_semantics=sem, vmem_limit_bytes=VMEM_LIMIT)


def _mod_kernel(c_ref, w_ref, b_ref, o_ref):
    c = c_ref[...]
    s = c * jax.nn.sigmoid(c)
    o_ref[0] = jnp.dot(s.astype(BF16), w_ref[0].astype(BF16), preferred_element_type=F32) + b_ref[0]


def _modulation(cond8, w_ada, b_ada):
    tn = 1536
    return pl.pallas_call(
        _mod_kernel,
        grid=(DEPTH, N_MOD * D_MODEL // tn),
        in_specs=[
            pl.BlockSpec((SUBLANES, D_MODEL), lambda l, j: (0, 0)),
            pl.BlockSpec((1, D_MODEL, tn), lambda l, j: (l, 0, j)),
            pl.BlockSpec((1, 1, tn), lambda l, j: (l, 0, j)),
        ],
        out_specs=pl.BlockSpec((1, SUBLANES, tn), lambda l, j: (l, 0, j)),
        out_shape=jax.ShapeDtypeStruct((DEPTH, SUBLANES, N_MOD * D_MODEL), F32),
        compiler_params=_cparams(("arbitrary", "arbitrary")),
        name="modulation",
    )(cond8, w_ada, b_ada.reshape(DEPTH, 1, N_MOD * D_MODEL))


def _fprep_kernel(w_ref, a_ref, b_ref):
    r = lax.broadcasted_iota(jnp.int32, (D_FNET, D_FNET), 0)
    c = lax.broadcasted_iota(jnp.int32, (D_FNET, D_FNET), 1)
    same = (r >> 6) == (c >> 6)
    ph = ((r & 63) * (c & 63)) & 63
    ang = ph.astype(F32) * (2.0 * math.pi / FNET_GROUP_DIM)
    scale = 1.0 / math.sqrt(FNET_GROUP_DIM)
    cm = jnp.where(same, jnp.cos(ang) * scale, 0.0)
    sm = jnp.where(same, jnp.sin(ang) * scale, 0.0)
    w = w_ref[0]
    a_ref[0] = jnp.dot(cm, w, precision=lax.Precision.HIGHEST, preferred_element_type=F32).astype(BF16)
    b_ref[0] = jnp.dot(sm, w, precision=lax.Precision.HIGHEST, preferred_element_type=F32).astype(BF16)


def _fnet_prep(wbd):
    spec = pl.BlockSpec((1, D_FNET, D_FNET), lambda l: (l, 0, 0))
    return pl.pallas_call(
        _fprep_kernel,
        grid=(DEPTH,),
        in_specs=[spec],
        out_specs=[spec, spec],
        out_shape=[jax.ShapeDtypeStruct((DEPTH, D_FNET, D_FNET), BF16)] * 2,
        compiler_params=_cparams(("arbitrary",)),
        name="fnet_prep",
    )(wbd)


TABLE_ROWS = 64


def _base_table_kernel(cj_ref, sj_ref, cm_ref, sm_ref, *, L):
    j = lax.broadcasted_iota(jnp.int32, (TABLE_ROWS, L // 2), 0)
    n = lax.broadcasted_iota(jnp.int32, (TABLE_ROWS, L // 2), 1)
    w = 2.0 * math.pi / L
    fine = ((j * n) & (L - 1)).astype(F32) * w
    coarse = ((j * TABLE_ROWS * n) & (L - 1)).astype(F32) * w
    scale = 1.0 / math.sqrt(L)
    cj_ref[...] = jnp.cos(fine)
    sj_ref[...] = jnp.sin(fine)
    cm_ref[...] = jnp.cos(coarse) * scale
    sm_ref[...] = jnp.sin(coarse) * scale


def _table_kernel(cj_ref, sj_ref, cm_ref, sm_ref, c_ref, s_ref):
    m = pl.program_id(0)
    c0 = cm_ref[pl.ds(m, 1), :]
    s0 = sm_ref[pl.ds(m, 1), :]
    cj = cj_ref[...]
    sj = sj_ref[...]
    c_ref[...] = (cj * c0 - sj * s0).astype(BF16)
    s_ref[...] = (-(sj * c0 + cj * s0)).astype(BF16)


def _dft_tables(L):
    half = L // 2
    small = pl.BlockSpec((TABLE_ROWS, half), lambda *_: (0, 0))
    base = pl.pallas_call(
        functools.partial(_base_table_kernel, L=L),
        out_specs=[small] * 4,
        out_shape=[jax.ShapeDtypeStruct((TABLE_ROWS, half), F32)] * 4,
        compiler_params=pltpu.CompilerParams(vmem_limit_bytes=VMEM_LIMIT),
        name=f"dft_base_tables_{L}",
    )()
    spec = pl.BlockSpec((TABLE_ROWS, half), lambda i: (i, 0))
    return pl.pallas_call(
        _table_kernel,
        grid=(L // TABLE_ROWS,),
        in_specs=[small] * 4,
        out_specs=[spec, spec],
        out_shape=[jax.ShapeDtypeStruct((L, half), BF16)] * 2,
        compiler_params=_cparams(("arbitrary",)),
        name=f"dft_tables_{L}",
    )(*base)


def _dft_kernel(c_ref, s_ref, va_ref, vb_ref, o_ref, vaf, vbf, *, L):
    tk, tc = o_ref.shape
    half = L // 2
    fb = min(256, half)

    @pl.when(pl.program_id(1) == 0)
    def _():
        r = lax.broadcasted_iota(jnp.int32, (fb, fb), 0)
        c = lax.broadcasted_iota(jnp.int32, (fb, fb), 1)
        flip = jnp.where(r + c == fb, 1.0, 0.0).astype(BF16)
        first_row = lax.broadcasted_iota(jnp.int32, (fb, tc), 0) == 0
        for m in range(half // fb):
            pm = L // fb - m - 1
            for src, dst, sign in ((va_ref, vaf, 1.0), (vb_ref, vbf, -1.0)):
                partner = jnp.dot(flip, src[pm * fb:(pm + 1) * fb, :], preferred_element_type=F32)
                if m >= 1:
                    partner = partner + jnp.where(
                        first_row, src[(pm + 1) * fb:(pm + 1) * fb + 1, :].astype(F32), 0.0)
                dst[m * fb:(m + 1) * fb, :] = (src[m * fb:(m + 1) * fb, :].astype(F32) + sign * partner).astype(BF16)

    k = lax.broadcasted_iota(jnp.int32, (tk, 1), 0) + pl.program_id(1) * tk
    alt = (1 - 2 * (k & 1)).astype(F32) * (1.0 / math.sqrt(L))
    o_ref[...] = (jnp.dot(c_ref[...], vaf[...], preferred_element_type=F32)
                  + jnp.dot(s_ref[...], vbf[...], preferred_element_type=F32)
                  + alt * va_ref[half:half + 1, :].astype(F32))


def _dft_apply(ct, st, va, vb):
    L, ncols = va.shape
    half = L // 2
    tk, tc = 256, 1024
    return pl.pallas_call(
        functools.partial(_dft_kernel, L=L),
        grid=(ncols // tc, L // tk),
        in_specs=[
            pl.BlockSpec((tk, half), lambda j, i: (i, 0)),
            pl.BlockSpec((tk, half), lambda j, i: (i, 0)),
            pl.BlockSpec((L, tc), lambda j, i: (0, j)),
            pl.BlockSpec((L, tc), lambda j, i: (0, j)),
        ],
        out_specs=pl.BlockSpec((tk, tc), lambda j, i: (i, j)),
        out_shape=jax.ShapeDtypeStruct((L, ncols), F32),
        scratch_shapes=[pltpu.VMEM((half, tc), BF16), pltpu.VMEM((half, tc), BF16)],
        compiler_params=_cparams(("arbitrary", "arbitrary")),
        name=f"dft_apply_{L}",
    )(ct, st, va, vb)


def _rms_mod(x, g, scale, shift):
    ms = jnp.mean(x * x, axis=-1, keepdims=True)
    return (x * lax.rsqrt(ms + EPS)) * g * (1.0 + scale) + shift


def _pos_table_kernel(s_ref, c_ref):
    n, nf = s_ref.shape
    j = lax.broadcasted_iota(jnp.int32, (n, nf), 0).astype(F32)
    k = lax.broadcasted_iota(jnp.int32, (n, nf), 1).astype(F32)
    ang = j * jnp.exp(k * (-math.log(10000.0) / nf))
    s_ref[...] = jnp.sin(ang)
    c_ref[...] = jnp.cos(ang)


def _pos_tables(n):
    nf = D_MODEL // 4
    return pl.pallas_call(
        _pos_table_kernel,
        out_shape=[jax.ShapeDtypeStruct((n, nf), F32)] * 2,
        name="pos_tables",
    )()


def _pos_tile(ps_ref, pc_ref, t, tm):
    nrow = tm // GRID_W
    nf = D_MODEL // 4

    def rows(tab):
        return jnp.concatenate(
            [jnp.broadcast_to(tab[pl.ds(t * nrow + k, 1), :], (GRID_W, nf)) for k in range(nrow)], axis=0)

    def cols(tab):
        return jnp.concatenate([tab[0:GRID_W, :]] * nrow, axis=0)

    return jnp.concatenate([rows(ps_ref), rows(pc_ref), cols(ps_ref), cols(pc_ref)], axis=-1)


def _in_kernel(*refs, add_pos, moe, row_base, row_step):
    if moe:
        (pos_ref, posn_ref, x1_ref, ri_ref, modp_ref, ys_ref, mod_ref, g_ref, w_ref, a_ref, b_ref,
         x_out_ref, zr_ref, zuv_ref, va_ref, vb_ref, ybuf, sem) = refs
    elif add_pos:
        x_ref, ps_ref, pc_ref, mod_ref, g_ref, w_ref, a_ref, b_ref, zr_ref, zuv_ref, va_ref, vb_ref = refs
    else:
        x_ref, mod_ref, g_ref, w_ref, a_ref, b_ref, zr_ref, zuv_ref, va_ref, vb_ref = refs
    row = row_base + pl.program_id(0) * row_step

    def project(x):
        m = mod_ref[pl.ds(row, 1), :]
        h = _rms_mod(x, g_ref[...], m[:, D_MODEL:2 * D_MODEL], m[:, 0:D_MODEL])
        z = jnp.dot(h.astype(BF16), w_ref[...], preferred_element_type=F32)
        zr_ref[0] = z[:, 0:2 * D_RNN]
        zuv_ref[0] = z[:, 2 * D_RNN:2 * D_RNN + 2 * D_TMLP]
        zf = z[:, 2 * D_RNN + 2 * D_TMLP:D_IN].astype(BF16)
        va_ref[...] = jnp.dot(zf, a_ref[...], preferred_element_type=F32).astype(BF16)
        vb_ref[...] = jnp.dot(zf, b_ref[...], preferred_element_type=F32).astype(BF16)

    if moe:
        steps_per_seq = pl.num_programs(1)
        i = pl.program_id(0) * steps_per_seq + pl.program_id(1)

        def body(slot):
            g2 = modp_ref[pl.ds(row, 1), 5 * D_MODEL:6 * D_MODEL]
            x = x1_ref[0] + g2 * _moe_output(ri_ref[0], ybuf, slot)
            x_out_ref[0] = x
            project(x)

        _with_gathered_rows(i, pl.num_programs(0) * steps_per_seq, pos_ref, posn_ref, ys_ref, ybuf, sem, body)
    else:
        x = x_ref[0]
        if add_pos:
            x = x + _pos_tile(ps_ref, pc_ref, pl.program_id(1), x.shape[0])
        project(x)


def _in_proj(x, pos, mods, g, w_in_b, fa, fb, *, layer, row_base, row_step, moe=None):
    fused = moe is not None
    nseq, L, _ = (moe[1] if fused else x).shape
    tm = min(L, PROJ_TILE)
    nt = L // tm
    add_pos = pos is not None
    tok = pl.BlockSpec((1, tm, D_MODEL), lambda b, t: (b, t, 0))
    in_specs, args, scratch = [], [], []
    if fused:
        slots, x1, ri, ys = moe
        n = nseq * nt
        in_specs += [
            pl.BlockSpec((1, 2, tm), lambda b, t: (b * nt + t, 0, 0), memory_space=pltpu.SMEM),
            pl.BlockSpec((1, 2, tm), lambda b, t: (jnp.minimum(b * nt + t + 1, n - 1), 0, 0),
                         memory_space=pltpu.SMEM),
            tok,
            pl.BlockSpec((1, tm, ROUTER_LANES), lambda b, t: (b, t, 0)),
            _layer_spec((SUBLANES, N_MOD * D_MODEL), layer - 1),
            pl.BlockSpec(memory_space=pl.ANY),
        ]
        pos_blocks = jnp.swapaxes(slots.reshape(2, n, tm), 0, 1)
        args += [pos_blocks, pos_blocks, x1, ri, mods, ys]
        scratch = [pltpu.VMEM((2, 2, tm * LANE_TILES, LANES), F32), pltpu.SemaphoreType.DMA((2,))]
    else:
        in_specs.append(tok)
        args.append(x)
        if add_pos:
            in_specs += [pl.BlockSpec(p.shape, lambda b, t: (0, 0)) for p in pos]
            args += list(pos)
    in_specs += [
        _layer_spec((SUBLANES, N_MOD * D_MODEL), layer),
        _layer_spec((1, D_MODEL), layer),
        _layer_spec((D_MODEL, D_IN), layer),
        _layer_spec((D_FNET, D_FNET), layer),
        _layer_spec((D_FNET, D_FNET), layer),
    ]
    args += [mods, g, w_in_b, fa, fb]
    out_specs = [
        pl.BlockSpec((1, tm, 2 * D_RNN), lambda b, t: (b, t, 0)),
        pl.BlockSpec((1, tm, 2 * D_TMLP), lambda b, t: (b, t, 0)),
        pl.BlockSpec((tm, D_FNET), lambda b, t: (t, b)),
        pl.BlockSpec((tm, D_FNET), lambda b, t: (t, b)),
    ]
    out_shape = [
        jax.ShapeDtypeStruct((nseq, L, 2 * D_RNN), F32),
        jax.ShapeDtypeStruct((nseq, L, 2 * D_TMLP), F32),
        jax.ShapeDtypeStruct((L, nseq * D_FNET), BF16),
        jax.ShapeDtypeStruct((L, nseq * D_FNET), BF16),
    ]
    if fused:
        out_specs.insert(0, tok)
        out_shape.insert(0, jax.ShapeDtypeStruct((nseq, L, D_MODEL), F32))
    return pl.pallas_call(
        functools.partial(_in_kernel, add_pos=add_pos, moe=fused, row_base=row_base, row_step=row_step),
        grid=(nseq, nt),
        in_specs=in_specs,
        out_specs=out_specs,
        out_shape=out_shape,
        scratch_shapes=scratch,
        compiler_params=_cparams(("arbitrary", "arbitrary")),
        name=f"in_proj_{L}",
    )(*args)


def _gelu_tanh(x):
    return 0.5 * x * (1.0 + jnp.tanh(math.sqrt(2.0 / math.pi) * (x + 0.044715 * (x * x * x))))


def _rows_to_tile(rows):
    sub = lax.broadcasted_iota(jnp.int32, (SUBLANES, LANES), 0)
    out = jnp.zeros((SUBLANES, LANES), F32)
    for s, r in enumerate(rows):
        out = jnp.where(sub == s, jnp.broadcast_to(r, (SUBLANES, LANES)), out)
    return out


def _lru_kernel(xr_ref, gr_ref, cw_ref, cb_ref, wa_ref, wx_ref, ba_ref, bx_ref, lam_ref, h0_ref,
                y_ref, st_ref, xnat, pext, af, bf, ab, bb, hfo, pfo, hbo, pbo, hnat, *, L, batched):
    S = L if batched else L // SUBLANES
    pitch = S + SUBLANES
    n = S * SUBLANES
    chunk = 256

    def seg(ref, s):
        return ref.at[s] if batched else ref.at[0, s * S:(s + 1) * S]

    for s in range(SUBLANES):
        xnat[s * pitch:s * pitch + S, :] = seg(xr_ref, s)[...]

    def perm_in(j, c):
        dst = pl.multiple_of((j + 2) * SUBLANES, SUBLANES)
        pext[pl.ds(dst, SUBLANES), :] = xnat[pl.ds(j, SUBLANES, stride=pitch), :]
        return c

    lax.fori_loop(0, S, perm_in, 0, unroll=8)

    sub = lax.broadcasted_iota(jnp.int32, (SUBLANES, LANES), 0)

    def from_prev_segment(v):
        return jnp.where(sub == 0, 0.0, pltpu.roll(v, 1, axis=0))

    def from_next_segment(v):
        return jnp.where(sub == SUBLANES - 1, 0.0, pltpu.roll(v, SUBLANES - 1, axis=0))

    if batched:
        pext[0:16, :] = jnp.zeros((16, LANES), F32)
        pext[(S + 2) * 8:(S + 3) * 8, :] = jnp.zeros((8, LANES), F32)
    else:
        pext[0:8, :] = from_prev_segment(pext[S * 8:(S + 1) * 8, :])
        pext[8:16, :] = from_prev_segment(pext[(S + 1) * 8:(S + 2) * 8, :])
        pext[(S + 2) * 8:(S + 3) * 8, :] = from_next_segment(pext[16:24, :])

    lam = lam_ref[...]
    nl = -lam
    sp = jnp.maximum(nl, 0.0) + jnp.log1p(jnp.exp(-jnp.abs(nl)))
    c_la = (-0.5 * LRU_C) * sp
    ba_h = 0.5 * ba_ref[...]
    bx_h = 0.5 * bx_ref[...]
    a_refs = (af, ab)
    b_refs = (bf, bb)

    def gates(i, c):
        base = pl.multiple_of(i * chunk, chunk)
        xc = (cw_ref[0:1, :] * pext[pl.ds(base, chunk), :]
              + cw_ref[1:2, :] * pext[pl.ds(base + 8, chunk), :]
              + cw_ref[2:3, :] * pext[pl.ds(base + 16, chunk), :]
              + cw_ref[3:4, :] * pext[pl.ds(base + 24, chunk), :]
              + cb_ref[...])
        xcb = xc.astype(BF16)
        xh = 0.5 * xc
        for d in range(2):
            tr = jnp.tanh(jnp.dot(xcb, wa_ref[d, 0], preferred_element_type=F32) + ba_h[d:d + 1, :])
            ti = jnp.tanh(jnp.dot(xcb, wx_ref[d, 0], preferred_element_type=F32) + bx_h[d:d + 1, :])
            log_a = c_la[d:d + 1, :] * (1.0 + tr)
            a = jnp.exp(log_a)
            v = jnp.tanh(log_a) * (-1.0 - a * a)
            coef = jnp.where(v > 0.0, v * lax.rsqrt(v), 0.0)
            a_refs[d][pl.ds(base, chunk), :] = a
            b_refs[d][pl.ds(base, chunk), :] = coef * ((1.0 + ti) * xh)
        return c

    lax.fori_loop(0, n // chunk, gates, 0, unroll=min(4, n // chunk))

    nq = LRU_SUB
    sq = S // nq

    def scan(i, carry):
        out = []
        for q in range(nq):
            hf, pf, hb, pb = carry[4 * q:4 * q + 4]
            jf = pl.multiple_of((q * sq + i) * SUBLANES, SUBLANES)
            jb = pl.multiple_of(((q + 1) * sq - 1 - i) * SUBLANES, SUBLANES)
            a1 = af[pl.ds(jf, SUBLANES), :]
            hf = a1 * hf + bf[pl.ds(jf, SUBLANES), :]
            pf = a1 * pf
            hfo[pl.ds(jf, SUBLANES), :] = hf
            pfo[pl.ds(jf, SUBLANES), :] = pf
            a2 = ab[pl.ds(jb, SUBLANES), :]
            hb = a2 * hb + bb[pl.ds(jb, SUBLANES), :]
            pb = a2 * pb
            hbo[pl.ds(jb, SUBLANES), :] = hb
            pbo[pl.ds(jb, SUBLANES), :] = pb
            out += [hf, pf, hb, pb]
        return tuple(out)

    zero = jnp.zeros((SUBLANES, LANES), F32)
    one = jnp.ones((SUBLANES, LANES), F32)
    ends = lax.fori_loop(0, sq, scan, (zero, one, zero, one) * nq, unroll=min(sq, 4))
    end_f = [(ends[4 * q], ends[4 * q + 1]) for q in range(nq)]
    end_b = [(ends[4 * q + 2], ends[4 * q + 3]) for q in range(nq)]

    hf, pf = end_f[0]
    for q in range(1, nq):
        hf, pf = end_f[q][1] * hf + end_f[q][0], end_f[q][1] * pf
    hb, pb = end_b[nq - 1]
    for q in range(nq - 2, -1, -1):
        hb, pb = end_b[q][1] * hb + end_b[q][0], end_b[q][1] * pb

    if batched:
        start_f = h0_ref[0]
        start_b = h0_ref[1]
        st_ref[0] = pf * start_f + hf
        st_ref[1] = pb * start_b + hb
    else:
        rows_f = [h0_ref[0, 0:1, :]]
        for s in range(1, SUBLANES):
            rows_f.append(pf[s - 1:s, :] * rows_f[-1] + hf[s - 1:s, :])
        st_ref[0, 0:1, :] = pf[7:8, :] * rows_f[7] + hf[7:8, :]
        rows_b = [None] * SUBLANES
        rows_b[7] = h0_ref[0, 1:2, :]
        for s in range(SUBLANES - 2, -1, -1):
            rows_b[s] = pb[s + 1:s + 2, :] * rows_b[s + 1] + hb[s + 1:s + 2, :]
        st_ref[0, 1:2, :] = pb[0:1, :] * rows_b[0] + hb[0:1, :]
        start_f = _rows_to_tile(rows_f)
        start_b = _rows_to_tile(rows_b)

    init_f = [start_f]
    for q in range(nq - 1):
        init_f.append(end_f[q][1] * init_f[q] + end_f[q][0])
    init_b = [None] * nq
    init_b[nq - 1] = start_b
    for q in range(nq - 1, 0, -1):
        init_b[q - 1] = end_b[q][1] * init_b[q] + end_b[q][0]

    for q in range(nq):
        def perm_out(j, c, q=q):
            src = pl.multiple_of(j * SUBLANES, SUBLANES)
            v = (hfo[pl.ds(src, SUBLANES), :] + pfo[pl.ds(src, SUBLANES), :] * init_f[q]
                 + hbo[pl.ds(src, SUBLANES), :] + pbo[pl.ds(src, SUBLANES), :] * init_b[q])
            hnat[pl.ds(j, SUBLANES, stride=pitch), :] = v
            return c

        lax.fori_loop(q * sq, (q + 1) * sq, perm_out, 0, unroll=min(sq, 8))

    for s in range(SUBLANES):
        seg(y_ref, s)[...] = hnat[s * pitch:s * pitch + S, :] * _gelu_tanh(seg(gr_ref, s)[...])


def _lru_mixer(zr, conv_w, conv_b, wa_bd, wx_bd, ba, bx, lam, h0, *, layer):
    nseq, L, _ = zr.shape
    ncb = D_RNN // LRU_CB
    batched = L <= LRU_BATCH_MAX_LEN
    nb = SUBLANES if batched else 1
    S = L if batched else L // SUBLANES
    pitch = S + SUBLANES
    n = S * SUBLANES
    vec2 = pl.BlockSpec((None, 2, LRU_CB), lambda b, c: (layer, 0, c))
    wspec = pl.BlockSpec((None, 2, 1, LRU_CB, LRU_CB), lambda b, c: (layer, 0, c, 0, 0))
    if batched:
        state = pl.BlockSpec((2, nb, LRU_CB), lambda b, c: (0, b, c))
        state_shape = (2, nseq, D_RNN)
    else:
        state = pl.BlockSpec((1, 2, LRU_CB), lambda b, c: (b, 0, c))
        state_shape = (nseq, 2, D_RNN)
    return pl.pallas_call(
        functools.partial(_lru_kernel, L=L, batched=batched),
        grid=(nseq // nb, ncb),
        in_specs=[
            pl.BlockSpec((nb, L, LRU_CB), lambda b, c: (b, 0, c)),
            pl.BlockSpec((nb, L, LRU_CB), lambda b, c: (b, 0, c + ncb)),
            pl.BlockSpec((None, CONV_W, LRU_CB), lambda b, c: (layer, 0, c)),
            pl.BlockSpec((None, 1, LRU_CB), lambda b, c: (layer, 0, c)),
            wspec, wspec, vec2, vec2, vec2,
            state,
        ],
        out_specs=[pl.BlockSpec((nb, L, LRU_CB), lambda b, c: (b, 0, c)), state],
        out_shape=[
            jax.ShapeDtypeStruct((nseq, L, D_RNN), F32),
            jax.ShapeDtypeStruct(state_shape, F32),
        ],
        scratch_shapes=[
            pltpu.VMEM((SUBLANES * pitch, LANES), F32),
            pltpu.VMEM(((S + 3) * SUBLANES, LANES), F32),
            *([pltpu.VMEM((n, LANES), F32)] * 8),
            pltpu.VMEM((SUBLANES * pitch, LANES), F32),
        ],
        compiler_params=_cparams(("arbitrary", "arbitrary")),
        name=f"lru_mixer_{L}",
    )(zr, zr, conv_w, conv_b, wa_bd, wx_bd, ba, bx, lam, h0)


def _out_kernel(*refs, add_pos, row_base, row_step):
    if add_pos:
        (x_ref, ps_ref, pc_ref, yr_ref, zuv_ref, yf_ref, mod_ref, ws_ref, bt_ref, wo_ref, g_ref, wr_ref, br_ref,
         cin_ref, x1_ref, h2_ref, ri_ref, cnt_ref, carry) = refs
    else:
        (x_ref, yr_ref, zuv_ref, yf_ref, mod_ref, ws_ref, bt_ref, wo_ref, g_ref, wr_ref, br_ref, cin_ref,
         x1_ref, h2_ref, ri_ref, cnt_ref, carry) = refs
    tm = x_ref.shape[1]
    x = x_ref[0]
    if add_pos:
        x = x + _pos_tile(ps_ref, pc_ref, pl.program_id(1), tm)
    row = row_base + pl.program_id(0) * row_step
    m = mod_ref[pl.ds(row, 1), :]
    g1 = m[:, 2 * D_MODEL:3 * D_MODEL]
    sh2 = m[:, 3 * D_MODEL:4 * D_MODEL]
    sc2 = m[:, 4 * D_MODEL:5 * D_MODEL]

    head = lax.broadcasted_iota(jnp.int32, (CHUNK, D_TMLP), 1) >> 6
    yt_parts = []
    for ci in range(tm // CHUNK):
        u = zuv_ref[0, ci * CHUNK:(ci + 1) * CHUNK, 0:D_TMLP]
        v = zuv_ref[0, ci * CHUNK:(ci + 1) * CHUNK, D_TMLP:2 * D_TMLP].astype(BF16)
        s = jnp.zeros((CHUNK, D_TMLP), F32)
        for h in range(TMLP_HEADS):
            sh = jnp.dot(ws_ref[h], v, preferred_element_type=F32) + bt_ref[:, h:h + 1]
            s = jnp.where(head == h, sh, s)
        yt_parts.append(u * s)
    yt = jnp.concatenate(yt_parts, axis=0) if len(yt_parts) > 1 else yt_parts[0]

    y = (jnp.dot(yr_ref[0].astype(BF16), wo_ref[0:D_RNN, :], preferred_element_type=F32)
         + jnp.dot(yt.astype(BF16), wo_ref[D_RNN:D_RNN + D_TMLP, :], preferred_element_type=F32)
         + jnp.dot(yf_ref[...].astype(BF16), wo_ref[D_RNN + D_TMLP:D_MODEL, :], preferred_element_type=F32))
    x1 = x + g1 * y
    x1_ref[0] = x1
    h2 = _rms_mod(x1, g_ref[...], sc2, sh2)
    _store_token_major(h2_ref, h2)

    wr = wr_ref[...]
    w_hi = wr.astype(BF16)
    w_lo = (wr - w_hi.astype(F32)).astype(BF16)
    h_hi = h2.astype(BF16)
    h_lo = (h2 - h_hi.astype(F32)).astype(BF16)
    p_hi = jnp.dot(h_hi, jnp.concatenate([w_hi, w_lo], axis=-1), preferred_element_type=F32)
    p_lo = jnp.dot(h_lo, w_hi, preferred_element_type=F32)
    logits = p_hi[:, 0:ROUTER_LANES] + p_hi[:, ROUTER_LANES:2 * ROUTER_LANES] + p_lo + br_ref[...]
    lane = lax.broadcasted_iota(jnp.int32, (tm, ROUTER_LANES), 1)
    lane_f = lane.astype(F32)
    is_g = lane < N_EGROUPS
    gl = jnp.where(is_g, logits, NEG_BIG)
    gmax = jnp.max(gl, axis=-1, keepdims=True)
    gsel = jnp.min(jnp.where(gl == gmax, lane_f, 1e4), axis=-1, keepdims=True)
    pg = 1.0 / jnp.sum(jnp.where(is_g, jnp.exp(logits - gmax), 0.0), axis=-1, keepdims=True)
    grp_f = ((lane - N_EGROUPS) >> 3).astype(F32)
    emask = (lane >= N_EGROUPS) & (lane < N_EGROUPS + N_EXPERTS) & (grp_f == gsel)
    el = jnp.where(emask, logits, NEG_BIG)
    v1 = jnp.max(el, axis=-1, keepdims=True)
    i1 = jnp.min(jnp.where(el == v1, lane_f, 1e4), axis=-1, keepdims=True)
    el2 = jnp.where(lane_f == i1, NEG_BIG, el)
    v2 = jnp.max(el2, axis=-1, keepdims=True)
    i2 = jnp.min(jnp.where(el2 == v2, lane_f, 1e4), axis=-1, keepdims=True)
    e2x = jnp.exp(v2 - v1)
    fw1 = 1.0 / (1.0 + e2x)
    fw2 = e2x * fw1
    @pl.when((pl.program_id(0) == 0) & (pl.program_id(1) == 0))
    def _():
        carry[...] = cin_ref[...]

    e1 = i1 - N_EGROUPS
    e2 = i2 - N_EGROUPS
    m1 = lane_f == e1
    m2 = lane_f == e2
    oh = jnp.where(m1 | m2, 1.0, 0.0)
    r_i = lax.broadcasted_iota(jnp.int32, (tm, tm), 0)
    c_i = lax.broadcasted_iota(jnp.int32, (tm, tm), 1)
    tri = jnp.where(c_i < r_i, 1.0, 0.0).astype(BF16)
    before = jnp.dot(tri, oh.astype(BF16), preferred_element_type=F32) + carry[0:1, :]
    rank1 = jnp.sum(jnp.where(m1, before, 0.0), axis=-1, keepdims=True)
    rank2 = jnp.sum(jnp.where(m2, before, 0.0), axis=-1, keepdims=True)
    total = carry[0:1, :] + jnp.sum(oh, axis=0, keepdims=True)
    carry[0:1, :] = total
    cnt_ref[...] = jnp.broadcast_to(total, cnt_ref.shape)

    vals = (e1, e2, pg * fw1, pg * fw2, rank1, rank2)
    ri = jnp.zeros((tm, ROUTER_LANES), F32)
    for k, v in enumerate(vals):
        ri = jnp.where(lane == k, v, ri)
    ri_ref[0] = ri


def _layer_spec(shape, layer):
    zeros = (0,) * len(shape)
    return pl.BlockSpec((None, *shape), lambda *_: (layer, *zeros))


def _out_proj(x, pos, yr, zuv, yf, mods, ws_b, bt, wo_b, g, wr, br, cin, *, layer, row_base, row_step):
    nseq, L, _ = x.shape
    tm = min(L, PROJ_TILE)
    add_pos = pos is not None
    in_specs = [pl.BlockSpec((1, tm, D_MODEL), lambda b, t: (b, t, 0))]
    args = [x]
    if add_pos:
        in_specs += [pl.BlockSpec(p.shape, lambda b, t: (0, 0)) for p in pos]
        args += list(pos)
    in_specs += [
        pl.BlockSpec((1, tm, D_RNN), lambda b, t: (b, t, 0)),
        pl.BlockSpec((1, tm, 2 * D_TMLP), lambda b, t: (b, t, 0)),
        pl.BlockSpec((tm, D_FNET), lambda b, t: (t, b)),
        _layer_spec((SUBLANES, N_MOD * D_MODEL), layer),
        _layer_spec((TMLP_HEADS, CHUNK, CHUNK), layer),
        _layer_spec((CHUNK, TMLP_HEADS), layer),
        _layer_spec((D_MODEL, D_MODEL), layer),
        _layer_spec((1, D_MODEL), layer),
        _layer_spec((D_MODEL, ROUTER_LANES), layer),
        _layer_spec((1, ROUTER_LANES), layer),
        pl.BlockSpec((SUBLANES, ROUTER_LANES), lambda b, t: (0, 0)),
    ]
    args += [yr, zuv, yf, mods, ws_b, bt, wo_b, g, wr, br, cin]
    tok = pl.BlockSpec((1, tm, D_MODEL), lambda b, t: (b, t, 0))
    return pl.pallas_call(
        functools.partial(_out_kernel, add_pos=add_pos, row_base=row_base, row_step=row_step),
        grid=(nseq, L // tm),
        in_specs=in_specs,
        out_specs=[tok, pl.BlockSpec((tm * LANE_TILES, LANES), lambda b, t: (b * (L // tm) + t, 0)),
                   pl.BlockSpec((1, tm, ROUTER_LANES), lambda b, t: (b, t, 0)),
                   pl.BlockSpec((SUBLANES, ROUTER_LANES), lambda b, t: (0, 0))],
        out_shape=[
            jax.ShapeDtypeStruct((nseq, L, D_MODEL), F32),
            jax.ShapeDtypeStruct((nseq * L * LANE_TILES, LANES), F32),
            jax.ShapeDtypeStruct((nseq, L, ROUTER_LANES), F32),
            jax.ShapeDtypeStruct((SUBLANES, ROUTER_LANES), F32),
        ],
        scratch_shapes=[pltpu.VMEM((SUBLANES, ROUTER_LANES), F32)],
        compiler_params=_cparams(("arbitrary", "arbitrary")),
        name=f"out_proj_{L}",
    )(*args)


def _store_token_major(ref, x):
    tm = x.shape[0]
    for j in range(LANE_TILES):
        ref[pl.ds(j, tm, stride=LANE_TILES), :] = x[:, j * LANES:(j + 1) * LANES]


def _load_token_major(ref):
    tm = ref.shape[0] // LANE_TILES
    return jnp.concatenate([ref[pl.ds(j, tm, stride=LANE_TILES), :] for j in range(LANE_TILES)], axis=-1)


def _token_copy(src_ref, src_tok, dst_ref, dst_tok, sem):
    return pltpu.make_async_copy(src_ref.at[pl.ds(pl.multiple_of(src_tok, LANE_TILES), LANE_TILES)],
                                 dst_ref.at[pl.ds(pl.multiple_of(dst_tok, LANE_TILES), LANE_TILES)], sem)


def _scatter_rows(pos_ref, h_ref, xs_ref, sem):
    rows = h_ref.shape[0]

    for r in range(0, rows, LANE_TILES):
        _token_copy(h_ref, r, xs_ref, pos_ref[0, 0, r // LANE_TILES], sem).start(priority=0)
        _token_copy(h_ref, r, xs_ref, pos_ref[0, 1, r // LANE_TILES], sem).start(priority=1)
    for _ in range(2):
        pltpu.make_async_copy(h_ref, xs_ref.at[pl.ds(0, rows)], sem).wait()


def _dispatch_kernel(pos_ref, hp_ref, hs_ref, xs_ref, sem, *, n_first):
    i = pl.program_id(0)

    @pl.when(i < n_first)
    def _():
        _scatter_rows(pos_ref, hp_ref, xs_ref, sem)

    @pl.when(i >= n_first)
    def _():
        _scatter_rows(pos_ref, hs_ref, xs_ref, sem)


def _dispatch(pos, h_first, h_second):
    tm = ROW_TILE
    blk = tm * LANE_TILES
    n_first = h_first.shape[0] // blk
    n_second = h_second.shape[0] // blk
    nrows = 2 * (h_first.shape[0] + h_second.shape[0])
    return pl.pallas_call(
        functools.partial(_dispatch_kernel, n_first=n_first),
        grid=(n_first + n_second,),
        in_specs=[
            pl.BlockSpec((1, 2, tm), lambda i: (i, 0, 0), memory_space=pltpu.SMEM),
            pl.BlockSpec((blk, LANES), lambda i: (jnp.minimum(i, n_first - 1), 0)),
            pl.BlockSpec((blk, LANES), lambda i: (jnp.maximum(i - n_first, 0), 0)),
        ],
        out_specs=pl.BlockSpec(memory_space=pl.ANY),
        out_shape=jax.ShapeDtypeStruct((nrows, LANES), F32),
        scratch_shapes=[pltpu.SemaphoreType.DMA(())],
        compiler_params=_cparams(("arbitrary",)),
        name="moe_dispatch",
    )(pos, h_first, h_second)


WEIGHT_LEAD = (3, 2, 1)
WEIGHT_SLOTS = 4
LEAD_ITEMS = max(WEIGHT_LEAD)


def _expert_kernel(wt_ref, we_ref, lo_ref, hi_ref, first_ref, ord_ref, xs_ref, w1_ref, w3_ref, w2_ref, ys_ref,
                   w1b, w3b, w2b):
    s = pl.program_id(0)
    last = pl.num_programs(0) - 1
    tm = EXPERT_TILE

    def stage(lead, src, dst):
        cur = jnp.minimum(s + lead, last)
        prv = jnp.minimum(s + lead - 1, last)

        @pl.when((s == 0) | (we_ref[cur] != we_ref[prv]))
        def _():
            dst[ord_ref[cur] & (WEIGHT_SLOTS - 1)] = src[0].astype(BF16)

    for lead, src, dst in zip(WEIGHT_LEAD, (w1_ref, w3_ref, w2_ref), (w1b, w3b, w2b)):
        stage(lead, src, dst)

    lo = lo_ref[s]
    hi = hi_ref[s]
    full = (lo == 0) & (hi == tm)

    @pl.when((first_ref[s] == 1) & jnp.logical_not(full))
    def _():
        ys_ref[...] = jnp.zeros_like(ys_ref)

    @pl.when(hi > lo)
    def _():
        slot = ord_ref[s] & (WEIGHT_SLOTS - 1)
        x = _load_token_major(xs_ref).astype(BF16)
        a = jnp.dot(x, w1b[slot], preferred_element_type=F32)
        b = jnp.dot(x, w3b[slot], preferred_element_type=F32)
        hid = (a * jax.nn.sigmoid(a)) * b
        res = jnp.dot(hid.astype(BF16), w2b[slot], preferred_element_type=F32)

        @pl.when(full)
        def _():
            _store_token_major(ys_ref, res)

        @pl.when(jnp.logical_not(full))
        def _():
            rows = lax.broadcasted_iota(jnp.int32, res.shape, 0)
            _store_token_major(ys_ref, jnp.where((rows >= lo) & (rows < hi), res, _load_token_major(ys_ref)))


def _experts(items, xs, w1, w3, w2):
    nw = items[0].shape[0]

    def row(s, wt, we, lo, hi, fi, od):
        return (wt[s], 0)

    def weight(lead):
        return lambda s, wt, we, lo, hi, fi, od: (we[jnp.minimum(s + lead, nw - 1)], 0, 0)

    return pl.pallas_call(
        _expert_kernel,
        grid_spec=pltpu.PrefetchScalarGridSpec(
            num_scalar_prefetch=6,
            grid=(nw,),
            in_specs=[
                pl.BlockSpec((EXPERT_TILE * LANE_TILES, LANES), row),
                pl.BlockSpec((1, D_MODEL, D_EXPERT), weight(WEIGHT_LEAD[0])),
                pl.BlockSpec((1, D_MODEL, D_EXPERT), weight(WEIGHT_LEAD[1])),
                pl.BlockSpec((1, D_EXPERT, D_MODEL), weight(WEIGHT_LEAD[2])),
            ],
            out_specs=pl.BlockSpec((EXPERT_TILE * LANE_TILES, LANES), row),
            scratch_shapes=[
                pltpu.VMEM((WEIGHT_SLOTS, D_MODEL, D_EXPERT), BF16),
                pltpu.VMEM((WEIGHT_SLOTS, D_MODEL, D_EXPERT), BF16),
                pltpu.VMEM((WEIGHT_SLOTS, D_EXPERT, D_MODEL), BF16),
            ],
        ),
        out_shape=jax.ShapeDtypeStruct(xs.shape, F32),
        compiler_params=_cparams(("arbitrary",)),
        name="moe_experts",
    )(*items, xs, w1, w3, w2)


def _with_gathered_rows(i, n, pos_ref, posn_ref, ys_ref, ybuf, sem, body):
    tm = pos_ref.shape[2]

    def gather(p_ref, slot):
        for r in range(tm):
            _token_copy(ys_ref, p_ref[0, 0, r], ybuf.at[slot, 0], r * LANE_TILES, sem.at[slot]).start(priority=0)
            _token_copy(ys_ref, p_ref[0, 1, r], ybuf.at[slot, 1], r * LANE_TILES, sem.at[slot]).start(priority=1)

    def drain(slot):
        for k in range(2):
            pltpu.make_async_copy(ys_ref.at[pl.ds(0, tm * LANE_TILES)], ybuf.at[slot, k], sem.at[slot]).wait()

    def tile(slot):
        if slot == 0:
            @pl.when(i == 0)
            def _():
                gather(pos_ref, 0)

        @pl.when(i + 1 < n)
        def _():
            gather(posn_ref, 1 - slot)

        drain(slot)
        body(slot)

    for slot in range(2):
        pl.when((i & 1) == slot)(functools.partial(tile, slot))


def _moe_output(ri, ybuf, slot):
    return (ri[:, 2:3] * _load_token_major(ybuf.at[slot, 0])
            + ri[:, 3:4] * _load_token_major(ybuf.at[slot, 1]))


def _combine_kernel(pos_ref, posn_ref, x1_ref, ri_ref, mod_ref, gf_ref, ys_ref, o_ref, ybuf, sem,
                    *, row_base, row_step, tiles_per_seq):
    i = pl.program_id(0)

    def body(slot):
        row = row_base + (i // tiles_per_seq) * row_step
        g2 = mod_ref[pl.ds(row, 1), 5 * D_MODEL:6 * D_MODEL]
        x2 = x1_ref[...] + g2 * _moe_output(ri_ref[...], ybuf, slot)
        ms = jnp.mean(x2 * x2, axis=-1, keepdims=True)
        o_ref[...] = (x2 * lax.rsqrt(ms + EPS)) * gf_ref[...]

    _with_gathered_rows(i, pl.num_programs(0), pos_ref, posn_ref, ys_ref, ybuf, sem, body)


def _combine(pos, x1, ri, mods, g_final, ys, *, layer, row_base, row_step, tiles_per_seq):
    ntok = x1.shape[0]
    tm = ROW_TILE
    tok = pl.BlockSpec((tm, D_MODEL), lambda i: (i, 0))
    return pl.pallas_call(
        functools.partial(_combine_kernel, row_base=row_base, row_step=row_step, tiles_per_seq=tiles_per_seq),
        grid=(ntok // tm,),
        in_specs=[
            pl.BlockSpec((1, 2, tm), lambda i: (i, 0, 0), memory_space=pltpu.SMEM),
            pl.BlockSpec((1, 2, tm), lambda i: (jnp.minimum(i + 1, ntok // tm - 1), 0, 0), memory_space=pltpu.SMEM),
            tok,
            pl.BlockSpec((tm, ROUTER_LANES), lambda i: (i, 0)),
            _layer_spec((SUBLANES, N_MOD * D_MODEL), layer),
            pl.BlockSpec((1, D_MODEL), lambda i: (0, 0)),
            pl.BlockSpec(memory_space=pl.ANY),
        ],
        out_specs=tok,
        out_shape=jax.ShapeDtypeStruct((ntok, D_MODEL), F32),
        scratch_shapes=[
            pltpu.VMEM((2, 2, tm * LANE_TILES, LANES), F32),
            pltpu.SemaphoreType.DMA((2,)),
        ],
        compiler_params=_cparams(("arbitrary",)),
        name=f"moe_combine_{ntok}",
    )(pos, pos, x1, ri, mods, g_final, ys)


def _route(ri_first, ri_second, cnt):
    counts = cnt[0, 0:N_EXPERTS].astype(jnp.int32)
    offs = jnp.cumsum(counts) - counts
    info = jnp.concatenate([ri_first[:, 0:SUBLANES], ri_second[:, 0:SUBLANES]], axis=0).T
    vals = info.astype(jnp.int32)
    base = jnp.zeros_like(vals)
    for k in range(N_EXPERTS):
        base = jnp.where(vals == k, offs[k], base)
    slot = (base[0:2] + vals[4:6]) * LANE_TILES
    return slot, counts, offs


def _slot_blocks(slot, tm):
    return jnp.swapaxes(slot.reshape(2, slot.shape[1] // tm, tm), 0, 1)


def _work_items(counts, offs, nw, layer):
    tm = EXPERT_TILE
    first_tile = offs // tm
    last_tile = (offs + counts - 1) // tm
    n_e = jnp.where(counts > 0, last_tile - first_tile + 1, 0)
    w_end = jnp.cumsum(n_e)
    w_start = w_end - n_e
    total = w_end[-1]
    w = jnp.arange(nw, dtype=jnp.int32)
    wc = jnp.minimum(w, total - 1)
    e_w = jnp.sum((wc[:, None] >= w_end[None, :]).astype(jnp.int32), axis=-1)
    sel = (e_w[:, None] == jnp.arange(N_EXPERTS, dtype=jnp.int32)).astype(jnp.int32)
    pick = lambda v: jnp.sum(sel * v[None, :], axis=-1)
    off_w = pick(offs)
    tile_w = pick(first_tile) + (wc - pick(w_start))
    lo = jnp.clip(off_w - tile_w * tm, 0, tm)
    hi = jnp.clip(off_w + pick(counts) - tile_w * tm, 0, tm)
    valid = w < total
    lo = jnp.where(valid, lo, 0)
    hi = jnp.where(valid, hi, 0)
    prev_tile = jnp.concatenate([jnp.full((1,), -1, tile_w.dtype), tile_w[:-1]])
    first = tile_w != prev_tile
    ordinal = pick(jnp.cumsum((counts > 0).astype(jnp.int32)) - 1)

    def lead(a, fill=None):
        head = jnp.broadcast_to(a[0] if fill is None else jnp.asarray(fill, a.dtype), (LEAD_ITEMS,))
        return jnp.concatenate([head, a]).astype(jnp.int32)

    return (lead(tile_w), lead(e_w + layer * N_EXPERTS), lead(lo, 0), lead(hi, 0), lead(first, 0), lead(ordinal))


def _block_diag(w, nblk):
    *lead, H, d, _ = w.shape
    w = w.reshape(*lead, H // nblk, nblk, d, d)
    eye = jnp.eye(nblk, dtype=w.dtype)
    out = jnp.einsum('...gij,gh->...gihj', w, eye)
    return out.reshape(*lead, H // nblk, nblk * d, nblk * d)


def kernel(x_prompt, x_sample, state_lru, c, c_ctx, w_ada, b_ada, g_mix, g_ffn, g_final, w_in, w_out, conv_w, conv_b, lru_wa, lru_ba, lru_wx, lru_bx, lru_lambda, tmlp_ws, tmlp_b, fnet_w, router_g, router_g_b, router_e, router_e_b, e_w1, e_w3, e_w2):
    bp, lp, _ = x_prompt.shape
    bs, ls, _ = x_sample.shape
    n_ctx = bp * lp
    n_lat = bs * ls
    n_tok = n_ctx + n_lat

    cond8 = jnp.concatenate([c_ctx[None, :], c, jnp.zeros((SUBLANES - 1 - bs, D_MODEL), F32)], axis=0)
    pos = _pos_tables(max(ls // GRID_W, GRID_W))
    w_in_b = w_in.astype(BF16)
    w_out_b = w_out.astype(BF16)
    heads_per_cb = LRU_CB // RNN_HEAD_DIM
    wa_bd = (0.5 * _block_diag(lru_wa, heads_per_cb)).astype(BF16)
    wx_bd = (0.5 * _block_diag(lru_wx, heads_per_cb)).astype(BF16)
    ws_b = tmlp_ws.astype(BF16)
    bt = jnp.swapaxes(tmlp_b, 1, 2)
    fnet_bd = _block_diag(fnet_w, D_FNET // FNET_GROUP_DIM)[:, 0]
    wr = jnp.concatenate([router_g, router_e,
                          jnp.zeros((DEPTH, D_MODEL, ROUTER_LANES - N_EGROUPS - N_EXPERTS), F32)], axis=-1)
    br = jnp.concatenate([router_g_b, router_e_b,
                          jnp.zeros((DEPTH, ROUTER_LANES - N_EGROUPS - N_EXPERTS), F32)], axis=-1)
    ew1 = e_w1.reshape(DEPTH * N_EXPERTS, D_MODEL, D_EXPERT)
    ew3 = e_w3.reshape(DEPTH * N_EXPERTS, D_MODEL, D_EXPERT)
    ew2 = e_w2.reshape(DEPTH * N_EXPERTS, D_EXPERT, D_MODEL)

    mods = _modulation(cond8, w_ada, b_ada)
    fa, fb = _fnet_prep(fnet_bd)
    ct_p, st_p = _dft_tables(lp)
    ct_s, st_s = _dft_tables(ls)

    nw = 2 * n_tok // EXPERT_TILE + N_EXPERTS
    h0_p = jnp.zeros((bp, 2, D_RNN), F32)
    g_mix3 = g_mix[:, None, :]
    g_ffn3 = g_ffn[:, None, :]
    conv_b3 = conv_b[:, None, :]
    br3 = br[:, None, :]

    xin = [x_prompt, x_sample]
    pending = None
    states = []
    for l in range(DEPTH):
        paths = []
        cnt = jnp.zeros((SUBLANES, ROUTER_LANES), F32)
        for p, (ct, st, h0, row_base, row_step) in enumerate((
                (ct_p, st_p, h0_p, 0, 0),
                (ct_s, st_s, state_lru[:, l], 1, 1))):
            pe = pos if (p == 1 and l == 0) else None
            if pending is None:
                x = xin[p]
                zr, zuv, va, vb = _in_proj(x, pe, mods, g_mix3, w_in_b, fa, fb,
                                           layer=l, row_base=row_base, row_step=row_step)
            else:
                x, zr, zuv, va, vb = _in_proj(None, None, mods, g_mix3, w_in_b, fa, fb, layer=l,
                                              row_base=row_base, row_step=row_step, moe=pending[p])
            swap = x.shape[1] <= LRU_BATCH_MAX_LEN
            h0 = jnp.swapaxes(h0, 0, 1) if swap else h0
            yr, st_new = _lru_mixer(zr, conv_w, conv_b3, wa_bd, wx_bd, lru_ba, lru_bx, lru_lambda, h0, layer=l)
            st_new = jnp.swapaxes(st_new, 0, 1) if swap else st_new
            yf = _dft_apply(ct, st, va, vb)
            x1, h2, ri, cnt = _out_proj(x, pe, yr, zuv, yf, mods, ws_b, bt, w_out_b, g_ffn3, wr, br3, cnt,
                                        layer=l, row_base=row_base, row_step=row_step)
            paths.append((x1, h2, ri, st_new))
        states.append(paths[0][3])

        slot, counts, offs = _route(paths[0][2].reshape(n_ctx, ROUTER_LANES),
                                    paths[1][2].reshape(n_lat, ROUTER_LANES), cnt)
        items = _work_items(counts, offs, nw, l)
        xsorted = _dispatch(_slot_blocks(slot, ROW_TILE), paths[0][1], paths[1][1])
        ysorted = _experts(items, xsorted, ew1, ew3, ew2)
        slots = (slot[:, :n_ctx], slot[:, n_ctx:])
        pending = [(slots[p], paths[p][0], paths[p][2], ysorted) for p in range(2)]

    gfin = g_final[None, :]
    outs = []
    for p, (ntok, shape, row_base, row_step, tps) in enumerate((
            (n_ctx, (bp, lp, D_MODEL), 0, 0, lp // ROW_TILE),
            (n_lat, (bs, ls, D_MODEL), 1, 1, ls // ROW_TILE))):
        s_p, x1, ri, ys = pending[p]
        outs.append(_combine(_slot_blocks(s_p, ROW_TILE), x1.reshape(ntok, D_MODEL),
                             ri.reshape(ntok, ROUTER_LANES), mods, gfin, ys, layer=DEPTH - 1,
                             row_base=row_base, row_step=row_step, tiles_per_seq=tps).reshape(shape))

    new_state = jnp.stack(states, axis=1).astype(x_prompt.dtype)
    return (outs[0], outs[1], new_state)
```

```python
import functools
import math

import jax
import jax.numpy as jnp
from jax import lax
from jax.experimental import pallas as pl
from jax.experimental.pallas import tpu as pltpu

F32 = jnp.float32
BF16 = jnp.bfloat16

D_MODEL = 1024
DEPTH = 2
GRID_W = 64
D_RNN = 512
RNN_HEAD_DIM = 64
CONV_W = 4
LRU_C = 8.0
D_TMLP = 256
TMLP_HEADS = 4
CHUNK = 128
D_FNET = 256
FNET_GROUP_DIM = 64
D_IN = 2 * D_RNN + 2 * D_TMLP + D_FNET
N_EGROUPS = 4
N_EPG = 8
N_EXPERTS = N_EGROUPS * N_EPG
D_EXPERT = 512
N_MOD = 6
EPS = 1e-6

LANES = 128
SUBLANES = 8
LANE_TILES = D_MODEL // LANES
assert LANE_TILES == SUBLANES
LRU_CB = LANES
LRU_SUB = 4
LRU_BATCH_MAX_LEN = 512
ROW_TILE = 256
PROJ_TILE = 512
EXPERT_TILE = 256
ROUTER_LANES = LANES
NEG_BIG = -1e30
VMEM_LIMIT = 56 * 1024 * 1024


def _cparams(sem):
    return pltpu.CompilerParams(dimension_semantics=sem, vmem_limit_bytes=VMEM_LIMIT)


def _mod_kernel(c_ref, w_ref, b_ref, o_ref):
    c = c_ref[...]
    s = c * jax.nn.sigmoid(c)
    o_ref[0] = jnp.dot(s.astype(BF16), w_ref[0].astype(BF16), preferred_element_type=F32) + b_ref[0]


def _modulation(cond8, w_ada, b_ada):
    tn = 1536
    return pl.pallas_call(
        _mod_kernel,
        grid=(DEPTH, N_MOD * D_MODEL // tn),
        in_specs=[
            pl.BlockSpec((SUBLANES, D_MODEL), lambda l, j: (0, 0)),
            pl.BlockSpec((1, D_MODEL, tn), lambda l, j: (l, 0, j)),
            pl.BlockSpec((1, 1, tn), lambda l, j: (l, 0, j)),
        ],
        out_specs=pl.BlockSpec((1, SUBLANES, tn), lambda l, j: (l, 0, j)),
        out_shape=jax.ShapeDtypeStruct((DEPTH, SUBLANES, N_MOD * D_MODEL), F32),
        compiler_params=_cparams(("arbitrary", "arbitrary")),
        name="modulation",
    )(cond8, w_ada, b_ada.reshape(DEPTH, 1, N_MOD * D_MODEL))


def _fprep_kernel(w_ref, a_ref, b_ref):
    r = lax.broadcasted_iota(jnp.int32, (D_FNET, D_FNET), 0)
    c = lax.broadcasted_iota(jnp.int32, (D_FNET, D_FNET), 1)
    same = (r >> 6) == (c >> 6)
    ph = ((r & 63) * (c & 63)) & 63
    ang = ph.astype(F32) * (2.0 * math.pi / FNET_GROUP_DIM)
    scale = 1.0 / math.sqrt(FNET_GROUP_DIM)
    cm = jnp.where(same, jnp.cos(ang) * scale, 0.0)
    sm = jnp.where(same, jnp.sin(ang) * scale, 0.0)
    w = w_ref[0]
    a_ref[0] = jnp.dot(cm, w, precision=lax.Precision.HIGHEST, preferred_element_type=F32).astype(BF16)
    b_ref[0] = jnp.dot(sm, w, precision=lax.Precision.HIGHEST, preferred_element_type=F32).astype(BF16)


def _fnet_prep(wbd):
    spec = pl.BlockSpec((1, D_FNET, D_FNET), lambda l: (l, 0, 0))
    return pl.pallas_call(
        _fprep_kernel,
        grid=(DEPTH,),
        in_specs=[spec],
        out_specs=[spec, spec],
        out_shape=[jax.ShapeDtypeStruct((DEPTH, D_FNET, D_FNET), BF16)] * 2,
        compiler_params=_cparams(("arbitrary",)),
        name="fnet_prep",
    )(wbd)


TABLE_ROWS = 64


def _base_table_kernel(cj_ref, sj_ref, cm_ref, sm_ref, *, L):
    j = lax.broadcasted_iota(jnp.int32, (TABLE_ROWS, L // 2), 0)
    n = lax.broadcasted_iota(jnp.int32, (TABLE_ROWS, L // 2), 1)
    w = 2.0 * math.pi / L
    fine = ((j * n) & (L - 1)).astype(F32) * w
    coarse = ((j * TABLE_ROWS * n) & (L - 1)).astype(F32) * w
    scale = 1.0 / math.sqrt(L)
    cj_ref[...] = jnp.cos(fine)
    sj_ref[...] = jnp.sin(fine)
    cm_ref[...] = jnp.cos(coarse) * scale
    sm_ref[...] = jnp.sin(coarse) * scale


def _table_kernel(cj_ref, sj_ref, cm_ref, sm_ref, c_ref, s_ref):
    m = pl.program_id(0)
    c0 = cm_ref[pl.ds(m, 1), :]
    s0 = sm_ref[pl.ds(m, 1), :]
    cj = cj_ref[...]
    sj = sj_ref[...]
    c_ref[...] = (cj * c0 - sj * s0).astype(BF16)
    s_ref[...] = (-(sj * c0 + cj * s0)).astype(BF16)


def _dft_tables(L):
    half = L // 2
    small = pl.BlockSpec((TABLE_ROWS, half), lambda *_: (0, 0))
    base = pl.pallas_call(
        functools.partial(_base_table_kernel, L=L),
        out_specs=[small] * 4,
        out_shape=[jax.ShapeDtypeStruct((TABLE_ROWS, half), F32)] * 4,
        compiler_params=pltpu.CompilerParams(vmem_limit_bytes=VMEM_LIMIT),
        name=f"dft_base_tables_{L}",
    )()
    spec = pl.BlockSpec((TABLE_ROWS, half), lambda i: (i, 0))
    return pl.pallas_call(
        _table_kernel,
        grid=(L // TABLE_ROWS,),
        in_specs=[small] * 4,
        out_specs=[spec, spec],
        out_shape=[jax.ShapeDtypeStruct((L, half), BF16)] * 2,
        compiler_params=_cparams(("arbitrary",)),
        name=f"dft_tables_{L}",
    )(*base)


def _dft_kernel(c_ref, s_ref, va_ref, vb_ref, o_ref, vaf, vbf, *, L):
    tk, tc = o_ref.shape
    half = L // 2
    fb = min(256, half)

    @pl.when(pl.program_id(1) == 0)
    def _():
        r = lax.broadcasted_iota(jnp.int32, (fb, fb), 0)
        c = lax.broadcasted_iota(jnp.int32, (fb, fb), 1)
        flip = jnp.where(r + c == fb, 1.0, 0.0).astype(BF16)
        first_row = lax.broadcasted_iota(jnp.int32, (fb, tc), 0) == 0
        for m in range(half // fb):
            pm = L // fb - m - 1
            for src, dst, sign in ((va_ref, vaf, 1.0), (vb_ref, vbf, -1.0)):
                partner = jnp.dot(flip, src[pm * fb:(pm + 1) * fb, :], preferred_element_type=F32)
                if m >= 1:
                    partner = partner + jnp.where(
                        first_row, src[(pm + 1) * fb:(pm + 1) * fb + 1, :].astype(F32), 0.0)
                dst[m * fb:(m + 1) * fb, :] = (src[m * fb:(m + 1) * fb, :].astype(F32) + sign * partner).astype(BF16)

    k = lax.broadcasted_iota(jnp.int32, (tk, 1), 0) + pl.program_id(1) * tk
    alt = (1 - 2 * (k & 1)).astype(F32) * (1.0 / math.sqrt(L))
    o_ref[...] = (jnp.dot(c_ref[...], vaf[...], preferred_element_type=F32)
                  + jnp.dot(s_ref[...], vbf[...], preferred_element_type=F32)
                  + alt * va_ref[half:half + 1, :].astype(F32))


def _dft_apply(ct, st, va, vb):
    L, ncols = va.shape
    half = L // 2
    tk, tc = 256, 1024
    return pl.pallas_call(
        functools.partial(_dft_kernel, L=L),
        grid=(ncols // tc, L // tk),
        in_specs=[
            pl.BlockSpec((tk, half), lambda j, i: (i, 0)),
            pl.BlockSpec((tk, half), lambda j, i: (i, 0)),
            pl.BlockSpec((L, tc), lambda j, i: (0, j)),
            pl.BlockSpec((L, tc), lambda j, i: (0, j)),
        ],
        out_specs=pl.BlockSpec((tk, tc), lambda j, i: (i, j)),
        out_shape=jax.ShapeDtypeStruct((L, ncols), F32),
        scratch_shapes=[pltpu.VMEM((half, tc), BF16), pltpu.VMEM((half, tc), BF16)],
        compiler_params=_cparams(("arbitrary", "arbitrary")),
        name=f"dft_apply_{L}",
    )(ct, st, va, vb)


def _rms_mod(x, g, scale, shift):
    ms = jnp.mean(x * x, axis=-1, keepdims=True)
    return (x * lax.rsqrt(ms + EPS)) * g * (1.0 + scale) + shift


def _pos_table_kernel(s_ref, c_ref):
    n, nf = s_ref.shape
    j = lax.broadcasted_iota(jnp.int32, (n, nf), 0).astype(F32)
    k = lax.broadcasted_iota(jnp.int32, (n, nf), 1).astype(F32)
    ang = j * jnp.exp(k * (-math.log(10000.0) / nf))
    s_ref[...] = jnp.sin(ang)
    c_ref[...] = jnp.cos(ang)


def _pos_tables(n):
    nf = D_MODEL // 4
    return pl.pallas_call(
        _pos_table_kernel,
        out_shape=[jax.ShapeDtypeStruct((n, nf), F32)] * 2,
        name="pos_tables",
    )()


def _pos_tile(ps_ref, pc_ref, t, tm):
    nrow = tm // GRID_W
    nf = D_MODEL // 4

    def rows(tab):
        return jnp.concatenate(
            [jnp.broadcast_to(tab[pl.ds(t * nrow + k, 1), :], (GRID_W, nf)) for k in range(nrow)], axis=0)

    def cols(tab):
        return jnp.concatenate([tab[0:GRID_W, :]] * nrow, axis=0)

    return jnp.concatenate([rows(ps_ref), rows(pc_ref), cols(ps_ref), cols(pc_ref)], axis=-1)


def _in_kernel(*refs, add_pos, pair, row_base, row_step):
    if add_pos:
        x_ref, ps_ref, pc_ref, mod_ref, g_ref, w_ref, a_ref, b_ref, zr_ref, zuv_ref, va_ref, vb_ref = refs
    else:
        x_ref, mod_ref, g_ref, w_ref, a_ref, b_ref, zr_ref, zuv_ref, va_ref, vb_ref = refs
    x = x_ref[0]
    if add_pos:
        x = x + _pos_tile(ps_ref, pc_ref, pl.program_id(1), x.shape[0])
    row = row_base + pl.program_id(0) * row_step
    m = mod_ref[pl.ds(row, 1), :]
    h = _rms_mod(x, g_ref[...], m[:, D_MODEL:2 * D_MODEL], m[:, 0:D_MODEL])
    z = jnp.dot(h.astype(BF16), w_ref[...], preferred_element_type=F32)
    zr_ref[0] = z[:, 0:2 * D_RNN]
    zuv_ref[0] = z[:, 2 * D_RNN:2 * D_RNN + 2 * D_TMLP]
    zf = z[:, 2 * D_RNN + 2 * D_TMLP:D_IN].astype(BF16)
    for ref, w in ((va_ref, a_ref), (vb_ref, b_ref)):
        v = jnp.dot(zf, w[...], preferred_element_type=F32).astype(BF16)
        if pair:
            half = v.shape[0] // 2
            ref[:, 0:D_FNET] = v[0:half]
            ref[:, D_FNET:2 * D_FNET] = v[half:]
        else:
            ref[...] = v


def _pair_short(L, nseq, row_step):
    return 2 * L <= PROJ_TILE and nseq % 2 == 0 and row_step == 0


def _in_proj(x, pos, mods, g, w_in_b, fa, fb, *, layer, row_base, row_step):
    nseq, L, _ = x.shape
    pair = _pair_short(L, nseq, row_step)
    ns, ll = (nseq // 2, 2 * L) if pair else (nseq, L)
    tm = min(ll, PROJ_TILE)
    add_pos = pos is not None
    in_specs = [pl.BlockSpec((1, tm, D_MODEL), lambda b, t: (b, t, 0))]
    args = [x.reshape(ns, ll, D_MODEL)]
    if add_pos:
        in_specs += [pl.BlockSpec(p.shape, lambda b, t: (0, 0)) for p in pos]
        args += list(pos)
    in_specs += [
        _layer_spec((SUBLANES, N_MOD * D_MODEL), layer),
        _layer_spec((1, D_MODEL), layer),
        _layer_spec((D_MODEL, D_IN), layer),
        _layer_spec((D_FNET, D_FNET), layer),
        _layer_spec((D_FNET, D_FNET), layer),
    ]
    args += [mods, g, w_in_b, fa, fb]
    vspec = (pl.BlockSpec((L, 2 * D_FNET), lambda b, t: (0, b)) if pair
             else pl.BlockSpec((tm, D_FNET), lambda b, t: (t, b)))
    zr, zuv, va, vb = pl.pallas_call(
        functools.partial(_in_kernel, add_pos=add_pos, pair=pair, row_base=row_base, row_step=row_step),
        grid=(ns, ll // tm),
        in_specs=in_specs,
        out_specs=[
            pl.BlockSpec((1, tm, 2 * D_RNN), lambda b, t: (b, t, 0)),
            pl.BlockSpec((1, tm, 2 * D_TMLP), lambda b, t: (b, t, 0)),
            vspec, vspec,
        ],
        out_shape=[
            jax.ShapeDtypeStruct((ns, ll, 2 * D_RNN), F32),
            jax.ShapeDtypeStruct((ns, ll, 2 * D_TMLP), F32),
            jax.ShapeDtypeStruct((L, nseq * D_FNET), BF16),
            jax.ShapeDtypeStruct((L, nseq * D_FNET), BF16),
        ],
        compiler_params=_cparams(("arbitrary", "arbitrary")),
        name=f"in_proj_{L}",
    )(*args)
    return zr.reshape(nseq, L, 2 * D_RNN), zuv.reshape(nseq, L, 2 * D_TMLP), va, vb


def _gelu_tanh(x):
    return 0.5 * x * (1.0 + jnp.tanh(math.sqrt(2.0 / math.pi) * (x + 0.044715 * (x * x * x))))


def _rows_to_tile(rows):
    sub = lax.broadcasted_iota(jnp.int32, (SUBLANES, LANES), 0)
    out = jnp.zeros((SUBLANES, LANES), F32)
    for s, r in enumerate(rows):
        out = jnp.where(sub == s, jnp.broadcast_to(r, (SUBLANES, LANES)), out)
    return out


def _lru_kernel(xr_ref, gr_ref, cw_ref, cb_ref, wa_ref, wx_ref, ba_ref, bx_ref, lam_ref, h0_ref,
                y_ref, st_ref, xnat, pext, af, bf, ab, bb, hfo, pfo, hbo, pbo, hnat, *, L, batched):
    S = L if batched else L // SUBLANES
    pitch = S + SUBLANES
    n = S * SUBLANES
    chunk = 256

    def seg(ref, s):
        return ref.at[s] if batched else ref.at[0, s * S:(s + 1) * S]

    for s in range(SUBLANES):
        xnat[s * pitch:s * pitch + S, :] = seg(xr_ref, s)[...]

    def perm_in(j, c):
        dst = pl.multiple_of((j + 2) * SUBLANES, SUBLANES)
        pext[pl.ds(dst, SUBLANES), :] = xnat[pl.ds(j, SUBLANES, stride=pitch), :]
        return c

    lax.fori_loop(0, S, perm_in, 0, unroll=8)

    sub = lax.broadcasted_iota(jnp.int32, (SUBLANES, LANES), 0)

    def from_prev_segment(v):
        return jnp.where(sub == 0, 0.0, pltpu.roll(v, 1, axis=0))

    def from_next_segment(v):
        return jnp.where(sub == SUBLANES - 1, 0.0, pltpu.roll(v, SUBLANES - 1, axis=0))

    if batched:
        pext[0:16, :] = jnp.zeros((16, LANES), F32)
        pext[(S + 2) * 8:(S + 3) * 8, :] = jnp.zeros((8, LANES), F32)
    else:
        pext[0:8, :] = from_prev_segment(pext[S * 8:(S + 1) * 8, :])
        pext[8:16, :] = from_prev_segment(pext[(S + 1) * 8:(S + 2) * 8, :])
        pext[(S + 2) * 8:(S + 3) * 8, :] = from_next_segment(pext[16:24, :])

    lam = lam_ref[...]
    nl = -lam
    sp = jnp.maximum(nl, 0.0) + jnp.log1p(jnp.exp(-jnp.abs(nl)))
    c_la = (-0.5 * LRU_C) * sp
    ba_h = 0.5 * ba_ref[...]
    bx_h = 0.5 * bx_ref[...]
    a_refs = (af, ab)
    b_refs = (bf, bb)

    def gates(i, c):
        base = pl.multiple_of(i * chunk, chunk)
        xc = (cw_ref[0:1, :] * pext[pl.ds(base, chunk), :]
              + cw_ref[1:2, :] * pext[pl.ds(base + 8, chunk), :]
              + cw_ref[2:3, :] * pext[pl.ds(base + 16, chunk), :]
              + cw_ref[3:4, :] * pext[pl.ds(base + 24, chunk), :]
              + cb_ref[...])
        xcb = xc.astype(BF16)
        xh = 0.5 * xc
        for d in range(2):
            tr = jnp.tanh(jnp.dot(xcb, wa_ref[d, 0], preferred_element_type=F32) + ba_h[d:d + 1, :])
            ti = jnp.tanh(jnp.dot(xcb, wx_ref[d, 0], preferred_element_type=F32) + bx_h[d:d + 1, :])
            log_a = c_la[d:d + 1, :] * (1.0 + tr)
            a = jnp.exp(log_a)
            v = jnp.tanh(log_a) * (-1.0 - a * a)
            coef = jnp.where(v > 0.0, v * lax.rsqrt(v), 0.0)
            a_refs[d][pl.ds(base, chunk), :] = a
            b_refs[d][pl.ds(base, chunk), :] = coef * ((1.0 + ti) * xh)
        return c

    lax.fori_loop(0, n // chunk, gates, 0, unroll=min(4, n // chunk))

    nq = LRU_SUB
    sq = S // nq

    def scan(i, carry):
        out = []
        for q in range(nq):
            hf, pf, hb, pb = carry[4 * q:4 * q + 4]
            jf = pl.multiple_of((q * sq + i) * SUBLANES, SUBLANES)
            jb = pl.multiple_of(((q + 1) * sq - 1 - i) * SUBLANES, SUBLANES)
            a1 = af[pl.ds(jf, SUBLANES), :]
            hf = a1 * hf + bf[pl.ds(jf, SUBLANES), :]
            pf = a1 * pf
            hfo[pl.ds(jf, SUBLANES), :] = hf
            pfo[pl.ds(jf, SUBLANES), :] = pf
            a2 = ab[pl.ds(jb, SUBLANES), :]
            hb = a2 * hb + bb[pl.ds(jb, SUBLANES), :]
            pb = a2 * pb
            hbo[pl.ds(jb, SUBLANES), :] = hb
            pbo[pl.ds(jb, SUBLANES), :] = pb
            out += [hf, pf, hb, pb]
        return tuple(out)

    zero = jnp.zeros((SUBLANES, LANES), F32)
    one = jnp.ones((SUBLANES, LANES), F32)
    ends = lax.fori_loop(0, sq, scan, (zero, one, zero, one) * nq, unroll=min(sq, 4))
    end_f = [(ends[4 * q], ends[4 * q + 1]) for q in range(nq)]
    end_b = [(ends[4 * q + 2], ends[4 * q + 3]) for q in range(nq)]

    hf, pf = end_f[0]
    for q in range(1, nq):
        hf, pf = end_f[q][1] * hf + end_f[q][0], end_f[q][1] * pf
    hb, pb = end_b[nq - 1]
    for q in range(nq - 2, -1, -1):
        hb, pb = end_b[q][1] * hb + end_b[q][0], end_b[q][1] * pb

    if batched:
        start_f = h0_ref[0]
        start_b = h0_ref[1]
        st_ref[0] = pf * start_f + hf
        st_ref[1] = pb * start_b + hb
    else:
        rows_f = [h0_ref[0, 0:1, :]]
        for s in range(1, SUBLANES):
            rows_f.append(pf[s - 1:s, :] * rows_f[-1] + hf[s - 1:s, :])
        st_ref[0, 0:1, :] = pf[7:8, :] * rows_f[7] + hf[7:8, :]
        rows_b = [None] * SUBLANES
        rows_b[7] = h0_ref[0, 1:2, :]
        for s in range(SUBLANES - 2, -1, -1):
            rows_b[s] = pb[s + 1:s + 2, :] * rows_b[s + 1] + hb[s + 1:s + 2, :]
        st_ref[0, 1:2, :] = pb[0:1, :] * rows_b[0] + hb[0:1, :]
        start_f = _rows_to_tile(rows_f)
        start_b = _rows_to_tile(rows_b)

    init_f = [start_f]
    for q in range(nq - 1):
        init_f.append(end_f[q][1] * init_f[q] + end_f[q][0])
    init_b = [None] * nq
    init_b[nq - 1] = start_b
    for q in range(nq - 1, 0, -1):
        init_b[q - 1] = end_b[q][1] * init_b[q] + end_b[q][0]

    for q in range(nq):
        def perm_out(j, c, q=q):
            src = pl.multiple_of(j * SUBLANES, SUBLANES)
            v = (hfo[pl.ds(src, SUBLANES), :] + pfo[pl.ds(src, SUBLANES), :] * init_f[q]
                 + hbo[pl.ds(src, SUBLANES), :] + pbo[pl.ds(src, SUBLANES), :] * init_b[q])
            hnat[pl.ds(j, SUBLANES, stride=pitch), :] = v
            return c

        lax.fori_loop(q * sq, (q + 1) * sq, perm_out, 0, unroll=min(sq, 8))

    for s in range(SUBLANES):
        seg(y_ref, s)[...] = hnat[s * pitch:s * pitch + S, :] * _gelu_tanh(seg(gr_ref, s)[...])


def _lru_mixer(zr, conv_w, conv_b, wa_bd, wx_bd, ba, bx, lam, h0, *, layer):
    nseq, L, _ = zr.shape
    ncb = D_RNN // LRU_CB
    batched = L <= LRU_BATCH_MAX_LEN
    nb = SUBLANES if batched else 1
    S = L if batched else L // SUBLANES
    pitch = S + SUBLANES
    n = S * SUBLANES
    vec2 = pl.BlockSpec((None, 2, LRU_CB), lambda b, c: (layer, 0, c))
    wspec = pl.BlockSpec((None, 2, 1, LRU_CB, LRU_CB), lambda b, c: (layer, 0, c, 0, 0))
    if batched:
        state = pl.BlockSpec((2, nb, LRU_CB), lambda b, c: (0, b, c))
        state_shape = (2, nseq, D_RNN)
    else:
        state = pl.BlockSpec((1, 2, LRU_CB), lambda b, c: (b, 0, c))
        state_shape = (nseq, 2, D_RNN)
    return pl.pallas_call(
        functools.partial(_lru_kernel, L=L, batched=batched),
        grid=(nseq // nb, ncb),
        in_specs=[
            pl.BlockSpec((nb, L, LRU_CB), lambda b, c: (b, 0, c)),
            pl.BlockSpec((nb, L, LRU_CB), lambda b, c: (b, 0, c + ncb)),
            pl.BlockSpec((None, CONV_W, LRU_CB), lambda b, c: (layer, 0, c)),
            pl.BlockSpec((None, 1, LRU_CB), lambda b, c: (layer, 0, c)),
            wspec, wspec, vec2, vec2, vec2,
            state,
        ],
        out_specs=[pl.BlockSpec((nb, L, LRU_CB), lambda b, c: (b, 0, c)), state],
        out_shape=[
            jax.ShapeDtypeStruct((nseq, L, D_RNN), F32),
            jax.ShapeDtypeStruct(state_shape, F32),
        ],
        scratch_shapes=[
            pltpu.VMEM((SUBLANES * pitch, LANES), F32),
            pltpu.VMEM(((S + 3) * SUBLANES, LANES), F32),
            *([pltpu.VMEM((n, LANES), F32)] * 8),
            pltpu.VMEM((SUBLANES * pitch, LANES), F32),
        ],
        compiler_params=_cparams(("arbitrary", "arbitrary")),
        name=f"lru_mixer_{L}",
    )(zr, zr, conv_w, conv_b, wa_bd, wx_bd, ba, bx, lam, h0)


def _out_kernel(*refs, add_pos, pair, row_base, row_step):
    if add_pos:
        (x_ref, ps_ref, pc_ref, yr_ref, zuv_ref, yf_ref, mod_ref, ws_ref, bt_ref, wo_ref, g_ref, wr_ref, br_ref,
         cin_ref, x1_ref, h2_ref, ri_ref, cnt_ref, carry) = refs
    else:
        (x_ref, yr_ref, zuv_ref, yf_ref, mod_ref, ws_ref, bt_ref, wo_ref, g_ref, wr_ref, br_ref, cin_ref,
         x1_ref, h2_ref, ri_ref, cnt_ref, carry) = refs
    tm = x_ref.shape[1]
    x = x_ref[0]
    if add_pos:
        x = x + _pos_tile(ps_ref, pc_ref, pl.program_id(1), tm)
    row = row_base + pl.program_id(0) * row_step
    m = mod_ref[pl.ds(row, 1), :]
    g1 = m[:, 2 * D_MODEL:3 * D_MODEL]
    sh2 = m[:, 3 * D_MODEL:4 * D_MODEL]
    sc2 = m[:, 4 * D_MODEL:5 * D_MODEL]

    head = lax.broadcasted_iota(jnp.int32, (CHUNK, D_TMLP), 1) >> 6
    yt_parts = []
    for ci in range(tm // CHUNK):
        u = zuv_ref[0, ci * CHUNK:(ci + 1) * CHUNK, 0:D_TMLP]
        v = zuv_ref[0, ci * CHUNK:(ci + 1) * CHUNK, D_TMLP:2 * D_TMLP].astype(BF16)
        s = jnp.zeros((CHUNK, D_TMLP), F32)
        for h in range(TMLP_HEADS):
            sh = jnp.dot(ws_ref[h], v, preferred_element_type=F32) + bt_ref[:, h:h + 1]
            s = jnp.where(head == h, sh, s)
        yt_parts.append(u * s)
    yt = jnp.concatenate(yt_parts, axis=0) if len(yt_parts) > 1 else yt_parts[0]

    if pair:
        yf = jnp.concatenate([yf_ref[:, 0:D_FNET], yf_ref[:, D_FNET:2 * D_FNET]], axis=0)
    else:
        yf = yf_ref[...]
    y = (jnp.dot(yr_ref[0].astype(BF16), wo_ref[0:D_RNN, :], preferred_element_type=F32)
         + jnp.dot(yt.astype(BF16), wo_ref[D_RNN:D_RNN + D_TMLP, :], preferred_element_type=F32)
         + jnp.dot(yf.astype(BF16), wo_ref[D_RNN + D_TMLP:D_MODEL, :], preferred_element_type=F32))
    x1 = x + g1 * y
    x1_ref[0] = x1
    h2 = _rms_mod(x1, g_ref[...], sc2, sh2)
    _store_token_major(h2_ref, h2)

    wr = wr_ref[...]
    w_hi = wr.astype(BF16)
    w_lo = (wr - w_hi.astype(F32)).astype(BF16)
    h_hi = h2.astype(BF16)
    h_lo = (h2 - h_hi.astype(F32)).astype(BF16)
    p_hi = jnp.dot(h_hi, jnp.concatenate([w_hi, w_lo], axis=-1), preferred_element_type=F32)
    p_lo = jnp.dot(h_lo, w_hi, preferred_element_type=F32)
    logits = p_hi[:, 0:ROUTER_LANES] + p_hi[:, ROUTER_LANES:2 * ROUTER_LANES] + p_lo + br_ref[...]
    lane = lax.broadcasted_iota(jnp.int32, (tm, ROUTER_LANES), 1)
    lane_f = lane.astype(F32)
    is_g = lane < N_EGROUPS
    gl = jnp.where(is_g, logits, NEG_BIG)
    gmax = jnp.max(gl, axis=-1, keepdims=True)
    gsel = jnp.min(jnp.where(gl == gmax, lane_f, 1e4), axis=-1, keepdims=True)
    pg = 1.0 / jnp.sum(jnp.where(is_g, jnp.exp(logits - gmax), 0.0), axis=-1, keepdims=True)
    grp_f = ((lane - N_EGROUPS) >> 3).astype(F32)
    emask = (lane >= N_EGROUPS) & (lane < N_EGROUPS + N_EXPERTS) & (grp_f == gsel)
    el = jnp.where(emask, logits, NEG_BIG)
    v1 = jnp.max(el, axis=-1, keepdims=True)
    i1 = jnp.min(jnp.where(el == v1, lane_f, 1e4), axis=-1, keepdims=True)
    el2 = jnp.where(lane_f == i1, NEG_BIG, el)
    v2 = jnp.max(el2, axis=-1, keepdims=True)
    i2 = jnp.min(jnp.where(el2 == v2, lane_f, 1e4), axis=-1, keepdims=True)
    e2x = jnp.exp(v2 - v1)
    fw1 = 1.0 / (1.0 + e2x)
    fw2 = e2x * fw1
    @pl.when((pl.program_id(0) == 0) & (pl.program_id(1) == 0))
    def _():
        carry[...] = cin_ref[...]

    e1 = i1 - N_EGROUPS
    e2 = i2 - N_EGROUPS
    m1 = lane_f == e1
    m2 = lane_f == e2
    oh = jnp.where(m1 | m2, 1.0, 0.0)
    r_i = lax.broadcasted_iota(jnp.int32, (tm, tm), 0)
    c_i = lax.broadcasted_iota(jnp.int32, (tm, tm), 1)
    tri = jnp.where(c_i < r_i, 1.0, 0.0).astype(BF16)
    before = jnp.dot(tri, oh.astype(BF16), preferred_element_type=F32) + carry[0:1, :]
    rank1 = jnp.sum(jnp.where(m1, before, 0.0), axis=-1, keepdims=True)
    rank2 = jnp.sum(jnp.where(m2, before, 0.0), axis=-1, keepdims=True)
    total = carry[0:1, :] + jnp.sum(oh, axis=0, keepdims=True)
    carry[0:1, :] = total
    cnt_ref[...] = jnp.broadcast_to(total, cnt_ref.shape)

    vals = (e1, e2, pg * fw1, pg * fw2, rank1, rank2)
    ri = jnp.zeros((tm, ROUTER_LANES), F32)
    for k, v in enumerate(vals):
        ri = jnp.where(lane == k, v, ri)
    ri_ref[0] = ri


def _layer_spec(shape, layer):
    zeros = (0,) * len(shape)
    return pl.BlockSpec((None, *shape), lambda *_: (layer, *zeros))


def _out_proj(x, pos, yr, zuv, yf, mods, ws_b, bt, wo_b, g, wr, br, cin, *, layer, row_base, row_step):
    nseq, L, _ = x.shape
    pair = _pair_short(L, nseq, row_step)
    ns, ll = (nseq // 2, 2 * L) if pair else (nseq, L)
    tm = min(ll, PROJ_TILE)
    add_pos = pos is not None
    in_specs = [pl.BlockSpec((1, tm, D_MODEL), lambda b, t: (b, t, 0))]
    args = [x.reshape(ns, ll, D_MODEL)]
    if add_pos:
        in_specs += [pl.BlockSpec(p.shape, lambda b, t: (0, 0)) for p in pos]
        args += list(pos)
    in_specs += [
        pl.BlockSpec((1, tm, D_RNN), lambda b, t: (b, t, 0)),
        pl.BlockSpec((1, tm, 2 * D_TMLP), lambda b, t: (b, t, 0)),
        (pl.BlockSpec((L, 2 * D_FNET), lambda b, t: (0, b)) if pair
         else pl.BlockSpec((tm, D_FNET), lambda b, t: (t, b))),
        _layer_spec((SUBLANES, N_MOD * D_MODEL), layer),
        _layer_spec((TMLP_HEADS, CHUNK, CHUNK), layer),
        _layer_spec((CHUNK, TMLP_HEADS), layer),
        _layer_spec((D_MODEL, D_MODEL), layer),
        _layer_spec((1, D_MODEL), layer),
        _layer_spec((D_MODEL, ROUTER_LANES), layer),
        _layer_spec((1, ROUTER_LANES), layer),
        pl.BlockSpec((SUBLANES, ROUTER_LANES), lambda b, t: (0, 0)),
    ]
    args += [yr.reshape(ns, ll, D_RNN), zuv.reshape(ns, ll, 2 * D_TMLP), yf, mods, ws_b, bt, wo_b, g, wr, br, cin]
    tok = pl.BlockSpec((1, tm, D_MODEL), lambda b, t: (b, t, 0))
    x1, h2, ri, cnt = pl.pallas_call(
        functools.partial(_out_kernel, add_pos=add_pos, pair=pair, row_base=row_base, row_step=row_step),
        grid=(ns, ll // tm),
        in_specs=in_specs,
        out_specs=[tok, pl.BlockSpec((tm * LANE_TILES, LANES), lambda b, t: (b * (ll // tm) + t, 0)),
                   pl.BlockSpec((1, tm, ROUTER_LANES), lambda b, t: (b, t, 0)),
                   pl.BlockSpec((SUBLANES, ROUTER_LANES), lambda b, t: (0, 0))],
        out_shape=[
            jax.ShapeDtypeStruct((ns, ll, D_MODEL), F32),
            jax.ShapeDtypeStruct((nseq * L * LANE_TILES, LANES), F32),
            jax.ShapeDtypeStruct((ns, ll, ROUTER_LANES), F32),
            jax.ShapeDtypeStruct((SUBLANES, ROUTER_LANES), F32),
        ],
        scratch_shapes=[pltpu.VMEM((SUBLANES, ROUTER_LANES), F32)],
        compiler_params=_cparams(("arbitrary", "arbitrary")),
        name=f"out_proj_{L}",
    )(*args)
    return x1.reshape(nseq, L, D_MODEL), h2, ri.reshape(nseq, L, ROUTER_LANES), cnt


def _store_token_major(ref, x):
    tm = x.shape[0]
    for j in range(LANE_TILES):
        ref[pl.ds(j, tm, stride=LANE_TILES), :] = x[:, j * LANES:(j + 1) * LANES]


def _load_token_major(ref):
    tm = ref.shape[0] // LANE_TILES
    return jnp.concatenate([ref[pl.ds(j, tm, stride=LANE_TILES), :] for j in range(LANE_TILES)], axis=-1)


def _token_copy(src_ref, src_tok, dst_ref, dst_tok, sem):
    return pltpu.make_async_copy(src_ref.at[pl.ds(pl.multiple_of(src_tok, LANE_TILES), LANE_TILES)],
                                 dst_ref.at[pl.ds(pl.multiple_of(dst_tok, LANE_TILES), LANE_TILES)], sem)


def _scatter_rows(pos_ref, h_ref, xs_ref, sem):
    rows = h_ref.shape[0]

    for r in range(0, rows, LANE_TILES):
        _token_copy(h_ref, r, xs_ref, pos_ref[0, 0, r // LANE_TILES], sem).start(priority=0)
        _token_copy(h_ref, r, xs_ref, pos_ref[0, 1, r // LANE_TILES], sem).start(priority=1)
    for _ in range(2):
        pltpu.make_async_copy(h_ref, xs_ref.at[pl.ds(0, rows)], sem).wait()


def _dispatch_kernel(pos_ref, hp_ref, hs_ref, xs_ref, sem, *, n_first):
    i = pl.program_id(0)

    @pl.when(i < n_first)
    def _():
        _scatter_rows(pos_ref, hp_ref, xs_ref, sem)

    @pl.when(i >= n_first)
    def _():
        _scatter_rows(pos_ref, hs_ref, xs_ref, sem)


def _dispatch(pos, h_first, h_second):
    tm = ROW_TILE
    blk = tm * LANE_TILES
    n_first = h_first.shape[0] // blk
    n_second = h_second.shape[0] // blk
    nrows = 2 * (h_first.shape[0] + h_second.shape[0])
    return pl.pallas_call(
        functools.partial(_dispatch_kernel, n_first=n_first),
        grid=(n_first + n_second,),
        in_specs=[
            pl.BlockSpec((1, 2, tm), lambda i: (i, 0, 0), memory_space=pltpu.SMEM),
            pl.BlockSpec((blk, LANES), lambda i: (jnp.minimum(i, n_first - 1), 0)),
            pl.BlockSpec((blk, LANES), lambda i: (jnp.maximum(i - n_first, 0), 0)),
        ],
        out_specs=pl.BlockSpec(memory_space=pl.ANY),
        out_shape=jax.ShapeDtypeStruct((nrows, LANES), F32),
        scratch_shapes=[pltpu.SemaphoreType.DMA(())],
        compiler_params=_cparams(("arbitrary",)),
        name="moe_dispatch",
    )(pos, h_first, h_second)


WEIGHT_LEAD = (3, 2, 1)
WEIGHT_SLOTS = 4
LEAD_ITEMS = max(WEIGHT_LEAD)


def _expert_kernel(wt_ref, we_ref, lo_ref, hi_ref, first_ref, ord_ref, xs_ref, w1_ref, w3_ref, w2_ref, ys_ref,
                   w1b, w3b, w2b):
    s = pl.program_id(0)
    last = pl.num_programs(0) - 1
    tm = EXPERT_TILE

    def stage(lead, src, dst):
        cur = jnp.minimum(s + lead, last)
        prv = jnp.minimum(s + lead - 1, last)

        @pl.when((s == 0) | (we_ref[cur] != we_ref[prv]))
        def _():
            dst[ord_ref[cur] & (WEIGHT_SLOTS - 1)] = src[0].astype(BF16)

    for lead, src, dst in zip(WEIGHT_LEAD, (w1_ref, w3_ref, w2_ref), (w1b, w3b, w2b)):
        stage(lead, src, dst)

    lo = lo_ref[s]
    hi = hi_ref[s]
    full = (lo == 0) & (hi == tm)

    @pl.when((first_ref[s] == 1) & jnp.logical_not(full))
    def _():
        ys_ref[...] = jnp.zeros_like(ys_ref)

    slot = ord_ref[s] & (WEIGHT_SLOTS - 1)
    half = tm // 2

    def run(t0, nt, merge):
        xs_part = xs_ref.at[pl.ds(t0 * LANE_TILES, nt * LANE_TILES)]
        ys_part = ys_ref.at[pl.ds(t0 * LANE_TILES, nt * LANE_TILES)]
        x = _load_token_major(xs_part).astype(BF16)
        a = jnp.dot(x, w1b[slot], preferred_element_type=F32)
        b = jnp.dot(x, w3b[slot], preferred_element_type=F32)
        hid = (a * jax.nn.sigmoid(a)) * b
        res = jnp.dot(hid.astype(BF16), w2b[slot], preferred_element_type=F32)
        if merge:
            rows = lax.broadcasted_iota(jnp.int32, res.shape, 0) + t0
            res = jnp.where((rows >= lo) & (rows < hi), res, _load_token_major(ys_part))
        _store_token_major(ys_part, res)

    in_low = hi <= half
    in_high = lo >= half
    some = hi > lo
    pl.when(full)(lambda: run(0, tm, False))
    pl.when(some & jnp.logical_not(full) & in_low)(lambda: run(0, half, True))
    pl.when(some & in_high)(lambda: run(half, half, True))
    pl.when(some & jnp.logical_not(full) & jnp.logical_not(in_low) & jnp.logical_not(in_high))(
        lambda: run(0, tm, True))


def _experts(items, xs, w1, w3, w2):
    nw = items[0].shape[0]

    def row(s, wt, we, lo, hi, fi, od):
        return (wt[s], 0)

    def weight(lead):
        return lambda s, wt, we, lo, hi, fi, od: (we[jnp.minimum(s + lead, nw - 1)], 0, 0)

    return pl.pallas_call(
        _expert_kernel,
        grid_spec=pltpu.PrefetchScalarGridSpec(
            num_scalar_prefetch=6,
            grid=(nw,),
            in_specs=[
                pl.BlockSpec((EXPERT_TILE * LANE_TILES, LANES), row),
                pl.BlockSpec((1, D_MODEL, D_EXPERT), weight(WEIGHT_LEAD[0])),
                pl.BlockSpec((1, D_MODEL, D_EXPERT), weight(WEIGHT_LEAD[1])),
                pl.BlockSpec((1, D_EXPERT, D_MODEL), weight(WEIGHT_LEAD[2])),
            ],
            out_specs=pl.BlockSpec((EXPERT_TILE * LANE_TILES, LANES), row),
            scratch_shapes=[
                pltpu.VMEM((WEIGHT_SLOTS, D_MODEL, D_EXPERT), BF16),
                pltpu.VMEM((WEIGHT_SLOTS, D_MODEL, D_EXPERT), BF16),
                pltpu.VMEM((WEIGHT_SLOTS, D_EXPERT, D_MODEL), BF16),
            ],
        ),
        out_shape=jax.ShapeDtypeStruct(xs.shape, F32),
        compiler_params=_cparams(("arbitrary",)),
        name="moe_experts",
    )(*items, xs, w1, w3, w2)


def _combine_kernel(pos_ref, posn_ref, x1_ref, ri_ref, mod_ref, gf_ref, ys_ref, o_ref, ybuf, sem,
                    *, row_base, row_step, tiles_per_seq, final):
    tm = x1_ref.shape[0]
    i = pl.program_id(0)
    n = pl.num_programs(0)

    def gather(p_ref, slot):
        for r in range(tm):
            _token_copy(ys_ref, p_ref[0, 0, r], ybuf.at[slot, 0], r * LANE_TILES, sem.at[slot]).start(priority=0)
            _token_copy(ys_ref, p_ref[0, 1, r], ybuf.at[slot, 1], r * LANE_TILES, sem.at[slot]).start(priority=1)

    def drain(slot):
        for k in range(2):
            pltpu.make_async_copy(ys_ref.at[pl.ds(0, tm * LANE_TILES)], ybuf.at[slot, k], sem.at[slot]).wait()

    def tile(slot):
        if slot == 0:
            @pl.when(i == 0)
            def _():
                gather(pos_ref, 0)

        @pl.when(i + 1 < n)
        def _():
            gather(posn_ref, 1 - slot)

        drain(slot)
        row = row_base + (i // tiles_per_seq) * row_step
        g2 = mod_ref[pl.ds(row, 1), 5 * D_MODEL:6 * D_MODEL]
        ri = ri_ref[...]
        y = (ri[:, 2:3] * _load_token_major(ybuf.at[slot, 0])
             + ri[:, 3:4] * _load_token_major(ybuf.at[slot, 1]))
        x2 = x1_ref[...] + g2 * y
        if final:
            ms = jnp.mean(x2 * x2, axis=-1, keepdims=True)
            o_ref[...] = (x2 * lax.rsqrt(ms + EPS)) * gf_ref[...]
        else:
            o_ref[...] = x2

    for slot in range(2):
        pl.when((i & 1) == slot)(functools.partial(tile, slot))


def _combine(pos, x1, ri, mods, g_final, ys, *, layer, row_base, row_step, tiles_per_seq, final):
    ntok = x1.shape[0]
    tm = ROW_TILE
    tok = pl.BlockSpec((tm, D_MODEL), lambda i: (i, 0))
    return pl.pallas_call(
        functools.partial(_combine_kernel, row_base=row_base, row_step=row_step, tiles_per_seq=tiles_per_seq,
                          final=final),
        grid=(ntok // tm,),
        in_specs=[
            pl.BlockSpec((1, 2, tm), lambda i: (i, 0, 0), memory_space=pltpu.SMEM),
            pl.BlockSpec((1, 2, tm), lambda i: (jnp.minimum(i + 1, ntok // tm - 1), 0, 0), memory_space=pltpu.SMEM),
            tok,
            pl.BlockSpec((tm, ROUTER_LANES), lambda i: (i, 0)),
            _layer_spec((SUBLANES, N_MOD * D_MODEL), layer),
            pl.BlockSpec((1, D_MODEL), lambda i: (0, 0)),
            pl.BlockSpec(memory_space=pl.ANY),
        ],
        out_specs=tok,
        out_shape=jax.ShapeDtypeStruct((ntok, D_MODEL), F32),
        scratch_shapes=[
            pltpu.VMEM((2, 2, tm * LANE_TILES, LANES), F32),
            pltpu.SemaphoreType.DMA((2,)),
        ],
        compiler_params=_cparams(("arbitrary",)),
        name=f"moe_combine_{ntok}",
    )(pos, pos, x1, ri, mods, g_final, ys)


def _route(ri_first, ri_second, cnt):
    counts = cnt[0, 0:N_EXPERTS].astype(jnp.int32)
    offs = jnp.cumsum(counts) - counts
    info = jnp.concatenate([ri_first[:, 0:SUBLANES], ri_second[:, 0:SUBLANES]], axis=0).T
    vals = info.astype(jnp.int32)
    base = jnp.zeros_like(vals)
    for k in range(N_EXPERTS):
        base = jnp.where(vals == k, offs[k], base)
    slot = (base[0:2] + vals[4:6]) * LANE_TILES
    n_tiles = slot.shape[1] // ROW_TILE
    pos_tiles = jnp.swapaxes(slot.reshape(2, n_tiles, ROW_TILE), 0, 1)
    return pos_tiles, counts, offs


def _work_items(counts, offs, nw, layer):
    tm = EXPERT_TILE
    first_tile = offs // tm
    last_tile = (offs + counts - 1) // tm
    n_e = jnp.where(counts > 0, last_tile - first_tile + 1, 0)
    w_end = jnp.cumsum(n_e)
    w_start = w_end - n_e
    total = w_end[-1]
    w = jnp.arange(nw, dtype=jnp.int32)
    wc = jnp.minimum(w, total - 1)
    e_w = jnp.sum((wc[:, None] >= w_end[None, :]).astype(jnp.int32), axis=-1)
    sel = (e_w[:, None] == jnp.arange(N_EXPERTS, dtype=jnp.int32)).astype(jnp.int32)
    pick = lambda v: jnp.sum(sel * v[None, :], axis=-1)
    off_w = pick(offs)
    tile_w = pick(first_tile) + (wc - pick(w_start))
    lo = jnp.clip(off_w - tile_w * tm, 0, tm)
    hi = jnp.clip(off_w + pick(counts) - tile_w * tm, 0, tm)
    valid = w < total
    lo = jnp.where(valid, lo, 0)
    hi = jnp.where(valid, hi, 0)
    prev_tile = jnp.concatenate([jnp.full((1,), -1, tile_w.dtype), tile_w[:-1]])
    first = tile_w != prev_tile
    ordinal = pick(jnp.cumsum((counts > 0).astype(jnp.int32)) - 1)

    def lead(a, fill=None):
        head = jnp.broadcast_to(a[0] if fill is None else jnp.asarray(fill, a.dtype), (LEAD_ITEMS,))
        return jnp.concatenate([head, a]).astype(jnp.int32)

    return (lead(tile_w), lead(e_w + layer * N_EXPERTS), lead(lo, 0), lead(hi, 0), lead(first, 0), lead(ordinal))


def _block_diag(w, nblk):
    *lead, H, d, _ = w.shape
    w = w.reshape(*lead, H // nblk, nblk, d, d)
    eye = jnp.eye(nblk, dtype=w.dtype)
    out = jnp.einsum('...gij,gh->...gihj', w, eye)
    return out.reshape(*lead, H // nblk, nblk * d, nblk * d)


def kernel(x_prompt, x_sample, state_lru, c, c_ctx, w_ada, b_ada, g_mix, g_ffn, g_final, w_in, w_out, conv_w, conv_b, lru_wa, lru_ba, lru_wx, lru_bx, lru_lambda, tmlp_ws, tmlp_b, fnet_w, router_g, router_g_b, router_e, router_e_b, e_w1, e_w3, e_w2):
    bp, lp, _ = x_prompt.shape
    bs, ls, _ = x_sample.shape
    n_ctx = bp * lp
    n_lat = bs * ls
    n_tok = n_ctx + n_lat

    cond8 = jnp.concatenate([c_ctx[None, :], c, jnp.zeros((SUBLANES - 1 - bs, D_MODEL), F32)], axis=0)
    pos = _pos_tables(max(ls // GRID_W, GRID_W))
    w_in_b = w_in.astype(BF16)
    w_out_b = w_out.astype(BF16)
    heads_per_cb = LRU_CB // RNN_HEAD_DIM
    wa_bd = (0.5 * _block_diag(lru_wa, heads_per_cb)).astype(BF16)
    wx_bd = (0.5 * _block_diag(lru_wx, heads_per_cb)).astype(BF16)
    ws_b = tmlp_ws.astype(BF16)
    bt = jnp.swapaxes(tmlp_b, 1, 2)
    fnet_bd = _block_diag(fnet_w, D_FNET // FNET_GROUP_DIM)[:, 0]
    wr = jnp.concatenate([router_g, router_e,
                          jnp.zeros((DEPTH, D_MODEL, ROUTER_LANES - N_EGROUPS - N_EXPERTS), F32)], axis=-1)
    br = jnp.concatenate([router_g_b, router_e_b,
                          jnp.zeros((DEPTH, ROUTER_LANES - N_EGROUPS - N_EXPERTS), F32)], axis=-1)
    ew1 = e_w1.reshape(DEPTH * N_EXPERTS, D_MODEL, D_EXPERT)
    ew3 = e_w3.reshape(DEPTH * N_EXPERTS, D_MODEL, D_EXPERT)
    ew2 = e_w2.reshape(DEPTH * N_EXPERTS, D_EXPERT, D_MODEL)

    mods = _modulation(cond8, w_ada, b_ada)
    fa, fb = _fnet_prep(fnet_bd)
    ct_p, st_p = _dft_tables(lp)
    ct_s, st_s = _dft_tables(ls)

    nw = 2 * n_tok // EXPERT_TILE + N_EXPERTS
    h0_p = jnp.zeros((bp, 2, D_RNN), F32)
    g_mix3 = g_mix[:, None, :]
    g_ffn3 = g_ffn[:, None, :]
    conv_b3 = conv_b[:, None, :]
    br3 = br[:, None, :]

    xp, xs = x_prompt, x_sample
    states = []
    for l in range(DEPTH):
        paths = []
        cnt = jnp.zeros((SUBLANES, ROUTER_LANES), F32)
        for (x, ct, st, h0, row_base, row_step, is_lat) in (
                (xp, ct_p, st_p, h0_p, 0, 0, False),
                (xs, ct_s, st_s, state_lru[:, l], 1, 1, True)):
            pe = pos if (is_lat and l == 0) else None
            zr, zuv, va, vb = _in_proj(x, pe, mods, g_mix3, w_in_b, fa, fb,
                                       layer=l, row_base=row_base, row_step=row_step)
            swap = x.shape[1] <= LRU_BATCH_MAX_LEN
            h0 = jnp.swapaxes(h0, 0, 1) if swap else h0
            yr, st_new = _lru_mixer(zr, conv_w, conv_b3, wa_bd, wx_bd, lru_ba, lru_bx, lru_lambda, h0, layer=l)
            st_new = jnp.swapaxes(st_new, 0, 1) if swap else st_new
            yf = _dft_apply(ct, st, va, vb)
            x1, h2, ri, cnt = _out_proj(x, pe, yr, zuv, yf, mods, ws_b, bt, w_out_b, g_ffn3, wr, br3, cnt,
                                        layer=l, row_base=row_base, row_step=row_step)
            paths.append((x1, h2, ri, st_new))
        states.append(paths[0][3])

        pos_tiles, counts, offs = _route(paths[0][2].reshape(n_ctx, ROUTER_LANES),
                                         paths[1][2].reshape(n_lat, ROUTER_LANES), cnt)
        items = _work_items(counts, offs, nw, l)
        nct = n_ctx // ROW_TILE
        xsorted = _dispatch(pos_tiles, paths[0][1], paths[1][1])
        ysorted = _experts(items, xsorted, ew1, ew3, ew2)
        gfin = g_final[None, :]
        final = l == DEPTH - 1
        xp = _combine(pos_tiles[:nct], paths[0][0].reshape(n_ctx, D_MODEL),
                      paths[0][2].reshape(n_ctx, ROUTER_LANES), mods, gfin, ysorted, layer=l,
                      row_base=0, row_step=0, tiles_per_seq=lp // ROW_TILE, final=final).reshape(bp, lp, D_MODEL)
        xs = _combine(pos_tiles[nct:], paths[1][0].reshape(n_lat, D_MODEL),
                      paths[1][2].reshape(n_lat, ROUTER_LANES), mods, gfin, ysorted, layer=l,
                      row_base=1, row_step=1, tiles_per_seq=ls // ROW_TILE, final=final).reshape(bs, ls, D_MODEL)

    new_state = jnp.stack(states, axis=1).astype(x_prompt.dtype)
    return (xp, xs, new_state)
```

```python
import functools
import math

import jax
import jax.numpy as jnp
from jax import lax
from jax.experimental import pallas as pl
from jax.experimental.pallas import tpu as pltpu

F32 = jnp.float32
BF16 = jnp.bfloat16

D_MODEL = 1024
DEPTH = 2
GRID_W = 64
D_RNN = 512
RNN_HEAD_DIM = 64
CONV_W = 4
LRU_C = 8.0
D_TMLP = 256
TMLP_HEADS = 4
CHUNK = 128
D_FNET = 256
FNET_GROUP_DIM = 64
D_IN = 2 * D_RNN + 2 * D_TMLP + D_FNET
N_EGROUPS = 4
N_EPG = 8
N_EXPERTS = N_EGROUPS * N_EPG
D_EXPERT = 512
N_MOD = 6
EPS = 1e-6

LANES = 128
SUBLANES = 8
LANE_TILES = D_MODEL // LANES
assert LANE_TILES == SUBLANES
LRU_CB = LANES
LRU_SUB = 4
LRU_BATCH_MAX_LEN = 512
ROW_TILE = 256
PROJ_TILE = 512
EXPERT_TILE = 256
ROUTER_LANES = LANES
NEG_BIG = -1e30
VMEM_LIMIT = 56 * 1024 * 1024


def _cparams(sem):
    return pltpu.CompilerParams(dimension_semantics=sem, vmem_limit_bytes=VMEM_LIMIT)


def _mod_kernel(c_ref, w_ref, b_ref, o_ref):
    c = c_ref[...]
    s = c * jax.nn.sigmoid(c)
    o_ref[0] = jnp.dot(s.astype(BF16), w_ref[0].astype(BF16), preferred_element_type=F32) + b_ref[0]


def _modulation(cond8, w_ada, b_ada):
    tn = 1536
    return pl.pallas_call(
        _mod_kernel,
        grid=(DEPTH, N_MOD * D_MODEL // tn),
        in_specs=[
            pl.BlockSpec((SUBLANES, D_MODEL), lambda l, j: (0, 0)),
            pl.BlockSpec((1, D_MODEL, tn), lambda l, j: (l, 0, j)),
            pl.BlockSpec((1, 1, tn), lambda l, j: (l, 0, j)),
        ],
        out_specs=pl.BlockSpec((1, SUBLANES, tn), lambda l, j: (l, 0, j)),
        out_shape=jax.ShapeDtypeStruct((DEPTH, SUBLANES, N_MOD * D_MODEL), F32),
        compiler_params=_cparams(("arbitrary", "arbitrary")),
        name="modulation",
    )(cond8, w_ada, b_ada.reshape(DEPTH, 1, N_MOD * D_MODEL))


def _fprep_kernel(w_ref, a_ref, b_ref):
    r = lax.broadcasted_iota(jnp.int32, (D_FNET, D_FNET), 0)
    c = lax.broadcasted_iota(jnp.int32, (D_FNET, D_FNET), 1)
    same = (r >> 6) == (c >> 6)
    ph = ((r & 63) * (c & 63)) & 63
    ang = ph.astype(F32) * (2.0 * math.pi / FNET_GROUP_DIM)
    scale = 1.0 / math.sqrt(FNET_GROUP_DIM)
    cm = jnp.where(same, jnp.cos(ang) * scale, 0.0)
    sm = jnp.where(same, jnp.sin(ang) * scale, 0.0)
    w = w_ref[0]
    a_ref[0] = jnp.dot(cm, w, precision=lax.Precision.HIGHEST, preferred_element_type=F32).astype(BF16)
    b_ref[0] = jnp.dot(sm, w, precision=lax.Precision.HIGHEST, preferred_element_type=F32).astype(BF16)


def _fnet_prep(wbd):
    spec = pl.BlockSpec((1, D_FNET, D_FNET), lambda l: (l, 0, 0))
    return pl.pallas_call(
        _fprep_kernel,
        grid=(DEPTH,),
        in_specs=[spec],
        out_specs=[spec, spec],
        out_shape=[jax.ShapeDtypeStruct((DEPTH, D_FNET, D_FNET), BF16)] * 2,
        compiler_params=_cparams(("arbitrary",)),
        name="fnet_prep",
    )(wbd)


TABLE_ROWS = 64


def _base_table_kernel(cj_ref, sj_ref, cm_ref, sm_ref, *, L):
    j = lax.broadcasted_iota(jnp.int32, (TABLE_ROWS, L // 2), 0)
    n = lax.broadcasted_iota(jnp.int32, (TABLE_ROWS, L // 2), 1)
    w = 2.0 * math.pi / L
    fine = ((j * n) & (L - 1)).astype(F32) * w
    coarse = ((j * TABLE_ROWS * n) & (L - 1)).astype(F32) * w
    scale = 1.0 / math.sqrt(L)
    cj_ref[...] = jnp.cos(fine)
    sj_ref[...] = jnp.sin(fine)
    cm_ref[...] = jnp.cos(coarse) * scale
    sm_ref[...] = jnp.sin(coarse) * scale


def _table_kernel(cj_ref, sj_ref, cm_ref, sm_ref, c_ref, s_ref):
    m = pl.program_id(0)
    c0 = cm_ref[pl.ds(m, 1), :]
    s0 = sm_ref[pl.ds(m, 1), :]
    cj = cj_ref[...]
    sj = sj_ref[...]
    c_ref[...] = (cj * c0 - sj * s0).astype(BF16)
    s_ref[...] = (-(sj * c0 + cj * s0)).astype(BF16)


def _dft_tables(L):
    half = L // 2
    small = pl.BlockSpec((TABLE_ROWS, half), lambda *_: (0, 0))
    base = pl.pallas_call(
        functools.partial(_base_table_kernel, L=L),
        out_specs=[small] * 4,
        out_shape=[jax.ShapeDtypeStruct((TABLE_ROWS, half), F32)] * 4,
        compiler_params=pltpu.CompilerParams(vmem_limit_bytes=VMEM_LIMIT),
        name=f"dft_base_tables_{L}",
    )()
    spec = pl.BlockSpec((TABLE_ROWS, half), lambda i: (i, 0))
    return pl.pallas_call(
        _table_kernel,
        grid=(L // TABLE_ROWS,),
        in_specs=[small] * 4,
        out_specs=[spec, spec],
        out_shape=[jax.ShapeDtypeStruct((L, half), BF16)] * 2,
        compiler_params=_cparams(("arbitrary",)),
        name=f"dft_tables_{L}",
    )(*base)


def _dft_kernel(c_ref, s_ref, va_ref, vb_ref, o_ref, vaf, vbf, *, L):
    tk, tc = o_ref.shape
    half = L // 2
    fb = min(256, half)

    @pl.when(pl.program_id(1) == 0)
    def _():
        r = lax.broadcasted_iota(jnp.int32, (fb, fb), 0)
        c = lax.broadcasted_iota(jnp.int32, (fb, fb), 1)
        flip = jnp.where(r + c == fb, 1.0, 0.0).astype(BF16)
        first_row = lax.broadcasted_iota(jnp.int32, (fb, tc), 0) == 0
        for m in range(half // fb):
            pm = L // fb - m - 1
            for src, dst, sign in ((va_ref, vaf, 1.0), (vb_ref, vbf, -1.0)):
                partner = jnp.dot(flip, src[pm * fb:(pm + 1) * fb, :], preferred_element_type=F32)
                if m >= 1:
                    partner = partner + jnp.where(
                        first_row, src[(pm + 1) * fb:(pm + 1) * fb + 1, :].astype(F32), 0.0)
                dst[m * fb:(m + 1) * fb, :] = (src[m * fb:(m + 1) * fb, :].astype(F32) + sign * partner).astype(BF16)

    k = lax.broadcasted_iota(jnp.int32, (tk, 1), 0) + pl.program_id(1) * tk
    alt = (1 - 2 * (k & 1)).astype(F32) * (1.0 / math.sqrt(L))
    o_ref[...] = (jnp.dot(c_ref[...], vaf[...], preferred_element_type=F32)
                  + jnp.dot(s_ref[...], vbf[...], preferred_element_type=F32)
                  + alt * va_ref[half:half + 1, :].astype(F32))


def _dft_apply(ct, st, va, vb):
    L, ncols = va.shape
    half = L // 2
    tk, tc = 256, 1024
    return pl.pallas_call(
        functools.partial(_dft_kernel, L=L),
        grid=(ncols // tc, L // tk),
        in_specs=[
            pl.BlockSpec((tk, half), lambda j, i: (i, 0)),
            pl.BlockSpec((tk, half), lambda j, i: (i, 0)),
            pl.BlockSpec((L, tc), lambda j, i: (0, j)),
            pl.BlockSpec((L, tc), lambda j, i: (0, j)),
        ],
        out_specs=pl.BlockSpec((tk, tc), lambda j, i: (i, j)),
        out_shape=jax.ShapeDtypeStruct((L, ncols), F32),
        scratch_shapes=[pltpu.VMEM((half, tc), BF16), pltpu.VMEM((half, tc), BF16)],
        compiler_params=_cparams(("arbitrary", "arbitrary")),
        name=f"dft_apply_{L}",
    )(ct, st, va, vb)


def _rms_mod(x, g, scale, shift):
    ms = jnp.mean(x * x, axis=-1, keepdims=True)
    return (x * lax.rsqrt(ms + EPS)) * g * (1.0 + scale) + shift


def _pos_table_kernel(s_ref, c_ref):
    n, nf = s_ref.shape
    j = lax.broadcasted_iota(jnp.int32, (n, nf), 0).astype(F32)
    k = lax.broadcasted_iota(jnp.int32, (n, nf), 1).astype(F32)
    ang = j * jnp.exp(k * (-math.log(10000.0) / nf))
    s_ref[...] = jnp.sin(ang)
    c_ref[...] = jnp.cos(ang)


def _pos_tables(n):
    nf = D_MODEL // 4
    return pl.pallas_call(
        _pos_table_kernel,
        out_shape=[jax.ShapeDtypeStruct((n, nf), F32)] * 2,
        name="pos_tables",
    )()


def _pos_tile(ps_ref, pc_ref, t, tm):
    nrow = tm // GRID_W
    nf = D_MODEL // 4

    def rows(tab):
        return jnp.concatenate(
            [jnp.broadcast_to(tab[pl.ds(t * nrow + k, 1), :], (GRID_W, nf)) for k in range(nrow)], axis=0)

    def cols(tab):
        return jnp.concatenate([tab[0:GRID_W, :]] * nrow, axis=0)

    return jnp.concatenate([rows(ps_ref), rows(pc_ref), cols(ps_ref), cols(pc_ref)], axis=-1)


def _in_kernel(*refs, add_pos, pair, row_base, row_step):
    if add_pos:
        x_ref, ps_ref, pc_ref, mod_ref, g_ref, w_ref, a_ref, b_ref, zr_ref, zuv_ref, va_ref, vb_ref = refs
    else:
        x_ref, mod_ref, g_ref, w_ref, a_ref, b_ref, zr_ref, zuv_ref, va_ref, vb_ref = refs
    x = x_ref[0]
    if add_pos:
        x = x + _pos_tile(ps_ref, pc_ref, pl.program_id(1), x.shape[0])
    row = row_base + pl.program_id(0) * row_step
    m = mod_ref[pl.ds(row, 1), :]
    h = _rms_mod(x, g_ref[...], m[:, D_MODEL:2 * D_MODEL], m[:, 0:D_MODEL])
    z = jnp.dot(h.astype(BF16), w_ref[...], preferred_element_type=F32)
    zr_ref[0] = z[:, 0:2 * D_RNN]
    zuv_ref[0] = z[:, 2 * D_RNN:2 * D_RNN + 2 * D_TMLP]
    zf = z[:, 2 * D_RNN + 2 * D_TMLP:D_IN].astype(BF16)
    for ref, w in ((va_ref, a_ref), (vb_ref, b_ref)):
        v = jnp.dot(zf, w[...], preferred_element_type=F32).astype(BF16)
        if pair:
            half = v.shape[0] // 2
            ref[:, 0:D_FNET] = v[0:half]
            ref[:, D_FNET:2 * D_FNET] = v[half:]
        else:
            ref[...] = v


def _pair_short(L, nseq, row_step):
    return 2 * L <= PROJ_TILE and nseq % 2 == 0 and row_step == 0


def _in_proj(x, pos, mods, g, w_in_b, fa, fb, *, layer, row_base, row_step):
    nseq, L, _ = x.shape
    pair = _pair_short(L, nseq, row_step)
    ns, ll = (nseq // 2, 2 * L) if pair else (nseq, L)
    tm = min(ll, PROJ_TILE)
    add_pos = pos is not None
    in_specs = [pl.BlockSpec((1, tm, D_MODEL), lambda b, t: (b, t, 0))]
    args = [x.reshape(ns, ll, D_MODEL)]
    if add_pos:
        in_specs += [pl.BlockSpec(p.shape, lambda b, t: (0, 0)) for p in pos]
        args += list(pos)
    in_specs += [
        _layer_spec((SUBLANES, N_MOD * D_MODEL), layer),
        _layer_spec((1, D_MODEL), layer),
        _layer_spec((D_MODEL, D_IN), layer),
        _layer_spec((D_FNET, D_FNET), layer),
        _layer_spec((D_FNET, D_FNET), layer),
    ]
    args += [mods, g, w_in_b, fa, fb]
    vspec = (pl.BlockSpec((L, 2 * D_FNET), lambda b, t: (0, b)) if pair
             else pl.BlockSpec((tm, D_FNET), lambda b, t: (t, b)))
    zr, zuv, va, vb = pl.pallas_call(
        functools.partial(_in_kernel, add_pos=add_pos, pair=pair, row_base=row_base, row_step=row_step),
        grid=(ns, ll // tm),
        in_specs=in_specs,
        out_specs=[
            pl.BlockSpec((1, tm, 2 * D_RNN), lambda b, t: (b, t, 0)),
            pl.BlockSpec((1, tm, 2 * D_TMLP), lambda b, t: (b, t, 0)),
            vspec, vspec,
        ],
        out_shape=[
            jax.ShapeDtypeStruct((ns, ll, 2 * D_RNN), F32),
            jax.ShapeDtypeStruct((ns, ll, 2 * D_TMLP), F32),
            jax.ShapeDtypeStruct((L, nseq * D_FNET), BF16),
            jax.ShapeDtypeStruct((L, nseq * D_FNET), BF16),
        ],
        compiler_params=_cparams(("arbitrary", "arbitrary")),
        name=f"in_proj_{L}",
    )(*args)
    return zr.reshape(nseq, L, 2 * D_RNN), zuv.reshape(nseq, L, 2 * D_TMLP), va, vb


def _gelu_tanh(x):
    return 0.5 * x * (1.0 + jnp.tanh(math.sqrt(2.0 / math.pi) * (x + 0.044715 * (x * x * x))))


def _rows_to_tile(rows):
    sub = lax.broadcasted_iota(jnp.int32, (SUBLANES, LANES), 0)
    out = jnp.zeros((SUBLANES, LANES), F32)
    for s, r in enumerate(rows):
        out = jnp.where(sub == s, jnp.broadcast_to(r, (SUBLANES, LANES)), out)
    return out


def _lru_kernel(xr_ref, gr_ref, cw_ref, cb_ref, wa_ref, wx_ref, ba_ref, bx_ref, lam_ref, h0_ref,
                y_ref, st_ref, xnat, pext, af, bf, ab, bb, hfo, pfo, hbo, pbo, hnat, *, L, batched):
    S = L if batched else L // SUBLANES
    pitch = S + SUBLANES
    n = S * SUBLANES
    chunk = 256

    def seg(ref, s):
        return ref.at[s] if batched else ref.at[0, s * S:(s + 1) * S]

    for s in range(SUBLANES):
        xnat[s * pitch:s * pitch + S, :] = seg(xr_ref, s)[...]

    def perm_in(j, c):
        dst = pl.multiple_of((j + 2) * SUBLANES, SUBLANES)
        pext[pl.ds(dst, SUBLANES), :] = xnat[pl.ds(j, SUBLANES, stride=pitch), :]
        return c

    lax.fori_loop(0, S, perm_in, 0, unroll=8)

    sub = lax.broadcasted_iota(jnp.int32, (SUBLANES, LANES), 0)

    def from_prev_segment(v):
        return jnp.where(sub == 0, 0.0, pltpu.roll(v, 1, axis=0))

    def from_next_segment(v):
        return jnp.where(sub == SUBLANES - 1, 0.0, pltpu.roll(v, SUBLANES - 1, axis=0))

    if batched:
        pext[0:16, :] = jnp.zeros((16, LANES), F32)
        pext[(S + 2) * 8:(S + 3) * 8, :] = jnp.zeros((8, LANES), F32)
    else:
        pext[0:8, :] = from_prev_segment(pext[S * 8:(S + 1) * 8, :])
        pext[8:16, :] = from_prev_segment(pext[(S + 1) * 8:(S + 2) * 8, :])
        pext[(S + 2) * 8:(S + 3) * 8, :] = from_next_segment(pext[16:24, :])

    lam = lam_ref[...]
    nl = -lam
    sp = jnp.maximum(nl, 0.0) + jnp.log1p(jnp.exp(-jnp.abs(nl)))
    c_la = (-0.5 * LRU_C) * sp
    ba_h = 0.5 * ba_ref[...]
    bx_h = 0.5 * bx_ref[...]
    a_refs = (af, ab)
    b_refs = (bf, bb)

    def gates(i, c):
        base = pl.multiple_of(i * chunk, chunk)
        xc = (cw_ref[0:1, :] * pext[pl.ds(base, chunk), :]
              + cw_ref[1:2, :] * pext[pl.ds(base + 8, chunk), :]
              + cw_ref[2:3, :] * pext[pl.ds(base + 16, chunk), :]
              + cw_ref[3:4, :] * pext[pl.ds(base + 24, chunk), :]
              + cb_ref[...])
        xcb = xc.astype(BF16)
        xh = 0.5 * xc
        for d in range(2):
            tr = jnp.tanh(jnp.dot(xcb, wa_ref[d, 0], preferred_element_type=F32) + ba_h[d:d + 1, :])
            ti = jnp.tanh(jnp.dot(xcb, wx_ref[d, 0], preferred_element_type=F32) + bx_h[d:d + 1, :])
            log_a = c_la[d:d + 1, :] * (1.0 + tr)
            a = jnp.exp(log_a)
            v = jnp.tanh(log_a) * (-1.0 - a * a)
            coef = jnp.where(v > 0.0, v * lax.rsqrt(v), 0.0)
            a_refs[d][pl.ds(base, chunk), :] = a
            b_refs[d][pl.ds(base, chunk), :] = coef * ((1.0 + ti) * xh)
        return c

    lax.fori_loop(0, n // chunk, gates, 0, unroll=min(4, n // chunk))

    nq = LRU_SUB
    sq = S // nq

    def scan(i, carry):
        out = []
        for q in range(nq):
            hf, pf, hb, pb = carry[4 * q:4 * q + 4]
            jf = pl.multiple_of((q * sq + i) * SUBLANES, SUBLANES)
            jb = pl.multiple_of(((q + 1) * sq - 1 - i) * SUBLANES, SUBLANES)
            a1 = af[pl.ds(jf, SUBLANES), :]
            hf = a1 * hf + bf[pl.ds(jf, SUBLANES), :]
            pf = a1 * pf
            hfo[pl.ds(jf, SUBLANES), :] = hf
            pfo[pl.ds(jf, SUBLANES), :] = pf
            a2 = ab[pl.ds(jb, SUBLANES), :]
            hb = a2 * hb + bb[pl.ds(jb, SUBLANES), :]
            pb = a2 * pb
            hbo[pl.ds(jb, SUBLANES), :] = hb
            pbo[pl.ds(jb, SUBLANES), :] = pb
            out += [hf, pf, hb, pb]
        return tuple(out)

    zero = jnp.zeros((SUBLANES, LANES), F32)
    one = jnp.ones((SUBLANES, LANES), F32)
    ends = lax.fori_loop(0, sq, scan, (zero, one, zero, one) * nq, unroll=min(sq, 4))
    end_f = [(ends[4 * q], ends[4 * q + 1]) for q in range(nq)]
    end_b = [(ends[4 * q + 2], ends[4 * q + 3]) for q in range(nq)]

    hf, pf = end_f[0]
    for q in range(1, nq):
        hf, pf = end_f[q][1] * hf + end_f[q][0], end_f[q][1] * pf
    hb, pb = end_b[nq - 1]
    for q in range(nq - 2, -1, -1):
        hb, pb = end_b[q][1] * hb + end_b[q][0], end_b[q][1] * pb

    if batched:
        start_f = h0_ref[0]
        start_b = h0_ref[1]
        st_ref[0] = pf * start_f + hf
        st_ref[1] = pb * start_b + hb
    else:
        rows_f = [h0_ref[0, 0:1, :]]
        for s in range(1, SUBLANES):
            rows_f.append(pf[s - 1:s, :] * rows_f[-1] + hf[s - 1:s, :])
        st_ref[0, 0:1, :] = pf[7:8, :] * rows_f[7] + hf[7:8, :]
        rows_b = [None] * SUBLANES
        rows_b[7] = h0_ref[0, 1:2, :]
        for s in range(SUBLANES - 2, -1, -1):
            rows_b[s] = pb[s + 1:s + 2, :] * rows_b[s + 1] + hb[s + 1:s + 2, :]
        st_ref[0, 1:2, :] = pb[0:1, :] * rows_b[0] + hb[0:1, :]
        start_f = _rows_to_tile(rows_f)
        start_b = _rows_to_tile(rows_b)

    init_f = [start_f]
    for q in range(nq - 1):
        init_f.append(end_f[q][1] * init_f[q] + end_f[q][0])
    init_b = [None] * nq
    init_b[nq - 1] = start_b
    for q in range(nq - 1, 0, -1):
        init_b[q - 1] = end_b[q][1] * init_b[q] + end_b[q][0]

    for q in range(nq):
        def perm_out(j, c, q=q):
            src = pl.multiple_of(j * SUBLANES, SUBLANES)
            v = (hfo[pl.ds(src, SUBLANES), :] + pfo[pl.ds(src, SUBLANES), :] * init_f[q]
                 + hbo[pl.ds(src, SUBLANES), :] + pbo[pl.ds(src, SUBLANES), :] * init_b[q])
            hnat[pl.ds(j, SUBLANES, stride=pitch), :] = v
            return c

        lax.fori_loop(q * sq, (q + 1) * sq, perm_out, 0, unroll=min(sq, 8))

    for s in range(SUBLANES):
        seg(y_ref, s)[...] = hnat[s * pitch:s * pitch + S, :] * _gelu_tanh(seg(gr_ref, s)[...])


def _lru_mixer(zr, conv_w, conv_b, wa_bd, wx_bd, ba, bx, lam, h0, *, layer):
    nseq, L, _ = zr.shape
    ncb = D_RNN // LRU_CB
    batched = L <= LRU_BATCH_MAX_LEN
    nb = SUBLANES if batched else 1
    S = L if batched else L // SUBLANES
    pitch = S + SUBLANES
    n = S * SUBLANES
    vec2 = pl.BlockSpec((None, 2, LRU_CB), lambda b, c: (layer, 0, c))
    wspec = pl.BlockSpec((None, 2, 1, LRU_CB, LRU_CB), lambda b, c: (layer, 0, c, 0, 0))
    if batched:
        state = pl.BlockSpec((2, nb, LRU_CB), lambda b, c: (0, b, c))
        state_shape = (2, nseq, D_RNN)
    else:
        state = pl.BlockSpec((1, 2, LRU_CB), lambda b, c: (b, 0, c))
        state_shape = (nseq, 2, D_RNN)
    return pl.pallas_call(
        functools.partial(_lru_kernel, L=L, batched=batched),
        grid=(nseq // nb, ncb),
        in_specs=[
            pl.BlockSpec((nb, L, LRU_CB), lambda b, c: (b, 0, c)),
            pl.BlockSpec((nb, L, LRU_CB), lambda b, c: (b, 0, c + ncb)),
            pl.BlockSpec((None, CONV_W, LRU_CB), lambda b, c: (layer, 0, c)),
            pl.BlockSpec((None, 1, LRU_CB), lambda b, c: (layer, 0, c)),
            wspec, wspec, vec2, vec2, vec2,
            state,
        ],
        out_specs=[pl.BlockSpec((nb, L, LRU_CB), lambda b, c: (b, 0, c)), state],
        out_shape=[
            jax.ShapeDtypeStruct((nseq, L, D_RNN), F32),
            jax.ShapeDtypeStruct(state_shape, F32),
        ],
        scratch_shapes=[
            pltpu.VMEM((SUBLANES * pitch, LANES), F32),
            pltpu.VMEM(((S + 3) * SUBLANES, LANES), F32),
            *([pltpu.VMEM((n, LANES), F32)] * 8),
            pltpu.VMEM((SUBLANES * pitch, LANES), F32),
        ],
        compiler_params=_cparams(("arbitrary", "arbitrary")),
        name=f"lru_mixer_{L}",
    )(zr, zr, conv_w, conv_b, wa_bd, wx_bd, ba, bx, lam, h0)


def _out_kernel(*refs, add_pos, pair, row_base, row_step):
    if add_pos:
        (x_ref, ps_ref, pc_ref, yr_ref, zuv_ref, yf_ref, mod_ref, ws_ref, bt_ref, wo_ref, g_ref, wr_ref, br_ref,
         cin_ref, x1_ref, h2_ref, ri_ref, cnt_ref, carry) = refs
    else:
        (x_ref, yr_ref, zuv_ref, yf_ref, mod_ref, ws_ref, bt_ref, wo_ref, g_ref, wr_ref, br_ref, cin_ref,
         x1_ref, h2_ref, ri_ref, cnt_ref, carry) = refs
    tm = x_ref.shape[1]
    x = x_ref[0]
    if add_pos:
        x = x + _pos_tile(ps_ref, pc_ref, pl.program_id(1), tm)
    row = row_base + pl.program_id(0) * row_step
    m = mod_ref[pl.ds(row, 1), :]
    g1 = m[:, 2 * D_MODEL:3 * D_MODEL]
    sh2 = m[:, 3 * D_MODEL:4 * D_MODEL]
    sc2 = m[:, 4 * D_MODEL:5 * D_MODEL]

    head = lax.broadcasted_iota(jnp.int32, (CHUNK, D_TMLP), 1) >> 6
    yt_parts = []
    for ci in range(tm // CHUNK):
        u = zuv_ref[0, ci * CHUNK:(ci + 1) * CHUNK, 0:D_TMLP]
        v = zuv_ref[0, ci * CHUNK:(ci + 1) * CHUNK, D_TMLP:2 * D_TMLP].astype(BF16)
        s = jnp.zeros((CHUNK, D_TMLP), F32)
        for h in range(TMLP_HEADS):
            sh = jnp.dot(ws_ref[h], v, preferred_element_type=F32) + bt_ref[:, h:h + 1]
            s = jnp.where(head == h, sh, s)
        yt_parts.append(u * s)
    yt = jnp.concatenate(yt_parts, axis=0) if len(yt_parts) > 1 else yt_parts[0]

    if pair:
        yf = jnp.concatenate([yf_ref[:, 0:D_FNET], yf_ref[:, D_FNET:2 * D_FNET]], axis=0)
    else:
        yf = yf_ref[...]
    y = (jnp.dot(yr_ref[0].astype(BF16), wo_ref[0:D_RNN, :], preferred_element_type=F32)
         + jnp.dot(yt.astype(BF16), wo_ref[D_RNN:D_RNN + D_TMLP, :], preferred_element_type=F32)
         + jnp.dot(yf.astype(BF16), wo_ref[D_RNN + D_TMLP:D_MODEL, :], preferred_element_type=F32))
    x1 = x + g1 * y
    x1_ref[0] = x1
    h2 = _rms_mod(x1, g_ref[...], sc2, sh2)
    _store_token_major(h2_ref, h2)

    wr = wr_ref[...]
    w_hi = wr.astype(BF16)
    w_lo = (wr - w_hi.astype(F32)).astype(BF16)
    h_hi = h2.astype(BF16)
    h_lo = (h2 - h_hi.astype(F32)).astype(BF16)
    p_hi = jnp.dot(h_hi, jnp.concatenate([w_hi, w_lo], axis=-1), preferred_element_type=F32)
    p_lo = jnp.dot(h_lo, w_hi, preferred_element_type=F32)
    logits = p_hi[:, 0:ROUTER_LANES] + p_hi[:, ROUTER_LANES:2 * ROUTER_LANES] + p_lo + br_ref[...]
    lane = lax.broadcasted_iota(jnp.int32, (tm, ROUTER_LANES), 1)
    lane_f = lane.astype(F32)
    is_g = lane < N_EGROUPS
    gl = jnp.where(is_g, logits, NEG_BIG)
    gmax = jnp.max(gl, axis=-1, keepdims=True)
    gsel = jnp.min(jnp.where(gl == gmax, lane_f, 1e4), axis=-1, keepdims=True)
    pg = 1.0 / jnp.sum(jnp.where(is_g, jnp.exp(logits - gmax), 0.0), axis=-1, keepdims=True)
    grp_f = ((lane - N_EGROUPS) >> 3).astype(F32)
    emask = (lane >= N_EGROUPS) & (lane < N_EGROUPS + N_EXPERTS) & (grp_f == gsel)
    el = jnp.where(emask, logits, NEG_BIG)
    v1 = jnp.max(el, axis=-1, keepdims=True)
    i1 = jnp.min(jnp.where(el == v1, lane_f, 1e4), axis=-1, keepdims=True)
    el2 = jnp.where(lane_f == i1, NEG_BIG, el)
    v2 = jnp.max(el2, axis=-1, keepdims=True)
    i2 = jnp.min(jnp.where(el2 == v2, lane_f, 1e4), axis=-1, keepdims=True)
    e2x = jnp.exp(v2 - v1)
    fw1 = 1.0 / (1.0 + e2x)
    fw2 = e2x * fw1
    @pl.when((pl.program_id(0) == 0) & (pl.program_id(1) == 0))
    def _():
        carry[...] = cin_ref[...]

    e1 = i1 - N_EGROUPS
    e2 = i2 - N_EGROUPS
    m1 = lane_f == e1
    m2 = lane_f == e2
    oh = jnp.where(m1 | m2, 1.0, 0.0)
    r_i = lax.broadcasted_iota(jnp.int32, (tm, tm), 0)
    c_i = lax.broadcasted_iota(jnp.int32, (tm, tm), 1)
    tri = jnp.where(c_i < r_i, 1.0, 0.0).astype(BF16)
    before = jnp.dot(tri, oh.astype(BF16), preferred_element_type=F32) + carry[0:1, :]
    rank1 = jnp.sum(jnp.where(m1, before, 0.0), axis=-1, keepdims=True)
    rank2 = jnp.sum(jnp.where(m2, before, 0.0), axis=-1, keepdims=True)
    total = carry[0:1, :] + jnp.sum(oh, axis=0, keepdims=True)
    carry[0:1, :] = total
    cnt_ref[...] = jnp.broadcast_to(total, cnt_ref.shape)

    vals = (e1, e2, pg * fw1, pg * fw2, rank1, rank2)
    ri = jnp.zeros((tm, ROUTER_LANES), F32)
    for k, v in enumerate(vals):
        ri = jnp.where(lane == k, v, ri)
    ri_ref[0] = ri


def _layer_spec(shape, layer):
    zeros = (0,) * len(shape)
    return pl.BlockSpec((None, *shape), lambda *_: (layer, *zeros))


def _out_proj(x, pos, yr, zuv, yf, mods, ws_b, bt, wo_b, g, wr, br, cin, *, layer, row_base, row_step):
    nseq, L, _ = x.shape
    pair = _pair_short(L, nseq, row_step)
    ns, ll = (nseq // 2, 2 * L) if pair else (nseq, L)
    tm = min(ll, PROJ_TILE)
    add_pos = pos is not None
    in_specs = [pl.BlockSpec((1, tm, D_MODEL), lambda b, t: (b, t, 0))]
    args = [x.reshape(ns, ll, D_MODEL)]
    if add_pos:
        in_specs += [pl.BlockSpec(p.shape, lambda b, t: (0, 0)) for p in pos]
        args += list(pos)
    in_specs += [
        pl.BlockSpec((1, tm, D_RNN), lambda b, t: (b, t, 0)),
        pl.BlockSpec((1, tm, 2 * D_TMLP), lambda b, t: (b, t, 0)),
        (pl.BlockSpec((L, 2 * D_FNET), lambda b, t: (0, b)) if pair
         else pl.BlockSpec((tm, D_FNET), lambda b, t: (t, b))),
        _layer_spec((SUBLANES, N_MOD * D_MODEL), layer),
        _layer_spec((TMLP_HEADS, CHUNK, CHUNK), layer),
        _layer_spec((CHUNK, TMLP_HEADS), layer),
        _layer_spec((D_MODEL, D_MODEL), layer),
        _layer_spec((1, D_MODEL), layer),
        _layer_spec((D_MODEL, ROUTER_LANES), layer),
        _layer_spec((1, ROUTER_LANES), layer),
        pl.BlockSpec((SUBLANES, ROUTER_LANES), lambda b, t: (0, 0)),
    ]
    args += [yr.reshape(ns, ll, D_RNN), zuv.reshape(ns, ll, 2 * D_TMLP), yf, mods, ws_b, bt, wo_b, g, wr, br, cin]
    tok = pl.BlockSpec((1, tm, D_MODEL), lambda b, t: (b, t, 0))
    x1, h2, ri, cnt = pl.pallas_call(
        functools.partial(_out_kernel, add_pos=add_pos, pair=pair, row_base=row_base, row_step=row_step),
        grid=(ns, ll // tm),
        in_specs=in_specs,
        out_specs=[tok, pl.BlockSpec((tm * LANE_TILES, LANES), lambda b, t: (b * (ll // tm) + t, 0)),
                   pl.BlockSpec((1, tm, ROUTER_LANES), lambda b, t: (b, t, 0)),
                   pl.BlockSpec((SUBLANES, ROUTER_LANES), lambda b, t: (0, 0))],
        out_shape=[
            jax.ShapeDtypeStruct((ns, ll, D_MODEL), F32),
            jax.ShapeDtypeStruct((nseq * L * LANE_TILES, LANES), F32),
            jax.ShapeDtypeStruct((ns, ll, ROUTER_LANES), F32),
            jax.ShapeDtypeStruct((SUBLANES, ROUTER_LANES), F32),
        ],
        scratch_shapes=[pltpu.VMEM((SUBLANES, ROUTER_LANES), F32)],
        compiler_params=_cparams(("arbitrary", "arbitrary")),
        name=f"out_proj_{L}",
    )(*args)
    return x1.reshape(nseq, L, D_MODEL), h2, ri.reshape(nseq, L, ROUTER_LANES), cnt


def _store_token_major(ref, x):
    tm = x.shape[0]
    for j in range(LANE_TILES):
        ref[pl.ds(j, tm, stride=LANE_TILES), :] = x[:, j * LANES:(j + 1) * LANES]


def _load_token_major(ref):
    tm = ref.shape[0] // LANE_TILES
    return jnp.concatenate([ref[pl.ds(j, tm, stride=LANE_TILES), :] for j in range(LANE_TILES)], axis=-1)


def _token_copy(src_ref, src_tok, dst_ref, dst_tok, sem):
    return pltpu.make_async_copy(src_ref.at[pl.ds(pl.multiple_of(src_tok, LANE_TILES), LANE_TILES)],
                                 dst_ref.at[pl.ds(pl.multiple_of(dst_tok, LANE_TILES), LANE_TILES)], sem)


def _scatter_rows(pos_ref, h_ref, h_hbm, row0, xs_ref, sem):
    rows = h_ref.shape[0]

    for r in range(0, rows, LANE_TILES):
        _token_copy(h_ref, r, xs_ref, pos_ref[0, 0, r // LANE_TILES], sem).start(priority=0)
        _token_copy(h_hbm, row0 + r, xs_ref, pos_ref[0, 1, r // LANE_TILES], sem).start(priority=1)
    for _ in range(2):
        pltpu.make_async_copy(h_ref, xs_ref.at[pl.ds(0, rows)], sem).wait()


def _dispatch_kernel(pos_ref, hp_ref, hs_ref, hp_hbm, hs_hbm, xs_ref, sem, *, n_first):
    i = pl.program_id(0)
    rows = hp_ref.shape[0]

    @pl.when(i < n_first)
    def _():
        _scatter_rows(pos_ref, hp_ref, hp_hbm, i * rows, xs_ref, sem)

    @pl.when(i >= n_first)
    def _():
        _scatter_rows(pos_ref, hs_ref, hs_hbm, (i - n_first) * rows, xs_ref, sem)


def _dispatch(pos, h_first, h_second):
    tm = ROW_TILE
    blk = tm * LANE_TILES
    n_first = h_first.shape[0] // blk
    n_second = h_second.shape[0] // blk
    nrows = 2 * (h_first.shape[0] + h_second.shape[0])
    return pl.pallas_call(
        functools.partial(_dispatch_kernel, n_first=n_first),
        grid=(n_first + n_second,),
        in_specs=[
            pl.BlockSpec((1, 2, tm), lambda i: (i, 0, 0), memory_space=pltpu.SMEM),
            pl.BlockSpec((blk, LANES), lambda i: (jnp.minimum(i, n_first - 1), 0)),
            pl.BlockSpec((blk, LANES), lambda i: (jnp.maximum(i - n_first, 0), 0)),
            pl.BlockSpec(memory_space=pl.ANY),
            pl.BlockSpec(memory_space=pl.ANY),
        ],
        out_specs=pl.BlockSpec(memory_space=pl.ANY),
        out_shape=jax.ShapeDtypeStruct((nrows, LANES), F32),
        scratch_shapes=[pltpu.SemaphoreType.DMA(())],
        compiler_params=_cparams(("arbitrary",)),
        name="moe_dispatch",
    )(pos, h_first, h_second, h_first, h_second)


WEIGHT_LEAD = (3, 2, 1)
WEIGHT_SLOTS = 4
LEAD_ITEMS = max(WEIGHT_LEAD)


def _expert_kernel(wt_ref, we_ref, lo_ref, hi_ref, first_ref, ord_ref, xs_ref, w1_ref, w3_ref, w2_ref, ys_ref,
                   w1b, w3b, w2b):
    s = pl.program_id(0)
    last = pl.num_programs(0) - 1
    tm = EXPERT_TILE

    def stage(lead, src, dst):
        cur = jnp.minimum(s + lead, last)
        prv = jnp.minimum(s + lead - 1, last)

        @pl.when((s == 0) | (we_ref[cur] != we_ref[prv]))
        def _():
            dst[ord_ref[cur] & (WEIGHT_SLOTS - 1)] = src[0].astype(BF16)

    for lead, src, dst in zip(WEIGHT_LEAD, (w1_ref, w3_ref, w2_ref), (w1b, w3b, w2b)):
        stage(lead, src, dst)

    lo = lo_ref[s]
    hi = hi_ref[s]
    full = (lo == 0) & (hi == tm)

    @pl.when((first_ref[s] == 1) & jnp.logical_not(full))
    def _():
        ys_ref[...] = jnp.zeros_like(ys_ref)

    slot = ord_ref[s] & (WEIGHT_SLOTS - 1)
    half = tm // 2

    def run(t0, nt, merge):
        xs_part = xs_ref.at[pl.ds(t0 * LANE_TILES, nt * LANE_TILES)]
        ys_part = ys_ref.at[pl.ds(t0 * LANE_TILES, nt * LANE_TILES)]
        x = _load_token_major(xs_part).astype(BF16)
        a = jnp.dot(x, w1b[slot], preferred_element_type=F32)
        b = jnp.dot(x, w3b[slot], preferred_element_type=F32)
        hid = (a * jax.nn.sigmoid(a)) * b
        res = jnp.dot(hid.astype(BF16), w2b[slot], preferred_element_type=F32)
        if merge:
            rows = lax.broadcasted_iota(jnp.int32, res.shape, 0) + t0
            res = jnp.where((rows >= lo) & (rows < hi), res, _load_token_major(ys_part))
        _store_token_major(ys_part, res)

    in_low = hi <= half
    in_high = lo >= half
    some = hi > lo
    pl.when(full)(lambda: run(0, tm, False))
    pl.when(some & jnp.logical_not(full) & in_low)(lambda: run(0, half, True))
    pl.when(some & in_high)(lambda: run(half, half, True))
    pl.when(some & jnp.logical_not(full) & jnp.logical_not(in_low) & jnp.logical_not(in_high))(
        lambda: run(0, tm, True))


def _experts(items, xs, w1, w3, w2):
    nw = items[0].shape[0]

    def row(s, wt, we, lo, hi, fi, od):
        return (wt[s], 0)

    def weight(lead):
        return lambda s, wt, we, lo, hi, fi, od: (we[jnp.minimum(s + lead, nw - 1)], 0, 0)

    return pl.pallas_call(
        _expert_kernel,
        grid_spec=pltpu.PrefetchScalarGridSpec(
            num_scalar_prefetch=6,
            grid=(nw,),
            in_specs=[
                pl.BlockSpec((EXPERT_TILE * LANE_TILES, LANES), row),
                pl.BlockSpec((1, D_MODEL, D_EXPERT), weight(WEIGHT_LEAD[0])),
                pl.BlockSpec((1, D_MODEL, D_EXPERT), weight(WEIGHT_LEAD[1])),
                pl.BlockSpec((1, D_EXPERT, D_MODEL), weight(WEIGHT_LEAD[2])),
            ],
            out_specs=pl.BlockSpec((EXPERT_TILE * LANE_TILES, LANES), row),
            scratch_shapes=[
                pltpu.VMEM((WEIGHT_SLOTS, D_MODEL, D_EXPERT), BF16),
                pltpu.VMEM((WEIGHT_SLOTS, D_MODEL, D_EXPERT), BF16),
                pltpu.VMEM((WEIGHT_SLOTS, D_EXPERT, D_MODEL), BF16),
            ],
        ),
        out_shape=jax.ShapeDtypeStruct(xs.shape, F32),
        compiler_params=_cparams(("arbitrary",)),
        name="moe_experts",
    )(*items, xs, w1, w3, w2)


def _combine_kernel(pos_ref, posn_ref, x1_ref, ri_ref, mod_ref, gf_ref, ys_ref, o_ref, ybuf, sem,
                    *, row_base, row_step, tiles_per_seq, final):
    tm = x1_ref.shape[0]
    i = pl.program_id(0)
    n = pl.num_programs(0)

    def gather(p_ref, slot):
        for r in range(tm):
            _token_copy(ys_ref, p_ref[0, 0, r], ybuf.at[slot, 0], r * LANE_TILES, sem.at[slot]).start(priority=0)
            _token_copy(ys_ref, p_ref[0, 1, r], ybuf.at[slot, 1], r * LANE_TILES, sem.at[slot]).start(priority=1)

    def drain(slot):
        for k in range(2):
            pltpu.make_async_copy(ys_ref.at[pl.ds(0, tm * LANE_TILES)], ybuf.at[slot, k], sem.at[slot]).wait()

    def tile(slot):
        if slot == 0:
            @pl.when(i == 0)
            def _():
                gather(pos_ref, 0)

        @pl.when(i + 1 < n)
        def _():
            gather(posn_ref, 1 - slot)

        drain(slot)
        row = row_base + (i // tiles_per_seq) * row_step
        g2 = mod_ref[pl.ds(row, 1), 5 * D_MODEL:6 * D_MODEL]
        ri = ri_ref[...]
        y = (ri[:, 2:3] * _load_token_major(ybuf.at[slot, 0])
             + ri[:, 3:4] * _load_token_major(ybuf.at[slot, 1]))
        x2 = x1_ref[...] + g2 * y
        if final:
            ms = jnp.mean(x2 * x2, axis=-1, keepdims=True)
            o_ref[...] = (x2 * lax.rsqrt(ms + EPS)) * gf_ref[...]
        else:
            o_ref[...] = x2

    for slot in range(2):
        pl.when((i & 1) == slot)(functools.partial(tile, slot))


def _combine(pos, x1, ri, mods, g_final, ys, *, layer, row_base, row_step, tiles_per_seq, final):
    ntok = x1.shape[0]
    tm = ROW_TILE
    tok = pl.BlockSpec((tm, D_MODEL), lambda i: (i, 0))
    return pl.pallas_call(
        functools.partial(_combine_kernel, row_base=row_base, row_step=row_step, tiles_per_seq=tiles_per_seq,
                          final=final),
        grid=(ntok // tm,),
        in_specs=[
            pl.BlockSpec((1, 2, tm), lambda i: (i, 0, 0), memory_space=pltpu.SMEM),
            pl.BlockSpec((1, 2, tm), lambda i: (jnp.minimum(i + 1, ntok // tm - 1), 0, 0), memory_space=pltpu.SMEM),
            tok,
            pl.BlockSpec((tm, ROUTER_LANES), lambda i: (i, 0)),
            _layer_spec((SUBLANES, N_MOD * D_MODEL), layer),
            pl.BlockSpec((1, D_MODEL), lambda i: (0, 0)),
            pl.BlockSpec(memory_space=pl.ANY),
        ],
        out_specs=tok,
        out_shape=jax.ShapeDtypeStruct((ntok, D_MODEL), F32),
        scratch_shapes=[
            pltpu.VMEM((2, 2, tm * LANE_TILES, LANES), F32),
            pltpu.SemaphoreType.DMA((2,)),
        ],
        compiler_params=_cparams(("arbitrary",)),
        name=f"moe_combine_{ntok}",
    )(pos, pos, x1, ri, mods, g_final, ys)


def _route(ri_first, ri_second, cnt):
    counts = cnt[0, 0:N_EXPERTS].astype(jnp.int32)
    offs = jnp.cumsum(counts) - counts
    info = jnp.concatenate([ri_first[:, 0:SUBLANES], ri_second[:, 0:SUBLANES]], axis=0).T
    vals = info.astype(jnp.int32)
    base = jnp.zeros_like(vals)
    for k in range(N_EXPERTS):
        base = jnp.where(vals == k, offs[k], base)
    slot = (base[0:2] + vals[4:6]) * LANE_TILES
    n_tiles = slot.shape[1] // ROW_TILE
    pos_tiles = jnp.swapaxes(slot.reshape(2, n_tiles, ROW_TILE), 0, 1)
    return pos_tiles, counts, offs


def _work_items(counts, offs, nw, layer):
    tm = EXPERT_TILE
    first_tile = offs // tm
    last_tile = (offs + counts - 1) // tm
    n_e = jnp.where(counts > 0, last_tile - first_tile + 1, 0)
    w_end = jnp.cumsum(n_e)
    w_start = w_end - n_e
    total = w_end[-1]
    w = jnp.arange(nw, dtype=jnp.int32)
    wc = jnp.minimum(w, total - 1)
    e_w = jnp.sum((wc[:, None] >= w_end[None, :]).astype(jnp.int32), axis=-1)
    sel = (e_w[:, None] == jnp.arange(N_EXPERTS, dtype=jnp.int32)).astype(jnp.int32)
    pick = lambda v: jnp.sum(sel * v[None, :], axis=-1)
    off_w = pick(offs)
    tile_w = pick(first_tile) + (wc - pick(w_start))
    lo = jnp.clip(off_w - tile_w * tm, 0, tm)
    hi = jnp.clip(off_w + pick(counts) - tile_w * tm, 0, tm)
    valid = w < total
    lo = jnp.where(valid, lo, 0)
    hi = jnp.where(valid, hi, 0)
    prev_tile = jnp.concatenate([jnp.full((1,), -1, tile_w.dtype), tile_w[:-1]])
    first = tile_w != prev_tile
    ordinal = pick(jnp.cumsum((counts > 0).astype(jnp.int32)) - 1)

    def lead(a, fill=None):
        head = jnp.broadcast_to(a[0] if fill is None else jnp.asarray(fill, a.dtype), (LEAD_ITEMS,))
        return jnp.concatenate([head, a]).astype(jnp.int32)

    return (lead(tile_w), lead(e_w + layer * N_EXPERTS), lead(lo, 0), lead(hi, 0), lead(first, 0), lead(ordinal))


def _block_diag(w, nblk):
    *lead, H, d, _ = w.shape
    w = w.reshape(*lead, H // nblk, nblk, d, d)
    eye = jnp.eye(nblk, dtype=w.dtype)
    out = jnp.einsum('...gij,gh->...gihj', w, eye)
    return out.reshape(*lead, H // nblk, nblk * d, nblk * d)


def kernel(x_prompt, x_sample, state_lru, c, c_ctx, w_ada, b_ada, g_mix, g_ffn, g_final, w_in, w_out, conv_w, conv_b, lru_wa, lru_ba, lru_wx, lru_bx, lru_lambda, tmlp_ws, tmlp_b, fnet_w, router_g, router_g_b, router_e, router_e_b, e_w1, e_w3, e_w2):
    bp, lp, _ = x_prompt.shape
    bs, ls, _ = x_sample.shape
    n_ctx = bp * lp
    n_lat = bs * ls
    n_tok = n_ctx + n_lat

    cond8 = jnp.concatenate([c_ctx[None, :], c, jnp.zeros((SUBLANES - 1 - bs, D_MODEL), F32)], axis=0)
    pos = _pos_tables(max(ls // GRID_W, GRID_W))
    w_in_b = w_in.astype(BF16)
    w_out_b = w_out.astype(BF16)
    heads_per_cb = LRU_CB // RNN_HEAD_DIM
    wa_bd = (0.5 * _block_diag(lru_wa, heads_per_cb)).astype(BF16)
    wx_bd = (0.5 * _block_diag(lru_wx, heads_per_cb)).astype(BF16)
    ws_b = tmlp_ws.astype(BF16)
    bt = jnp.swapaxes(tmlp_b, 1, 2)
    fnet_bd = _block_diag(fnet_w, D_FNET // FNET_GROUP_DIM)[:, 0]
    wr = jnp.concatenate([router_g, router_e,
                          jnp.zeros((DEPTH, D_MODEL, ROUTER_LANES - N_EGROUPS - N_EXPERTS), F32)], axis=-1)
    br = jnp.concatenate([router_g_b, router_e_b,
                          jnp.zeros((DEPTH, ROUTER_LANES - N_EGROUPS - N_EXPERTS), F32)], axis=-1)
    ew1 = e_w1.reshape(DEPTH * N_EXPERTS, D_MODEL, D_EXPERT)
    ew3 = e_w3.reshape(DEPTH * N_EXPERTS, D_MODEL, D_EXPERT)
    ew2 = e_w2.reshape(DEPTH * N_EXPERTS, D_EXPERT, D_MODEL)

    mods = _modulation(cond8, w_ada, b_ada)
    fa, fb = _fnet_prep(fnet_bd)
    ct_p, st_p = _dft_tables(lp)
    ct_s, st_s = _dft_tables(ls)

    nw = 2 * n_tok // EXPERT_TILE + N_EXPERTS
    h0_p = jnp.zeros((bp, 2, D_RNN), F32)
    g_mix3 = g_mix[:, None, :]
    g_ffn3 = g_ffn[:, None, :]
    conv_b3 = conv_b[:, None, :]
    br3 = br[:, None, :]

    xp, xs = x_prompt, x_sample
    states = []
    for l in range(DEPTH):
        paths = []
        cnt = jnp.zeros((SUBLANES, ROUTER_LANES), F32)
        for (x, ct, st, h0, row_base, row_step, is_lat) in (
                (xp, ct_p, st_p, h0_p, 0, 0, False),
                (xs, ct_s, st_s, state_lru[:, l], 1, 1, True)):
            pe = pos if (is_lat and l == 0) else None
            zr, zuv, va, vb = _in_proj(x, pe, mods, g_mix3, w_in_b, fa, fb,
                                       layer=l, row_base=row_base, row_step=row_step)
            swap = x.shape[1] <= LRU_BATCH_MAX_LEN
            h0 = jnp.swapaxes(h0, 0, 1) if swap else h0
            yr, st_new = _lru_mixer(zr, conv_w, conv_b3, wa_bd, wx_bd, lru_ba, lru_bx, lru_lambda, h0, layer=l)
            st_new = jnp.swapaxes(st_new, 0, 1) if swap else st_new
            yf = _dft_apply(ct, st, va, vb)
            x1, h2, ri, cnt = _out_proj(x, pe, yr, zuv, yf, mods, ws_b, bt, w_out_b, g_ffn3, wr, br3, cnt,
                                        layer=l, row_base=row_base, row_step=row_step)
            paths.append((x1, h2, ri, st_new))
        states.append(paths[0][3])

        pos_tiles, counts, offs = _route(paths[0][2].reshape(n_ctx, ROUTER_LANES),
                                         paths[1][2].reshape(n_lat, ROUTER_LANES), cnt)
        items = _work_items(counts, offs, nw, l)
        nct = n_ctx // ROW_TILE
        xsorted = _dispatch(pos_tiles, paths[0][1], paths[1][1])
        ysorted = _experts(items, xsorted, ew1, ew3, ew2)
        gfin = g_final[None, :]
        final = l == DEPTH - 1
        xp = _combine(pos_tiles[:nct], paths[0][0].reshape(n_ctx, D_MODEL),
                      paths[0][2].reshape(n_ctx, ROUTER_LANES), mods, gfin, ysorted, layer=l,
                      row_base=0, row_step=0, tiles_per_seq=lp // ROW_TILE, final=final).reshape(bp, lp, D_MODEL)
        xs = _combine(pos_tiles[nct:], paths[1][0].reshape(n_lat, D_MODEL),
                      paths[1][2].reshape(n_lat, ROUTER_LANES), mods, gfin, ysorted, layer=l,
                      row_base=1, row_step=1, tiles_per_seq=ls // ROW_TILE, final=final).reshape(bs, ls, D_MODEL)

    new_state = jnp.stack(states, axis=1).astype(x_prompt.dtype)
    return (xp, xs, new_state)
```

```python
import functools
import math

import jax
import jax.numpy as jnp
from jax import lax
from jax.experimental import pallas as pl
from jax.experimental.pallas import tpu as pltpu

F32 = jnp.float32
BF16 = jnp.bfloat16

D_MODEL = 1024
DEPTH = 2
GRID_W = 64
D_RNN = 512
RNN_HEAD_DIM = 64
CONV_W = 4
LRU_C = 8.0
D_TMLP = 256
TMLP_HEADS = 4
CHUNK = 128
D_FNET = 256
FNET_GROUP_DIM = 64
D_IN = 2 * D_RNN + 2 * D_TMLP + D_FNET
N_EGROUPS = 4
N_EPG = 8
N_EXPERTS = N_EGROUPS * N_EPG
D_EXPERT = 512
N_MOD = 6
EPS = 1e-6

LANES = 128
SUBLANES = 8
LANE_TILES = D_MODEL // LANES
assert LANE_TILES == SUBLANES
LRU_CB = LANES
LRU_SUB = 4
LRU_BATCH_MAX_LEN = 512
ROW_TILE = 256
PROJ_TILE = 512
EXPERT_TILE = 512
EXPERT_PART = 256
ROUTER_LANES = LANES
NEG_BIG = -1e30
VMEM_LIMIT = 56 * 1024 * 1024


def _cparams(sem):
    return pltpu.CompilerParams(dimension_semantics=sem, vmem_limit_bytes=VMEM_LIMIT)


def _mod_kernel(c_ref, w_ref, b_ref, o_ref):
    c = c_ref[...]
    s = c * jax.nn.sigmoid(c)
    o_ref[0] = jnp.dot(s.astype(BF16), w_ref[0].astype(BF16), preferred_element_type=F32) + b_ref[0]


def _modulation(cond8, w_ada, b_ada):
    tn = 1536
    return pl.pallas_call(
        _mod_kernel,
        grid=(DEPTH, N_MOD * D_MODEL // tn),
        in_specs=[
            pl.BlockSpec((SUBLANES, D_MODEL), lambda l, j: (0, 0)),
            pl.BlockSpec((1, D_MODEL, tn), lambda l, j: (l, 0, j)),
            pl.BlockSpec((1, 1, tn), lambda l, j: (l, 0, j)),
        ],
        out_specs=pl.BlockSpec((1, SUBLANES, tn), lambda l, j: (l, 0, j)),
        out_shape=jax.ShapeDtypeStruct((DEPTH, SUBLANES, N_MOD * D_MODEL), F32),
        compiler_params=_cparams(("arbitrary", "arbitrary")),
        name="modulation",
    )(cond8, w_ada, b_ada.reshape(DEPTH, 1, N_MOD * D_MODEL))


def _fprep_kernel(w_ref, a_ref, b_ref):
    r = lax.broadcasted_iota(jnp.int32, (D_FNET, D_FNET), 0)
    c = lax.broadcasted_iota(jnp.int32, (D_FNET, D_FNET), 1)
    same = (r >> 6) == (c >> 6)
    ph = ((r & 63) * (c & 63)) & 63
    ang = ph.astype(F32) * (2.0 * math.pi / FNET_GROUP_DIM)
    scale = 1.0 / math.sqrt(FNET_GROUP_DIM)
    cm = jnp.where(same, jnp.cos(ang) * scale, 0.0)
    sm = jnp.where(same, jnp.sin(ang) * scale, 0.0)
    w = w_ref[0]
    a_ref[0] = jnp.dot(cm, w, precision=lax.Precision.HIGHEST, preferred_element_type=F32).astype(BF16)
    b_ref[0] = jnp.dot(sm, w, precision=lax.Precision.HIGHEST, preferred_element_type=F32).astype(BF16)


def _fnet_prep(wbd):
    spec = pl.BlockSpec((1, D_FNET, D_FNET), lambda l: (l, 0, 0))
    return pl.pallas_call(
        _fprep_kernel,
        grid=(DEPTH,),
        in_specs=[spec],
        out_specs=[spec, spec],
        out_shape=[jax.ShapeDtypeStruct((DEPTH, D_FNET, D_FNET), BF16)] * 2,
        compiler_params=_cparams(("arbitrary",)),
        name="fnet_prep",
    )(wbd)


TABLE_ROWS = 64


def _base_table_kernel(cj_ref, sj_ref, cm_ref, sm_ref, *, L):
    j = lax.broadcasted_iota(jnp.int32, (TABLE_ROWS, L // 2), 0)
    n = lax.broadcasted_iota(jnp.int32, (TABLE_ROWS, L // 2), 1)
    w = 2.0 * math.pi / L
    fine = ((j * n) & (L - 1)).astype(F32) * w
    coarse = ((j * TABLE_ROWS * n) & (L - 1)).astype(F32) * w
    scale = 1.0 / math.sqrt(L)
    cj_ref[...] = jnp.cos(fine)
    sj_ref[...] = jnp.sin(fine)
    cm_ref[...] = jnp.cos(coarse) * scale
    sm_ref[...] = jnp.sin(coarse) * scale


def _table_kernel(cj_ref, sj_ref, cm_ref, sm_ref, c_ref, s_ref):
    m = pl.program_id(0)
    c0 = cm_ref[pl.ds(m, 1), :]
    s0 = sm_ref[pl.ds(m, 1), :]
    cj = cj_ref[...]
    sj = sj_ref[...]
    c_ref[...] = (cj * c0 - sj * s0).astype(BF16)
    s_ref[...] = (-(sj * c0 + cj * s0)).astype(BF16)


def _dft_tables(L):
    half = L // 2
    small = pl.BlockSpec((TABLE_ROWS, half), lambda *_: (0, 0))
    base = pl.pallas_call(
        functools.partial(_base_table_kernel, L=L),
        out_specs=[small] * 4,
        out_shape=[jax.ShapeDtypeStruct((TABLE_ROWS, half), F32)] * 4,
        compiler_params=pltpu.CompilerParams(vmem_limit_bytes=VMEM_LIMIT),
        name=f"dft_base_tables_{L}",
    )()
    spec = pl.BlockSpec((TABLE_ROWS, half), lambda i: (i, 0))
    return pl.pallas_call(
        _table_kernel,
        grid=(L // TABLE_ROWS,),
        in_specs=[small] * 4,
        out_specs=[spec, spec],
        out_shape=[jax.ShapeDtypeStruct((L, half), BF16)] * 2,
        compiler_params=_cparams(("arbitrary",)),
        name=f"dft_tables_{L}",
    )(*base)


def _dft_kernel(c_ref, s_ref, va_ref, vb_ref, o_ref, vaf, vbf, *, L):
    tk, tc = o_ref.shape
    half = L // 2
    fb = min(256, half)

    @pl.when(pl.program_id(1) == 0)
    def _():
        r = lax.broadcasted_iota(jnp.int32, (fb, fb), 0)
        c = lax.broadcasted_iota(jnp.int32, (fb, fb), 1)
        flip = jnp.where(r + c == fb, 1.0, 0.0).astype(BF16)
        first_row = lax.broadcasted_iota(jnp.int32, (fb, tc), 0) == 0
        for m in range(half // fb):
            pm = L // fb - m - 1
            for src, dst, sign in ((va_ref, vaf, 1.0), (vb_ref, vbf, -1.0)):
                partner = jnp.dot(flip, src[pm * fb:(pm + 1) * fb, :], preferred_element_type=F32)
                if m >= 1:
                    partner = partner + jnp.where(
                        first_row, src[(pm + 1) * fb:(pm + 1) * fb + 1, :].astype(F32), 0.0)
                dst[m * fb:(m + 1) * fb, :] = (src[m * fb:(m + 1) * fb, :].astype(F32) + sign * partner).astype(BF16)

    k = lax.broadcasted_iota(jnp.int32, (tk, 1), 0) + pl.program_id(1) * tk
    alt = (1 - 2 * (k & 1)).astype(F32) * (1.0 / math.sqrt(L))
    o_ref[...] = (jnp.dot(c_ref[...], vaf[...], preferred_element_type=F32)
                  + jnp.dot(s_ref[...], vbf[...], preferred_element_type=F32)
                  + alt * va_ref[half:half + 1, :].astype(F32))


def _dft_apply(ct, st, va, vb):
    L, ncols = va.shape
    half = L // 2
    tk, tc = 256, 1024
    return pl.pallas_call(
        functools.partial(_dft_kernel, L=L),
        grid=(ncols // tc, L // tk),
        in_specs=[
            pl.BlockSpec((tk, half), lambda j, i: (i, 0)),
            pl.BlockSpec((tk, half), lambda j, i: (i, 0)),
            pl.BlockSpec((L, tc), lambda j, i: (0, j)),
            pl.BlockSpec((L, tc), lambda j, i: (0, j)),
        ],
        out_specs=pl.BlockSpec((tk, tc), lambda j, i: (i, j)),
        out_shape=jax.ShapeDtypeStruct((L, ncols), F32),
        scratch_shapes=[pltpu.VMEM((half, tc), BF16), pltpu.VMEM((half, tc), BF16)],
        compiler_params=_cparams(("arbitrary", "arbitrary")),
        name=f"dft_apply_{L}",
    )(ct, st, va, vb)


def _rms_mod(x, g, scale, shift):
    ms = jnp.mean(x * x, axis=-1, keepdims=True)
    return (x * lax.rsqrt(ms + EPS)) * g * (1.0 + scale) + shift


def _pos_table_kernel(s_ref, c_ref):
    n, nf = s_ref.shape
    j = lax.broadcasted_iota(jnp.int32, (n, nf), 0).astype(F32)
    k = lax.broadcasted_iota(jnp.int32, (n, nf), 1).astype(F32)
    ang = j * jnp.exp(k * (-math.log(10000.0) / nf))
    s_ref[...] = jnp.sin(ang)
    c_ref[...] = jnp.cos(ang)


def _pos_tables(n):
    nf = D_MODEL // 4
    return pl.pallas_call(
        _pos_table_kernel,
        out_shape=[jax.ShapeDtypeStruct((n, nf), F32)] * 2,
        name="pos_tables",
    )()


def _pos_tile(ps_ref, pc_ref, t, tm):
    nrow = tm // GRID_W
    nf = D_MODEL // 4

    def rows(tab):
        return jnp.concatenate(
            [jnp.broadcast_to(tab[pl.ds(t * nrow + k, 1), :], (GRID_W, nf)) for k in range(nrow)], axis=0)

    def cols(tab):
        return jnp.concatenate([tab[0:GRID_W, :]] * nrow, axis=0)

    return jnp.concatenate([rows(ps_ref), rows(pc_ref), cols(ps_ref), cols(pc_ref)], axis=-1)


def _in_kernel(*refs, add_pos, pair, row_base, row_step):
    if add_pos:
        x_ref, ps_ref, pc_ref, mod_ref, g_ref, w_ref, a_ref, b_ref, zr_ref, zuv_ref, va_ref, vb_ref = refs
    else:
        x_ref, mod_ref, g_ref, w_ref, a_ref, b_ref, zr_ref, zuv_ref, va_ref, vb_ref = refs
    x = x_ref[0]
    if add_pos:
        x = x + _pos_tile(ps_ref, pc_ref, pl.program_id(1), x.shape[0])
    row = row_base + pl.program_id(0) * row_step
    m = mod_ref[pl.ds(row, 1), :]
    h = _rms_mod(x, g_ref[...], m[:, D_MODEL:2 * D_MODEL], m[:, 0:D_MODEL])
    z = jnp.dot(h.astype(BF16), w_ref[...], preferred_element_type=F32)
    zr_ref[0] = z[:, 0:2 * D_RNN]
    zuv_ref[0] = z[:, 2 * D_RNN:2 * D_RNN + 2 * D_TMLP]
    zf = z[:, 2 * D_RNN + 2 * D_TMLP:D_IN].astype(BF16)
    for ref, w in ((va_ref, a_ref), (vb_ref, b_ref)):
        v = jnp.dot(zf, w[...], preferred_element_type=F32).astype(BF16)
        if pair:
            half = v.shape[0] // 2
            ref[:, 0:D_FNET] = v[0:half]
            ref[:, D_FNET:2 * D_FNET] = v[half:]
        else:
            ref[...] = v


def _pair_short(L, nseq, row_step):
    return 2 * L <= PROJ_TILE and nseq % 2 == 0 and row_step == 0


def _in_proj(x, pos, mods, g, w_in_b, fa, fb, *, layer, row_base, row_step):
    nseq, L, _ = x.shape
    pair = _pair_short(L, nseq, row_step)
    ns, ll = (nseq // 2, 2 * L) if pair else (nseq, L)
    tm = min(ll, PROJ_TILE)
    add_pos = pos is not None
    in_specs = [pl.BlockSpec((1, tm, D_MODEL), lambda b, t: (b, t, 0))]
    args = [x.reshape(ns, ll, D_MODEL)]
    if add_pos:
        in_specs += [pl.BlockSpec(p.shape, lambda b, t: (0, 0)) for p in pos]
        args += list(pos)
    in_specs += [
        _layer_spec((SUBLANES, N_MOD * D_MODEL), layer),
        _layer_spec((1, D_MODEL), layer),
        _layer_spec((D_MODEL, D_IN), layer),
        _layer_spec((D_FNET, D_FNET), layer),
        _layer_spec((D_FNET, D_FNET), layer),
    ]
    args += [mods, g, w_in_b, fa, fb]
    vspec = (pl.BlockSpec((L, 2 * D_FNET), lambda b, t: (0, b)) if pair
             else pl.BlockSpec((tm, D_FNET), lambda b, t: (t, b)))
    zr, zuv, va, vb = pl.pallas_call(
        functools.partial(_in_kernel, add_pos=add_pos, pair=pair, row_base=row_base, row_step=row_step),
        grid=(ns, ll // tm),
        in_specs=in_specs,
        out_specs=[
            pl.BlockSpec((1, tm, 2 * D_RNN), lambda b, t: (b, t, 0)),
            pl.BlockSpec((1, tm, 2 * D_TMLP), lambda b, t: (b, t, 0)),
            vspec, vspec,
        ],
        out_shape=[
            jax.ShapeDtypeStruct((ns, ll, 2 * D_RNN), F32),
            jax.ShapeDtypeStruct((ns, ll, 2 * D_TMLP), F32),
            jax.ShapeDtypeStruct((L, nseq * D_FNET), BF16),
            jax.ShapeDtypeStruct((L, nseq * D_FNET), BF16),
        ],
        compiler_params=_cparams(("arbitrary", "arbitrary")),
        name=f"in_proj_{L}",
    )(*args)
    return zr.reshape(nseq, L, 2 * D_RNN), zuv.reshape(nseq, L, 2 * D_TMLP), va, vb


def _gelu_tanh(x):
    return 0.5 * x * (1.0 + jnp.tanh(math.sqrt(2.0 / math.pi) * (x + 0.044715 * (x * x * x))))


def _rows_to_tile(rows):
    sub = lax.broadcasted_iota(jnp.int32, (SUBLANES, LANES), 0)
    out = jnp.zeros((SUBLANES, LANES), F32)
    for s, r in enumerate(rows):
        out = jnp.where(sub == s, jnp.broadcast_to(r, (SUBLANES, LANES)), out)
    return out


def _lru_kernel(xr_ref, gr_ref, cw_ref, cb_ref, wa_ref, wx_ref, ba_ref, bx_ref, lam_ref, h0_ref,
                y_ref, st_ref, xnat, pext, af, bf, ab, bb, hfo, pfo, hbo, pbo, hnat, *, L, batched):
    S = L if batched else L // SUBLANES
    pitch = S + SUBLANES
    n = S * SUBLANES
    chunk = 256

    def seg(ref, s):
        return ref.at[s] if batched else ref.at[0, s * S:(s + 1) * S]

    for s in range(SUBLANES):
        xnat[s * pitch:s * pitch + S, :] = seg(xr_ref, s)[...]

    def perm_in(j, c):
        dst = pl.multiple_of((j + 2) * SUBLANES, SUBLANES)
        pext[pl.ds(dst, SUBLANES), :] = xnat[pl.ds(j, SUBLANES, stride=pitch), :]
        return c

    lax.fori_loop(0, S, perm_in, 0, unroll=8)

    sub = lax.broadcasted_iota(jnp.int32, (SUBLANES, LANES), 0)

    def from_prev_segment(v):
        return jnp.where(sub == 0, 0.0, pltpu.roll(v, 1, axis=0))

    def from_next_segment(v):
        return jnp.where(sub == SUBLANES - 1, 0.0, pltpu.roll(v, SUBLANES - 1, axis=0))

    if batched:
        pext[0:16, :] = jnp.zeros((16, LANES), F32)
        pext[(S + 2) * 8:(S + 3) * 8, :] = jnp.zeros((8, LANES), F32)
    else:
        pext[0:8, :] = from_prev_segment(pext[S * 8:(S + 1) * 8, :])
        pext[8:16, :] = from_prev_segment(pext[(S + 1) * 8:(S + 2) * 8, :])
        pext[(S + 2) * 8:(S + 3) * 8, :] = from_next_segment(pext[16:24, :])

    lam = lam_ref[...]
    nl = -lam
    sp = jnp.maximum(nl, 0.0) + jnp.log1p(jnp.exp(-jnp.abs(nl)))
    c_la = (-0.5 * LRU_C) * sp
    ba_h = 0.5 * ba_ref[...]
    bx_h = 0.5 * bx_ref[...]
    a_refs = (af, ab)
    b_refs = (bf, bb)

    def gates(i, c):
        base = pl.multiple_of(i * chunk, chunk)
        xc = (cw_ref[0:1, :] * pext[pl.ds(base, chunk), :]
              + cw_ref[1:2, :] * pext[pl.ds(base + 8, chunk), :]
              + cw_ref[2:3, :] * pext[pl.ds(base + 16, chunk), :]
              + cw_ref[3:4, :] * pext[pl.ds(base + 24, chunk), :]
              + cb_ref[...])
        xcb = xc.astype(BF16)
        xh = 0.5 * xc
        for d in range(2):
            tr = jnp.tanh(jnp.dot(xcb, wa_ref[d, 0], preferred_element_type=F32) + ba_h[d:d + 1, :])
            ti = jnp.tanh(jnp.dot(xcb, wx_ref[d, 0], preferred_element_type=F32) + bx_h[d:d + 1, :])
            log_a = c_la[d:d + 1, :] * (1.0 + tr)
            a = jnp.exp(log_a)
            v = jnp.tanh(log_a) * (-1.0 - a * a)
            coef = jnp.where(v > 0.0, v * lax.rsqrt(v), 0.0)
            a_refs[d][pl.ds(base, chunk), :] = a
            b_refs[d][pl.ds(base, chunk), :] = coef * ((1.0 + ti) * xh)
        return c

    lax.fori_loop(0, n // chunk, gates, 0, unroll=min(4, n // chunk))

    nq = LRU_SUB
    sq = S // nq

    def scan(i, carry):
        out = []
        for q in range(nq):
            hf, pf, hb, pb = carry[4 * q:4 * q + 4]
            jf = pl.multiple_of((q * sq + i) * SUBLANES, SUBLANES)
            jb = pl.multiple_of(((q + 1) * sq - 1 - i) * SUBLANES, SUBLANES)
            a1 = af[pl.ds(jf, SUBLANES), :]
            hf = a1 * hf + bf[pl.ds(jf, SUBLANES), :]
            pf = a1 * pf
            hfo[pl.ds(jf, SUBLANES), :] = hf
            pfo[pl.ds(jf, SUBLANES), :] = pf
            a2 = ab[pl.ds(jb, SUBLANES), :]
            hb = a2 * hb + bb[pl.ds(jb, SUBLANES), :]
            pb = a2 * pb
            hbo[pl.ds(jb, SUBLANES), :] = hb
            pbo[pl.ds(jb, SUBLANES), :] = pb
            out += [hf, pf, hb, pb]
        return tuple(out)

    zero = jnp.zeros((SUBLANES, LANES), F32)
    one = jnp.ones((SUBLANES, LANES), F32)
    ends = lax.fori_loop(0, sq, scan, (zero, one, zero, one) * nq, unroll=min(sq, 4))
    end_f = [(ends[4 * q], ends[4 * q + 1]) for q in range(nq)]
    end_b = [(ends[4 * q + 2], ends[4 * q + 3]) for q in range(nq)]

    hf, pf = end_f[0]
    for q in range(1, nq):
        hf, pf = end_f[q][1] * hf + end_f[q][0], end_f[q][1] * pf
    hb, pb = end_b[nq - 1]
    for q in range(nq - 2, -1, -1):
        hb, pb = end_b[q][1] * hb + end_b[q][0], end_b[q][1] * pb

    if batched:
        start_f = h0_ref[0]
        start_b = h0_ref[1]
        st_ref[0] = pf * start_f + hf
        st_ref[1] = pb * start_b + hb
    else:
        rows_f = [h0_ref[0, 0:1, :]]
        for s in range(1, SUBLANES):
            rows_f.append(pf[s - 1:s, :] * rows_f[-1] + hf[s - 1:s, :])
        st_ref[0, 0:1, :] = pf[7:8, :] * rows_f[7] + hf[7:8, :]
        rows_b = [None] * SUBLANES
        rows_b[7] = h0_ref[0, 1:2, :]
        for s in range(SUBLANES - 2, -1, -1):
            rows_b[s] = pb[s + 1:s + 2, :] * rows_b[s + 1] + hb[s + 1:s + 2, :]
        st_ref[0, 1:2, :] = pb[0:1, :] * rows_b[0] + hb[0:1, :]
        start_f = _rows_to_tile(rows_f)
        start_b = _rows_to_tile(rows_b)

    init_f = [start_f]
    for q in range(nq - 1):
        init_f.append(end_f[q][1] * init_f[q] + end_f[q][0])
    init_b = [None] * nq
    init_b[nq - 1] = start_b
    for q in range(nq - 1, 0, -1):
        init_b[q - 1] = end_b[q][1] * init_b[q] + end_b[q][0]

    for q in range(nq):
        def perm_out(j, c, q=q):
            src = pl.multiple_of(j * SUBLANES, SUBLANES)
            v = (hfo[pl.ds(src, SUBLANES), :] + pfo[pl.ds(src, SUBLANES), :] * init_f[q]
                 + hbo[pl.ds(src, SUBLANES), :] + pbo[pl.ds(src, SUBLANES), :] * init_b[q])
            hnat[pl.ds(j, SUBLANES, stride=pitch), :] = v
            return c

        lax.fori_loop(q * sq, (q + 1) * sq, perm_out, 0, unroll=min(sq, 8))

    for s in range(SUBLANES):
        seg(y_ref, s)[...] = hnat[s * pitch:s * pitch + S, :] * _gelu_tanh(seg(gr_ref, s)[...])


def _lru_mixer(zr, conv_w, conv_b, wa_bd, wx_bd, ba, bx, lam, h0, *, layer):
    nseq, L, _ = zr.shape
    ncb = D_RNN // LRU_CB
    batched = L <= LRU_BATCH_MAX_LEN
    nb = SUBLANES if batched else 1
    S = L if batched else L // SUBLANES
    pitch = S + SUBLANES
    n = S * SUBLANES
    vec2 = pl.BlockSpec((None, 2, LRU_CB), lambda b, c: (layer, 0, c))
    wspec = pl.BlockSpec((None, 2, 1, LRU_CB, LRU_CB), lambda b, c: (layer, 0, c, 0, 0))
    if batched:
        state = pl.BlockSpec((2, nb, LRU_CB), lambda b, c: (0, b, c))
        state_shape = (2, nseq, D_RNN)
    else:
        state = pl.BlockSpec((1, 2, LRU_CB), lambda b, c: (b, 0, c))
        state_shape = (nseq, 2, D_RNN)
    return pl.pallas_call(
        functools.partial(_lru_kernel, L=L, batched=batched),
        grid=(nseq // nb, ncb),
        in_specs=[
            pl.BlockSpec((nb, L, LRU_CB), lambda b, c: (b, 0, c)),
            pl.BlockSpec((nb, L, LRU_CB), lambda b, c: (b, 0, c + ncb)),
            pl.BlockSpec((None, CONV_W, LRU_CB), lambda b, c: (layer, 0, c)),
            pl.BlockSpec((None, 1, LRU_CB), lambda b, c: (layer, 0, c)),
            wspec, wspec, vec2, vec2, vec2,
            state,
        ],
        out_specs=[pl.BlockSpec((nb, L, LRU_CB), lambda b, c: (b, 0, c)), state],
        out_shape=[
            jax.ShapeDtypeStruct((nseq, L, D_RNN), F32),
            jax.ShapeDtypeStruct(state_shape, F32),
        ],
        scratch_shapes=[
            pltpu.VMEM((SUBLANES * pitch, LANES), F32),
            pltpu.VMEM(((S + 3) * SUBLANES, LANES), F32),
            *([pltpu.VMEM((n, LANES), F32)] * 8),
            pltpu.VMEM((SUBLANES * pitch, LANES), F32),
        ],
        compiler_params=_cparams(("arbitrary", "arbitrary")),
        name=f"lru_mixer_{L}",
    )(zr, zr, conv_w, conv_b, wa_bd, wx_bd, ba, bx, lam, h0)


def _out_kernel(*refs, add_pos, pair, row_base, row_step):
    if add_pos:
        (x_ref, ps_ref, pc_ref, yr_ref, zuv_ref, yf_ref, mod_ref, ws_ref, bt_ref, wo_ref, g_ref, wr_ref, br_ref,
         cin_ref, x1_ref, h2_ref, ri_ref, cnt_ref, carry) = refs
    else:
        (x_ref, yr_ref, zuv_ref, yf_ref, mod_ref, ws_ref, bt_ref, wo_ref, g_ref, wr_ref, br_ref, cin_ref,
         x1_ref, h2_ref, ri_ref, cnt_ref, carry) = refs
    tm = x_ref.shape[1]
    x = x_ref[0]
    if add_pos:
        x = x + _pos_tile(ps_ref, pc_ref, pl.program_id(1), tm)
    row = row_base + pl.program_id(0) * row_step
    m = mod_ref[pl.ds(row, 1), :]
    g1 = m[:, 2 * D_MODEL:3 * D_MODEL]
    sh2 = m[:, 3 * D_MODEL:4 * D_MODEL]
    sc2 = m[:, 4 * D_MODEL:5 * D_MODEL]

    head = lax.broadcasted_iota(jnp.int32, (CHUNK, D_TMLP), 1) >> 6
    yt_parts = []
    for ci in range(tm // CHUNK):
        u = zuv_ref[0, ci * CHUNK:(ci + 1) * CHUNK, 0:D_TMLP]
        v = zuv_ref[0, ci * CHUNK:(ci + 1) * CHUNK, D_TMLP:2 * D_TMLP].astype(BF16)
        s = jnp.zeros((CHUNK, D_TMLP), F32)
        for h in range(TMLP_HEADS):
            sh = jnp.dot(ws_ref[h], v, preferred_element_type=F32) + bt_ref[:, h:h + 1]
            s = jnp.where(head == h, sh, s)
        yt_parts.append(u * s)
    yt = jnp.concatenate(yt_parts, axis=0) if len(yt_parts) > 1 else yt_parts[0]

    if pair:
        yf = jnp.concatenate([yf_ref[:, 0:D_FNET], yf_ref[:, D_FNET:2 * D_FNET]], axis=0)
    else:
        yf = yf_ref[...]
    y = (jnp.dot(yr_ref[0].astype(BF16), wo_ref[0:D_RNN, :], preferred_element_type=F32)
         + jnp.dot(yt.astype(BF16), wo_ref[D_RNN:D_RNN + D_TMLP, :], preferred_element_type=F32)
         + jnp.dot(yf.astype(BF16), wo_ref[D_RNN + D_TMLP:D_MODEL, :], preferred_element_type=F32))
    x1 = x + g1 * y
    x1_ref[0] = x1
    h2 = _rms_mod(x1, g_ref[...], sc2, sh2)
    _store_token_major(h2_ref, h2)

    wr = wr_ref[...]
    w_hi = wr.astype(BF16)
    w_lo = (wr - w_hi.astype(F32)).astype(BF16)
    h_hi = h2.astype(BF16)
    h_lo = (h2 - h_hi.astype(F32)).astype(BF16)
    p_hi = jnp.dot(h_hi, jnp.concatenate([w_hi, w_lo], axis=-1), preferred_element_type=F32)
    p_lo = jnp.dot(h_lo, w_hi, preferred_element_type=F32)
    logits = p_hi[:, 0:ROUTER_LANES] + p_hi[:, ROUTER_LANES:2 * ROUTER_LANES] + p_lo + br_ref[...]
    lane = lax.broadcasted_iota(jnp.int32, (tm, ROUTER_LANES), 1)
    lane_f = lane.astype(F32)
    is_g = lane < N_EGROUPS
    gl = jnp.where(is_g, logits, NEG_BIG)
    gmax = jnp.max(gl, axis=-1, keepdims=True)
    gsel = jnp.min(jnp.where(gl == gmax, lane_f, 1e4), axis=-1, keepdims=True)
    pg = 1.0 / jnp.sum(jnp.where(is_g, jnp.exp(logits - gmax), 0.0), axis=-1, keepdims=True)
    grp_f = ((lane - N_EGROUPS) >> 3).astype(F32)
    emask = (lane >= N_EGROUPS) & (lane < N_EGROUPS + N_EXPERTS) & (grp_f == gsel)
    el = jnp.where(emask, logits, NEG_BIG)
    v1 = jnp.max(el, axis=-1, keepdims=True)
    i1 = jnp.min(jnp.where(el == v1, lane_f, 1e4), axis=-1, keepdims=True)
    el2 = jnp.where(lane_f == i1, NEG_BIG, el)
    v2 = jnp.max(el2, axis=-1, keepdims=True)
    i2 = jnp.min(jnp.where(el2 == v2, lane_f, 1e4), axis=-1, keepdims=True)
    e2x = jnp.exp(v2 - v1)
    fw1 = 1.0 / (1.0 + e2x)
    fw2 = e2x * fw1
    @pl.when((pl.program_id(0) == 0) & (pl.program_id(1) == 0))
    def _():
        carry[...] = cin_ref[...]

    e1 = i1 - N_EGROUPS
    e2 = i2 - N_EGROUPS
    m1 = lane_f == e1
    m2 = lane_f == e2
    oh = jnp.where(m1 | m2, 1.0, 0.0)
    r_i = lax.broadcasted_iota(jnp.int32, (tm, tm), 0)
    c_i = lax.broadcasted_iota(jnp.int32, (tm, tm), 1)
    tri = jnp.where(c_i < r_i, 1.0, 0.0).astype(BF16)
    before = jnp.dot(tri, oh.astype(BF16), preferred_element_type=F32) + carry[0:1, :]
    rank1 = jnp.sum(jnp.where(m1, before, 0.0), axis=-1, keepdims=True)
    rank2 = jnp.sum(jnp.where(m2, before, 0.0), axis=-1, keepdims=True)
    total = carry[0:1, :] + jnp.sum(oh, axis=0, keepdims=True)
    carry[0:1, :] = total
    cnt_ref[...] = jnp.broadcast_to(total, cnt_ref.shape)

    vals = (e1, e2, pg * fw1, pg * fw2, rank1, rank2)
    ri = jnp.zeros((tm, ROUTER_LANES), F32)
    for k, v in enumerate(vals):
        ri = jnp.where(lane == k, v, ri)
    ri_ref[0] = ri


def _layer_spec(shape, layer):
    zeros = (0,) * len(shape)
    return pl.BlockSpec((None, *shape), lambda *_: (layer, *zeros))


def _out_proj(x, pos, yr, zuv, yf, mods, ws_b, bt, wo_b, g, wr, br, cin, *, layer, row_base, row_step):
    nseq, L, _ = x.shape
    pair = _pair_short(L, nseq, row_step)
    ns, ll = (nseq // 2, 2 * L) if pair else (nseq, L)
    tm = min(ll, PROJ_TILE)
    add_pos = pos is not None
    in_specs = [pl.BlockSpec((1, tm, D_MODEL), lambda b, t: (b, t, 0))]
    args = [x.reshape(ns, ll, D_MODEL)]
    if add_pos:
        in_specs += [pl.BlockSpec(p.shape, lambda b, t: (0, 0)) for p in pos]
        args += list(pos)
    in_specs += [
        pl.BlockSpec((1, tm, D_RNN), lambda b, t: (b, t, 0)),
        pl.BlockSpec((1, tm, 2 * D_TMLP), lambda b, t: (b, t, 0)),
        (pl.BlockSpec((L, 2 * D_FNET), lambda b, t: (0, b)) if pair
         else pl.BlockSpec((tm, D_FNET), lambda b, t: (t, b))),
        _layer_spec((SUBLANES, N_MOD * D_MODEL), layer),
        _layer_spec((TMLP_HEADS, CHUNK, CHUNK), layer),
        _layer_spec((CHUNK, TMLP_HEADS), layer),
        _layer_spec((D_MODEL, D_MODEL), layer),
        _layer_spec((1, D_MODEL), layer),
        _layer_spec((D_MODEL, ROUTER_LANES), layer),
        _layer_spec((1, ROUTER_LANES), layer),
        pl.BlockSpec((SUBLANES, ROUTER_LANES), lambda b, t: (0, 0)),
    ]
    args += [yr.reshape(ns, ll, D_RNN), zuv.reshape(ns, ll, 2 * D_TMLP), yf, mods, ws_b, bt, wo_b, g, wr, br, cin]
    tok = pl.BlockSpec((1, tm, D_MODEL), lambda b, t: (b, t, 0))
    x1, h2, ri, cnt = pl.pallas_call(
        functools.partial(_out_kernel, add_pos=add_pos, pair=pair, row_base=row_base, row_step=row_step),
        grid=(ns, ll // tm),
        in_specs=in_specs,
        out_specs=[tok, pl.BlockSpec((tm * LANE_TILES, LANES), lambda b, t: (b * (ll // tm) + t, 0)),
                   pl.BlockSpec((1, tm, ROUTER_LANES), lambda b, t: (b, t, 0)),
                   pl.BlockSpec((SUBLANES, ROUTER_LANES), lambda b, t: (0, 0))],
        out_shape=[
            jax.ShapeDtypeStruct((ns, ll, D_MODEL), F32),
            jax.ShapeDtypeStruct((nseq * L * LANE_TILES, LANES), F32),
            jax.ShapeDtypeStruct((ns, ll, ROUTER_LANES), F32),
            jax.ShapeDtypeStruct((SUBLANES, ROUTER_LANES), F32),
        ],
        scratch_shapes=[pltpu.VMEM((SUBLANES, ROUTER_LANES), F32)],
        compiler_params=_cparams(("arbitrary", "arbitrary")),
        name=f"out_proj_{L}",
    )(*args)
    return x1.reshape(nseq, L, D_MODEL), h2, ri.reshape(nseq, L, ROUTER_LANES), cnt


def _store_token_major(ref, x):
    tm = x.shape[0]
    for j in range(LANE_TILES):
        ref[pl.ds(j, tm, stride=LANE_TILES), :] = x[:, j * LANES:(j + 1) * LANES]


def _load_token_major(ref):
    tm = ref.shape[0] // LANE_TILES
    return jnp.concatenate([ref[pl.ds(j, tm, stride=LANE_TILES), :] for j in range(LANE_TILES)], axis=-1)


def _token_copy(src_ref, src_tok, dst_ref, dst_tok, sem):
    return pltpu.make_async_copy(src_ref.at[pl.ds(pl.multiple_of(src_tok, LANE_TILES), LANE_TILES)],
                                 dst_ref.at[pl.ds(pl.multiple_of(dst_tok, LANE_TILES), LANE_TILES)], sem)


def _scatter_rows(pos_ref, h_ref, xs_ref, sem):
    rows = h_ref.shape[0]

    for r in range(0, rows, LANE_TILES):
        _token_copy(h_ref, r, xs_ref, pos_ref[0, 0, r // LANE_TILES], sem).start(priority=0)
        _token_copy(h_ref, r, xs_ref, pos_ref[0, 1, r // LANE_TILES], sem).start(priority=1)
    for _ in range(2):
        pltpu.make_async_copy(h_ref, xs_ref.at[pl.ds(0, rows)], sem).wait()


def _dispatch_kernel(pos_ref, hp_ref, hs_ref, xs_ref, sem, *, n_first):
    i = pl.program_id(0)

    @pl.when(i < n_first)
    def _():
        _scatter_rows(pos_ref, hp_ref, xs_ref, sem)

    @pl.when(i >= n_first)
    def _():
        _scatter_rows(pos_ref, hs_ref, xs_ref, sem)


def _dispatch(pos, h_first, h_second):
    tm = ROW_TILE
    blk = tm * LANE_TILES
    n_first = h_first.shape[0] // blk
    n_second = h_second.shape[0] // blk
    nrows = 2 * (h_first.shape[0] + h_second.shape[0])
    return pl.pallas_call(
        functools.partial(_dispatch_kernel, n_first=n_first),
        grid=(n_first + n_second,),
        in_specs=[
            pl.BlockSpec((1, 2, tm), lambda i: (i, 0, 0), memory_space=pltpu.SMEM),
            pl.BlockSpec((blk, LANES), lambda i: (jnp.minimum(i, n_first - 1), 0)),
            pl.BlockSpec((blk, LANES), lambda i: (jnp.maximum(i - n_first, 0), 0)),
        ],
        out_specs=pl.BlockSpec(memory_space=pl.ANY),
        out_shape=jax.ShapeDtypeStruct((nrows, LANES), F32),
        scratch_shapes=[pltpu.SemaphoreType.DMA(())],
        compiler_params=_cparams(("arbitrary",)),
        name="moe_dispatch",
    )(pos, h_first, h_second)


WEIGHT_LEAD = (3, 2, 1)
WEIGHT_SLOTS = 4
LEAD_ITEMS = max(WEIGHT_LEAD)


def _expert_kernel(wt_ref, we_ref, lo_ref, hi_ref, first_ref, ord_ref, xs_ref, w1_ref, w3_ref, w2_ref, ys_ref,
                   w1b, w3b, w2b):
    s = pl.program_id(0)
    last = pl.num_programs(0) - 1
    tm = EXPERT_TILE

    def stage(lead, src, dst):
        cur = jnp.minimum(s + lead, last)
        prv = jnp.minimum(s + lead - 1, last)

        @pl.when((s == 0) | (we_ref[cur] != we_ref[prv]))
        def _():
            dst[ord_ref[cur] & (WEIGHT_SLOTS - 1)] = src[0].astype(BF16)

    for lead, src, dst in zip(WEIGHT_LEAD, (w1_ref, w3_ref, w2_ref), (w1b, w3b, w2b)):
        stage(lead, src, dst)

    lo = lo_ref[s]
    hi = hi_ref[s]
    full = (lo == 0) & (hi == tm)

    @pl.when((first_ref[s] == 1) & jnp.logical_not(full))
    def _():
        ys_ref[...] = jnp.zeros_like(ys_ref)

    slot = ord_ref[s] & (WEIGHT_SLOTS - 1)

    def run(t0, nt, merge):
        xs_part = xs_ref.at[pl.ds(t0 * LANE_TILES, nt * LANE_TILES)]
        ys_part = ys_ref.at[pl.ds(t0 * LANE_TILES, nt * LANE_TILES)]
        x = _load_token_major(xs_part).astype(BF16)
        a = jnp.dot(x, w1b[slot], preferred_element_type=F32)
        b = jnp.dot(x, w3b[slot], preferred_element_type=F32)
        hid = (a * jax.nn.sigmoid(a)) * b
        res = jnp.dot(hid.astype(BF16), w2b[slot], preferred_element_type=F32)
        if merge:
            rows = lax.broadcasted_iota(jnp.int32, res.shape, 0) + t0
            res = jnp.where((rows >= lo) & (rows < hi), res, _load_token_major(ys_part))
        _store_token_major(ys_part, res)

    pl.when(full)(lambda: run(0, tm, False))
    partial = (hi > lo) & jnp.logical_not(full)
    for t0 in range(0, tm, EXPERT_PART):
        pl.when(partial & (lo < t0 + EXPERT_PART) & (hi > t0))(functools.partial(run, t0, EXPERT_PART, True))


def _experts(items, xs, w1, w3, w2):
    nw = items[0].shape[0]

    def row(s, wt, we, lo, hi, fi, od):
        return (wt[s], 0)

    def weight(lead):
        return lambda s, wt, we, lo, hi, fi, od: (we[jnp.minimum(s + lead, nw - 1)], 0, 0)

    return pl.pallas_call(
        _expert_kernel,
        grid_spec=pltpu.PrefetchScalarGridSpec(
            num_scalar_prefetch=6,
            grid=(nw,),
            in_specs=[
                pl.BlockSpec((EXPERT_TILE * LANE_TILES, LANES), row),
                pl.BlockSpec((1, D_MODEL, D_EXPERT), weight(WEIGHT_LEAD[0])),
                pl.BlockSpec((1, D_MODEL, D_EXPERT), weight(WEIGHT_LEAD[1])),
                pl.BlockSpec((1, D_EXPERT, D_MODEL), weight(WEIGHT_LEAD[2])),
            ],
            out_specs=pl.BlockSpec((EXPERT_TILE * LANE_TILES, LANES), row),
            scratch_shapes=[
                pltpu.VMEM((WEIGHT_SLOTS, D_MODEL, D_EXPERT), BF16),
                pltpu.VMEM((WEIGHT_SLOTS, D_MODEL, D_EXPERT), BF16),
                pltpu.VMEM((WEIGHT_SLOTS, D_EXPERT, D_MODEL), BF16),
            ],
        ),
        out_shape=jax.ShapeDtypeStruct(xs.shape, F32),
        compiler_params=_cparams(("arbitrary",)),
        name="moe_experts",
    )(*items, xs, w1, w3, w2)


def _combine_kernel(pos_ref, posn_ref, x1_ref, ri_ref, mod_ref, gf_ref, ys_ref, o_ref, ybuf, sem,
                    *, row_base, row_step, tiles_per_seq, final):
    tm = x1_ref.shape[0]
    i = pl.program_id(0)
    n = pl.num_programs(0)

    def gather(p_ref, slot):
        for r in range(tm):
            _token_copy(ys_ref, p_ref[0, 0, r], ybuf.at[slot, 0], r * LANE_TILES, sem.at[slot]).start(priority=0)
            _token_copy(ys_ref, p_ref[0, 1, r], ybuf.at[slot, 1], r * LANE_TILES, sem.at[slot]).start(priority=1)

    def drain(slot):
        for k in range(2):
            pltpu.make_async_copy(ys_ref.at[pl.ds(0, tm * LANE_TILES)], ybuf.at[slot, k], sem.at[slot]).wait()

    def tile(slot):
        if slot == 0:
            @pl.when(i == 0)
            def _():
                gather(pos_ref, 0)

        @pl.when(i + 1 < n)
        def _():
            gather(posn_ref, 1 - slot)

        drain(slot)
        row = row_base + (i // tiles_per_seq) * row_step
        g2 = mod_ref[pl.ds(row, 1), 5 * D_MODEL:6 * D_MODEL]
        ri = ri_ref[...]
        y = (ri[:, 2:3] * _load_token_major(ybuf.at[slot, 0])
             + ri[:, 3:4] * _load_token_major(ybuf.at[slot, 1]))
        x2 = x1_ref[...] + g2 * y
        if final:
            ms = jnp.mean(x2 * x2, axis=-1, keepdims=True)
            o_ref[...] = (x2 * lax.rsqrt(ms + EPS)) * gf_ref[...]
        else:
            o_ref[...] = x2

    for slot in range(2):
        pl.when((i & 1) == slot)(functools.partial(tile, slot))


def _combine(pos, x1, ri, mods, g_final, ys, *, layer, row_base, row_step, tiles_per_seq, final):
    ntok = x1.shape[0]
    tm = ROW_TILE
    tok = pl.BlockSpec((tm, D_MODEL), lambda i: (i, 0))
    return pl.pallas_call(
        functools.partial(_combine_kernel, row_base=row_base, row_step=row_step, tiles_per_seq=tiles_per_seq,
                          final=final),
        grid=(ntok // tm,),
        in_specs=[
            pl.BlockSpec((1, 2, tm), lambda i: (i, 0, 0), memory_space=pltpu.SMEM),
            pl.BlockSpec((1, 2, tm), lambda i: (jnp.minimum(i + 1, ntok // tm - 1), 0, 0), memory_space=pltpu.SMEM),
            tok,
            pl.BlockSpec((tm, ROUTER_LANES), lambda i: (i, 0)),
            _layer_spec((SUBLANES, N_MOD * D_MODEL), layer),
            pl.BlockSpec((1, D_MODEL), lambda i: (0, 0)),
            pl.BlockSpec(memory_space=pl.ANY),
        ],
        out_specs=tok,
        out_shape=jax.ShapeDtypeStruct((ntok, D_MODEL), F32),
        scratch_shapes=[
            pltpu.VMEM((2, 2, tm * LANE_TILES, LANES), F32),
            pltpu.SemaphoreType.DMA((2,)),
        ],
        compiler_params=_cparams(("arbitrary",)),
        name=f"moe_combine_{ntok}",
    )(pos, pos, x1, ri, mods, g_final, ys)


def _route(ri_first, ri_second, cnt):
    counts = cnt[0, 0:N_EXPERTS].astype(jnp.int32)
    offs = jnp.cumsum(counts) - counts
    info = jnp.concatenate([ri_first[:, 0:SUBLANES], ri_second[:, 0:SUBLANES]], axis=0).T
    vals = info.astype(jnp.int32)
    base = jnp.zeros_like(vals)
    for k in range(N_EXPERTS):
        base = jnp.where(vals == k, offs[k], base)
    slot = (base[0:2] + vals[4:6]) * LANE_TILES
    n_tiles = slot.shape[1] // ROW_TILE
    pos_tiles = jnp.swapaxes(slot.reshape(2, n_tiles, ROW_TILE), 0, 1)
    return pos_tiles, counts, offs


def _work_items(counts, offs, nw, layer):
    tm = EXPERT_TILE
    first_tile = offs // tm
    last_tile = (offs + counts - 1) // tm
    n_e = jnp.where(counts > 0, last_tile - first_tile + 1, 0)
    w_end = jnp.cumsum(n_e)
    w_start = w_end - n_e
    total = w_end[-1]
    w = jnp.arange(nw, dtype=jnp.int32)
    wc = jnp.minimum(w, total - 1)
    e_w = jnp.sum((wc[:, None] >= w_end[None, :]).astype(jnp.int32), axis=-1)
    sel = (e_w[:, None] == jnp.arange(N_EXPERTS, dtype=jnp.int32)).astype(jnp.int32)
    pick = lambda v: jnp.sum(sel * v[None, :], axis=-1)
    off_w = pick(offs)
    tile_w = pick(first_tile) + (wc - pick(w_start))
    lo = jnp.clip(off_w - tile_w * tm, 0, tm)
    hi = jnp.clip(off_w + pick(counts) - tile_w * tm, 0, tm)
    valid = w < total
    lo = jnp.where(valid, lo, 0)
    hi = jnp.where(valid, hi, 0)
    prev_tile = jnp.concatenate([jnp.full((1,), -1, tile_w.dtype), tile_w[:-1]])
    first = tile_w != prev_tile
    ordinal = pick(jnp.cumsum((counts > 0).astype(jnp.int32)) - 1)

    def lead(a, fill=None):
        head = jnp.broadcast_to(a[0] if fill is None else jnp.asarray(fill, a.dtype), (LEAD_ITEMS,))
        return jnp.concatenate([head, a]).astype(jnp.int32)

    return (lead(tile_w), lead(e_w + layer * N_EXPERTS), lead(lo, 0), lead(hi, 0), lead(first, 0), lead(ordinal))


def _block_diag(w, nblk):
    *lead, H, d, _ = w.shape
    w = w.reshape(*lead, H // nblk, nblk, d, d)
    eye = jnp.eye(nblk, dtype=w.dtype)
    out = jnp.einsum('...gij,gh->...gihj', w, eye)
    return out.reshape(*lead, H // nblk, nblk * d, nblk * d)


def kernel(x_prompt, x_sample, state_lru, c, c_ctx, w_ada, b_ada, g_mix, g_ffn, g_final, w_in, w_out, conv_w, conv_b, lru_wa, lru_ba, lru_wx, lru_bx, lru_lambda, tmlp_ws, tmlp_b, fnet_w, router_g, router_g_b, router_e, router_e_b, e_w1, e_w3, e_w2):
    bp, lp, _ = x_prompt.shape
    bs, ls, _ = x_sample.shape
    n_ctx = bp * lp
    n_lat = bs * ls
    n_tok = n_ctx + n_lat

    cond8 = jnp.concatenate([c_ctx[None, :], c, jnp.zeros((SUBLANES - 1 - bs, D_MODEL), F32)], axis=0)
    pos = _pos_tables(max(ls // GRID_W, GRID_W))
    w_in_b = w_in.astype(BF16)
    w_out_b = w_out.astype(BF16)
    heads_per_cb = LRU_CB // RNN_HEAD_DIM
    wa_bd = (0.5 * _block_diag(lru_wa, heads_per_cb)).astype(BF16)
    wx_bd = (0.5 * _block_diag(lru_wx, heads_per_cb)).astype(BF16)
    ws_b = tmlp_ws.astype(BF16)
    bt = jnp.swapaxes(tmlp_b, 1, 2)
    fnet_bd = _block_diag(fnet_w, D_FNET // FNET_GROUP_DIM)[:, 0]
    wr = jnp.concatenate([router_g, router_e,
                          jnp.zeros((DEPTH, D_MODEL, ROUTER_LANES - N_EGROUPS - N_EXPERTS), F32)], axis=-1)
    br = jnp.concatenate([router_g_b, router_e_b,
                          jnp.zeros((DEPTH, ROUTER_LANES - N_EGROUPS - N_EXPERTS), F32)], axis=-1)
    ew1 = e_w1.reshape(DEPTH * N_EXPERTS, D_MODEL, D_EXPERT)
    ew3 = e_w3.reshape(DEPTH * N_EXPERTS, D_MODEL, D_EXPERT)
    ew2 = e_w2.reshape(DEPTH * N_EXPERTS, D_EXPERT, D_MODEL)

    mods = _modulation(cond8, w_ada, b_ada)
    fa, fb = _fnet_prep(fnet_bd)
    ct_p, st_p = _dft_tables(lp)
    ct_s, st_s = _dft_tables(ls)

    nw = 2 * n_tok // EXPERT_TILE + N_EXPERTS
    h0_p = jnp.zeros((bp, 2, D_RNN), F32)
    g_mix3 = g_mix[:, None, :]
    g_ffn3 = g_ffn[:, None, :]
    conv_b3 = conv_b[:, None, :]
    br3 = br[:, None, :]

    xp, xs = x_prompt, x_sample
    states = []
    for l in range(DEPTH):
        paths = []
        cnt = jnp.zeros((SUBLANES, ROUTER_LANES), F32)
        for (x, ct, st, h0, row_base, row_step, is_lat) in (
                (xp, ct_p, st_p, h0_p, 0, 0, False),
                (xs, ct_s, st_s, state_lru[:, l], 1, 1, True)):
            pe = pos if (is_lat and l == 0) else None
            zr, zuv, va, vb = _in_proj(x, pe, mods, g_mix3, w_in_b, fa, fb,
                                       layer=l, row_base=row_base, row_step=row_step)
            swap = x.shape[1] <= LRU_BATCH_MAX_LEN
            h0 = jnp.swapaxes(h0, 0, 1) if swap else h0
            yr, st_new = _lru_mixer(zr, conv_w, conv_b3, wa_bd, wx_bd, lru_ba, lru_bx, lru_lambda, h0, layer=l)
            st_new = jnp.swapaxes(st_new, 0, 1) if swap else st_new
            yf = _dft_apply(ct, st, va, vb)
            x1, h2, ri, cnt = _out_proj(x, pe, yr, zuv, yf, mods, ws_b, bt, w_out_b, g_ffn3, wr, br3, cnt,
                                        layer=l, row_base=row_base, row_step=row_step)
            paths.append((x1, h2, ri, st_new))
        states.append(paths[0][3])

        pos_tiles, counts, offs = _route(paths[0][2].reshape(n_ctx, ROUTER_LANES),
                                         paths[1][2].reshape(n_lat, ROUTER_LANES), cnt)
        items = _work_items(counts, offs, nw, l)
        nct = n_ctx // ROW_TILE
        xsorted = _dispatch(pos_tiles, paths[0][1], paths[1][1])
        ysorted = _experts(items, xsorted, ew1, ew3, ew2)
        gfin = g_final[None, :]
        final = l == DEPTH - 1
        xp = _combine(pos_tiles[:nct], paths[0][0].reshape(n_ctx, D_MODEL),
                      paths[0][2].reshape(n_ctx, ROUTER_LANES), mods, gfin, ysorted, layer=l,
                      row_base=0, row_step=0, tiles_per_seq=lp // ROW_TILE, final=final).reshape(bp, lp, D_MODEL)
        xs = _combine(pos_tiles[nct:], paths[1][0].reshape(n_lat, D_MODEL),
                      paths[1][2].reshape(n_lat, ROUTER_LANES), mods, gfin, ysorted, layer=l,
                      row_base=1, row_step=1, tiles_per_seq=ls // ROW_TILE, final=final).reshape(bs, ls, D_MODEL)

    new_state = jnp.stack(states, axis=1).astype(x_prompt.dtype)
    return (xp, xs, new_state)
```

```python
import functools
import math

import jax
import jax.numpy as jnp
from jax import lax
from jax.experimental import pallas as pl
from jax.experimental.pallas import tpu as pltpu

F32 = jnp.float32
BF16 = jnp.bfloat16

D_MODEL = 1024
DEPTH = 2
GRID_W = 64
D_RNN = 512
RNN_HEAD_DIM = 64
CONV_W = 4
LRU_C = 8.0
D_TMLP = 256
TMLP_HEADS = 4
CHUNK = 128
D_FNET = 256
FNET_GROUP_DIM = 64
D_IN = 2 * D_RNN + 2 * D_TMLP + D_FNET
N_EGROUPS = 4
N_EPG = 8
N_EXPERTS = N_EGROUPS * N_EPG
D_EXPERT = 512
N_MOD = 6
EPS = 1e-6

LANES = 128
SUBLANES = 8
LANE_TILES = D_MODEL // LANES
assert LANE_TILES == SUBLANES
LRU_CB = LANES
LRU_SUB = 4
LRU_BATCH_MAX_LEN = 512
ROW_TILE = 256
PROJ_TILE = 512
EXPERT_TILE = 1024
EXPERT_PART = 256
ROUTER_LANES = LANES
NEG_BIG = -1e30
VMEM_LIMIT = 56 * 1024 * 1024


def _cparams(sem):
    return pltpu.CompilerParams(dimension_semantics=sem, vmem_limit_bytes=VMEM_LIMIT)


def _mod_kernel(c_ref, w_ref, b_ref, o_ref):
    c = c_ref[...]
    s = c * jax.nn.sigmoid(c)
    o_ref[0] = jnp.dot(s.astype(BF16), w_ref[0].astype(BF16), preferred_element_type=F32) + b_ref[0]


def _modulation(cond8, w_ada, b_ada):
    tn = 1536
    return pl.pallas_call(
        _mod_kernel,
        grid=(DEPTH, N_MOD * D_MODEL // tn),
        in_specs=[
            pl.BlockSpec((SUBLANES, D_MODEL), lambda l, j: (0, 0)),
            pl.BlockSpec((1, D_MODEL, tn), lambda l, j: (l, 0, j)),
            pl.BlockSpec((1, 1, tn), lambda l, j: (l, 0, j)),
        ],
        out_specs=pl.BlockSpec((1, SUBLANES, tn), lambda l, j: (l, 0, j)),
        out_shape=jax.ShapeDtypeStruct((DEPTH, SUBLANES, N_MOD * D_MODEL), F32),
        compiler_params=_cparams(("arbitrary", "arbitrary")),
        name="modulation",
    )(cond8, w_ada, b_ada.reshape(DEPTH, 1, N_MOD * D_MODEL))


def _fprep_kernel(w_ref, a_ref, b_ref):
    r = lax.broadcasted_iota(jnp.int32, (D_FNET, D_FNET), 0)
    c = lax.broadcasted_iota(jnp.int32, (D_FNET, D_FNET), 1)
    same = (r >> 6) == (c >> 6)
    ph = ((r & 63) * (c & 63)) & 63
    ang = ph.astype(F32) * (2.0 * math.pi / FNET_GROUP_DIM)
    scale = 1.0 / math.sqrt(FNET_GROUP_DIM)
    cm = jnp.where(same, jnp.cos(ang) * scale, 0.0)
    sm = jnp.where(same, jnp.sin(ang) * scale, 0.0)
    w = w_ref[0]
    a_ref[0] = jnp.dot(cm, w, precision=lax.Precision.HIGHEST, preferred_element_type=F32).astype(BF16)
    b_ref[0] = jnp.dot(sm, w, precision=lax.Precision.HIGHEST, preferred_element_type=F32).astype(BF16)


def _fnet_prep(wbd):
    spec = pl.BlockSpec((1, D_FNET, D_FNET), lambda l: (l, 0, 0))
    return pl.pallas_call(
        _fprep_kernel,
        grid=(DEPTH,),
        in_specs=[spec],
        out_specs=[spec, spec],
        out_shape=[jax.ShapeDtypeStruct((DEPTH, D_FNET, D_FNET), BF16)] * 2,
        compiler_params=_cparams(("arbitrary",)),
        name="fnet_prep",
    )(wbd)


TABLE_ROWS = 64
TABLE_BLOCKS = 4


def _base_table_kernel(cj_ref, sj_ref, cm_ref, sm_ref, *, L):
    j = lax.broadcasted_iota(jnp.int32, (TABLE_ROWS, L // 2), 0)
    n = lax.broadcasted_iota(jnp.int32, (TABLE_ROWS, L // 2), 1)
    w = 2.0 * math.pi / L
    fine = ((j * n) & (L - 1)).astype(F32) * w
    coarse = ((j * TABLE_ROWS * n) & (L - 1)).astype(F32) * w
    scale = 1.0 / math.sqrt(L)
    cj_ref[...] = jnp.cos(fine)
    sj_ref[...] = jnp.sin(fine)
    cm_ref[...] = jnp.cos(coarse) * scale
    sm_ref[...] = jnp.sin(coarse) * scale


def _table_kernel(cj_ref, sj_ref, cm_ref, sm_ref, c_ref, s_ref):
    cj = cj_ref[...]
    sj = sj_ref[...]
    for q in range(c_ref.shape[0] // TABLE_ROWS):
        m = pl.program_id(0) * (c_ref.shape[0] // TABLE_ROWS) + q
        c0 = cm_ref[pl.ds(m, 1), :]
        s0 = sm_ref[pl.ds(m, 1), :]
        c_ref[q * TABLE_ROWS:(q + 1) * TABLE_ROWS, :] = (cj * c0 - sj * s0).astype(BF16)
        s_ref[q * TABLE_ROWS:(q + 1) * TABLE_ROWS, :] = (-(sj * c0 + cj * s0)).astype(BF16)


def _dft_tables(L):
    half = L // 2
    small = pl.BlockSpec((TABLE_ROWS, half), lambda *_: (0, 0))
    base = pl.pallas_call(
        functools.partial(_base_table_kernel, L=L),
        out_specs=[small] * 4,
        out_shape=[jax.ShapeDtypeStruct((TABLE_ROWS, half), F32)] * 4,
        compiler_params=pltpu.CompilerParams(vmem_limit_bytes=VMEM_LIMIT),
        name=f"dft_base_tables_{L}",
    )()
    rows = min(L, TABLE_ROWS * TABLE_BLOCKS)
    spec = pl.BlockSpec((rows, half), lambda i: (i, 0))
    return pl.pallas_call(
        _table_kernel,
        grid=(L // rows,),
        in_specs=[small] * 4,
        out_specs=[spec, spec],
        out_shape=[jax.ShapeDtypeStruct((L, half), BF16)] * 2,
        compiler_params=_cparams(("arbitrary",)),
        name=f"dft_tables_{L}",
    )(*base)


def _dft_kernel(c_ref, s_ref, va_ref, vb_ref, o_ref, vaf, vbf, *, L):
    tk, tc = o_ref.shape
    half = L // 2
    fb = min(256, half)

    @pl.when(pl.program_id(1) == 0)
    def _():
        r = lax.broadcasted_iota(jnp.int32, (fb, fb), 0)
        c = lax.broadcasted_iota(jnp.int32, (fb, fb), 1)
        flip = jnp.where(r + c == fb, 1.0, 0.0).astype(BF16)
        first_row = lax.broadcasted_iota(jnp.int32, (fb, tc), 0) == 0
        for m in range(half // fb):
            pm = L // fb - m - 1
            for src, dst, sign in ((va_ref, vaf, 1.0), (vb_ref, vbf, -1.0)):
                partner = jnp.dot(flip, src[pm * fb:(pm + 1) * fb, :], preferred_element_type=F32)
                if m >= 1:
                    partner = partner + jnp.where(
                        first_row, src[(pm + 1) * fb:(pm + 1) * fb + 1, :].astype(F32), 0.0)
                dst[m * fb:(m + 1) * fb, :] = (src[m * fb:(m + 1) * fb, :].astype(F32) + sign * partner).astype(BF16)

    k = lax.broadcasted_iota(jnp.int32, (tk, 1), 0) + pl.program_id(1) * tk
    alt = (1 - 2 * (k & 1)).astype(F32) * (1.0 / math.sqrt(L))
    o_ref[...] = (jnp.dot(c_ref[...], vaf[...], preferred_element_type=F32)
                  + jnp.dot(s_ref[...], vbf[...], preferred_element_type=F32)
                  + alt * va_ref[half:half + 1, :].astype(F32))


def _dft_apply(ct, st, va, vb):
    L, ncols = va.shape
    half = L // 2
    tk, tc = 256, 1024
    return pl.pallas_call(
        functools.partial(_dft_kernel, L=L),
        grid=(ncols // tc, L // tk),
        in_specs=[
            pl.BlockSpec((tk, half), lambda j, i: (i, 0)),
            pl.BlockSpec((tk, half), lambda j, i: (i, 0)),
            pl.BlockSpec((L, tc), lambda j, i: (0, j)),
            pl.BlockSpec((L, tc), lambda j, i: (0, j)),
        ],
        out_specs=pl.BlockSpec((tk, tc), lambda j, i: (i, j)),
        out_shape=jax.ShapeDtypeStruct((L, ncols), F32),
        scratch_shapes=[pltpu.VMEM((half, tc), BF16), pltpu.VMEM((half, tc), BF16)],
        compiler_params=_cparams(("arbitrary", "arbitrary")),
        name=f"dft_apply_{L}",
    )(ct, st, va, vb)


def _rms_mod(x, g, scale, shift):
    ms = jnp.mean(x * x, axis=-1, keepdims=True)
    return (x * lax.rsqrt(ms + EPS)) * g * (1.0 + scale) + shift


def _pos_table_kernel(s_ref, c_ref):
    n, nf = s_ref.shape
    j = lax.broadcasted_iota(jnp.int32, (n, nf), 0).astype(F32)
    k = lax.broadcasted_iota(jnp.int32, (n, nf), 1).astype(F32)
    ang = j * jnp.exp(k * (-math.log(10000.0) / nf))
    s_ref[...] = jnp.sin(ang)
    c_ref[...] = jnp.cos(ang)


def _pos_tables(n):
    nf = D_MODEL // 4
    return pl.pallas_call(
        _pos_table_kernel,
        out_shape=[jax.ShapeDtypeStruct((n, nf), F32)] * 2,
        name="pos_tables",
    )()


def _pos_tile(ps_ref, pc_ref, t, tm):
    nrow = tm // GRID_W
    nf = D_MODEL // 4

    def rows(tab):
        return jnp.concatenate(
            [jnp.broadcast_to(tab[pl.ds(t * nrow + k, 1), :], (GRID_W, nf)) for k in range(nrow)], axis=0)

    def cols(tab):
        return jnp.concatenate([tab[0:GRID_W, :]] * nrow, axis=0)

    return jnp.concatenate([rows(ps_ref), rows(pc_ref), cols(ps_ref), cols(pc_ref)], axis=-1)


def _in_kernel(*refs, add_pos, pair, row_base, row_step):
    if add_pos:
        x_ref, ps_ref, pc_ref, mod_ref, g_ref, w_ref, a_ref, b_ref, zr_ref, zuv_ref, va_ref, vb_ref = refs
    else:
        x_ref, mod_ref, g_ref, w_ref, a_ref, b_ref, zr_ref, zuv_ref, va_ref, vb_ref = refs
    x = x_ref[0]
    if add_pos:
        x = x + _pos_tile(ps_ref, pc_ref, pl.program_id(1), x.shape[0])
    row = row_base + pl.program_id(0) * row_step
    m = mod_ref[pl.ds(row, 1), :]
    h = _rms_mod(x, g_ref[...], m[:, D_MODEL:2 * D_MODEL], m[:, 0:D_MODEL])
    z = jnp.dot(h.astype(BF16), w_ref[...], preferred_element_type=F32)
    zr_ref[0] = z[:, 0:2 * D_RNN]
    zuv_ref[0] = z[:, 2 * D_RNN:2 * D_RNN + 2 * D_TMLP]
    zf = z[:, 2 * D_RNN + 2 * D_TMLP:D_IN].astype(BF16)
    for ref, w in ((va_ref, a_ref), (vb_ref, b_ref)):
        v = jnp.dot(zf, w[...], preferred_element_type=F32).astype(BF16)
        if pair:
            half = v.shape[0] // 2
            ref[:, 0:D_FNET] = v[0:half]
            ref[:, D_FNET:2 * D_FNET] = v[half:]
        else:
            ref[...] = v


def _pair_short(L, nseq, row_step):
    return 2 * L <= PROJ_TILE and nseq % 2 == 0 and row_step == 0


def _in_proj(x, pos, mods, g, w_in_b, fa, fb, *, layer, row_base, row_step):
    nseq, L, _ = x.shape
    pair = _pair_short(L, nseq, row_step)
    ns, ll = (nseq // 2, 2 * L) if pair else (nseq, L)
    tm = min(ll, PROJ_TILE)
    add_pos = pos is not None
    in_specs = [pl.BlockSpec((1, tm, D_MODEL), lambda b, t: (b, t, 0))]
    args = [x.reshape(ns, ll, D_MODEL)]
    if add_pos:
        in_specs += [pl.BlockSpec(p.shape, lambda b, t: (0, 0)) for p in pos]
        args += list(pos)
    in_specs += [
        _layer_spec((SUBLANES, N_MOD * D_MODEL), layer),
        _layer_spec((1, D_MODEL), layer),
        _layer_spec((D_MODEL, D_IN), layer),
        _layer_spec((D_FNET, D_FNET), layer),
        _layer_spec((D_FNET, D_FNET), layer),
    ]
    args += [mods, g, w_in_b, fa, fb]
    vspec = (pl.BlockSpec((L, 2 * D_FNET), lambda b, t: (0, b)) if pair
             else pl.BlockSpec((tm, D_FNET), lambda b, t: (t, b)))
    zr, zuv, va, vb = pl.pallas_call(
        functools.partial(_in_kernel, add_pos=add_pos, pair=pair, row_base=row_base, row_step=row_step),
        grid=(ns, ll // tm),
        in_specs=in_specs,
        out_specs=[
            pl.BlockSpec((1, tm, 2 * D_RNN), lambda b, t: (b, t, 0)),
            pl.BlockSpec((1, tm, 2 * D_TMLP), lambda b, t: (b, t, 0)),
            vspec, vspec,
        ],
        out_shape=[
            jax.ShapeDtypeStruct((ns, ll, 2 * D_RNN), F32),
            jax.ShapeDtypeStruct((ns, ll, 2 * D_TMLP), F32),
            jax.ShapeDtypeStruct((L, nseq * D_FNET), BF16),
            jax.ShapeDtypeStruct((L, nseq * D_FNET), BF16),
        ],
        compiler_params=_cparams(("arbitrary", "arbitrary")),
        name=f"in_proj_{L}",
    )(*args)
    return zr.reshape(nseq, L, 2 * D_RNN), zuv.reshape(nseq, L, 2 * D_TMLP), va, vb


def _gelu_tanh(x):
    return 0.5 * x * (1.0 + jnp.tanh(math.sqrt(2.0 / math.pi) * (x + 0.044715 * (x * x * x))))


def _rows_to_tile(rows):
    sub = lax.broadcasted_iota(jnp.int32, (SUBLANES, LANES), 0)
    out = jnp.zeros((SUBLANES, LANES), F32)
    for s, r in enumerate(rows):
        out = jnp.where(sub == s, jnp.broadcast_to(r, (SUBLANES, LANES)), out)
    return out


def _lru_kernel(xr_ref, gr_ref, cw_ref, cb_ref, wa_ref, wx_ref, ba_ref, bx_ref, lam_ref, h0_ref,
                y_ref, st_ref, xnat, pext, af, bf, ab, bb, hfo, pfo, hbo, pbo, hnat, *, L, batched):
    S = L if batched else L // SUBLANES
    pitch = S + SUBLANES
    n = S * SUBLANES
    chunk = 256

    def seg(ref, s):
        return ref.at[s] if batched else ref.at[0, s * S:(s + 1) * S]

    for s in range(SUBLANES):
        xnat[s * pitch:s * pitch + S, :] = seg(xr_ref, s)[...]

    def perm_in(j, c):
        dst = pl.multiple_of((j + 2) * SUBLANES, SUBLANES)
        pext[pl.ds(dst, SUBLANES), :] = xnat[pl.ds(j, SUBLANES, stride=pitch), :]
        return c

    lax.fori_loop(0, S, perm_in, 0, unroll=8)

    sub = lax.broadcasted_iota(jnp.int32, (SUBLANES, LANES), 0)

    def from_prev_segment(v):
        return jnp.where(sub == 0, 0.0, pltpu.roll(v, 1, axis=0))

    def from_next_segment(v):
        return jnp.where(sub == SUBLANES - 1, 0.0, pltpu.roll(v, SUBLANES - 1, axis=0))

    if batched:
        pext[0:16, :] = jnp.zeros((16, LANES), F32)
        pext[(S + 2) * 8:(S + 3) * 8, :] = jnp.zeros((8, LANES), F32)
    else:
        pext[0:8, :] = from_prev_segment(pext[S * 8:(S + 1) * 8, :])
        pext[8:16, :] = from_prev_segment(pext[(S + 1) * 8:(S + 2) * 8, :])
        pext[(S + 2) * 8:(S + 3) * 8, :] = from_next_segment(pext[16:24, :])

    lam = lam_ref[...]
    nl = -lam
    sp = jnp.maximum(nl, 0.0) + jnp.log1p(jnp.exp(-jnp.abs(nl)))
    c_la = (-0.5 * LRU_C) * sp
    ba_h = 0.5 * ba_ref[...]
    bx_h = 0.5 * bx_ref[...]
    a_refs = (af, ab)
    b_refs = (bf, bb)

    def gates(i, c):
        base = pl.multiple_of(i * chunk, chunk)
        xc = (cw_ref[0:1, :] * pext[pl.ds(base, chunk), :]
              + cw_ref[1:2, :] * pext[pl.ds(base + 8, chunk), :]
              + cw_ref[2:3, :] * pext[pl.ds(base + 16, chunk), :]
              + cw_ref[3:4, :] * pext[pl.ds(base + 24, chunk), :]
              + cb_ref[...])
        xcb = xc.astype(BF16)
        xh = 0.5 * xc
        for d in range(2):
            tr = jnp.tanh(jnp.dot(xcb, wa_ref[d, 0], preferred_element_type=F32) + ba_h[d:d + 1, :])
            ti = jnp.tanh(jnp.dot(xcb, wx_ref[d, 0], preferred_element_type=F32) + bx_h[d:d + 1, :])
            log_a = c_la[d:d + 1, :] * (1.0 + tr)
            a = jnp.exp(log_a)
            v = jnp.tanh(log_a) * (-1.0 - a * a)
            coef = jnp.where(v > 0.0, v * lax.rsqrt(v), 0.0)
            a_refs[d][pl.ds(base, chunk), :] = a
            b_refs[d][pl.ds(base, chunk), :] = coef * ((1.0 + ti) * xh)
        return c

    lax.fori_loop(0, n // chunk, gates, 0, unroll=min(4, n // chunk))

    nq = LRU_SUB
    sq = S // nq

    def scan(i, carry):
        out = []
        for q in range(nq):
            hf, pf, hb, pb = carry[4 * q:4 * q + 4]
            jf = pl.multiple_of((q * sq + i) * SUBLANES, SUBLANES)
            jb = pl.multiple_of(((q + 1) * sq - 1 - i) * SUBLANES, SUBLANES)
            a1 = af[pl.ds(jf, SUBLANES), :]
            hf = a1 * hf + bf[pl.ds(jf, SUBLANES), :]
            pf = a1 * pf
            hfo[pl.ds(jf, SUBLANES), :] = hf
            pfo[pl.ds(jf, SUBLANES), :] = pf
            a2 = ab[pl.ds(jb, SUBLANES), :]
            hb = a2 * hb + bb[pl.ds(jb, SUBLANES), :]
            pb = a2 * pb
            hbo[pl.ds(jb, SUBLANES), :] = hb
            pbo[pl.ds(jb, SUBLANES), :] = pb
            out += [hf, pf, hb, pb]
        return tuple(out)

    zero = jnp.zeros((SUBLANES, LANES), F32)
    one = jnp.ones((SUBLANES, LANES), F32)
    ends = lax.fori_loop(0, sq, scan, (zero, one, zero, one) * nq, unroll=min(sq, 4))
    end_f = [(ends[4 * q], ends[4 * q + 1]) for q in range(nq)]
    end_b = [(ends[4 * q + 2], ends[4 * q + 3]) for q in range(nq)]

    hf, pf = end_f[0]
    for q in range(1, nq):
        hf, pf = end_f[q][1] * hf + end_f[q][0], end_f[q][1] * pf
    hb, pb = end_b[nq - 1]
    for q in range(nq - 2, -1, -1):
        hb, pb = end_b[q][1] * hb + end_b[q][0], end_b[q][1] * pb

    if batched:
        start_f = h0_ref[0]
        start_b = h0_ref[1]
        st_ref[0] = pf * start_f + hf
        st_ref[1] = pb * start_b + hb
    else:
        rows_f = [h0_ref[0, 0:1, :]]
        for s in range(1, SUBLANES):
            rows_f.append(pf[s - 1:s, :] * rows_f[-1] + hf[s - 1:s, :])
        st_ref[0, 0:1, :] = pf[7:8, :] * rows_f[7] + hf[7:8, :]
        rows_b = [None] * SUBLANES
        rows_b[7] = h0_ref[0, 1:2, :]
        for s in range(SUBLANES - 2, -1, -1):
            rows_b[s] = pb[s + 1:s + 2, :] * rows_b[s + 1] + hb[s + 1:s + 2, :]
        st_ref[0, 1:2, :] = pb[0:1, :] * rows_b[0] + hb[0:1, :]
        start_f = _rows_to_tile(rows_f)
        start_b = _rows_to_tile(rows_b)

    init_f = [start_f]
    for q in range(nq - 1):
        init_f.append(end_f[q][1] * init_f[q] + end_f[q][0])
    init_b = [None] * nq
    init_b[nq - 1] = start_b
    for q in range(nq - 1, 0, -1):
        init_b[q - 1] = end_b[q][1] * init_b[q] + end_b[q][0]

    for q in range(nq):
        def perm_out(j, c, q=q):
            src = pl.multiple_of(j * SUBLANES, SUBLANES)
            v = (hfo[pl.ds(src, SUBLANES), :] + pfo[pl.ds(src, SUBLANES), :] * init_f[q]
                 + hbo[pl.ds(src, SUBLANES), :] + pbo[pl.ds(src, SUBLANES), :] * init_b[q])
            hnat[pl.ds(j, SUBLANES, stride=pitch), :] = v
            return c

        lax.fori_loop(q * sq, (q + 1) * sq, perm_out, 0, unroll=min(sq, 8))

    for s in range(SUBLANES):
        seg(y_ref, s)[...] = hnat[s * pitch:s * pitch + S, :] * _gelu_tanh(seg(gr_ref, s)[...])


def _lru_mixer(zr, conv_w, conv_b, wa_bd, wx_bd, ba, bx, lam, h0, *, layer):
    nseq, L, _ = zr.shape
    ncb = D_RNN // LRU_CB
    batched = L <= LRU_BATCH_MAX_LEN
    nb = SUBLANES if batched else 1
    S = L if batched else L // SUBLANES
    pitch = S + SUBLANES
    n = S * SUBLANES
    vec2 = pl.BlockSpec((None, 2, LRU_CB), lambda b, c: (layer, 0, c))
    wspec = pl.BlockSpec((None, 2, 1, LRU_CB, LRU_CB), lambda b, c: (layer, 0, c, 0, 0))
    if batched:
        state = pl.BlockSpec((2, nb, LRU_CB), lambda b, c: (0, b, c))
        state_shape = (2, nseq, D_RNN)
    else:
        state = pl.BlockSpec((1, 2, LRU_CB), lambda b, c: (b, 0, c))
        state_shape = (nseq, 2, D_RNN)
    return pl.pallas_call(
        functools.partial(_lru_kernel, L=L, batched=batched),
        grid=(nseq // nb, ncb),
        in_specs=[
            pl.BlockSpec((nb, L, LRU_CB), lambda b, c: (b, 0, c)),
            pl.BlockSpec((nb, L, LRU_CB), lambda b, c: (b, 0, c + ncb)),
            pl.BlockSpec((None, CONV_W, LRU_CB), lambda b, c: (layer, 0, c)),
            pl.BlockSpec((None, 1, LRU_CB), lambda b, c: (layer, 0, c)),
            wspec, wspec, vec2, vec2, vec2,
            state,
        ],
        out_specs=[pl.BlockSpec((nb, L, LRU_CB), lambda b, c: (b, 0, c)), state],
        out_shape=[
            jax.ShapeDtypeStruct((nseq, L, D_RNN), F32),
            jax.ShapeDtypeStruct(state_shape, F32),
        ],
        scratch_shapes=[
            pltpu.VMEM((SUBLANES * pitch, LANES), F32),
            pltpu.VMEM(((S + 3) * SUBLANES, LANES), F32),
            *([pltpu.VMEM((n, LANES), F32)] * 8),
            pltpu.VMEM((SUBLANES * pitch, LANES), F32),
        ],
        compiler_params=_cparams(("arbitrary", "arbitrary")),
        name=f"lru_mixer_{L}",
    )(zr, zr, conv_w, conv_b, wa_bd, wx_bd, ba, bx, lam, h0)


def _out_kernel(*refs, add_pos, pair, row_base, row_step):
    if add_pos:
        (x_ref, ps_ref, pc_ref, yr_ref, zuv_ref, yf_ref, mod_ref, ws_ref, bt_ref, wo_ref, g_ref, wr_ref, br_ref,
         cin_ref, x1_ref, h2_ref, ri_ref, cnt_ref, carry) = refs
    else:
        (x_ref, yr_ref, zuv_ref, yf_ref, mod_ref, ws_ref, bt_ref, wo_ref, g_ref, wr_ref, br_ref, cin_ref,
         x1_ref, h2_ref, ri_ref, cnt_ref, carry) = refs
    tm = x_ref.shape[1]
    x = x_ref[0]
    if add_pos:
        x = x + _pos_tile(ps_ref, pc_ref, pl.program_id(1), tm)
    row = row_base + pl.program_id(0) * row_step
    m = mod_ref[pl.ds(row, 1), :]
    g1 = m[:, 2 * D_MODEL:3 * D_MODEL]
    sh2 = m[:, 3 * D_MODEL:4 * D_MODEL]
    sc2 = m[:, 4 * D_MODEL:5 * D_MODEL]

    head = lax.broadcasted_iota(jnp.int32, (CHUNK, D_TMLP), 1) >> 6
    yt_parts = []
    for ci in range(tm // CHUNK):
        u = zuv_ref[0, ci * CHUNK:(ci + 1) * CHUNK, 0:D_TMLP]
        v = zuv_ref[0, ci * CHUNK:(ci + 1) * CHUNK, D_TMLP:2 * D_TMLP].astype(BF16)
        s = jnp.zeros((CHUNK, D_TMLP), F32)
        for h in range(TMLP_HEADS):
            sh = jnp.dot(ws_ref[h], v, preferred_element_type=F32) + bt_ref[:, h:h + 1]
            s = jnp.where(head == h, sh, s)
        yt_parts.append(u * s)
    yt = jnp.concatenate(yt_parts, axis=0) if len(yt_parts) > 1 else yt_parts[0]

    if pair:
        yf = jnp.concatenate([yf_ref[:, 0:D_FNET], yf_ref[:, D_FNET:2 * D_FNET]], axis=0)
    else:
        yf = yf_ref[...]
    y = (jnp.dot(yr_ref[0].astype(BF16), wo_ref[0:D_RNN, :], preferred_element_type=F32)
         + jnp.dot(yt.astype(BF16), wo_ref[D_RNN:D_RNN + D_TMLP, :], preferred_element_type=F32)
         + jnp.dot(yf.astype(BF16), wo_ref[D_RNN + D_TMLP:D_MODEL, :], preferred_element_type=F32))
    x1 = x + g1 * y
    x1_ref[0] = x1
    h2 = _rms_mod(x1, g_ref[...], sc2, sh2)
    _store_token_major(h2_ref, h2)

    wr = wr_ref[...]
    w_hi = wr.astype(BF16)
    w_lo = (wr - w_hi.astype(F32)).astype(BF16)
    h_hi = h2.astype(BF16)
    h_lo = (h2 - h_hi.astype(F32)).astype(BF16)
    p_hi = jnp.dot(h_hi, jnp.concatenate([w_hi, w_lo], axis=-1), preferred_element_type=F32)
    p_lo = jnp.dot(h_lo, w_hi, preferred_element_type=F32)
    logits = p_hi[:, 0:ROUTER_LANES] + p_hi[:, ROUTER_LANES:2 * ROUTER_LANES] + p_lo + br_ref[...]
    lane = lax.broadcasted_iota(jnp.int32, (tm, ROUTER_LANES), 1)
    lane_f = lane.astype(F32)
    is_g = lane < N_EGROUPS
    gl = jnp.where(is_g, logits, NEG_BIG)
    gmax = jnp.max(gl, axis=-1, keepdims=True)
    gsel = jnp.min(jnp.where(gl == gmax, lane_f, 1e4), axis=-1, keepdims=True)
    pg = 1.0 / jnp.sum(jnp.where(is_g, jnp.exp(logits - gmax), 0.0), axis=-1, keepdims=True)
    grp_f = ((lane - N_EGROUPS) >> 3).astype(F32)
    emask = (lane >= N_EGROUPS) & (lane < N_EGROUPS + N_EXPERTS) & (grp_f == gsel)
    el = jnp.where(emask, logits, NEG_BIG)
    v1 = jnp.max(el, axis=-1, keepdims=True)
    i1 = jnp.min(jnp.where(el == v1, lane_f, 1e4), axis=-1, keepdims=True)
    el2 = jnp.where(lane_f == i1, NEG_BIG, el)
    v2 = jnp.max(el2, axis=-1, keepdims=True)
    i2 = jnp.min(jnp.where(el2 == v2, lane_f, 1e4), axis=-1, keepdims=True)
    e2x = jnp.exp(v2 - v1)
    fw1 = 1.0 / (1.0 + e2x)
    fw2 = e2x * fw1
    @pl.when((pl.program_id(0) == 0) & (pl.program_id(1) == 0))
    def _():
        carry[...] = cin_ref[...]

    e1 = i1 - N_EGROUPS
    e2 = i2 - N_EGROUPS
    m1 = lane_f == e1
    m2 = lane_f == e2
    oh = jnp.where(m1 | m2, 1.0, 0.0)
    r_i = lax.broadcasted_iota(jnp.int32, (tm, tm), 0)
    c_i = lax.broadcasted_iota(jnp.int32, (tm, tm), 1)
    tri = jnp.where(c_i < r_i, 1.0, 0.0).astype(BF16)
    before = jnp.dot(tri, oh.astype(BF16), preferred_element_type=F32) + carry[0:1, :]
    rank1 = jnp.sum(jnp.where(m1, before, 0.0), axis=-1, keepdims=True)
    rank2 = jnp.sum(jnp.where(m2, before, 0.0), axis=-1, keepdims=True)
    total = carry[0:1, :] + jnp.sum(oh, axis=0, keepdims=True)
    carry[0:1, :] = total
    cnt_ref[...] = jnp.broadcast_to(total, cnt_ref.shape)

    vals = (e1, e2, pg * fw1, pg * fw2, rank1, rank2)
    ri = jnp.zeros((tm, ROUTER_LANES), F32)
    for k, v in enumerate(vals):
        ri = jnp.where(lane == k, v, ri)
    ri_ref[0] = ri


def _layer_spec(shape, layer):
    zeros = (0,) * len(shape)
    return pl.BlockSpec((None, *shape), lambda *_: (layer, *zeros))


def _out_proj(x, pos, yr, zuv, yf, mods, ws_b, bt, wo_b, g, wr, br, cin, *, layer, row_base, row_step):
    nseq, L, _ = x.shape
    pair = _pair_short(L, nseq, row_step)
    ns, ll = (nseq // 2, 2 * L) if pair else (nseq, L)
    tm = min(ll, PROJ_TILE)
    add_pos = pos is not None
    in_specs = [pl.BlockSpec((1, tm, D_MODEL), lambda b, t: (b, t, 0))]
    args = [x.reshape(ns, ll, D_MODEL)]
    if add_pos:
        in_specs += [pl.BlockSpec(p.shape, lambda b, t: (0, 0)) for p in pos]
        args += list(pos)
    in_specs += [
        pl.BlockSpec((1, tm, D_RNN), lambda b, t: (b, t, 0)),
        pl.BlockSpec((1, tm, 2 * D_TMLP), lambda b, t: (b, t, 0)),
        (pl.BlockSpec((L, 2 * D_FNET), lambda b, t: (0, b)) if pair
         else pl.BlockSpec((tm, D_FNET), lambda b, t: (t, b))),
        _layer_spec((SUBLANES, N_MOD * D_MODEL), layer),
        _layer_spec((TMLP_HEADS, CHUNK, CHUNK), layer),
        _layer_spec((CHUNK, TMLP_HEADS), layer),
        _layer_spec((D_MODEL, D_MODEL), layer),
        _layer_spec((1, D_MODEL), layer),
        _layer_spec((D_MODEL, ROUTER_LANES), layer),
        _layer_spec((1, ROUTER_LANES), layer),
        pl.BlockSpec((SUBLANES, ROUTER_LANES), lambda b, t: (0, 0)),
    ]
    args += [yr.reshape(ns, ll, D_RNN), zuv.reshape(ns, ll, 2 * D_TMLP), yf, mods, ws_b, bt, wo_b, g, wr, br, cin]
    tok = pl.BlockSpec((1, tm, D_MODEL), lambda b, t: (b, t, 0))
    x1, h2, ri, cnt = pl.pallas_call(
        functools.partial(_out_kernel, add_pos=add_pos, pair=pair, row_base=row_base, row_step=row_step),
        grid=(ns, ll // tm),
        in_specs=in_specs,
        out_specs=[tok, pl.BlockSpec((tm * LANE_TILES, LANES), lambda b, t: (b * (ll // tm) + t, 0)),
                   pl.BlockSpec((1, tm, ROUTER_LANES), lambda b, t: (b, t, 0)),
                   pl.BlockSpec((SUBLANES, ROUTER_LANES), lambda b, t: (0, 0))],
        out_shape=[
            jax.ShapeDtypeStruct((ns, ll, D_MODEL), F32),
            jax.ShapeDtypeStruct((nseq * L * LANE_TILES, LANES), F32),
            jax.ShapeDtypeStruct((ns, ll, ROUTER_LANES), F32),
            jax.ShapeDtypeStruct((SUBLANES, ROUTER_LANES), F32),
        ],
        scratch_shapes=[pltpu.VMEM((SUBLANES, ROUTER_LANES), F32)],
        compiler_params=_cparams(("arbitrary", "arbitrary")),
        name=f"out_proj_{L}",
    )(*args)
    return x1.reshape(nseq, L, D_MODEL), h2, ri.reshape(nseq, L, ROUTER_LANES), cnt


def _store_token_major(ref, x):
    tm = x.shape[0]
    for j in range(LANE_TILES):
        ref[pl.ds(j, tm, stride=LANE_TILES), :] = x[:, j * LANES:(j + 1) * LANES]


def _load_token_major(ref):
    tm = ref.shape[0] // LANE_TILES
    return jnp.concatenate([ref[pl.ds(j, tm, stride=LANE_TILES), :] for j in range(LANE_TILES)], axis=-1)


def _token_copy(src_ref, src_tok, dst_ref, dst_tok, sem):
    return pltpu.make_async_copy(src_ref.at[pl.ds(pl.multiple_of(src_tok, LANE_TILES), LANE_TILES)],
                                 dst_ref.at[pl.ds(pl.multiple_of(dst_tok, LANE_TILES), LANE_TILES)], sem)


def _scatter_rows(pos_ref, h_ref, xs_ref, sem):
    rows = h_ref.shape[0]

    for r in range(0, rows, LANE_TILES):
        _token_copy(h_ref, r, xs_ref, pos_ref[0, 0, r // LANE_TILES], sem).start(priority=0)
        _token_copy(h_ref, r, xs_ref, pos_ref[0, 1, r // LANE_TILES], sem).start(priority=1)
    for _ in range(2):
        pltpu.make_async_copy(h_ref, xs_ref.at[pl.ds(0, rows)], sem).wait()


def _dispatch_kernel(pos_ref, hp_ref, hs_ref, xs_ref, sem, *, n_first):
    i = pl.program_id(0)

    @pl.when(i < n_first)
    def _():
        _scatter_rows(pos_ref, hp_ref, xs_ref, sem)

    @pl.when(i >= n_first)
    def _():
        _scatter_rows(pos_ref, hs_ref, xs_ref, sem)


def _dispatch(pos, h_first, h_second):
    tm = ROW_TILE
    blk = tm * LANE_TILES
    n_first = h_first.shape[0] // blk
    n_second = h_second.shape[0] // blk
    nrows = 2 * (h_first.shape[0] + h_second.shape[0])
    return pl.pallas_call(
        functools.partial(_dispatch_kernel, n_first=n_first),
        grid=(n_first + n_second,),
        in_specs=[
            pl.BlockSpec((1, 2, tm), lambda i: (i, 0, 0), memory_space=pltpu.SMEM),
            pl.BlockSpec((blk, LANES), lambda i: (jnp.minimum(i, n_first - 1), 0)),
            pl.BlockSpec((blk, LANES), lambda i: (jnp.maximum(i - n_first, 0), 0)),
        ],
        out_specs=pl.BlockSpec(memory_space=pl.ANY),
        out_shape=jax.ShapeDtypeStruct((nrows, LANES), F32),
        scratch_shapes=[pltpu.SemaphoreType.DMA(())],
        compiler_params=_cparams(("arbitrary",)),
        name="moe_dispatch",
    )(pos, h_first, h_second)


WEIGHT_LEAD = (3, 2, 1)
WEIGHT_SLOTS = 4
LEAD_ITEMS = max(WEIGHT_LEAD)


def _expert_kernel(wt_ref, we_ref, lo_ref, hi_ref, first_ref, ord_ref, xs_ref, w1_ref, w3_ref, w2_ref, ys_ref,
                   w1b, w3b, w2b):
    s = pl.program_id(0)
    last = pl.num_programs(0) - 1
    tm = EXPERT_TILE

    def stage(lead, src, dst):
        cur = jnp.minimum(s + lead, last)
        prv = jnp.minimum(s + lead - 1, last)

        @pl.when((s == 0) | (we_ref[cur] != we_ref[prv]))
        def _():
            dst[ord_ref[cur] & (WEIGHT_SLOTS - 1)] = src[0].astype(BF16)

    for lead, src, dst in zip(WEIGHT_LEAD, (w1_ref, w3_ref, w2_ref), (w1b, w3b, w2b)):
        stage(lead, src, dst)

    lo = lo_ref[s]
    hi = hi_ref[s]
    full = (lo == 0) & (hi == tm)

    @pl.when((first_ref[s] == 1) & jnp.logical_not(full))
    def _():
        ys_ref[...] = jnp.zeros_like(ys_ref)

    slot = ord_ref[s] & (WEIGHT_SLOTS - 1)

    def run(t0, nt, merge):
        xs_part = xs_ref.at[pl.ds(t0 * LANE_TILES, nt * LANE_TILES)]
        ys_part = ys_ref.at[pl.ds(t0 * LANE_TILES, nt * LANE_TILES)]
        x = _load_token_major(xs_part).astype(BF16)
        a = jnp.dot(x, w1b[slot], preferred_element_type=F32)
        b = jnp.dot(x, w3b[slot], preferred_element_type=F32)
        hid = (a * jax.nn.sigmoid(a)) * b
        res = jnp.dot(hid.astype(BF16), w2b[slot], preferred_element_type=F32)
        if merge:
            rows = lax.broadcasted_iota(jnp.int32, res.shape, 0) + t0
            res = jnp.where((rows >= lo) & (rows < hi), res, _load_token_major(ys_part))
        _store_token_major(ys_part, res)

    pl.when(full)(lambda: run(0, tm, False))
    partial = (hi > lo) & jnp.logical_not(full)
    for t0 in range(0, tm, EXPERT_PART):
        pl.when(partial & (lo < t0 + EXPERT_PART) & (hi > t0))(functools.partial(run, t0, EXPERT_PART, True))


def _experts(items, xs, w1, w3, w2):
    nw = items[0].shape[0]

    def row(s, wt, we, lo, hi, fi, od):
        return (wt[s], 0)

    def weight(lead):
        return lambda s, wt, we, lo, hi, fi, od: (we[jnp.minimum(s + lead, nw - 1)], 0, 0)

    return pl.pallas_call(
        _expert_kernel,
        grid_spec=pltpu.PrefetchScalarGridSpec(
            num_scalar_prefetch=6,
            grid=(nw,),
            in_specs=[
                pl.BlockSpec((EXPERT_TILE * LANE_TILES, LANES), row),
                pl.BlockSpec((1, D_MODEL, D_EXPERT), weight(WEIGHT_LEAD[0])),
                pl.BlockSpec((1, D_MODEL, D_EXPERT), weight(WEIGHT_LEAD[1])),
                pl.BlockSpec((1, D_EXPERT, D_MODEL), weight(WEIGHT_LEAD[2])),
            ],
            out_specs=pl.BlockSpec((EXPERT_TILE * LANE_TILES, LANES), row),
            scratch_shapes=[
                pltpu.VMEM((WEIGHT_SLOTS, D_MODEL, D_EXPERT), BF16),
                pltpu.VMEM((WEIGHT_SLOTS, D_MODEL, D_EXPERT), BF16),
                pltpu.VMEM((WEIGHT_SLOTS, D_EXPERT, D_MODEL), BF16),
            ],
        ),
        out_shape=jax.ShapeDtypeStruct(xs.shape, F32),
        compiler_params=_cparams(("arbitrary",)),
        name="moe_experts",
    )(*items, xs, w1, w3, w2)


def _combine_kernel(pos_ref, posn_ref, x1_ref, ri_ref, mod_ref, gf_ref, ys_ref, o_ref, ybuf, sem,
                    *, row_base, row_step, tiles_per_seq, final):
    tm = x1_ref.shape[0]
    i = pl.program_id(0)
    n = pl.num_programs(0)

    def gather(p_ref, slot):
        for r in range(tm):
            _token_copy(ys_ref, p_ref[0, 0, r], ybuf.at[slot, 0], r * LANE_TILES, sem.at[slot]).start(priority=0)
            _token_copy(ys_ref, p_ref[0, 1, r], ybuf.at[slot, 1], r * LANE_TILES, sem.at[slot]).start(priority=1)

    def drain(slot):
        for k in range(2):
            pltpu.make_async_copy(ys_ref.at[pl.ds(0, tm * LANE_TILES)], ybuf.at[slot, k], sem.at[slot]).wait()

    def tile(slot):
        if slot == 0:
            @pl.when(i == 0)
            def _():
                gather(pos_ref, 0)

        @pl.when(i + 1 < n)
        def _():
            gather(posn_ref, 1 - slot)

        drain(slot)
        row = row_base + (i // tiles_per_seq) * row_step
        g2 = mod_ref[pl.ds(row, 1), 5 * D_MODEL:6 * D_MODEL]
        ri = ri_ref[...]
        y = (ri[:, 2:3] * _load_token_major(ybuf.at[slot, 0])
             + ri[:, 3:4] * _load_token_major(ybuf.at[slot, 1]))
        x2 = x1_ref[...] + g2 * y
        if final:
            ms = jnp.mean(x2 * x2, axis=-1, keepdims=True)
            o_ref[...] = (x2 * lax.rsqrt(ms + EPS)) * gf_ref[...]
        else:
            o_ref[...] = x2

    for slot in range(2):
        pl.when((i & 1) == slot)(functools.partial(tile, slot))


def _combine(pos, x1, ri, mods, g_final, ys, *, layer, row_base, row_step, tiles_per_seq, final):
    ntok = x1.shape[0]
    tm = ROW_TILE
    tok = pl.BlockSpec((tm, D_MODEL), lambda i: (i, 0))
    return pl.pallas_call(
        functools.partial(_combine_kernel, row_base=row_base, row_step=row_step, tiles_per_seq=tiles_per_seq,
                          final=final),
        grid=(ntok // tm,),
        in_specs=[
            pl.BlockSpec((1, 2, tm), lambda i: (i, 0, 0), memory_space=pltpu.SMEM),
            pl.BlockSpec((1, 2, tm), lambda i: (jnp.minimum(i + 1, ntok // tm - 1), 0, 0), memory_space=pltpu.SMEM),
            tok,
            pl.BlockSpec((tm, ROUTER_LANES), lambda i: (i, 0)),
            _layer_spec((SUBLANES, N_MOD * D_MODEL), layer),
            pl.BlockSpec((1, D_MODEL), lambda i: (0, 0)),
            pl.BlockSpec(memory_space=pl.ANY),
        ],
        out_specs=tok,
        out_shape=jax.ShapeDtypeStruct((ntok, D_MODEL), F32),
        scratch_shapes=[
            pltpu.VMEM((2, 2, tm * LANE_TILES, LANES), F32),
            pltpu.SemaphoreType.DMA((2,)),
        ],
        compiler_params=_cparams(("arbitrary",)),
        name=f"moe_combine_{ntok}",
    )(pos, pos, x1, ri, mods, g_final, ys)


def _route(ri_first, ri_second, cnt):
    counts = cnt[0, 0:N_EXPERTS].astype(jnp.int32)
    offs = jnp.cumsum(counts) - counts
    info = jnp.concatenate([ri_first[:, 0:SUBLANES], ri_second[:, 0:SUBLANES]], axis=0).T
    vals = info.astype(jnp.int32)
    base = jnp.zeros_like(vals)
    for k in range(N_EXPERTS):
        base = jnp.where(vals == k, offs[k], base)
    slot = (base[0:2] + vals[4:6]) * LANE_TILES
    n_tiles = slot.shape[1] // ROW_TILE
    pos_tiles = jnp.swapaxes(slot.reshape(2, n_tiles, ROW_TILE), 0, 1)
    return pos_tiles, counts, offs


def _work_items(counts, offs, nw, layer):
    tm = EXPERT_TILE
    first_tile = offs // tm
    last_tile = (offs + counts - 1) // tm
    n_e = jnp.where(counts > 0, last_tile - first_tile + 1, 0)
    w_end = jnp.cumsum(n_e)
    w_start = w_end - n_e
    total = w_end[-1]
    w = jnp.arange(nw, dtype=jnp.int32)
    wc = jnp.minimum(w, total - 1)
    e_w = jnp.sum((wc[:, None] >= w_end[None, :]).astype(jnp.int32), axis=-1)
    sel = (e_w[:, None] == jnp.arange(N_EXPERTS, dtype=jnp.int32)).astype(jnp.int32)
    pick = lambda v: jnp.sum(sel * v[None, :], axis=-1)
    off_w = pick(offs)
    tile_w = pick(first_tile) + (wc - pick(w_start))
    lo = jnp.clip(off_w - tile_w * tm, 0, tm)
    hi = jnp.clip(off_w + pick(counts) - tile_w * tm, 0, tm)
    valid = w < total
    lo = jnp.where(valid, lo, 0)
    hi = jnp.where(valid, hi, 0)
    prev_tile = jnp.concatenate([jnp.full((1,), -1, tile_w.dtype), tile_w[:-1]])
    first = tile_w != prev_tile
    ordinal = pick(jnp.cumsum((counts > 0).astype(jnp.int32)) - 1)

    def lead(a, fill=None):
        head = jnp.broadcast_to(a[0] if fill is None else jnp.asarray(fill, a.dtype), (LEAD_ITEMS,))
        return jnp.concatenate([head, a]).astype(jnp.int32)

    return (lead(tile_w), lead(e_w + layer * N_EXPERTS), lead(lo, 0), lead(hi, 0), lead(first, 0), lead(ordinal))


def _block_diag(w, nblk):
    *lead, H, d, _ = w.shape
    w = w.reshape(*lead, H // nblk, nblk, d, d)
    eye = jnp.eye(nblk, dtype=w.dtype)
    out = jnp.einsum('...gij,gh->...gihj', w, eye)
    return out.reshape(*lead, H // nblk, nblk * d, nblk * d)


def kernel(x_prompt, x_sample, state_lru, c, c_ctx, w_ada, b_ada, g_mix, g_ffn, g_final, w_in, w_out, conv_w, conv_b, lru_wa, lru_ba, lru_wx, lru_bx, lru_lambda, tmlp_ws, tmlp_b, fnet_w, router_g, router_g_b, router_e, router_e_b, e_w1, e_w3, e_w2):
    bp, lp, _ = x_prompt.shape
    bs, ls, _ = x_sample.shape
    n_ctx = bp * lp
    n_lat = bs * ls
    n_tok = n_ctx + n_lat

    cond8 = jnp.concatenate([c_ctx[None, :], c, jnp.zeros((SUBLANES - 1 - bs, D_MODEL), F32)], axis=0)
    pos = _pos_tables(max(ls // GRID_W, GRID_W))
    w_in_b = w_in.astype(BF16)
    w_out_b = w_out.astype(BF16)
    heads_per_cb = LRU_CB // RNN_HEAD_DIM
    wa_bd = (0.5 * _block_diag(lru_wa, heads_per_cb)).astype(BF16)
    wx_bd = (0.5 * _block_diag(lru_wx, heads_per_cb)).astype(BF16)
    ws_b = tmlp_ws.astype(BF16)
    bt = jnp.swapaxes(tmlp_b, 1, 2)
    fnet_bd = _block_diag(fnet_w, D_FNET // FNET_GROUP_DIM)[:, 0]
    wr = jnp.concatenate([router_g, router_e,
                          jnp.zeros((DEPTH, D_MODEL, ROUTER_LANES - N_EGROUPS - N_EXPERTS), F32)], axis=-1)
    br = jnp.concatenate([router_g_b, router_e_b,
                          jnp.zeros((DEPTH, ROUTER_LANES - N_EGROUPS - N_EXPERTS), F32)], axis=-1)
    ew1 = e_w1.reshape(DEPTH * N_EXPERTS, D_MODEL, D_EXPERT)
    ew3 = e_w3.reshape(DEPTH * N_EXPERTS, D_MODEL, D_EXPERT)
    ew2 = e_w2.reshape(DEPTH * N_EXPERTS, D_EXPERT, D_MODEL)

    mods = _modulation(cond8, w_ada, b_ada)
    fa, fb = _fnet_prep(fnet_bd)
    ct_p, st_p = _dft_tables(lp)
    ct_s, st_s = _dft_tables(ls)

    nw = 2 * n_tok // EXPERT_TILE + N_EXPERTS
    h0_p = jnp.zeros((bp, 2, D_RNN), F32)
    g_mix3 = g_mix[:, None, :]
    g_ffn3 = g_ffn[:, None, :]
    conv_b3 = conv_b[:, None, :]
    br3 = br[:, None, :]

    xp, xs = x_prompt, x_sample
    states = []
    for l in range(DEPTH):
        paths = []
        cnt = jnp.zeros((SUBLANES, ROUTER_LANES), F32)
        for (x, ct, st, h0, row_base, row_step, is_lat) in (
                (xp, ct_p, st_p, h0_p, 0, 0, False),
                (xs, ct_s, st_s, state_lru[:, l], 1, 1, True)):
            pe = pos if (is_lat and l == 0) else None
            zr, zuv, va, vb = _in_proj(x, pe, mods, g_mix3, w_in_b, fa, fb,
                                       layer=l, row_base=row_base, row_step=row_step)
            swap = x.shape[1] <= LRU_BATCH_MAX_LEN
            h0 = jnp.swapaxes(h0, 0, 1) if swap else h0
            yr, st_new = _lru_mixer(zr, conv_w, conv_b3, wa_bd, wx_bd, lru_ba, lru_bx, lru_lambda, h0, layer=l)
            st_new = jnp.swapaxes(st_new, 0, 1) if swap else st_new
            yf = _dft_apply(ct, st, va, vb)
            x1, h2, ri, cnt = _out_proj(x, pe, yr, zuv, yf, mods, ws_b, bt, w_out_b, g_ffn3, wr, br3, cnt,
                                        layer=l, row_base=row_base, row_step=row_step)
            paths.append((x1, h2, ri, st_new))
        states.append(paths[0][3])

        pos_tiles, counts, offs = _route(paths[0][2].reshape(n_ctx, ROUTER_LANES),
                                         paths[1][2].reshape(n_lat, ROUTER_LANES), cnt)
        items = _work_items(counts, offs, nw, l)
        nct = n_ctx // ROW_TILE
        xsorted = _dispatch(pos_tiles, paths[0][1], paths[1][1])
        ysorted = _experts(items, xsorted, ew1, ew3, ew2)
        gfin = g_final[None, :]
        final = l == DEPTH - 1
        xp = _combine(pos_tiles[:nct], paths[0][0].reshape(n_ctx, D_MODEL),
                      paths[0][2].reshape(n_ctx, ROUTER_LANES), mods, gfin, ysorted, layer=l,
                      row_base=0, row_step=0, tiles_per_seq=lp // ROW_TILE, final=final).reshape(bp, lp, D_MODEL)
        xs = _combine(pos_tiles[nct:], paths[1][0].reshape(n_lat, D_MODEL),
                      paths[1][2].reshape(n_lat, ROUTER_LANES), mods, gfin, ysorted, layer=l,
                      row_base=1, row_step=1, tiles_per_seq=ls // ROW_TILE, final=final).reshape(bs, ls, D_MODEL)

    new_state = jnp.stack(states, axis=1).astype(x_prompt.dtype)
    return (xp, xs, new_state)
```

```python
import functools
import math

import jax
import jax.numpy as jnp
from jax import lax
from jax.experimental import pallas as pl
from jax.experimental.pallas import tpu as pltpu

F32 = jnp.float32
BF16 = jnp.bfloat16

D_MODEL = 1024
DEPTH = 2
GRID_W = 64
D_RNN = 512
RNN_HEAD_DIM = 64
CONV_W = 4
LRU_C = 8.0
D_TMLP = 256
TMLP_HEADS = 4
CHUNK = 128
D_FNET = 256
FNET_GROUP_DIM = 64
D_IN = 2 * D_RNN + 2 * D_TMLP + D_FNET
N_EGROUPS = 4
N_EPG = 8
N_EXPERTS = N_EGROUPS * N_EPG
D_EXPERT = 512
N_MOD = 6
EPS = 1e-6

LANES = 128
SUBLANES = 8
LANE_TILES = D_MODEL // LANES
assert LANE_TILES == SUBLANES
LRU_CB = LANES
LRU_SUB = 4
LRU_BATCH_MAX_LEN = 512
ROW_TILE = 256
PROJ_TILE = 512
EXPERT_TILE = 512
EXPERT_PART = 256
ROUTER_LANES = LANES
NEG_BIG = -1e30
VMEM_LIMIT = 56 * 1024 * 1024


def _cparams(sem):
    return pltpu.CompilerParams(dimension_semantics=sem, vmem_limit_bytes=VMEM_LIMIT)


def _mod_kernel(c_ref, w_ref, b_ref, o_ref):
    c = c_ref[...]
    s = c * jax.nn.sigmoid(c)
    o_ref[0] = jnp.dot(s.astype(BF16), w_ref[0].astype(BF16), preferred_element_type=F32) + b_ref[0]


def _modulation(cond8, w_ada, b_ada):
    tn = 1536
    return pl.pallas_call(
        _mod_kernel,
        grid=(DEPTH, N_MOD * D_MODEL // tn),
        in_specs=[
            pl.BlockSpec((SUBLANES, D_MODEL), lambda l, j: (0, 0)),
            pl.BlockSpec((1, D_MODEL, tn), lambda l, j: (l, 0, j)),
            pl.BlockSpec((1, 1, tn), lambda l, j: (l, 0, j)),
        ],
        out_specs=pl.BlockSpec((1, SUBLANES, tn), lambda l, j: (l, 0, j)),
        out_shape=jax.ShapeDtypeStruct((DEPTH, SUBLANES, N_MOD * D_MODEL), F32),
        compiler_params=_cparams(("arbitrary", "arbitrary")),
        name="modulation",
    )(cond8, w_ada, b_ada.reshape(DEPTH, 1, N_MOD * D_MODEL))


def _fprep_kernel(w_ref, a_ref, b_ref):
    r = lax.broadcasted_iota(jnp.int32, (D_FNET, D_FNET), 0)
    c = lax.broadcasted_iota(jnp.int32, (D_FNET, D_FNET), 1)
    same = (r >> 6) == (c >> 6)
    ph = ((r & 63) * (c & 63)) & 63
    ang = ph.astype(F32) * (2.0 * math.pi / FNET_GROUP_DIM)
    scale = 1.0 / math.sqrt(FNET_GROUP_DIM)
    cm = jnp.where(same, jnp.cos(ang) * scale, 0.0)
    sm = jnp.where(same, jnp.sin(ang) * scale, 0.0)
    w = w_ref[0]
    a_ref[0] = jnp.dot(cm, w, precision=lax.Precision.HIGHEST, preferred_element_type=F32).astype(BF16)
    b_ref[0] = jnp.dot(sm, w, precision=lax.Precision.HIGHEST, preferred_element_type=F32).astype(BF16)


def _fnet_prep(wbd):
    spec = pl.BlockSpec((1, D_FNET, D_FNET), lambda l: (l, 0, 0))
    return pl.pallas_call(
        _fprep_kernel,
        grid=(DEPTH,),
        in_specs=[spec],
        out_specs=[spec, spec],
        out_shape=[jax.ShapeDtypeStruct((DEPTH, D_FNET, D_FNET), BF16)] * 2,
        compiler_params=_cparams(("arbitrary",)),
        name="fnet_prep",
    )(wbd)


TABLE_ROWS = 64
TABLE_BLOCKS = 8


def _base_table_kernel(cj_ref, sj_ref, cm_ref, sm_ref, *, L):
    j = lax.broadcasted_iota(jnp.int32, (TABLE_ROWS, L // 2), 0)
    n = lax.broadcasted_iota(jnp.int32, (TABLE_ROWS, L // 2), 1)
    w = 2.0 * math.pi / L
    fine = ((j * n) & (L - 1)).astype(F32) * w
    coarse = ((j * TABLE_ROWS * n) & (L - 1)).astype(F32) * w
    scale = 1.0 / math.sqrt(L)
    cj_ref[...] = jnp.cos(fine)
    sj_ref[...] = jnp.sin(fine)
    cm_ref[...] = jnp.cos(coarse) * scale
    sm_ref[...] = jnp.sin(coarse) * scale


def _table_kernel(cj_ref, sj_ref, cm_ref, sm_ref, c_ref, s_ref):
    cj = cj_ref[...]
    sj = sj_ref[...]
    for q in range(c_ref.shape[0] // TABLE_ROWS):
        m = pl.program_id(0) * (c_ref.shape[0] // TABLE_ROWS) + q
        c0 = cm_ref[pl.ds(m, 1), :]
        s0 = sm_ref[pl.ds(m, 1), :]
        c_ref[q * TABLE_ROWS:(q + 1) * TABLE_ROWS, :] = (cj * c0 - sj * s0).astype(BF16)
        s_ref[q * TABLE_ROWS:(q + 1) * TABLE_ROWS, :] = (-(sj * c0 + cj * s0)).astype(BF16)


def _dft_tables(L):
    half = L // 2
    small = pl.BlockSpec((TABLE_ROWS, half), lambda *_: (0, 0))
    base = pl.pallas_call(
        functools.partial(_base_table_kernel, L=L),
        out_specs=[small] * 4,
        out_shape=[jax.ShapeDtypeStruct((TABLE_ROWS, half), F32)] * 4,
        compiler_params=pltpu.CompilerParams(vmem_limit_bytes=VMEM_LIMIT),
        name=f"dft_base_tables_{L}",
    )()
    rows = min(L, TABLE_ROWS * TABLE_BLOCKS)
    spec = pl.BlockSpec((rows, half), lambda i: (i, 0))
    return pl.pallas_call(
        _table_kernel,
        grid=(L // rows,),
        in_specs=[small] * 4,
        out_specs=[spec, spec],
        out_shape=[jax.ShapeDtypeStruct((L, half), BF16)] * 2,
        compiler_params=_cparams(("arbitrary",)),
        name=f"dft_tables_{L}",
    )(*base)


def _dft_kernel(c_ref, s_ref, va_ref, vb_ref, o_ref, vaf, vbf, *, L):
    tk, tc = o_ref.shape
    half = L // 2
    fb = min(256, half)

    @pl.when(pl.program_id(1) == 0)
    def _():
        r = lax.broadcasted_iota(jnp.int32, (fb, fb), 0)
        c = lax.broadcasted_iota(jnp.int32, (fb, fb), 1)
        flip = jnp.where(r + c == fb, 1.0, 0.0).astype(BF16)
        first_row = lax.broadcasted_iota(jnp.int32, (fb, tc), 0) == 0
        for m in range(half // fb):
            pm = L // fb - m - 1
            for src, dst, sign in ((va_ref, vaf, 1.0), (vb_ref, vbf, -1.0)):
                partner = jnp.dot(flip, src[pm * fb:(pm + 1) * fb, :], preferred_element_type=F32)
                if m >= 1:
                    partner = partner + jnp.where(
                        first_row, src[(pm + 1) * fb:(pm + 1) * fb + 1, :].astype(F32), 0.0)
                dst[m * fb:(m + 1) * fb, :] = (src[m * fb:(m + 1) * fb, :].astype(F32) + sign * partner).astype(BF16)

    k = lax.broadcasted_iota(jnp.int32, (tk, 1), 0) + pl.program_id(1) * tk
    alt = (1 - 2 * (k & 1)).astype(F32) * (1.0 / math.sqrt(L))
    o_ref[...] = (jnp.dot(c_ref[...], vaf[...], preferred_element_type=F32)
                  + jnp.dot(s_ref[...], vbf[...], preferred_element_type=F32)
                  + alt * va_ref[half:half + 1, :].astype(F32))


def _dft_apply(ct, st, va, vb):
    L, ncols = va.shape
    half = L // 2
    tk, tc = 256, 1024
    return pl.pallas_call(
        functools.partial(_dft_kernel, L=L),
        grid=(ncols // tc, L // tk),
        in_specs=[
            pl.BlockSpec((tk, half), lambda j, i: (i, 0)),
            pl.BlockSpec((tk, half), lambda j, i: (i, 0)),
            pl.BlockSpec((L, tc), lambda j, i: (0, j)),
            pl.BlockSpec((L, tc), lambda j, i: (0, j)),
        ],
        out_specs=pl.BlockSpec((tk, tc), lambda j, i: (i, j)),
        out_shape=jax.ShapeDtypeStruct((L, ncols), F32),
        scratch_shapes=[pltpu.VMEM((half, tc), BF16), pltpu.VMEM((half, tc), BF16)],
        compiler_params=_cparams(("arbitrary", "arbitrary")),
        name=f"dft_apply_{L}",
    )(ct, st, va, vb)


def _rms_mod(x, g, scale, shift):
    ms = jnp.mean(x * x, axis=-1, keepdims=True)
    return (x * lax.rsqrt(ms + EPS)) * g * (1.0 + scale) + shift


def _pos_table_kernel(s_ref, c_ref):
    n, nf = s_ref.shape
    j = lax.broadcasted_iota(jnp.int32, (n, nf), 0).astype(F32)
    k = lax.broadcasted_iota(jnp.int32, (n, nf), 1).astype(F32)
    ang = j * jnp.exp(k * (-math.log(10000.0) / nf))
    s_ref[...] = jnp.sin(ang)
    c_ref[...] = jnp.cos(ang)


def _pos_tables(n):
    nf = D_MODEL // 4
    return pl.pallas_call(
        _pos_table_kernel,
        out_shape=[jax.ShapeDtypeStruct((n, nf), F32)] * 2,
        name="pos_tables",
    )()


def _pos_tile(ps_ref, pc_ref, t, tm):
    nrow = tm // GRID_W
    nf = D_MODEL // 4

    def rows(tab):
        return jnp.concatenate(
            [jnp.broadcast_to(tab[pl.ds(t * nrow + k, 1), :], (GRID_W, nf)) for k in range(nrow)], axis=0)

    def cols(tab):
        return jnp.concatenate([tab[0:GRID_W, :]] * nrow, axis=0)

    return jnp.concatenate([rows(ps_ref), rows(pc_ref), cols(ps_ref), cols(pc_ref)], axis=-1)


def _in_kernel(*refs, add_pos, pair, row_base, row_step):
    if add_pos:
        x_ref, ps_ref, pc_ref, mod_ref, g_ref, w_ref, a_ref, b_ref, zr_ref, zuv_ref, va_ref, vb_ref = refs
    else:
        x_ref, mod_ref, g_ref, w_ref, a_ref, b_ref, zr_ref, zuv_ref, va_ref, vb_ref = refs
    x = x_ref[0]
    if add_pos:
        x = x + _pos_tile(ps_ref, pc_ref, pl.program_id(1), x.shape[0])
    row = row_base + pl.program_id(0) * row_step
    m = mod_ref[pl.ds(row, 1), :]
    h = _rms_mod(x, g_ref[...], m[:, D_MODEL:2 * D_MODEL], m[:, 0:D_MODEL])
    z = jnp.dot(h.astype(BF16), w_ref[...], preferred_element_type=F32)
    zr_ref[0] = z[:, 0:2 * D_RNN]
    zuv_ref[0] = z[:, 2 * D_RNN:2 * D_RNN + 2 * D_TMLP]
    zf = z[:, 2 * D_RNN + 2 * D_TMLP:D_IN].astype(BF16)
    for ref, w in ((va_ref, a_ref), (vb_ref, b_ref)):
        v = jnp.dot(zf, w[...], preferred_element_type=F32).astype(BF16)
        if pair:
            half = v.shape[0] // 2
            ref[:, 0:D_FNET] = v[0:half]
            ref[:, D_FNET:2 * D_FNET] = v[half:]
        else:
            ref[...] = v


def _pair_short(L, nseq, row_step):
    return 2 * L <= PROJ_TILE and nseq % 2 == 0 and row_step == 0


def _in_proj(x, pos, mods, g, w_in_b, fa, fb, *, layer, row_base, row_step):
    nseq, L, _ = x.shape
    pair = _pair_short(L, nseq, row_step)
    ns, ll = (nseq // 2, 2 * L) if pair else (nseq, L)
    tm = min(ll, PROJ_TILE)
    add_pos = pos is not None
    in_specs = [pl.BlockSpec((1, tm, D_MODEL), lambda b, t: (b, t, 0))]
    args = [x.reshape(ns, ll, D_MODEL)]
    if add_pos:
        in_specs += [pl.BlockSpec(p.shape, lambda b, t: (0, 0)) for p in pos]
        args += list(pos)
    in_specs += [
        _layer_spec((SUBLANES, N_MOD * D_MODEL), layer),
        _layer_spec((1, D_MODEL), layer),
        _layer_spec((D_MODEL, D_IN), layer),
        _layer_spec((D_FNET, D_FNET), layer),
        _layer_spec((D_FNET, D_FNET), layer),
    ]
    args += [mods, g, w_in_b, fa, fb]
    vspec = (pl.BlockSpec((L, 2 * D_FNET), lambda b, t: (0, b)) if pair
             else pl.BlockSpec((tm, D_FNET), lambda b, t: (t, b)))
    zr, zuv, va, vb = pl.pallas_call(
        functools.partial(_in_kernel, add_pos=add_pos, pair=pair, row_base=row_base, row_step=row_step),
        grid=(ns, ll // tm),
        in_specs=in_specs,
        out_specs=[
            pl.BlockSpec((1, tm, 2 * D_RNN), lambda b, t: (b, t, 0)),
            pl.BlockSpec((1, tm, 2 * D_TMLP), lambda b, t: (b, t, 0)),
            vspec, vspec,
        ],
        out_shape=[
            jax.ShapeDtypeStruct((ns, ll, 2 * D_RNN), F32),
            jax.ShapeDtypeStruct((ns, ll, 2 * D_TMLP), F32),
            jax.ShapeDtypeStruct((L, nseq * D_FNET), BF16),
            jax.ShapeDtypeStruct((L, nseq * D_FNET), BF16),
        ],
        compiler_params=_cparams(("arbitrary", "arbitrary")),
        name=f"in_proj_{L}",
    )(*args)
    return zr.reshape(nseq, L, 2 * D_RNN), zuv.reshape(nseq, L, 2 * D_TMLP), va, vb


def _gelu_tanh(x):
    return 0.5 * x * (1.0 + jnp.tanh(math.sqrt(2.0 / math.pi) * (x + 0.044715 * (x * x * x))))


def _rows_to_tile(rows):
    sub = lax.broadcasted_iota(jnp.int32, (SUBLANES, LANES), 0)
    out = jnp.zeros((SUBLANES, LANES), F32)
    for s, r in enumerate(rows):
        out = jnp.where(sub == s, jnp.broadcast_to(r, (SUBLANES, LANES)), out)
    return out


def _lru_kernel(xr_ref, gr_ref, cw_ref, cb_ref, wa_ref, wx_ref, ba_ref, bx_ref, lam_ref, h0_ref,
                y_ref, st_ref, xnat, pext, af, bf, ab, bb, hfo, pfo, hbo, pbo, hnat, *, L, batched):
    S = L if batched else L // SUBLANES
    pitch = S + SUBLANES
    n = S * SUBLANES
    chunk = 256

    def seg(ref, s):
        return ref.at[s] if batched else ref.at[0, s * S:(s + 1) * S]

    for s in range(SUBLANES):
        xnat[s * pitch:s * pitch + S, :] = seg(xr_ref, s)[...]

    def perm_in(j, c):
        dst = pl.multiple_of((j + 2) * SUBLANES, SUBLANES)
        pext[pl.ds(dst, SUBLANES), :] = xnat[pl.ds(j, SUBLANES, stride=pitch), :]
        return c

    lax.fori_loop(0, S, perm_in, 0, unroll=8)

    sub = lax.broadcasted_iota(jnp.int32, (SUBLANES, LANES), 0)

    def from_prev_segment(v):
        return jnp.where(sub == 0, 0.0, pltpu.roll(v, 1, axis=0))

    def from_next_segment(v):
        return jnp.where(sub == SUBLANES - 1, 0.0, pltpu.roll(v, SUBLANES - 1, axis=0))

    if batched:
        pext[0:16, :] = jnp.zeros((16, LANES), F32)
        pext[(S + 2) * 8:(S + 3) * 8, :] = jnp.zeros((8, LANES), F32)
    else:
        pext[0:8, :] = from_prev_segment(pext[S * 8:(S + 1) * 8, :])
        pext[8:16, :] = from_prev_segment(pext[(S + 1) * 8:(S + 2) * 8, :])
        pext[(S + 2) * 8:(S + 3) * 8, :] = from_next_segment(pext[16:24, :])

    lam = lam_ref[...]
    nl = -lam
    sp = jnp.maximum(nl, 0.0) + jnp.log1p(jnp.exp(-jnp.abs(nl)))
    c_la = (-0.5 * LRU_C) * sp
    ba_h = 0.5 * ba_ref[...]
    bx_h = 0.5 * bx_ref[...]
    a_refs = (af, ab)
    b_refs = (bf, bb)

    def gates(i, c):
        base = pl.multiple_of(i * chunk, chunk)
        xc = (cw_ref[0:1, :] * pext[pl.ds(base, chunk), :]
              + cw_ref[1:2, :] * pext[pl.ds(base + 8, chunk), :]
              + cw_ref[2:3, :] * pext[pl.ds(base + 16, chunk), :]
              + cw_ref[3:4, :] * pext[pl.ds(base + 24, chunk), :]
              + cb_ref[...])
        xcb = xc.astype(BF16)
        xh = 0.5 * xc
        for d in range(2):
            tr = jnp.tanh(jnp.dot(xcb, wa_ref[d, 0], preferred_element_type=F32) + ba_h[d:d + 1, :])
            ti = jnp.tanh(jnp.dot(xcb, wx_ref[d, 0], preferred_element_type=F32) + bx_h[d:d + 1, :])
            log_a = c_la[d:d + 1, :] * (1.0 + tr)
            a = jnp.exp(log_a)
            v = jnp.tanh(log_a) * (-1.0 - a * a)
            coef = jnp.where(v > 0.0, v * lax.rsqrt(v), 0.0)
            a_refs[d][pl.ds(base, chunk), :] = a
            b_refs[d][pl.ds(base, chunk), :] = coef * ((1.0 + ti) * xh)
        return c

    lax.fori_loop(0, n // chunk, gates, 0, unroll=min(4, n // chunk))

    nq = LRU_SUB
    sq = S // nq

    def scan(i, carry):
        out = []
        for q in range(nq):
            hf, pf, hb, pb = carry[4 * q:4 * q + 4]
            jf = pl.multiple_of((q * sq + i) * SUBLANES, SUBLANES)
            jb = pl.multiple_of(((q + 1) * sq - 1 - i) * SUBLANES, SUBLANES)
            a1 = af[pl.ds(jf, SUBLANES), :]
            hf = a1 * hf + bf[pl.ds(jf, SUBLANES), :]
            pf = a1 * pf
            hfo[pl.ds(jf, SUBLANES), :] = hf
            pfo[pl.ds(jf, SUBLANES), :] = pf
            a2 = ab[pl.ds(jb, SUBLANES), :]
            hb = a2 * hb + bb[pl.ds(jb, SUBLANES), :]
            pb = a2 * pb
            hbo[pl.ds(jb, SUBLANES), :] = hb
            pbo[pl.ds(jb, SUBLANES), :] = pb
            out += [hf, pf, hb, pb]
        return tuple(out)

    zero = jnp.zeros((SUBLANES, LANES), F32)
    one = jnp.ones((SUBLANES, LANES), F32)
    ends = lax.fori_loop(0, sq, scan, (zero, one, zero, one) * nq, unroll=min(sq, 4))
    end_f = [(ends[4 * q], ends[4 * q + 1]) for q in range(nq)]
    end_b = [(ends[4 * q + 2], ends[4 * q + 3]) for q in range(nq)]

    hf, pf = end_f[0]
    for q in range(1, nq):
        hf, pf = end_f[q][1] * hf + end_f[q][0], end_f[q][1] * pf
    hb, pb = end_b[nq - 1]
    for q in range(nq - 2, -1, -1):
        hb, pb = end_b[q][1] * hb + end_b[q][0], end_b[q][1] * pb

    if batched:
        start_f = h0_ref[0]
        start_b = h0_ref[1]
        st_ref[0] = pf * start_f + hf
        st_ref[1] = pb * start_b + hb
    else:
        rows_f = [h0_ref[0, 0:1, :]]
        for s in range(1, SUBLANES):
            rows_f.append(pf[s - 1:s, :] * rows_f[-1] + hf[s - 1:s, :])
        st_ref[0, 0:1, :] = pf[7:8, :] * rows_f[7] + hf[7:8, :]
        rows_b = [None] * SUBLANES
        rows_b[7] = h0_ref[0, 1:2, :]
        for s in range(SUBLANES - 2, -1, -1):
            rows_b[s] = pb[s + 1:s + 2, :] * rows_b[s + 1] + hb[s + 1:s + 2, :]
        st_ref[0, 1:2, :] = pb[0:1, :] * rows_b[0] + hb[0:1, :]
        start_f = _rows_to_tile(rows_f)
        start_b = _rows_to_tile(rows_b)

    init_f = [start_f]
    for q in range(nq - 1):
        init_f.append(end_f[q][1] * init_f[q] + end_f[q][0])
    init_b = [None] * nq
    init_b[nq - 1] = start_b
    for q in range(nq - 1, 0, -1):
        init_b[q - 1] = end_b[q][1] * init_b[q] + end_b[q][0]

    for q in range(nq):
        def perm_out(j, c, q=q):
            src = pl.multiple_of(j * SUBLANES, SUBLANES)
            v = (hfo[pl.ds(src, SUBLANES), :] + pfo[pl.ds(src, SUBLANES), :] * init_f[q]
                 + hbo[pl.ds(src, SUBLANES), :] + pbo[pl.ds(src, SUBLANES), :] * init_b[q])
            hnat[pl.ds(j, SUBLANES, stride=pitch), :] = v
            return c

        lax.fori_loop(q * sq, (q + 1) * sq, perm_out, 0, unroll=min(sq, 8))

    for s in range(SUBLANES):
        seg(y_ref, s)[...] = hnat[s * pitch:s * pitch + S, :] * _gelu_tanh(seg(gr_ref, s)[...])


def _lru_mixer(zr, conv_w, conv_b, wa_bd, wx_bd, ba, bx, lam, h0, *, layer):
    nseq, L, _ = zr.shape
    ncb = D_RNN // LRU_CB
    batched = L <= LRU_BATCH_MAX_LEN
    nb = SUBLANES if batched else 1
    S = L if batched else L // SUBLANES
    pitch = S + SUBLANES
    n = S * SUBLANES
    vec2 = pl.BlockSpec((None, 2, LRU_CB), lambda b, c: (layer, 0, c))
    wspec = pl.BlockSpec((None, 2, 1, LRU_CB, LRU_CB), lambda b, c: (layer, 0, c, 0, 0))
    if batched:
        state = pl.BlockSpec((2, nb, LRU_CB), lambda b, c: (0, b, c))
        state_shape = (2, nseq, D_RNN)
    else:
        state = pl.BlockSpec((1, 2, LRU_CB), lambda b, c: (b, 0, c))
        state_shape = (nseq, 2, D_RNN)
    return pl.pallas_call(
        functools.partial(_lru_kernel, L=L, batched=batched),
        grid=(nseq // nb, ncb),
        in_specs=[
            pl.BlockSpec((nb, L, LRU_CB), lambda b, c: (b, 0, c)),
            pl.BlockSpec((nb, L, LRU_CB), lambda b, c: (b, 0, c + ncb)),
            pl.BlockSpec((None, CONV_W, LRU_CB), lambda b, c: (layer, 0, c)),
            pl.BlockSpec((None, 1, LRU_CB), lambda b, c: (layer, 0, c)),
            wspec, wspec, vec2, vec2, vec2,
            state,
        ],
        out_specs=[pl.BlockSpec((nb, L, LRU_CB), lambda b, c: (b, 0, c)), state],
        out_shape=[
            jax.ShapeDtypeStruct((nseq, L, D_RNN), F32),
            jax.ShapeDtypeStruct(state_shape, F32),
        ],
        scratch_shapes=[
            pltpu.VMEM((SUBLANES * pitch, LANES), F32),
            pltpu.VMEM(((S + 3) * SUBLANES, LANES), F32),
            *([pltpu.VMEM((n, LANES), F32)] * 8),
            pltpu.VMEM((SUBLANES * pitch, LANES), F32),
        ],
        compiler_params=_cparams(("arbitrary", "arbitrary")),
        name=f"lru_mixer_{L}",
    )(zr, zr, conv_w, conv_b, wa_bd, wx_bd, ba, bx, lam, h0)


def _out_kernel(*refs, add_pos, pair, row_base, row_step):
    if add_pos:
        (x_ref, ps_ref, pc_ref, yr_ref, zuv_ref, yf_ref, mod_ref, ws_ref, bt_ref, wo_ref, g_ref, wr_ref, br_ref,
         cin_ref, x1_ref, h2_ref, ri_ref, cnt_ref, carry) = refs
    else:
        (x_ref, yr_ref, zuv_ref, yf_ref, mod_ref, ws_ref, bt_ref, wo_ref, g_ref, wr_ref, br_ref, cin_ref,
         x1_ref, h2_ref, ri_ref, cnt_ref, carry) = refs
    tm = x_ref.shape[1]
    x = x_ref[0]
    if add_pos:
        x = x + _pos_tile(ps_ref, pc_ref, pl.program_id(1), tm)
    row = row_base + pl.program_id(0) * row_step
    m = mod_ref[pl.ds(row, 1), :]
    g1 = m[:, 2 * D_MODEL:3 * D_MODEL]
    sh2 = m[:, 3 * D_MODEL:4 * D_MODEL]
    sc2 = m[:, 4 * D_MODEL:5 * D_MODEL]

    head = lax.broadcasted_iota(jnp.int32, (CHUNK, D_TMLP), 1) >> 6
    yt_parts = []
    for ci in range(tm // CHUNK):
        u = zuv_ref[0, ci * CHUNK:(ci + 1) * CHUNK, 0:D_TMLP]
        v = zuv_ref[0, ci * CHUNK:(ci + 1) * CHUNK, D_TMLP:2 * D_TMLP].astype(BF16)
        s = jnp.zeros((CHUNK, D_TMLP), F32)
        for h in range(TMLP_HEADS):
            sh = jnp.dot(ws_ref[h], v, preferred_element_type=F32) + bt_ref[:, h:h + 1]
            s = jnp.where(head == h, sh, s)
        yt_parts.append(u * s)
    yt = jnp.concatenate(yt_parts, axis=0) if len(yt_parts) > 1 else yt_parts[0]

    if pair:
        yf = jnp.concatenate([yf_ref[:, 0:D_FNET], yf_ref[:, D_FNET:2 * D_FNET]], axis=0)
    else:
        yf = yf_ref[...]
    y = (jnp.dot(yr_ref[0].astype(BF16), wo_ref[0:D_RNN, :], preferred_element_type=F32)
         + jnp.dot(yt.astype(BF16), wo_ref[D_RNN:D_RNN + D_TMLP, :], preferred_element_type=F32)
         + jnp.dot(yf.astype(BF16), wo_ref[D_RNN + D_TMLP:D_MODEL, :], preferred_element_type=F32))
    x1 = x + g1 * y
    x1_ref[0] = x1
    h2 = _rms_mod(x1, g_ref[...], sc2, sh2)
    _store_token_major(h2_ref, h2)

    wr = wr_ref[...]
    w_hi = wr.astype(BF16)
    w_lo = (wr - w_hi.astype(F32)).astype(BF16)
    h_hi = h2.astype(BF16)
    h_lo = (h2 - h_hi.astype(F32)).astype(BF16)
    p_hi = jnp.dot(h_hi, jnp.concatenate([w_hi, w_lo], axis=-1), preferred_element_type=F32)
    p_lo = jnp.dot(h_lo, w_hi, preferred_element_type=F32)
    logits = p_hi[:, 0:ROUTER_LANES] + p_hi[:, ROUTER_LANES:2 * ROUTER_LANES] + p_lo + br_ref[...]
    lane = lax.broadcasted_iota(jnp.int32, (tm, ROUTER_LANES), 1)
    lane_f = lane.astype(F32)
    is_g = lane < N_EGROUPS
    gl = jnp.where(is_g, logits, NEG_BIG)
    gmax = jnp.max(gl, axis=-1, keepdims=True)
    gsel = jnp.min(jnp.where(gl == gmax, lane_f, 1e4), axis=-1, keepdims=True)
    pg = 1.0 / jnp.sum(jnp.where(is_g, jnp.exp(logits - gmax), 0.0), axis=-1, keepdims=True)
    grp_f = ((lane - N_EGROUPS) >> 3).astype(F32)
    emask = (lane >= N_EGROUPS) & (lane < N_EGROUPS + N_EXPERTS) & (grp_f == gsel)
    el = jnp.where(emask, logits, NEG_BIG)
    v1 = jnp.max(el, axis=-1, keepdims=True)
    i1 = jnp.min(jnp.where(el == v1, lane_f, 1e4), axis=-1, keepdims=True)
    el2 = jnp.where(lane_f == i1, NEG_BIG, el)
    v2 = jnp.max(el2, axis=-1, keepdims=True)
    i2 = jnp.min(jnp.where(el2 == v2, lane_f, 1e4), axis=-1, keepdims=True)
    e2x = jnp.exp(v2 - v1)
    fw1 = 1.0 / (1.0 + e2x)
    fw2 = e2x * fw1
    @pl.when((pl.program_id(0) == 0) & (pl.program_id(1) == 0))
    def _():
        carry[...] = cin_ref[...]

    e1 = i1 - N_EGROUPS
    e2 = i2 - N_EGROUPS
    m1 = lane_f == e1
    m2 = lane_f == e2
    oh = jnp.where(m1 | m2, 1.0, 0.0)
    r_i = lax.broadcasted_iota(jnp.int32, (tm, tm), 0)
    c_i = lax.broadcasted_iota(jnp.int32, (tm, tm), 1)
    tri = jnp.where(c_i < r_i, 1.0, 0.0).astype(BF16)
    before = jnp.dot(tri, oh.astype(BF16), preferred_element_type=F32) + carry[0:1, :]
    rank1 = jnp.sum(jnp.where(m1, before, 0.0), axis=-1, keepdims=True)
    rank2 = jnp.sum(jnp.where(m2, before, 0.0), axis=-1, keepdims=True)
    total = carry[0:1, :] + jnp.sum(oh, axis=0, keepdims=True)
    carry[0:1, :] = total
    cnt_ref[...] = jnp.broadcast_to(total, cnt_ref.shape)

    vals = (e1, e2, pg * fw1, pg * fw2, rank1, rank2)
    ri = jnp.zeros((tm, ROUTER_LANES), F32)
    for k, v in enumerate(vals):
        ri = jnp.where(lane == k, v, ri)
    ri_ref[0] = ri


def _layer_spec(shape, layer):
    zeros = (0,) * len(shape)
    return pl.BlockSpec((None, *shape), lambda *_: (layer, *zeros))


def _out_proj(x, pos, yr, zuv, yf, mods, ws_b, bt, wo_b, g, wr, br, cin, *, layer, row_base, row_step):
    nseq, L, _ = x.shape
    pair = _pair_short(L, nseq, row_step)
    ns, ll = (nseq // 2, 2 * L) if pair else (nseq, L)
    tm = min(ll, PROJ_TILE)
    add_pos = pos is not None
    in_specs = [pl.BlockSpec((1, tm, D_MODEL), lambda b, t: (b, t, 0))]
    args = [x.reshape(ns, ll, D_MODEL)]
    if add_pos:
        in_specs += [pl.BlockSpec(p.shape, lambda b, t: (0, 0)) for p in pos]
        args += list(pos)
    in_specs += [
        pl.BlockSpec((1, tm, D_RNN), lambda b, t: (b, t, 0)),
        pl.BlockSpec((1, tm, 2 * D_TMLP), lambda b, t: (b, t, 0)),
        (pl.BlockSpec((L, 2 * D_FNET), lambda b, t: (0, b)) if pair
         else pl.BlockSpec((tm, D_FNET), lambda b, t: (t, b))),
        _layer_spec((SUBLANES, N_MOD * D_MODEL), layer),
        _layer_spec((TMLP_HEADS, CHUNK, CHUNK), layer),
        _layer_spec((CHUNK, TMLP_HEADS), layer),
        _layer_spec((D_MODEL, D_MODEL), layer),
        _layer_spec((1, D_MODEL), layer),
        _layer_spec((D_MODEL, ROUTER_LANES), layer),
        _layer_spec((1, ROUTER_LANES), layer),
        pl.BlockSpec((SUBLANES, ROUTER_LANES), lambda b, t: (0, 0)),
    ]
    args += [yr.reshape(ns, ll, D_RNN), zuv.reshape(ns, ll, 2 * D_TMLP), yf, mods, ws_b, bt, wo_b, g, wr, br, cin]
    tok = pl.BlockSpec((1, tm, D_MODEL), lambda b, t: (b, t, 0))
    x1, h2, ri, cnt = pl.pallas_call(
        functools.partial(_out_kernel, add_pos=add_pos, pair=pair, row_base=row_base, row_step=row_step),
        grid=(ns, ll // tm),
        in_specs=in_specs,
        out_specs=[tok, pl.BlockSpec((tm * LANE_TILES, LANES), lambda b, t: (b * (ll // tm) + t, 0)),
                   pl.BlockSpec((1, tm, ROUTER_LANES), lambda b, t: (b, t, 0)),
                   pl.BlockSpec((SUBLANES, ROUTER_LANES), lambda b, t: (0, 0))],
        out_shape=[
            jax.ShapeDtypeStruct((ns, ll, D_MODEL), F32),
            jax.ShapeDtypeStruct((nseq * L * LANE_TILES, LANES), F32),
            jax.ShapeDtypeStruct((ns, ll, ROUTER_LANES), F32),
            jax.ShapeDtypeStruct((SUBLANES, ROUTER_LANES), F32),
        ],
        scratch_shapes=[pltpu.VMEM((SUBLANES, ROUTER_LANES), F32)],
        compiler_params=_cparams(("arbitrary", "arbitrary")),
        name=f"out_proj_{L}",
    )(*args)
    return x1.reshape(nseq, L, D_MODEL), h2, ri.reshape(nseq, L, ROUTER_LANES), cnt


def _store_token_major(ref, x):
    tm = x.shape[0]
    for j in range(LANE_TILES):
        ref[pl.ds(j, tm, stride=LANE_TILES), :] = x[:, j * LANES:(j + 1) * LANES]


def _load_token_major(ref):
    tm = ref.shape[0] // LANE_TILES
    return jnp.concatenate([ref[pl.ds(j, tm, stride=LANE_TILES), :] for j in range(LANE_TILES)], axis=-1)


def _token_copy(src_ref, src_tok, dst_ref, dst_tok, sem):
    return pltpu.make_async_copy(src_ref.at[pl.ds(pl.multiple_of(src_tok, LANE_TILES), LANE_TILES)],
                                 dst_ref.at[pl.ds(pl.multiple_of(dst_tok, LANE_TILES), LANE_TILES)], sem)


def _scatter_rows(pos_ref, h_ref, xs_ref, sem):
    rows = h_ref.shape[0]

    for r in range(0, rows, LANE_TILES):
        _token_copy(h_ref, r, xs_ref, pos_ref[0, 0, r // LANE_TILES], sem).start(priority=0)
        _token_copy(h_ref, r, xs_ref, pos_ref[0, 1, r // LANE_TILES], sem).start(priority=1)
    for _ in range(2):
        pltpu.make_async_copy(h_ref, xs_ref.at[pl.ds(0, rows)], sem).wait()


def _dispatch_kernel(pos_ref, hp_ref, hs_ref, xs_ref, sem, *, n_first):
    i = pl.program_id(0)

    @pl.when(i < n_first)
    def _():
        _scatter_rows(pos_ref, hp_ref, xs_ref, sem)

    @pl.when(i >= n_first)
    def _():
        _scatter_rows(pos_ref, hs_ref, xs_ref, sem)


def _dispatch(pos, h_first, h_second):
    tm = ROW_TILE
    blk = tm * LANE_TILES
    n_first = h_first.shape[0] // blk
    n_second = h_second.shape[0] // blk
    nrows = 2 * (h_first.shape[0] + h_second.shape[0])
    return pl.pallas_call(
        functools.partial(_dispatch_kernel, n_first=n_first),
        grid=(n_first + n_second,),
        in_specs=[
            pl.BlockSpec((1, 2, tm), lambda i: (i, 0, 0), memory_space=pltpu.SMEM),
            pl.BlockSpec((blk, LANES), lambda i: (jnp.minimum(i, n_first - 1), 0)),
            pl.BlockSpec((blk, LANES), lambda i: (jnp.maximum(i - n_first, 0), 0)),
        ],
        out_specs=pl.BlockSpec(memory_space=pl.ANY),
        out_shape=jax.ShapeDtypeStruct((nrows, LANES), F32),
        scratch_shapes=[pltpu.SemaphoreType.DMA(())],
        compiler_params=_cparams(("arbitrary",)),
        name="moe_dispatch",
    )(pos, h_first, h_second)


WEIGHT_LEAD = (3, 2, 1)
WEIGHT_SLOTS = 4
LEAD_ITEMS = max(WEIGHT_LEAD)


def _expert_kernel(wt_ref, we_ref, lo_ref, hi_ref, first_ref, ord_ref, xs_ref, w1_ref, w3_ref, w2_ref, ys_ref,
                   w1b, w3b, w2b):
    s = pl.program_id(0)
    last = pl.num_programs(0) - 1
    tm = EXPERT_TILE

    def stage(lead, src, dst):
        cur = jnp.minimum(s + lead, last)
        prv = jnp.minimum(s + lead - 1, last)

        @pl.when((s == 0) | (we_ref[cur] != we_ref[prv]))
        def _():
            dst[ord_ref[cur] & (WEIGHT_SLOTS - 1)] = src[0].astype(BF16)

    for lead, src, dst in zip(WEIGHT_LEAD, (w1_ref, w3_ref, w2_ref), (w1b, w3b, w2b)):
        stage(lead, src, dst)

    lo = lo_ref[s]
    hi = hi_ref[s]
    full = (lo == 0) & (hi == tm)

    @pl.when((first_ref[s] == 1) & jnp.logical_not(full))
    def _():
        ys_ref[...] = jnp.zeros_like(ys_ref)

    slot = ord_ref[s] & (WEIGHT_SLOTS - 1)

    def run(t0, nt, merge):
        xs_part = xs_ref.at[pl.ds(t0 * LANE_TILES, nt * LANE_TILES)]
        ys_part = ys_ref.at[pl.ds(t0 * LANE_TILES, nt * LANE_TILES)]
        x = _load_token_major(xs_part).astype(BF16)
        a = jnp.dot(x, w1b[slot], preferred_element_type=F32)
        b = jnp.dot(x, w3b[slot], preferred_element_type=F32)
        hid = (a * jax.nn.sigmoid(a)) * b
        res = jnp.dot(hid.astype(BF16), w2b[slot], preferred_element_type=F32)
        if merge:
            rows = lax.broadcasted_iota(jnp.int32, res.shape, 0) + t0
            res = jnp.where((rows >= lo) & (rows < hi), res, _load_token_major(ys_part))
        _store_token_major(ys_part, res)

    pl.when(full)(lambda: run(0, tm, False))
    partial = (hi > lo) & jnp.logical_not(full)
    for t0 in range(0, tm, EXPERT_PART):
        pl.when(partial & (lo < t0 + EXPERT_PART) & (hi > t0))(functools.partial(run, t0, EXPERT_PART, True))


def _experts(items, xs, w1, w3, w2):
    nw = items[0].shape[0]

    def row(s, wt, we, lo, hi, fi, od):
        return (wt[s], 0)

    def weight(lead):
        return lambda s, wt, we, lo, hi, fi, od: (we[jnp.minimum(s + lead, nw - 1)], 0, 0)

    return pl.pallas_call(
        _expert_kernel,
        grid_spec=pltpu.PrefetchScalarGridSpec(
            num_scalar_prefetch=6,
            grid=(nw,),
            in_specs=[
                pl.BlockSpec((EXPERT_TILE * LANE_TILES, LANES), row),
                pl.BlockSpec((1, D_MODEL, D_EXPERT), weight(WEIGHT_LEAD[0])),
                pl.BlockSpec((1, D_MODEL, D_EXPERT), weight(WEIGHT_LEAD[1])),
                pl.BlockSpec((1, D_EXPERT, D_MODEL), weight(WEIGHT_LEAD[2])),
            ],
            out_specs=pl.BlockSpec((EXPERT_TILE * LANE_TILES, LANES), row),
            scratch_shapes=[
                pltpu.VMEM((WEIGHT_SLOTS, D_MODEL, D_EXPERT), BF16),
                pltpu.VMEM((WEIGHT_SLOTS, D_MODEL, D_EXPERT), BF16),
                pltpu.VMEM((WEIGHT_SLOTS, D_EXPERT, D_MODEL), BF16),
            ],
        ),
        out_shape=jax.ShapeDtypeStruct(xs.shape, F32),
        compiler_params=_cparams(("arbitrary",)),
        name="moe_experts",
    )(*items, xs, w1, w3, w2)


def _combine_kernel(pos_ref, posn_ref, x1_ref, ri_ref, mod_ref, gf_ref, ys_ref, o_ref, ybuf, sem,
                    *, row_base, row_step, tiles_per_seq, final):
    tm = x1_ref.shape[0]
    i = pl.program_id(0)
    n = pl.num_programs(0)

    def gather(p_ref, slot):
        for r in range(tm):
            _token_copy(ys_ref, p_ref[0, 0, r], ybuf.at[slot, 0], r * LANE_TILES, sem.at[slot]).start(priority=0)
            _token_copy(ys_ref, p_ref[0, 1, r], ybuf.at[slot, 1], r * LANE_TILES, sem.at[slot]).start(priority=1)

    def drain(slot):
        for k in range(2):
            pltpu.make_async_copy(ys_ref.at[pl.ds(0, tm * LANE_TILES)], ybuf.at[slot, k], sem.at[slot]).wait()

    def tile(slot):
        if slot == 0:
            @pl.when(i == 0)
            def _():
                gather(pos_ref, 0)

        @pl.when(i + 1 < n)
        def _():
            gather(posn_ref, 1 - slot)

        drain(slot)
        row = row_base + (i // tiles_per_seq) * row_step
        g2 = mod_ref[pl.ds(row, 1), 5 * D_MODEL:6 * D_MODEL]
        ri = ri_ref[...]
        y = (ri[:, 2:3] * _load_token_major(ybuf.at[slot, 0])
             + ri[:, 3:4] * _load_token_major(ybuf.at[slot, 1]))
        x2 = x1_ref[...] + g2 * y
        if final:
            ms = jnp.mean(x2 * x2, axis=-1, keepdims=True)
            o_ref[...] = (x2 * lax.rsqrt(ms + EPS)) * gf_ref[...]
        else:
            o_ref[...] = x2

    for slot in range(2):
        pl.when((i & 1) == slot)(functools.partial(tile, slot))


def _combine(pos, x1, ri, mods, g_final, ys, *, layer, row_base, row_step, tiles_per_seq, final):
    ntok = x1.shape[0]
    tm = ROW_TILE
    tok = pl.BlockSpec((tm, D_MODEL), lambda i: (i, 0))
    return pl.pallas_call(
        functools.partial(_combine_kernel, row_base=row_base, row_step=row_step, tiles_per_seq=tiles_per_seq,
                          final=final),
        grid=(ntok // tm,),
        in_specs=[
            pl.BlockSpec((1, 2, tm), lambda i: (i, 0, 0), memory_space=pltpu.SMEM),
            pl.BlockSpec((1, 2, tm), lambda i: (jnp.minimum(i + 1, ntok // tm - 1), 0, 0), memory_space=pltpu.SMEM),
            tok,
            pl.BlockSpec((tm, ROUTER_LANES), lambda i: (i, 0)),
            _layer_spec((SUBLANES, N_MOD * D_MODEL), layer),
            pl.BlockSpec((1, D_MODEL), lambda i: (0, 0)),
            pl.BlockSpec(memory_space=pl.ANY),
        ],
        out_specs=tok,
        out_shape=jax.ShapeDtypeStruct((ntok, D_MODEL), F32),
        scratch_shapes=[
            pltpu.VMEM((2, 2, tm * LANE_TILES, LANES), F32),
            pltpu.SemaphoreType.DMA((2,)),
        ],
        compiler_params=_cparams(("arbitrary",)),
        name=f"moe_combine_{ntok}",
    )(pos, pos, x1, ri, mods, g_final, ys)


def _route(ri_first, ri_second, cnt):
    counts = cnt[0, 0:N_EXPERTS].astype(jnp.int32)
    offs = jnp.cumsum(counts) - counts
    info = jnp.concatenate([ri_first[:, 0:SUBLANES], ri_second[:, 0:SUBLANES]], axis=0).T
    vals = info.astype(jnp.int32)
    base = jnp.zeros_like(vals)
    for k in range(N_EXPERTS):
        base = jnp.where(vals == k, offs[k], base)
    slot = (base[0:2] + vals[4:6]) * LANE_TILES
    n_tiles = slot.shape[1] // ROW_TILE
    pos_tiles = jnp.swapaxes(slot.reshape(2, n_tiles, ROW_TILE), 0, 1)
    return pos_tiles, counts, offs


def _work_items(counts, offs, nw, layer):
    tm = EXPERT_TILE
    first_tile = offs // tm
    last_tile = (offs + counts - 1) // tm
    n_e = jnp.where(counts > 0, last_tile - first_tile + 1, 0)
    w_end = jnp.cumsum(n_e)
    w_start = w_end - n_e
    total = w_end[-1]
    w = jnp.arange(nw, dtype=jnp.int32)
    wc = jnp.minimum(w, total - 1)
    e_w = jnp.sum((wc[:, None] >= w_end[None, :]).astype(jnp.int32), axis=-1)
    sel = (e_w[:, None] == jnp.arange(N_EXPERTS, dtype=jnp.int32)).astype(jnp.int32)
    pick = lambda v: jnp.sum(sel * v[None, :], axis=-1)
    off_w = pick(offs)
    tile_w = pick(first_tile) + (wc - pick(w_start))
    lo = jnp.clip(off_w - tile_w * tm, 0, tm)
    hi = jnp.clip(off_w + pick(counts) - tile_w * tm, 0, tm)
    valid = w < total
    lo = jnp.where(valid, lo, 0)
    hi = jnp.where(valid, hi, 0)
    prev_tile = jnp.concatenate([jnp.full((1,), -1, tile_w.dtype), tile_w[:-1]])
    first = tile_w != prev_tile
    ordinal = pick(jnp.cumsum((counts > 0).astype(jnp.int32)) - 1)

    def lead(a, fill=None):
        head = jnp.broadcast_to(a[0] if fill is None else jnp.asarray(fill, a.dtype), (LEAD_ITEMS,))
        return jnp.concatenate([head, a]).astype(jnp.int32)

    return (lead(tile_w), lead(e_w + layer * N_EXPERTS), lead(lo, 0), lead(hi, 0), lead(first, 0), lead(ordinal))


def _block_diag(w, nblk):
    *lead, H, d, _ = w.shape
    w = w.reshape(*lead, H // nblk, nblk, d, d)
    eye = jnp.eye(nblk, dtype=w.dtype)
    out = jnp.einsum('...gij,gh->...gihj', w, eye)
    return out.reshape(*lead, H // nblk, nblk * d, nblk * d)


def kernel(x_prompt, x_sample, state_lru, c, c_ctx, w_ada, b_ada, g_mix, g_ffn, g_final, w_in, w_out, conv_w, conv_b, lru_wa, lru_ba, lru_wx, lru_bx, lru_lambda, tmlp_ws, tmlp_b, fnet_w, router_g, router_g_b, router_e, router_e_b, e_w1, e_w3, e_w2):
    bp, lp, _ = x_prompt.shape
    bs, ls, _ = x_sample.shape
    n_ctx = bp * lp
    n_lat = bs * ls
    n_tok = n_ctx + n_lat

    cond8 = jnp.concatenate([c_ctx[None, :], c, jnp.zeros((SUBLANES - 1 - bs, D_MODEL), F32)], axis=0)
    pos = _pos_tables(max(ls // GRID_W, GRID_W))
    w_in_b = w_in.astype(BF16)
    w_out_b = w_out.astype(BF16)
    heads_per_cb = LRU_CB // RNN_HEAD_DIM
    wa_bd = (0.5 * _block_diag(lru_wa, heads_per_cb)).astype(BF16)
    wx_bd = (0.5 * _block_diag(lru_wx, heads_per_cb)).astype(BF16)
    ws_b = tmlp_ws.astype(BF16)
    bt = jnp.swapaxes(tmlp_b, 1, 2)
    fnet_bd = _block_diag(fnet_w, D_FNET // FNET_GROUP_DIM)[:, 0]
    wr = jnp.concatenate([router_g, router_e,
                          jnp.zeros((DEPTH, D_MODEL, ROUTER_LANES - N_EGROUPS - N_EXPERTS), F32)], axis=-1)
    br = jnp.concatenate([router_g_b, router_e_b,
                          jnp.zeros((DEPTH, ROUTER_LANES - N_EGROUPS - N_EXPERTS), F32)], axis=-1)
    ew1 = e_w1.reshape(DEPTH * N_EXPERTS, D_MODEL, D_EXPERT)
    ew3 = e_w3.reshape(DEPTH * N_EXPERTS, D_MODEL, D_EXPERT)
    ew2 = e_w2.reshape(DEPTH * N_EXPERTS, D_EXPERT, D_MODEL)

    mods = _modulation(cond8, w_ada, b_ada)
    fa, fb = _fnet_prep(fnet_bd)
    ct_p, st_p = _dft_tables(lp)
    ct_s, st_s = _dft_tables(ls)

    nw = 2 * n_tok // EXPERT_TILE + N_EXPERTS
    h0_p = jnp.zeros((bp, 2, D_RNN), F32)
    g_mix3 = g_mix[:, None, :]
    g_ffn3 = g_ffn[:, None, :]
    conv_b3 = conv_b[:, None, :]
    br3 = br[:, None, :]

    xp, xs = x_prompt, x_sample
    states = []
    for l in range(DEPTH):
        paths = []
        cnt = jnp.zeros((SUBLANES, ROUTER_LANES), F32)
        for (x, ct, st, h0, row_base, row_step, is_lat) in (
                (xp, ct_p, st_p, h0_p, 0, 0, False),
                (xs, ct_s, st_s, state_lru[:, l], 1, 1, True)):
            pe = pos if (is_lat and l == 0) else None
            zr, zuv, va, vb = _in_proj(x, pe, mods, g_mix3, w_in_b, fa, fb,
                                       layer=l, row_base=row_base, row_step=row_step)
            swap = x.shape[1] <= LRU_BATCH_MAX_LEN
            h0 = jnp.swapaxes(h0, 0, 1) if swap else h0
            yr, st_new = _lru_mixer(zr, conv_w, conv_b3, wa_bd, wx_bd, lru_ba, lru_bx, lru_lambda, h0, layer=l)
            st_new = jnp.swapaxes(st_new, 0, 1) if swap else st_new
            yf = _dft_apply(ct, st, va, vb)
            x1, h2, ri, cnt = _out_proj(x, pe, yr, zuv, yf, mods, ws_b, bt, w_out_b, g_ffn3, wr, br3, cnt,
                                        layer=l, row_base=row_base, row_step=row_step)
            paths.append((x1, h2, ri, st_new))
        states.append(paths[0][3])

        pos_tiles, counts, offs = _route(paths[0][2].reshape(n_ctx, ROUTER_LANES),
                                         paths[1][2].reshape(n_lat, ROUTER_LANES), cnt)
        items = _work_items(counts, offs, nw, l)
        nct = n_ctx // ROW_TILE
        xsorted = _dispatch(pos_tiles, paths[0][1], paths[1][1])
        ysorted = _experts(items, xsorted, ew1, ew3, ew2)
        gfin = g_final[None, :]
        final = l == DEPTH - 1
        xp = _combine(pos_tiles[:nct], paths[0][0].reshape(n_ctx, D_MODEL),
                      paths[0][2].reshape(n_ctx, ROUTER_LANES), mods, gfin, ysorted, layer=l,
                      row_base=0, row_step=0, tiles_per_seq=lp // ROW_TILE, final=final).reshape(bp, lp, D_MODEL)
        xs = _combine(pos_tiles[nct:], paths[1][0].reshape(n_lat, D_MODEL),
                      paths[1][2].reshape(n_lat, ROUTER_LANES), mods, gfin, ysorted, layer=l,
                      row_base=1, row_step=1, tiles_per_seq=ls // ROW_TILE, final=final).reshape(bs, ls, D_MODEL)

    new_state = jnp.stack(states, axis=1).astype(x_prompt.dtype)
    return (xp, xs, new_state)
```
